```python
import jax, jax.numpy as jnp
from jax import lax
import numpy as np

D_MODEL = 1024
BATCH = 4
SEQ = 4096
DEPTH = 2
DEC_BATCH = 8
DEC_SEQ = 16
PAST_LEN = 2048

CHUNK = 64
N_A = DEPTH // 2
MIX_W = D_MODEL // 2
POOL_WINDOWS = (2, 4, 8, 16)
POOL_GROUP = MIX_W // len(POOL_WINDOWS)
POOL_BUF = max(POOL_WINDOWS) - 1
FOX_HEADS = 8
FOX_HD = MIX_W // FOX_HEADS
MEM_HEADS = 4
MEM_W = D_MODEL - MIX_W
MEM_HD = MEM_W // MEM_HEADS
N_MEM = 256
D_FF = 4 * D_MODEL
Q_BLOCK = 128
EPS = 1e-6
FORGET_BIAS = 3.0

kernel_name = "yoco_pool_fox_mem_streaming_step"


def rmsnorm(x, g):
    xf = x.astype(jnp.float32)
    y = xf * lax.rsqrt(jnp.mean(xf * xf, axis=-1, keepdims=True) + EPS) * g.astype(jnp.float32)
    return y.astype(x.dtype)


def pool_mixer(u, u_prev, w_pool, scale):
    B, T, _ = u.shape
    P = u_prev.shape[1]
    ext = jnp.concatenate([u_prev.astype(u.dtype), u], axis=1)
    cs = jnp.cumsum(ext.astype(jnp.float32), axis=1)
    cs = jnp.concatenate([jnp.zeros((B, 1, MIX_W), jnp.float32), cs], axis=1)
    hi = np.arange(P, P + T) + 1
    outs = []
    for g, w in enumerate(POOL_WINDOWS):
        lo = np.maximum(hi - w, 0)
        cnt = jnp.asarray((hi - lo).astype(np.float32))[None, :, None]
        csg = cs[..., g * POOL_GROUP:(g + 1) * POOL_GROUP]
        mean = (csg[:, hi] - csg[:, lo]) / cnt
        outs.append(mean - u[..., g * POOL_GROUP:(g + 1) * POOL_GROUP].astype(jnp.float32))
    pooled = jnp.stack(outs, axis=2).astype(u.dtype)
    y = jnp.einsum('btgc,gcd->btgd', pooled, w_pool).reshape(B, T, MIX_W) * scale
    return y, ext[:, -POOL_BUF:]


def shared_kvf(x, g_kv, w_kvf, b_f):
    B, T, _ = x.shape
    kvf = rmsnorm(x, g_kv) @ w_kvf
    k = kvf[..., :MIX_W].reshape(B, T, FOX_HEADS, FOX_HD)
    v = kvf[..., MIX_W:2 * MIX_W].reshape(B, T, FOX_HEADS, FOX_HD)
    logf = jax.nn.log_sigmoid(kvf[..., 2 * MIX_W:].astype(jnp.float32) + b_f.astype(jnp.float32))
    return k, v, logf


def fox_attention(q, k_all, v_all, logf_all, P):
    B, T = q.shape[:2]
    L = k_all.shape[1]
    C = jnp.cumsum(logf_all.astype(jnp.float32), axis=1)
    ck = jnp.transpose(C, (0, 2, 1))[:, :, None, :]
    cq = C[:, P:]
    kpos = jnp.arange(L)
    qpos = P + jnp.arange(T)
    scale = FOX_HD ** -0.5

    def block(args):
        qb, cqb, qpb = args
        s = jnp.einsum('bqhd,bkhd->bhqk', qb, k_all, preferred_element_type=jnp.float32) * scale
        s = s + jnp.transpose(cqb, (0, 2, 1))[..., None] - ck
        s = jnp.where(kpos[None, None, None, :] <= qpb[None, None, :, None], s, -jnp.inf)
        p = jax.nn.softmax(s, axis=-1).astype(v_all.dtype)
        return jnp.einsum('bhqk,bkhd->bqhd', p, v_all)

    if T % Q_BLOCK == 0:
        nb = T // Q_BLOCK
        qb = q.reshape(B, nb, Q_BLOCK, FOX_HEADS, FOX_HD).transpose(1, 0, 2, 3, 4)
        cqb = cq.reshape(B, nb, Q_BLOCK, FOX_HEADS).transpose(1, 0, 2, 3)
        qpb = qpos.reshape(nb, Q_BLOCK)
        out = lax.map(block, (qb, cqb, qpb))
        out = out.transpose(1, 0, 2, 3, 4)
    else:
        out = block((q, cq, qpos))
    return out.reshape(B, T, MIX_W)


def mem_project(mem, g_mem_l, w_mem_kv_l):
    B, N, _ = mem.shape
    kv = rmsnorm(mem, g_mem_l) @ w_mem_kv_l
    mk = kv[..., :MEM_W].reshape(B, N, MEM_HEADS, MEM_HD)
    mv = kv[..., MEM_W:].reshape(B, N, MEM_HEADS, MEM_HD)
    return mk, mv


def mem_attention(q, mk, mv):
    B, T, _ = q.shape
    qh = q.reshape(B, T, MEM_HEADS, MEM_HD)
    s = jnp.einsum('bqhd,bkhd->bhqk', qh, mk, preferred_element_type=jnp.float32) * (MEM_HD ** -0.5)
    p = jax.nn.softmax(s, axis=-1).astype(mv.dtype)
    return jnp.einsum('bhqk,bkhd->bqhd', p, mv).reshape(B, T, MEM_W)


def squared_relu_mlp(x, g_pre, g_post, w_up, w_down):
    a = jnp.square(jax.nn.relu(rmsnorm(x, g_pre) @ w_up))
    return rmsnorm(a @ w_down, g_post)


def run_trunk(x, pool_prev, k_prev, v_prev, logf_prev, mem_k, mem_v,
              g_mix_pre, g_mix_post, g_mlp_pre, g_mlp_post, w_in, w_out, w_pool, pool_scale,
              g_kv, w_kvf, b_f, w_up, w_down):
    P_kv = k_prev.shape[1]
    pool_states = []
    k_new = v_new = logf_new = None
    k_all = v_all = logf_all = None
    for l in range(DEPTH):
        h = rmsnorm(x, g_mix_pre[l])
        proj = h @ w_in[l]
        mix_in, mem_q = proj[..., :MIX_W], proj[..., MIX_W:]
        if l < N_A:
            mix_out, st = pool_mixer(mix_in, pool_prev[l], w_pool[l], pool_scale[l])
            pool_states.append(st)
        else:
            if k_new is None:
                k_new, v_new, logf_new = shared_kvf(x, g_kv, w_kvf, b_f)
                k_all = jnp.concatenate([k_prev.astype(k_new.dtype), k_new], axis=1)
                v_all = jnp.concatenate([v_prev.astype(v_new.dtype), v_new], axis=1)
                logf_all = jnp.concatenate([logf_prev.astype(jnp.float32), logf_new], axis=1)
            B, T, _ = mix_in.shape
            q = mix_in.reshape(B, T, FOX_HEADS, FOX_HD)
            mix_out = fox_attention(q, k_all, v_all, logf_all, P_kv)
        mem_out = mem_attention(mem_q, mem_k[l], mem_v[l])
        x = x + rmsnorm(jnp.concatenate([mix_out.astype(x.dtype), mem_out.astype(x.dtype)], axis=-1) @ w_out[l],
                        g_mix_post[l])
        x = x + squared_relu_mlp(x, g_mlp_pre[l], g_mlp_post[l], w_up[l], w_down[l])
    return x, jnp.stack(pool_states, axis=0), k_new, v_new, logf_new


def setup_inputs(seed: int = 0) -> dict:
    key = jax.random.key(seed)
    ks = jax.random.split(key, 32)

    def nrm(k, shape, scale=1.0):
        return jax.random.normal(k, shape, jnp.float32) * scale

    def gain(k, shape):
        return 1.0 + 0.05 * jax.random.normal(k, shape, jnp.float32)

    w_kvf = jnp.concatenate([nrm(ks[20], (D_MODEL, 2 * MIX_W), D_MODEL ** -0.5),
                             nrm(ks[21], (D_MODEL, FOX_HEADS), 0.5 * D_MODEL ** -0.5)], axis=1)
    return {
        "x_prompt": nrm(ks[0], (BATCH, SEQ, D_MODEL)),
        "x_sample": nrm(ks[1], (DEC_BATCH, DEC_SEQ, D_MODEL)),
        "cache_pool": nrm(ks[2], (N_A, DEC_BATCH, POOL_BUF, MIX_W)),
        "cache_k": nrm(ks[3], (DEC_BATCH, PAST_LEN, FOX_HEADS, FOX_HD)),
        "cache_v": nrm(ks[4], (DEC_BATCH, PAST_LEN, FOX_HEADS, FOX_HD)),
        "cache_logf": jax.nn.log_sigmoid(FORGET_BIAS + nrm(ks[5], (DEC_BATCH, PAST_LEN, FOX_HEADS), 0.5)),
        "cache_mem_k": nrm(ks[6], (DEPTH, DEC_BATCH, N_MEM, MEM_HEADS, MEM_HD)),
        "cache_mem_v": nrm(ks[7], (DEPTH, DEC_BATCH, N_MEM, MEM_HEADS, MEM_HD)),
        "mem_prompt": nrm(ks[8], (BATCH, N_MEM, D_MODEL)),
        "g_mix_pre": gain(ks[9], (DEPTH, D_MODEL)),
        "g_mix_post": gain(ks[10], (DEPTH, D_MODEL)),
        "g_mlp_pre": gain(ks[11], (DEPTH, D_MODEL)),
        "g_mlp_post": gain(ks[12], (DEPTH, D_MODEL)),
        "w_in": nrm(ks[13], (DEPTH, D_MODEL, MIX_W + MEM_W), D_MODEL ** -0.5),
        "w_out": nrm(ks[14], (DEPTH, MIX_W + MEM_W, D_MODEL), (MIX_W + MEM_W) ** -0.5),
        "w_pool": nrm(ks[15], (N_A, len(POOL_WINDOWS), POOL_GROUP, POOL_GROUP), POOL_GROUP ** -0.5),
        "pool_scale": 1.0 + 0.1 * nrm(ks[16], (N_A, MIX_W)),
        "g_kv": gain(ks[17], (D_MODEL,)),
        "w_kvf": w_kvf,
        "b_f": FORGET_BIAS + 0.5 * nrm(ks[18], (FOX_HEADS,)),
        "g_mem": gain(ks[19], (DEPTH, D_MODEL)),
        "w_mem_kv": nrm(ks[22], (DEPTH, D_MODEL, 2 * MEM_W), D_MODEL ** -0.5),
        "w_up": nrm(ks[23], (DEPTH, D_MODEL, D_FF), D_MODEL ** -0.5),
        "w_down": nrm(ks[24], (DEPTH, D_FF, D_MODEL), D_FF ** -0.5),
    }


def reference(x_prompt, x_sample, cache_pool, cache_k, cache_v, cache_logf, cache_mem_k, cache_mem_v,
              mem_prompt, g_mix_pre, g_mix_post, g_mlp_pre, g_mlp_post, w_in, w_out, w_pool, pool_scale,
              g_kv, w_kvf, b_f, g_mem, w_mem_kv, w_up, w_down):
    mks, mvs = [], []
    for l in range(DEPTH):
        mk, mv = mem_project(mem_prompt, g_mem[l], w_mem_kv[l])
        mks.append(mk)
        mvs.append(mv)
    mem_k_prompt = jnp.stack(mks, axis=0)
    mem_v_prompt = jnp.stack(mvs, axis=0)

    B = x_prompt.shape[0]
    dt = x_prompt.dtype
    no_pool = jnp.zeros((N_A, B, 0, MIX_W), dt)
    no_k = jnp.zeros((B, 0, FOX_HEADS, FOX_HD), dt)
    no_logf = jnp.zeros((B, 0, FOX_HEADS), jnp.float32)

    y_prompt, pool_state_prompt, k_prompt, v_prompt, logf_prompt = run_trunk(
        x_prompt, no_pool, no_k, no_k, no_logf, mem_k_prompt, mem_v_prompt,
        g_mix_pre, g_mix_post, g_mlp_pre, g_mlp_post, w_in, w_out, w_pool, pool_scale,
        g_kv, w_kvf, b_f, w_up, w_down)

    y_sample, pool_state_sample, k_sample, v_sample, logf_sample = run_trunk(
        x_sample, cache_pool, cache_k, cache_v, cache_logf, cache_mem_k, cache_mem_v,
        g_mix_pre, g_mix_post, g_mlp_pre, g_mlp_post, w_in, w_out, w_pool, pool_scale,
        g_kv, w_kvf, b_f, w_up, w_down)

    return (y_prompt, y_sample, pool_state_prompt, pool_state_sample,
            k_prompt, v_prompt, logf_prompt, k_sample, v_sample, logf_sample,
            mem_k_prompt, mem_v_prompt)
```

```python
import functools

import jax
import jax.numpy as jnp
from jax import lax
from jax.experimental import pallas as pl
from jax.experimental.pallas import tpu as pltpu

F32 = jnp.float32
BF16 = jnp.bfloat16

D_MODEL = 1024
MIX_W = D_MODEL // 2
MEM_W = D_MODEL - MIX_W
POOL_WINDOWS = (2, 4, 8, 16)
POOL_GROUP = MIX_W // len(POOL_WINDOWS)
POOL_BUF = max(POOL_WINDOWS) - 1
FOX_HEADS = 8
FOX_HD = MIX_W // FOX_HEADS
MEM_HEADS = 4
MEM_HD = MEM_W // MEM_HEADS
D_FF = 4 * D_MODEL
EPS = 1e-6

V7X_LANES = 128
V7X_SUBLANES_F32 = 8
V7X_VMEM_BYTES = 64 * 1024 * 1024

HALO = 2 * V7X_SUBLANES_F32
FF_CHUNK = 1024
PAIR_W = 2 * FOX_HD
NEG_BIG = -1e30
VMEM_LIMIT = V7X_VMEM_BYTES - 8 * 1024 * 1024

assert HALO >= POOL_BUF and PAIR_W == V7X_LANES and POOL_GROUP == V7X_LANES and MEM_HD == V7X_LANES


def _const_spec(shape):
    zeros = (0,) * len(shape)
    return pl.BlockSpec(shape, lambda *_: zeros, pipeline_mode=pl.Buffered(1))


def _rms(x, g):
    return x * lax.rsqrt(jnp.mean(x * x, axis=-1, keepdims=True) + EPS) * g


def _dot(a, b):
    return jnp.dot(a, b, preferred_element_type=F32)


def _dot_nt(a, b):
    return lax.dot_general(a, b, (((1,), (1,)), ((), ())), preferred_element_type=F32)


def _log_sigmoid(x):
    return jnp.minimum(x, 0.0) - jnp.log1p(jnp.exp(-jnp.abs(x)))


def _cumsum(x, axis, seg):
    assert seg & (seg - 1) == 0 and x.shape[axis] % seg == 0
    idx = lax.broadcasted_iota(jnp.int32, x.shape, axis) & (seg - 1)
    k = 1
    while k < seg:
        x = x + jnp.where(idx >= k, pltpu.roll(x, k, axis=axis), 0.0)
        k *= 2
    return x


def _mem_attention(q, mk, mv):
    outs = []
    for h in range(MEM_HEADS):
        sl = slice(h * MEM_HD, (h + 1) * MEM_HD)
        s = _dot_nt(q[:, sl].astype(BF16), mk[:, sl]) * (MEM_HD ** -0.5)
        m = jnp.max(s, axis=-1, keepdims=True)
        p = jnp.exp(s - m)
        l = jnp.sum(p, axis=-1, keepdims=True)
        outs.append(_dot(p.astype(BF16), mv[:, sl]) / l)
    return jnp.concatenate(outs, axis=-1)


def _pool_mixer(u, halo, pos0, w_pool_ref, scale):
    tt = u.shape[0]
    ext = jnp.concatenate([halo, u], axis=0)
    pos = pos0 + lax.broadcasted_iota(jnp.int32, (tt, 1), 0)
    outs = []
    for g, w in enumerate(POOL_WINDOWS):
        sl = slice(g * POOL_GROUP, (g + 1) * POOL_GROUP)
        s = ext[:, sl]
        k = 1
        while k < w:
            s = s + pltpu.roll(s, k, axis=0)
            k *= 2
        cnt = jnp.minimum(pos + 1, w).astype(F32)
        pooled = s[HALO:] / cnt - u[:, sl]
        outs.append(_dot(pooled.astype(BF16), w_pool_ref[g]))
    return jnp.concatenate(outs, axis=-1) * scale


def _mlp(x1, g_pre, g_post, w_up_ref, w_down_ref):
    hm = _rms(x1, g_pre).astype(BF16)
    acc = jnp.zeros(x1.shape, F32)
    for c in range(D_FF // FF_CHUNK):
        cs = slice(c * FF_CHUNK, (c + 1) * FF_CHUNK)
        a = jnp.square(jnp.maximum(_dot(hm, w_up_ref[:, cs]), 0.0)).astype(BF16)
        acc = acc + _dot(a, w_down_ref[cs, :])
    return x1 + _rms(acc, g_post)


def _mem_project_kernel(mem_ref, g_ref, w_ref, mk_ref, mv_ref):
    kv = _dot(_rms(mem_ref[0], g_ref[0]).astype(BF16), w_ref[0])
    mk_ref[0, 0] = kv[:, :MEM_W]
    mv_ref[0, 0] = kv[:, MEM_W:]


def _mem_project(mem, g_mem, w_mem_kv):
    b, n, _ = mem.shape
    depth = w_mem_kv.shape[0]
    out = jax.ShapeDtypeStruct((depth, b, n, MEM_W), F32)
    return pl.pallas_call(
        _mem_project_kernel,
        grid=(depth, b),
        in_specs=[pl.BlockSpec((1, n, D_MODEL), lambda l, i: (i, 0, 0)),
                  pl.BlockSpec((1, 1, D_MODEL), lambda l, i: (l, 0, 0)),
                  pl.BlockSpec((1, D_MODEL, 2 * MEM_W), lambda l, i: (l, 0, 0))],
        out_specs=[pl.BlockSpec((1, 1, n, MEM_W), lambda l, i: (l, i, 0, 0))] * 2,
        out_shape=[out, out],
        compiler_params=pltpu.CompilerParams(dimension_semantics=("arbitrary", "arbitrary")),
        name="mem_project",
    )(mem, g_mem.reshape(depth, 1, D_MODEL), w_mem_kv)


def _layer0_kernel(x_ref, prev_ref, mk_ref, mv_ref, g_pre, g_post, g_mpre, g_mpost,
                   w_in, w_out, w_pool, pscale, w_up, w_down,
                   y_ref, state_ref, halo_ref, *, nb, tt, pos_base):
    t = pl.program_id(1)

    @pl.when(t == 0)
    def _():
        halo_ref[...] = prev_ref[...]

    x = x_ref[...].reshape(nb * tt, D_MODEL)
    proj = _dot(_rms(x, g_pre[...]).astype(BF16), w_in[...])
    cats = []
    for b in range(nb):
        rows = slice(b * tt, (b + 1) * tt)
        u = proj[rows, :MIX_W]
        mix = _pool_mixer(u, halo_ref[b], pos_base + t * tt, w_pool, pscale[...])
        halo_ref[b] = u[tt - HALO:, :]

        @pl.when(t == pl.num_programs(1) - 1)
        def _():
            state_ref[0, b] = u[tt - POOL_BUF:, :]

        mem = _mem_attention(proj[rows, MIX_W:], mk_ref[b].astype(BF16), mv_ref[b].astype(BF16))
        cats.append(jnp.concatenate([mix, mem], axis=-1))
    cat = cats[0] if nb == 1 else jnp.concatenate(cats, axis=0)
    x1 = x + _rms(_dot(cat.astype(BF16), w_out[...]), g_post[...])
    y = _mlp(x1, g_mpre[...], g_mpost[...], w_up, w_down)
    y_ref[...] = y.reshape(nb, tt, D_MODEL)


def _layer0(x, prev, mem_k, mem_v, g_pre, g_post, g_mpre, g_mpost, w_in, w_out, w_pool, pscale,
            w_up, w_down, *, nb, tt, pos_base):
    b, t, _ = x.shape
    n_mem = mem_k.shape[1]
    grid = (b // nb, t // tt)
    row = lambda a: a.reshape(1, -1)
    tile = pl.BlockSpec((nb, tt, D_MODEL), lambda i, j: (i, j, 0))
    per_b = lambda shape: pl.BlockSpec((nb,) + shape, lambda i, j: (i, 0, 0))
    return pl.pallas_call(
        functools.partial(_layer0_kernel, nb=nb, tt=tt, pos_base=pos_base),
        grid=grid,
        in_specs=[tile, per_b((HALO, MIX_W)), per_b((n_mem, MEM_W)), per_b((n_mem, MEM_W)),
                  _const_spec((1, D_MODEL)), _const_spec((1, D_MODEL)),
                  _const_spec((1, D_MODEL)), _const_spec((1, D_MODEL)),
                  _const_spec((D_MODEL, D_MODEL)), _const_spec((D_MODEL, D_MODEL)),
                  _const_spec(w_pool.shape), _const_spec((1, MIX_W)),
                  _const_spec((D_MODEL, D_FF)), _const_spec((D_FF, D_MODEL))],
        out_specs=[tile, pl.BlockSpec((1, nb, POOL_BUF, MIX_W), lambda i, j: (0, i, 0, 0))],
        out_shape=[jax.ShapeDtypeStruct((b, t, D_MODEL), F32),
                   jax.ShapeDtypeStruct((1, b, POOL_BUF, MIX_W), F32)],
        scratch_shapes=[pltpu.VMEM((nb, HALO, MIX_W), F32)],
        compiler_params=pltpu.CompilerParams(dimension_semantics=("arbitrary", "arbitrary"),
                                             vmem_limit_bytes=VMEM_LIMIT),
        name="layer0",
    )(x, prev, mem_k, mem_v, row(g_pre), row(g_post), row(g_mpre), row(g_mpost),
      w_in, w_out, w_pool, row(pscale), w_up, w_down)


def _layer1_front_kernel(x_ref, mk_ref, mv_ref, g_kv, g_pre, w_k, w_v, w_f, w_ft, b_f, b_ft, w_in,
                         k_ref, v_ref, logf_ref, kb_ref, vb_ref, q_ref, mem_ref, c_ref, ct_ref,
                         carry_ref, carry_t_ref, *, nb, tt):
    carried = nb == 1
    if carried:
        @pl.when(pl.program_id(1) == 0)
        def _():
            carry_ref[...] = jnp.zeros_like(carry_ref)
            carry_t_ref[...] = jnp.zeros_like(carry_t_ref)

    x = x_ref[...].reshape(nb * tt, D_MODEL)
    hk = _rms(x, g_kv[...]).astype(BF16)
    k = _dot(hk, w_k[...])
    v = _dot(hk, w_v[...])
    k_ref[...] = k.reshape(nb, tt, MIX_W)
    v_ref[...] = v.reshape(nb, tt, MIX_W)
    kb_ref[...] = k.astype(BF16).reshape(nb, tt, MIX_W)
    vb_ref[...] = v.astype(BF16).reshape(nb, tt, MIX_W)
    logf = _log_sigmoid(_dot(hk, w_f[...]) + b_f[...])
    logf_t = _log_sigmoid(_dot_nt(w_ft[...], hk) + b_ft[...])

    proj = _dot(_rms(x, g_pre[...]).astype(BF16), w_in[...])
    q_ref[...] = (proj[:, :MIX_W] * (FOX_HD ** -0.5)).astype(BF16).reshape(nb, tt, MIX_W)
    c = _cumsum(logf, 0, tt)
    c_t = _cumsum(logf_t, 1, tt)
    if carried:
        c = c + carry_ref[...]
        c_t = c_t + carry_t_ref[...]
        carry_ref[...] = c[tt - 1:, :]
        carry_t_ref[...] = c_t[:, tt - 1:]
    ct_ref[0, 0] = c_t
    for b in range(nb):
        rows = slice(b * tt, (b + 1) * tt)
        logf_ref[b] = logf[rows, :FOX_HEADS]
        c_ref[b] = c[rows, :FOX_HEADS]
        mem = _mem_attention(proj[rows, MIX_W:], mk_ref[b].astype(BF16), mv_ref[b].astype(BF16))
        mem_ref[b] = mem.astype(BF16)


def _layer1_front(x, mem_k, mem_v, g_kv, g_pre, w_k, w_v, w_f, w_ft, b_f, b_ft, w_in, *, nb, tt):
    b, t, _ = x.shape
    assert nb == 1 or t == tt
    n_mem = mem_k.shape[1]
    row = lambda a: a.reshape(1, -1)
    tile = lambda w: pl.BlockSpec((nb, tt, w), lambda i, j: (i, j, 0))
    per_b = lambda shape: pl.BlockSpec((nb,) + shape, lambda i, j: (i, 0, 0))
    act = lambda w, dt: jax.ShapeDtypeStruct((b, t, w), dt)
    return pl.pallas_call(
        functools.partial(_layer1_front_kernel, nb=nb, tt=tt),
        grid=(b // nb, t // tt),
        in_specs=[tile(D_MODEL), per_b((n_mem, MEM_W)), per_b((n_mem, MEM_W)),
                  _const_spec((1, D_MODEL)), _const_spec((1, D_MODEL)),
                  _const_spec((D_MODEL, MIX_W)), _const_spec((D_MODEL, MIX_W)),
                  _const_spec((D_MODEL, V7X_LANES)), _const_spec((FOX_HEADS, D_MODEL)),
                  _const_spec((1, V7X_LANES)), _const_spec((FOX_HEADS, 1)),
                  _const_spec((D_MODEL, D_MODEL))],
        out_specs=[tile(MIX_W), tile(MIX_W), tile(FOX_HEADS), tile(MIX_W), tile(MIX_W), tile(MIX_W),
                   tile(MEM_W), tile(FOX_HEADS),
                   pl.BlockSpec((1, 1, FOX_HEADS, nb * tt), lambda i, j: (i, j, 0, 0))],
        out_shape=[act(MIX_W, F32), act(MIX_W, F32), act(FOX_HEADS, F32), act(MIX_W, BF16), act(MIX_W, BF16),
                   act(MIX_W, BF16), act(MEM_W, BF16), act(FOX_HEADS, F32),
                   jax.ShapeDtypeStruct((b // nb, t // tt, FOX_HEADS, nb * tt), F32)],
        scratch_shapes=[pltpu.VMEM((1, V7X_LANES), F32), pltpu.VMEM((FOX_HEADS, 1), F32)],
        compiler_params=pltpu.CompilerParams(dimension_semantics=("arbitrary", "arbitrary"),
                                             vmem_limit_bytes=VMEM_LIMIT),
        name="layer1_front",
    )(x, mem_k, mem_v, row(g_kv), row(g_pre), w_k, w_v, w_f, w_ft, b_f, b_ft, w_in)


def _fox_tile(qm, cq, k_pair, v_pair, ck, carry, mask):
    m, l, acc = carry
    s = _dot_nt(qm, k_pair) + cq - ck
    if mask is not None:
        s = jnp.where(mask, s, NEG_BIG)
    m_new = jnp.maximum(m, jnp.max(s, axis=-1, keepdims=True))
    alpha = jnp.exp(m - m_new)
    p = jnp.exp(s - m_new)
    l = alpha * l + jnp.sum(p, axis=-1, keepdims=True)
    acc = alpha * acc + _dot(p.astype(BF16), v_pair)
    return m_new, l, acc


def _fox_prompt_kernel(q_ref, k_ref, v_ref, c_ref, ct_ref, o_ref, *, tq):
    i = pl.program_id(1)
    lane = lax.broadcasted_iota(jnp.int32, (1, PAIR_W), 1)
    causal = (lax.broadcasted_iota(jnp.int32, (tq, tq), 1) <= lax.broadcasted_iota(jnp.int32, (tq, tq), 0))
    for pair in range(FOX_HEADS // 2):
        cols = slice(pair * PAIR_W, (pair + 1) * PAIR_W)
        q_pair = q_ref[0, :, cols]
        outs = []
        for hh in range(2):
            h = 2 * pair + hh
            in_head = (lane >= hh * FOX_HD) & (lane < (hh + 1) * FOX_HD)
            qm = jnp.where(in_head, q_pair, jnp.zeros_like(q_pair))
            cq = c_ref[0, :, h:h + 1]

            def tile(j, carry, mask):
                rows = pl.ds(pl.multiple_of(j * tq, tq), tq)
                return _fox_tile(qm, cq, k_ref[0, rows, cols], v_ref[0, rows, cols],
                                 ct_ref[0, j, h:h + 1, :], carry, mask)

            init = (jnp.full((tq, 1), NEG_BIG, F32), jnp.zeros((tq, 1), F32), jnp.zeros((tq, PAIR_W), F32))
            carry = lax.fori_loop(0, i, lambda j, c: tile(j, c, None), init)
            _, l, acc = tile(i, carry, causal)
            outs.append(acc / l)
        o_ref[0, :, cols] = jnp.where(lane < FOX_HD, outs[0], outs[1]).astype(BF16)


def _fox_prompt(q, kb, vb, c, ct, *, tq):
    b, t, _ = q.shape
    return pl.pallas_call(
        functools.partial(_fox_prompt_kernel, tq=tq),
        grid=(b, t // tq),
        in_specs=[pl.BlockSpec((1, tq, MIX_W), lambda i, j: (i, j, 0)),
                  pl.BlockSpec((1, t, MIX_W), lambda i, j: (i, 0, 0)),
                  pl.BlockSpec((1, t, MIX_W), lambda i, j: (i, 0, 0)),
                  pl.BlockSpec((1, tq, FOX_HEADS), lambda i, j: (i, j, 0)),
                  pl.BlockSpec((1, t // tq, FOX_HEADS, tq), lambda i, j: (i, 0, 0, 0))],
        out_specs=pl.BlockSpec((1, tq, MIX_W), lambda i, j: (i, j, 0)),
        out_shape=jax.ShapeDtypeStruct((b, t, MIX_W), BF16),
        compiler_params=pltpu.CompilerParams(dimension_semantics=("arbitrary", "arbitrary"),
                                             vmem_limit_bytes=VMEM_LIMIT),
        name="fox_prompt",
    )(q, kb, vb, c, ct)


def _fox_sample_kernel(q_ref, kc_ref, vc_ref, kn_ref, vn_ref, lfc_t_ref, c_ref, ct_ref, o_ref):
    tn = q_ref.shape[1]
    p = lfc_t_ref.shape[2]
    cc = _cumsum(lfc_t_ref[0], 1, p)
    ck_cache = cc - cc[:, p - 1:]
    lane = lax.broadcasted_iota(jnp.int32, (1, PAIR_W), 1)
    causal = (lax.broadcasted_iota(jnp.int32, (tn, V7X_LANES), 1)
              <= lax.broadcasted_iota(jnp.int32, (tn, V7X_LANES), 0))
    pad = jnp.zeros((V7X_LANES - tn, PAIR_W), BF16)
    for pair in range(FOX_HEADS // 2):
        cols = slice(pair * PAIR_W, (pair + 1) * PAIR_W)
        q_pair = q_ref[0, :, cols]
        kc = kc_ref[0, :, cols].astype(BF16)
        vc = vc_ref[0, :, cols].astype(BF16)
        kn = jnp.concatenate([kn_ref[0, :, cols], pad], axis=0)
        vn = jnp.concatenate([vn_ref[0, :, cols], pad], axis=0)
        outs = []
        for hh in range(2):
            h = 2 * pair + hh
            in_head = (lane >= hh * FOX_HD) & (lane < (hh + 1) * FOX_HD)
            qm = jnp.where(in_head, q_pair, jnp.zeros_like(q_pair))
            cq = c_ref[0, :, h:h + 1]
            init = (jnp.full((tn, 1), NEG_BIG, F32), jnp.zeros((tn, 1), F32), jnp.zeros((tn, PAIR_W), F32))
            carry = _fox_tile(qm, cq, kc, vc, ck_cache[h:h + 1, :], init, None)
            _, l, acc = _fox_tile(qm, cq, kn, vn, ct_ref[0, h:h + 1, :], carry, causal)
            outs.append(acc / l)
        o_ref[0, :, cols] = jnp.where(lane < FOX_HD, outs[0], outs[1]).astype(BF16)


def _fox_sample(q, cache_k, cache_v, kb, vb, cache_logf_t, c, ct):
    b, tn, _ = q.shape
    p = cache_k.shape[1]
    new = lambda w: pl.BlockSpec((1, tn, w), lambda i: (i, 0, 0))
    old = pl.BlockSpec((1, p, MIX_W), lambda i: (i, 0, 0))
    return pl.pallas_call(
        _fox_sample_kernel,
        grid=(b,),
        in_specs=[new(MIX_W), old, old, new(MIX_W), new(MIX_W),
                  pl.BlockSpec((1, FOX_HEADS, p), lambda i: (i, 0, 0)),
                  new(FOX_HEADS), pl.BlockSpec((1, FOX_HEADS, V7X_LANES), lambda i: (i, 0, 0))],
        out_specs=new(MIX_W),
        out_shape=jax.ShapeDtypeStruct((b, tn, MIX_W), BF16),
        compiler_params=pltpu.CompilerParams(dimension_semantics=("arbitrary",), vmem_limit_bytes=VMEM_LIMIT),
        name="fox_sample",
    )(q, cache_k, cache_v, kb, vb, cache_logf_t, c, ct)


def _layer1_back_kernel(x_ref, mix_ref, mem_ref, g_post, g_mpre, g_mpost, w_out_mix, w_out_mem, w_up, w_down,
                        y_ref, *, nb, tt):
    m = nb * tt
    x = x_ref[...].reshape(m, D_MODEL)
    o = (_dot(mix_ref[...].reshape(m, MIX_W), w_out_mix[...])
         + _dot(mem_ref[...].reshape(m, MEM_W), w_out_mem[...]))
    x1 = x + _rms(o, g_post[...])
    y_ref[...] = _mlp(x1, g_mpre[...], g_mpost[...], w_up, w_down).reshape(nb, tt, D_MODEL)


def _layer1_back(x, mix, mem, g_post, g_mpre, g_mpost, w_out, w_up, w_down, *, nb, tt):
    b, t, _ = x.shape
    row = lambda a: a.reshape(1, -1)
    tile = lambda w: pl.BlockSpec((nb, tt, w), lambda i, j: (i, j, 0))
    return pl.pallas_call(
        functools.partial(_layer1_back_kernel, nb=nb, tt=tt),
        grid=(b // nb, t // tt),
        in_specs=[tile(D_MODEL), tile(MIX_W), tile(MEM_W),
                  _const_spec((1, D_MODEL)), _const_spec((1, D_MODEL)), _const_spec((1, D_MODEL)),
                  _const_spec((MIX_W, D_MODEL)), _const_spec((MEM_W, D_MODEL)),
                  _const_spec((D_MODEL, D_FF)), _const_spec((D_FF, D_MODEL))],
        out_specs=tile(D_MODEL),
        out_shape=jax.ShapeDtypeStruct((b, t, D_MODEL), F32),
        compiler_params=pltpu.CompilerParams(dimension_semantics=("arbitrary", "arbitrary"),
                                             vmem_limit_bytes=VMEM_LIMIT),
        name="layer1_back",
    )(x, mix, mem, row(g_post), row(g_mpre), row(g_mpost), w_out[:MIX_W], w_out[MIX_W:], w_up, w_down)


def _token_tile(b, t):
    if t >= 512:
        return 1, 512
    return b, t


def _trunk(x, pool_prev, cache, mem_k, mem_v, p):
    b, t, _ = x.shape
    nb, tt = _token_tile(b, t)
    pos_base = 0 if cache is None else POOL_BUF
    y0, pool_state = _layer0(x, pool_prev, mem_k[0], mem_v[0], p["g_mix_pre"][0], p["g_mix_post"][0],
                             p["g_mlp_pre"][0], p["g_mlp_post"][0], p["w_in"][0], p["w_out"][0], p["w_pool"][0],
                             p["pool_scale"][0], p["w_up"][0], p["w_down"][0], nb=nb, tt=tt, pos_base=pos_base)
    k, v, logf, kb, vb, q, mem, c, ct = _layer1_front(
        y0, mem_k[1], mem_v[1], p["g_kv"], p["g_mix_pre"][1], p["w_k"], p["w_v"], p["w_f"], p["w_ft"],
        p["b_f"], p["b_ft"], p["w_in"][1], nb=nb, tt=tt)
    if cache is None:
        mix = _fox_prompt(q, kb, vb, c, ct, tq=512)
    else:
        cache_k, cache_v, cache_logf = cache
        past = cache_k.shape[1]
        ct_new = jnp.swapaxes(ct.reshape(FOX_HEADS, b, t), 0, 1)
        ct_new = jnp.pad(ct_new, ((0, 0), (0, 0), (0, V7X_LANES - t)))
        mix = _fox_sample(q, cache_k.reshape(b, past, MIX_W), cache_v.reshape(b, past, MIX_W), kb, vb,
                          jnp.swapaxes(cache_logf, 1, 2), c, ct_new)
    y = _layer1_back(y0, mix, mem, p["g_mix_post"][1], p["g_mlp_pre"][1], p["g_mlp_post"][1],
                     p["w_out"][1], p["w_up"][1], p["w_down"][1], nb=nb, tt=tt)
    return (y, pool_state, k.reshape(b, t, FOX_HEADS, FOX_HD), v.reshape(b, t, FOX_HEADS, FOX_HD), logf)


def kernel(x_prompt, x_sample, cache_pool, cache_k, cache_v, cache_logf, cache_mem_k, cache_mem_v, mem_prompt,
           g_mix_pre, g_mix_post, g_mlp_pre, g_mlp_post, w_in, w_out, w_pool, pool_scale, g_kv, w_kvf, b_f,
           g_mem, w_mem_kv, w_up, w_down):
    depth = w_in.shape[0]
    assert depth == 2 and w_pool.shape[0] == 1, "one pooling layer followed by one forgetting-attention layer"
    bp = x_prompt.shape[0]
    bs = x_sample.shape[0]
    n_mem = mem_prompt.shape[1]
    w_f = w_kvf[:, 2 * MIX_W:]
    p = dict(
        g_mix_pre=g_mix_pre, g_mix_post=g_mix_post, g_mlp_pre=g_mlp_pre, g_mlp_post=g_mlp_post, g_kv=g_kv,
        w_in=w_in.astype(BF16), w_out=w_out.astype(BF16), w_pool=w_pool.astype(BF16), pool_scale=pool_scale,
        w_up=w_up.astype(BF16), w_down=w_down.astype(BF16),
        w_k=w_kvf[:, :MIX_W].astype(BF16), w_v=w_kvf[:, MIX_W:2 * MIX_W].astype(BF16),
        w_f=jnp.pad(w_f, ((0, 0), (0, V7X_LANES - FOX_HEADS))).astype(BF16), w_ft=w_f.T.astype(BF16),
        b_f=jnp.pad(b_f, (0, V7X_LANES - FOX_HEADS)).reshape(1, V7X_LANES), b_ft=b_f.reshape(FOX_HEADS, 1),
    )

    mk, mv = _mem_project(mem_prompt, g_mem, w_mem_kv.astype(BF16))
    mem_k_prompt = mk.reshape(depth, bp, n_mem, MEM_HEADS, MEM_HD)
    mem_v_prompt = mv.reshape(depth, bp, n_mem, MEM_HEADS, MEM_HD)

    y_p, pool_p, k_p, v_p, logf_p = _trunk(x_prompt, jnp.zeros((bp, HALO, MIX_W), F32), None, mk, mv, p)

    pool_prev = jnp.pad(cache_pool[0], ((0, 0), (HALO - POOL_BUF, 0), (0, 0)))
    y_s, pool_s, k_s, v_s, logf_s = _trunk(
        x_sample, pool_prev, (cache_k, cache_v, cache_logf),
        cache_mem_k.reshape(depth, bs, n_mem, MEM_W), cache_mem_v.reshape(depth, bs, n_mem, MEM_W), p)

    return (y_p, y_s, pool_p, pool_s, k_p, v_p, logf_p, k_s, v_s, logf_s, mem_k_prompt, mem_v_prompt)
```

```python
import functools

import jax
import jax.numpy as jnp
from jax import lax
from jax.experimental import pallas as pl
from jax.experimental.pallas import tpu as pltpu

F32 = jnp.float32
BF16 = jnp.bfloat16

D_MODEL = 1024
MIX_W = D_MODEL // 2
MEM_W = D_MODEL - MIX_W
POOL_WINDOWS = (2, 4, 8, 16)
POOL_GROUP = MIX_W // len(POOL_WINDOWS)
POOL_BUF = max(POOL_WINDOWS) - 1
FOX_HEADS = 8
FOX_HD = MIX_W // FOX_HEADS
MEM_HEADS = 4
MEM_HD = MEM_W // MEM_HEADS
D_FF = 4 * D_MODEL
EPS = 1e-6

V7X_LANES = 128
V7X_SUBLANES_F32 = 8
V7X_VMEM_BYTES = 64 * 1024 * 1024

HALO = 2 * V7X_SUBLANES_F32
FF_CHUNK = 1024
PAIR_W = 2 * FOX_HD
NEG_BIG = -1e30
LOG2E = 1.4426950408889634
FOLD_SRC = FOX_HEADS
FOLD_W = 6
FOX_GROUP = 8
assert FOLD_SRC + FOLD_W * FOX_HEADS <= V7X_LANES
VMEM_LIMIT = V7X_VMEM_BYTES - 8 * 1024 * 1024

assert HALO >= POOL_BUF and PAIR_W == V7X_LANES and POOL_GROUP == V7X_LANES and MEM_HD == V7X_LANES


def _const_spec(shape):
    zeros = (0,) * len(shape)
    return pl.BlockSpec(shape, lambda *_: zeros, pipeline_mode=pl.Buffered(1))


def _rms(x, g):
    return x * lax.rsqrt(jnp.mean(x * x, axis=-1, keepdims=True) + EPS) * g


def _dot(a, b):
    return jnp.dot(a, b, preferred_element_type=F32)


def _dot_nt(a, b):
    return lax.dot_general(a, b, (((1,), (1,)), ((), ())), preferred_element_type=F32)


def _log_sigmoid(x):
    return jnp.minimum(x, 0.0) - jnp.log1p(jnp.exp(-jnp.abs(x)))


def _cumsum(x, axis, seg):
    assert seg & (seg - 1) == 0 and x.shape[axis] % seg == 0
    idx = lax.broadcasted_iota(jnp.int32, x.shape, axis) & (seg - 1)
    k = 1
    while k < seg:
        x = x + jnp.where(idx >= k, pltpu.roll(x, k, axis=axis), 0.0)
        k *= 2
    return x


def _mem_attention(q, mk, mv):
    outs = []
    for h in range(MEM_HEADS):
        sl = slice(h * MEM_HD, (h + 1) * MEM_HD)
        s = _dot_nt(q[:, sl].astype(BF16), mk[:, sl]) * (MEM_HD ** -0.5)
        m = jnp.max(s, axis=-1, keepdims=True)
        p = jnp.exp(s - m)
        l = jnp.sum(p, axis=-1, keepdims=True)
        outs.append(_dot(p.astype(BF16), mv[:, sl]) / l)
    return jnp.concatenate(outs, axis=-1)


def _pool_mixer(u, halo, pos0, w_pool_ref, scale):
    tt = u.shape[0]
    ext = jnp.concatenate([halo, u], axis=0)
    pos = pos0 + lax.broadcasted_iota(jnp.int32, (tt, 1), 0)
    outs = []
    for g, w in enumerate(POOL_WINDOWS):
        sl = slice(g * POOL_GROUP, (g + 1) * POOL_GROUP)
        s = ext[:, sl]
        k = 1
        while k < w:
            s = s + pltpu.roll(s, k, axis=0)
            k *= 2
        cnt = jnp.minimum(pos + 1, w).astype(F32)
        pooled = s[HALO:] / cnt - u[:, sl]
        outs.append(_dot(pooled.astype(BF16), w_pool_ref[g]))
    return jnp.concatenate(outs, axis=-1) * scale


def _mlp(x1, g_pre, g_post, w_up_ref, w_down_ref):
    hm = _rms(x1, g_pre).astype(BF16)
    acc = jnp.zeros(x1.shape, F32)
    for c in range(D_FF // FF_CHUNK):
        cs = slice(c * FF_CHUNK, (c + 1) * FF_CHUNK)
        a = jnp.square(jnp.maximum(_dot(hm, w_up_ref[:, cs]), 0.0)).astype(BF16)
        acc = acc + _dot(a, w_down_ref[cs, :])
    return x1 + _rms(acc, g_post)


def _mem_project_kernel(mem_ref, g_ref, w_ref, mk_ref, mv_ref):
    kv = _dot(_rms(mem_ref[0], g_ref[0]).astype(BF16), w_ref[0])
    mk_ref[0, 0] = kv[:, :MEM_W]
    mv_ref[0, 0] = kv[:, MEM_W:]


def _mem_project(mem, g_mem, w_mem_kv):
    b, n, _ = mem.shape
    depth = w_mem_kv.shape[0]
    out = jax.ShapeDtypeStruct((depth, b, n, MEM_W), F32)
    return pl.pallas_call(
        _mem_project_kernel,
        grid=(depth, b),
        in_specs=[pl.BlockSpec((1, n, D_MODEL), lambda l, i: (i, 0, 0)),
                  pl.BlockSpec((1, 1, D_MODEL), lambda l, i: (l, 0, 0)),
                  pl.BlockSpec((1, D_MODEL, 2 * MEM_W), lambda l, i: (l, 0, 0))],
        out_specs=[pl.BlockSpec((1, 1, n, MEM_W), lambda l, i: (l, i, 0, 0))] * 2,
        out_shape=[out, out],
        compiler_params=pltpu.CompilerParams(dimension_semantics=("arbitrary", "arbitrary")),
        name="mem_project",
    )(mem, g_mem.reshape(depth, 1, D_MODEL), w_mem_kv)


def _layer0_kernel(x_ref, prev_ref, mk_ref, mv_ref, g_pre, g_post, g_mpre, g_mpost,
                   w_in, w_out, w_pool, pscale, w_up, w_down,
                   y_ref, state_ref, halo_ref, *, nb, tt, pos_base):
    t = pl.program_id(1)

    @pl.when(t == 0)
    def _():
        halo_ref[...] = prev_ref[...]

    x = x_ref[...].reshape(nb * tt, D_MODEL)
    proj = _dot(_rms(x, g_pre[...]).astype(BF16), w_in[...])
    cats = []
    for b in range(nb):
        rows = slice(b * tt, (b + 1) * tt)
        u = proj[rows, :MIX_W]
        mix = _pool_mixer(u, halo_ref[b], pos_base + t * tt, w_pool, pscale[...])
        halo_ref[b] = u[tt - HALO:, :]

        @pl.when(t == pl.num_programs(1) - 1)
        def _():
            state_ref[0, b] = u[tt - POOL_BUF:, :]

        mem = _mem_attention(proj[rows, MIX_W:], mk_ref[b].astype(BF16), mv_ref[b].astype(BF16))
        cats.append(jnp.concatenate([mix, mem], axis=-1))
    cat = cats[0] if nb == 1 else jnp.concatenate(cats, axis=0)
    x1 = x + _rms(_dot(cat.astype(BF16), w_out[...]), g_post[...])
    y = _mlp(x1, g_mpre[...], g_mpost[...], w_up, w_down)
    y_ref[...] = y.reshape(nb, tt, D_MODEL)


def _layer0(x, prev, mem_k, mem_v, g_pre, g_post, g_mpre, g_mpost, w_in, w_out, w_pool, pscale,
            w_up, w_down, *, nb, tt, pos_base):
    b, t, _ = x.shape
    n_mem = mem_k.shape[1]
    grid = (b // nb, t // tt)
    row = lambda a: a.reshape(1, -1)
    tile = pl.BlockSpec((nb, tt, D_MODEL), lambda i, j: (i, j, 0))
    per_b = lambda shape: pl.BlockSpec((nb,) + shape, lambda i, j: (i, 0, 0))
    return pl.pallas_call(
        functools.partial(_layer0_kernel, nb=nb, tt=tt, pos_base=pos_base),
        grid=grid,
        in_specs=[tile, per_b((HALO, MIX_W)), per_b((n_mem, MEM_W)), per_b((n_mem, MEM_W)),
                  _const_spec((1, D_MODEL)), _const_spec((1, D_MODEL)),
                  _const_spec((1, D_MODEL)), _const_spec((1, D_MODEL)),
                  _const_spec((D_MODEL, D_MODEL)), _const_spec((D_MODEL, D_MODEL)),
                  _const_spec(w_pool.shape), _const_spec((1, MIX_W)),
                  _const_spec((D_MODEL, D_FF)), _const_spec((D_FF, D_MODEL))],
        out_specs=[tile, pl.BlockSpec((1, nb, POOL_BUF, MIX_W), lambda i, j: (0, i, 0, 0))],
        out_shape=[jax.ShapeDtypeStruct((b, t, D_MODEL), F32),
                   jax.ShapeDtypeStruct((1, b, POOL_BUF, MIX_W), F32)],
        scratch_shapes=[pltpu.VMEM((nb, HALO, MIX_W), F32)],
        compiler_params=pltpu.CompilerParams(dimension_semantics=("arbitrary", "arbitrary"),
                                             vmem_limit_bytes=VMEM_LIMIT),
        name="layer0",
    )(x, prev, mem_k, mem_v, row(g_pre), row(g_post), row(g_mpre), row(g_mpost),
      w_in, w_out, w_pool, row(pscale), w_up, w_down)


def _bf16_part(x):
    return x.astype(BF16).astype(F32)


def _fold_operands(q, k, v, c):
    cl = c * LOG2E
    hi = _bf16_part(cl)
    mid = _bf16_part(cl - hi)
    lo = _bf16_part(cl - hi - mid)
    lane = lax.broadcasted_iota(jnp.int32, (1, V7X_LANES), 1)
    piece = lax.rem(lane + (3 - FOLD_SRC % 3), 3)
    d = jnp.where(piece == 0, hi, jnp.where(piece == 1, mid, lo))
    qs, ks, vs = [], [], []
    for h in range(FOX_HEADS):
        own = (h % 2) * FOX_HD
        e0 = FOX_HD - own
        cols = slice((h // 2) * PAIR_W, (h // 2 + 1) * PAIR_W)
        dh = pltpu.roll(d, (e0 - (FOLD_SRC + FOLD_W * h)) % V7X_LANES, axis=1)
        in_head = (lane >= own) & (lane < own + FOX_HD)
        first = (lane >= e0) & (lane < e0 + 3)
        second = (lane >= e0 + 3) & (lane < e0 + 6)
        xq = jnp.where(first, dh, jnp.where(second, 1.0, 0.0))
        xk = jnp.where(first, 1.0, jnp.where(second, -dh, 0.0))
        xv = jnp.where(lane == e0, 1.0, 0.0)
        qs.append(jnp.where(in_head, q[:, cols] * (FOX_HD ** -0.5 * LOG2E), xq).astype(BF16))
        ks.append(jnp.where(in_head, k[:, cols], xk).astype(BF16))
        vs.append(jnp.where(in_head, v[:, cols], xv).astype(BF16))
    return jnp.concatenate(qs, axis=-1), jnp.concatenate(ks, axis=-1), jnp.concatenate(vs, axis=-1)


def _layer1_front_kernel(x_ref, mk_ref, mv_ref, g_kv, g_pre, w_k, w_v, w_f, w_ft, b_f, b_ft, w_in, *refs,
                         nb, tt, fold):
    if fold:
        k_ref, v_ref, logf_ref, mem_ref, qa_ref, ka_ref, va_ref, carry_ref = refs
    else:
        k_ref, v_ref, logf_ref, mem_ref, q_ref, kb_ref, vb_ref, c_ref, ct_ref = refs

    x = x_ref[...].reshape(nb * tt, D_MODEL)
    hk = _rms(x, g_kv[...]).astype(BF16)
    k = _dot(hk, w_k[...])
    v = _dot(hk, w_v[...])
    k_ref[...] = k.reshape(nb, tt, MIX_W)
    v_ref[...] = v.reshape(nb, tt, MIX_W)
    logf = _log_sigmoid(_dot(hk, w_f[...]) + b_f[...])
    proj = _dot(_rms(x, g_pre[...]).astype(BF16), w_in[...])
    q = proj[:, :MIX_W]
    c = _cumsum(logf, 0, tt)
    if fold:
        @pl.when(pl.program_id(1) == 0)
        def _():
            carry_ref[...] = jnp.zeros_like(carry_ref)

        c = c + carry_ref[...]
        carry_ref[...] = c[tt - 1:, :]
        qa, ka, va = _fold_operands(q, k, v, c)
        qa_ref[0] = qa
        ka_ref[0] = ka
        va_ref[0] = va
    else:
        q_ref[...] = (q * (FOX_HD ** -0.5)).astype(BF16).reshape(nb, tt, MIX_W)
        kb_ref[...] = k.astype(BF16).reshape(nb, tt, MIX_W)
        vb_ref[...] = v.astype(BF16).reshape(nb, tt, MIX_W)
        logf_t = _log_sigmoid(_dot_nt(w_ft[...], hk) + b_ft[...])
        ct_ref[0, 0] = _cumsum(logf_t, 1, tt)
    for b in range(nb):
        rows = slice(b * tt, (b + 1) * tt)
        logf_ref[b] = logf[rows, :FOX_HEADS]
        if not fold:
            c_ref[b] = c[rows, :FOX_HEADS]
        mem = _mem_attention(proj[rows, MIX_W:], mk_ref[b].astype(BF16), mv_ref[b].astype(BF16))
        mem_ref[b] = mem.astype(BF16)


def _layer1_front(x, mem_k, mem_v, g_kv, g_pre, w_k, w_v, w_f, w_ft, b_f, b_ft, w_in, *, nb, tt, fold):
    b, t, _ = x.shape
    assert (nb == 1) if fold else (t == tt)
    n_mem = mem_k.shape[1]
    row = lambda a: a.reshape(1, -1)
    tile = lambda w: pl.BlockSpec((nb, tt, w), lambda i, j: (i, j, 0))
    per_b = lambda shape: pl.BlockSpec((nb,) + shape, lambda i, j: (i, 0, 0))
    act = lambda w, dt: jax.ShapeDtypeStruct((b, t, w), dt)
    out_specs = [tile(MIX_W), tile(MIX_W), tile(FOX_HEADS), tile(MEM_W)]
    out_shape = [act(MIX_W, F32), act(MIX_W, F32), act(FOX_HEADS, F32), act(MEM_W, BF16)]
    if fold:
        out_specs += [tile(FOX_HEADS * PAIR_W)] * 3
        out_shape += [act(FOX_HEADS * PAIR_W, BF16)] * 3
        scratch = [pltpu.VMEM((1, V7X_LANES), F32)]
    else:
        out_specs += [tile(MIX_W)] * 3 + [tile(FOX_HEADS),
                                          pl.BlockSpec((1, 1, FOX_HEADS, nb * tt), lambda i, j: (i, j, 0, 0))]
        out_shape += [act(MIX_W, BF16)] * 3 + [act(FOX_HEADS, F32),
                                               jax.ShapeDtypeStruct((b // nb, t // tt, FOX_HEADS, nb * tt), F32)]
        scratch = []
    return pl.pallas_call(
        functools.partial(_layer1_front_kernel, nb=nb, tt=tt, fold=fold),
        grid=(b // nb, t // tt),
        in_specs=[tile(D_MODEL), per_b((n_mem, MEM_W)), per_b((n_mem, MEM_W)),
                  _const_spec((1, D_MODEL)), _const_spec((1, D_MODEL)),
                  _const_spec((D_MODEL, MIX_W)), _const_spec((D_MODEL, MIX_W)),
                  _const_spec((D_MODEL, V7X_LANES)), _const_spec((FOX_HEADS, D_MODEL)),
                  _const_spec((1, V7X_LANES)), _const_spec((FOX_HEADS, 1)),
                  _const_spec((D_MODEL, D_MODEL))],
        out_specs=out_specs,
        out_shape=out_shape,
        scratch_shapes=scratch,
        compiler_params=pltpu.CompilerParams(dimension_semantics=("arbitrary", "arbitrary"),
                                             vmem_limit_bytes=VMEM_LIMIT),
        name="layer1_front",
    )(x, mem_k, mem_v, row(g_kv), row(g_pre), w_k, w_v, w_f, w_ft, b_f, b_ft, w_in)


def _fox_tile(qm, cq, k_pair, v_pair, ck, carry, mask):
    m, l, acc = carry
    s = _dot_nt(qm, k_pair) + cq - ck
    if mask is not None:
        s = jnp.where(mask, s, NEG_BIG)
    m_new = jnp.maximum(m, jnp.max(s, axis=-1, keepdims=True))
    alpha = jnp.exp(m - m_new)
    p = jnp.exp(s - m_new)
    l = alpha * l + jnp.sum(p, axis=-1, keepdims=True)
    acc = alpha * acc + _dot(p.astype(BF16), v_pair)
    return m_new, l, acc


def _fox_prompt_kernel(q_ref, k_ref, v_ref, o_ref, *, tq):
    i = pl.program_id(1)
    lane = lax.broadcasted_iota(jnp.int32, (1, PAIR_W), 1)
    causal = (lax.broadcasted_iota(jnp.int32, (tq, tq), 1) <= lax.broadcasted_iota(jnp.int32, (tq, tq), 0))
    for g in range(FOX_HEADS // FOX_GROUP):
        heads = range(g * FOX_GROUP, (g + 1) * FOX_GROUP)
        col = [slice(h * PAIR_W, (h + 1) * PAIR_W) for h in heads]
        qa = [q_ref[0, :, c] for c in col]

        def tile(j, carry, mask):
            rows = pl.ds(pl.multiple_of(j * tq, tq), tq)
            out = []
            for n in range(FOX_GROUP):
                m, acc = carry[n]
                s = _dot_nt(qa[n], k_ref[0, rows, col[n]])
                if mask is not None:
                    s = jnp.where(mask, s, NEG_BIG)
                m_new = jnp.maximum(m, jnp.max(s, axis=-1, keepdims=True))
                p = jnp.exp2(s - m_new).astype(BF16)
                out.append((m_new, jnp.exp2(m - m_new) * acc + _dot(p, v_ref[0, rows, col[n]])))
            return tuple(out)

        init = ((jnp.full((tq, 1), NEG_BIG, F32), jnp.zeros((tq, PAIR_W), F32)),) * FOX_GROUP
        carry = lax.fori_loop(0, i, lambda j, c: tile(j, c, None), init)
        carry = tile(i, carry, causal)
        outs = []
        for h, (_, acc) in zip(heads, carry):
            denom = FOX_HD * (1 - h % 2)
            outs.append(acc / acc[:, denom:denom + 1])
        for n in range(0, FOX_GROUP, 2):
            pair = (g * FOX_GROUP + n) // 2
            o_ref[0, :, pair * PAIR_W:(pair + 1) * PAIR_W] = (
                jnp.where(lane < FOX_HD, outs[n], outs[n + 1]).astype(BF16))


def _fox_prompt(qa, ka, va, *, tq):
    b, t, w = qa.shape
    return pl.pallas_call(
        functools.partial(_fox_prompt_kernel, tq=tq),
        grid=(b, t // tq),
        in_specs=[pl.BlockSpec((1, tq, w), lambda i, j: (i, j, 0)),
                  pl.BlockSpec((1, t, w), lambda i, j: (i, 0, 0), pipeline_mode=pl.Buffered(1)),
                  pl.BlockSpec((1, t, w), lambda i, j: (i, 0, 0), pipeline_mode=pl.Buffered(1))],
        out_specs=pl.BlockSpec((1, tq, MIX_W), lambda i, j: (i, j, 0)),
        out_shape=jax.ShapeDtypeStruct((b, t, MIX_W), BF16),
        compiler_params=pltpu.CompilerParams(dimension_semantics=("arbitrary", "arbitrary"),
                                             vmem_limit_bytes=VMEM_LIMIT),
        name="fox_prompt",
    )(qa, ka, va)


def _fox_sample_kernel(q_ref, kc_ref, vc_ref, kn_ref, vn_ref, lfc_t_ref, c_ref, ct_ref, o_ref):
    tn = q_ref.shape[1]
    p = lfc_t_ref.shape[2]
    cc = _cumsum(lfc_t_ref[0], 1, p)
    ck_cache = cc - cc[:, p - 1:]
    lane = lax.broadcasted_iota(jnp.int32, (1, PAIR_W), 1)
    causal = (lax.broadcasted_iota(jnp.int32, (tn, V7X_LANES), 1)
              <= lax.broadcasted_iota(jnp.int32, (tn, V7X_LANES), 0))
    pad = jnp.zeros((V7X_LANES - tn, PAIR_W), BF16)
    for pair in range(FOX_HEADS // 2):
        cols = slice(pair * PAIR_W, (pair + 1) * PAIR_W)
        q_pair = q_ref[0, :, cols]
        kc = kc_ref[0, :, cols].astype(BF16)
        vc = vc_ref[0, :, cols].astype(BF16)
        kn = jnp.concatenate([kn_ref[0, :, cols], pad], axis=0)
        vn = jnp.concatenate([vn_ref[0, :, cols], pad], axis=0)
        outs = []
        for hh in range(2):
            h = 2 * pair + hh
            in_head = (lane >= hh * FOX_HD) & (lane < (hh + 1) * FOX_HD)
            qm = jnp.where(in_head, q_pair, jnp.zeros_like(q_pair))
            cq = c_ref[0, :, h:h + 1]
            init = (jnp.full((tn, 1), NEG_BIG, F32), jnp.zeros((tn, 1), F32), jnp.zeros((tn, PAIR_W), F32))
            carry = _fox_tile(qm, cq, kc, vc, ck_cache[h:h + 1, :], init, None)
            _, l, acc = _fox_tile(qm, cq, kn, vn, ct_ref[0, h:h + 1, :], carry, causal)
            outs.append(acc / l)
        o_ref[0, :, cols] = jnp.where(lane < FOX_HD, outs[0], outs[1]).astype(BF16)


def _fox_sample(q, cache_k, cache_v, kb, vb, cache_logf_t, c, ct):
    b, tn, _ = q.shape
    p = cache_k.shape[1]
    new = lambda w: pl.BlockSpec((1, tn, w), lambda i: (i, 0, 0))
    old = pl.BlockSpec((1, p, MIX_W), lambda i: (i, 0, 0))
    return pl.pallas_call(
        _fox_sample_kernel,
        grid=(b,),
        in_specs=[new(MIX_W), old, old, new(MIX_W), new(MIX_W),
                  pl.BlockSpec((1, FOX_HEADS, p), lambda i: (i, 0, 0)),
                  new(FOX_HEADS), pl.BlockSpec((1, FOX_HEADS, V7X_LANES), lambda i: (i, 0, 0))],
        out_specs=new(MIX_W),
        out_shape=jax.ShapeDtypeStruct((b, tn, MIX_W), BF16),
        compiler_params=pltpu.CompilerParams(dimension_semantics=("arbitrary",), vmem_limit_bytes=VMEM_LIMIT),
        name="fox_sample",
    )(q, cache_k, cache_v, kb, vb, cache_logf_t, c, ct)


def _layer1_back_kernel(x_ref, mix_ref, mem_ref, g_post, g_mpre, g_mpost, w_out_mix, w_out_mem, w_up, w_down,
                        y_ref, *, nb, tt):
    m = nb * tt
    x = x_ref[...].reshape(m, D_MODEL)
    o = (_dot(mix_ref[...].reshape(m, MIX_W), w_out_mix[...])
         + _dot(mem_ref[...].reshape(m, MEM_W), w_out_mem[...]))
    x1 = x + _rms(o, g_post[...])
    y_ref[...] = _mlp(x1, g_mpre[...], g_mpost[...], w_up, w_down).reshape(nb, tt, D_MODEL)


def _layer1_back(x, mix, mem, g_post, g_mpre, g_mpost, w_out, w_up, w_down, *, nb, tt):
    b, t, _ = x.shape
    row = lambda a: a.reshape(1, -1)
    tile = lambda w: pl.BlockSpec((nb, tt, w), lambda i, j: (i, j, 0))
    return pl.pallas_call(
        functools.partial(_layer1_back_kernel, nb=nb, tt=tt),
        grid=(b // nb, t // tt),
        in_specs=[tile(D_MODEL), tile(MIX_W), tile(MEM_W),
                  _const_spec((1, D_MODEL)), _const_spec((1, D_MODEL)), _const_spec((1, D_MODEL)),
                  _const_spec((MIX_W, D_MODEL)), _const_spec((MEM_W, D_MODEL)),
                  _const_spec((D_MODEL, D_FF)), _const_spec((D_FF, D_MODEL))],
        out_specs=tile(D_MODEL),
        out_shape=jax.ShapeDtypeStruct((b, t, D_MODEL), F32),
        compiler_params=pltpu.CompilerParams(dimension_semantics=("arbitrary", "arbitrary"),
                                             vmem_limit_bytes=VMEM_LIMIT),
        name="layer1_back",
    )(x, mix, mem, row(g_post), row(g_mpre), row(g_mpost), w_out[:MIX_W], w_out[MIX_W:], w_up, w_down)


def _token_tile(b, t):
    if t >= 512:
        return 1, 512
    return b, t


def _trunk(x, pool_prev, cache, mem_k, mem_v, p):
    b, t, _ = x.shape
    nb, tt = _token_tile(b, t)
    pos_base = 0 if cache is None else POOL_BUF
    y0, pool_state = _layer0(x, pool_prev, mem_k[0], mem_v[0], p["g_mix_pre"][0], p["g_mix_post"][0],
                             p["g_mlp_pre"][0], p["g_mlp_post"][0], p["w_in"][0], p["w_out"][0], p["w_pool"][0],
                             p["pool_scale"][0], p["w_up"][0], p["w_down"][0], nb=nb, tt=tt, pos_base=pos_base)
    k, v, logf, mem, *att = _layer1_front(
        y0, mem_k[1], mem_v[1], p["g_kv"], p["g_mix_pre"][1], p["w_k"], p["w_v"], p["w_f"], p["w_ft"],
        p["b_f"], p["b_ft"], p["w_in"][1], nb=nb, tt=tt, fold=cache is None)
    if cache is None:
        mix = _fox_prompt(*att, tq=tt)
    else:
        q, kb, vb, c, ct = att
        cache_k, cache_v, cache_logf = cache
        past = cache_k.shape[1]
        ct_new = jnp.swapaxes(ct.reshape(FOX_HEADS, b, t), 0, 1)
        ct_new = jnp.pad(ct_new, ((0, 0), (0, 0), (0, V7X_LANES - t)))
        mix = _fox_sample(q, cache_k.reshape(b, past, MIX_W), cache_v.reshape(b, past, MIX_W), kb, vb,
                          jnp.swapaxes(cache_logf, 1, 2), c, ct_new)
    y = _layer1_back(y0, mix, mem, p["g_mix_post"][1], p["g_mlp_pre"][1], p["g_mlp_post"][1],
                     p["w_out"][1], p["w_up"][1], p["w_down"][1], nb=nb, tt=tt)
    return (y, pool_state, k.reshape(b, t, FOX_HEADS, FOX_HD), v.reshape(b, t, FOX_HEADS, FOX_HD), logf)


def kernel(x_prompt, x_sample, cache_pool, cache_k, cache_v, cache_logf, cache_mem_k, cache_mem_v, mem_prompt,
           g_mix_pre, g_mix_post, g_mlp_pre, g_mlp_post, w_in, w_out, w_pool, pool_scale, g_kv, w_kvf, b_f,
           g_mem, w_mem_kv, w_up, w_down):
    depth = w_in.shape[0]
    assert depth == 2 and w_pool.shape[0] == 1, "one pooling layer followed by one forgetting-attention layer"
    bp = x_prompt.shape[0]
    bs = x_sample.shape[0]
    n_mem = mem_prompt.shape[1]
    w_f = w_kvf[:, 2 * MIX_W:]
    fold_cols = lambda a: jnp.pad(jnp.concatenate([a, jnp.repeat(a, FOLD_W, axis=-1)], axis=-1),
                                  [(0, 0)] * (a.ndim - 1) + [(0, V7X_LANES - FOLD_SRC - FOLD_W * FOX_HEADS)])
    p = dict(
        g_mix_pre=g_mix_pre, g_mix_post=g_mix_post, g_mlp_pre=g_mlp_pre, g_mlp_post=g_mlp_post, g_kv=g_kv,
        w_in=w_in.astype(BF16), w_out=w_out.astype(BF16), w_pool=w_pool.astype(BF16), pool_scale=pool_scale,
        w_up=w_up.astype(BF16), w_down=w_down.astype(BF16),
        w_k=w_kvf[:, :MIX_W].astype(BF16), w_v=w_kvf[:, MIX_W:2 * MIX_W].astype(BF16),
        w_f=fold_cols(w_f).astype(BF16), w_ft=w_f.T.astype(BF16),
        b_f=fold_cols(b_f).reshape(1, V7X_LANES), b_ft=b_f.reshape(FOX_HEADS, 1),
    )

    mk, mv = _mem_project(mem_prompt, g_mem, w_mem_kv.astype(BF16))
    mem_k_prompt = mk.reshape(depth, bp, n_mem, MEM_HEADS, MEM_HD)
    mem_v_prompt = mv.reshape(depth, bp, n_mem, MEM_HEADS, MEM_HD)

    y_p, pool_p, k_p, v_p, logf_p = _trunk(x_prompt, jnp.zeros((bp, HALO, MIX_W), F32), None, mk, mv, p)

    pool_prev = jnp.pad(cache_pool[0], ((0, 0), (HALO - POOL_BUF, 0), (0, 0)))
    y_s, pool_s, k_s, v_s, logf_s = _trunk(
        x_sample, pool_prev, (cache_k, cache_v, cache_logf),
        cache_mem_k.reshape(depth, bs, n_mem, MEM_W), cache_mem_v.reshape(depth, bs, n_mem, MEM_W), p)

    return (y_p, y_s, pool_p, pool_s, k_p, v_p, logf_p, k_s, v_s, logf_s, mem_k_prompt, mem_v_prompt)
```

```python
import functools

import jax
import jax.numpy as jnp
from jax import lax
from jax.experimental import pallas as pl
from jax.experimental.pallas import tpu as pltpu

F32 = jnp.float32
BF16 = jnp.bfloat16

D_MODEL = 1024
MIX_W = D_MODEL // 2
MEM_W = D_MODEL - MIX_W
POOL_WINDOWS = (2, 4, 8, 16)
POOL_GROUP = MIX_W // len(POOL_WINDOWS)
POOL_BUF = max(POOL_WINDOWS) - 1
FOX_HEADS = 8
FOX_HD = MIX_W // FOX_HEADS
MEM_HEADS = 4
MEM_HD = MEM_W // MEM_HEADS
D_FF = 4 * D_MODEL
EPS = 1e-6

V7X_LANES = 128
V7X_SUBLANES_F32 = 8
V7X_VMEM_BYTES = 64 * 1024 * 1024

HALO = 2 * V7X_SUBLANES_F32
FF_CHUNK = 1024
PAIR_W = 2 * FOX_HD
NEG_BIG = -1e30
LOG2E = 1.4426950408889634
FOLD_SRC = FOX_HEADS
FOLD_W = 6
FOX_AHEAD = 2
FOX_TQ = 512
FOX_GROUP = 8
assert FOLD_SRC + FOLD_W * FOX_HEADS <= V7X_LANES
VMEM_LIMIT = V7X_VMEM_BYTES - 8 * 1024 * 1024

assert HALO >= POOL_BUF and PAIR_W == V7X_LANES and POOL_GROUP == V7X_LANES and MEM_HD == V7X_LANES


def _const_spec(shape):
    zeros = (0,) * len(shape)
    return pl.BlockSpec(shape, lambda *_: zeros, pipeline_mode=pl.Buffered(1))


def _rms(x, g):
    return x * lax.rsqrt(jnp.mean(x * x, axis=-1, keepdims=True) + EPS) * g


def _dot(a, b):
    return jnp.dot(a, b, preferred_element_type=F32)


def _dot_nt(a, b):
    return lax.dot_general(a, b, (((1,), (1,)), ((), ())), preferred_element_type=F32)


def _log_sigmoid(x):
    return jnp.minimum(x, 0.0) - jnp.log1p(jnp.exp(-jnp.abs(x)))


def _cumsum(x, axis, seg):
    assert seg & (seg - 1) == 0 and x.shape[axis] % seg == 0
    idx = lax.broadcasted_iota(jnp.int32, x.shape, axis) & (seg - 1)
    k = 1
    while k < seg:
        x = x + jnp.where(idx >= k, pltpu.roll(x, k, axis=axis), 0.0)
        k *= 2
    return x


def _mem_attention(q, mk, mv):
    outs = []
    for h in range(MEM_HEADS):
        sl = slice(h * MEM_HD, (h + 1) * MEM_HD)
        s = _dot_nt(q[:, sl].astype(BF16), mk[:, sl]) * (MEM_HD ** -0.5)
        m = jnp.max(s, axis=-1, keepdims=True)
        p = jnp.exp(s - m)
        l = jnp.sum(p, axis=-1, keepdims=True)
        outs.append(_dot(p.astype(BF16), mv[:, sl]) / l)
    return jnp.concatenate(outs, axis=-1)


def _pool_mixer(u, halo, pos0, w_pool_ref, scale):
    tt = u.shape[0]
    ext = jnp.concatenate([halo, u], axis=0)
    pos = pos0 + lax.broadcasted_iota(jnp.int32, (tt, 1), 0)
    outs = []
    for g, w in enumerate(POOL_WINDOWS):
        sl = slice(g * POOL_GROUP, (g + 1) * POOL_GROUP)
        s = ext[:, sl]
        k = 1
        while k < w:
            s = s + pltpu.roll(s, k, axis=0)
            k *= 2
        cnt = jnp.minimum(pos + 1, w).astype(F32)
        pooled = s[HALO:] / cnt - u[:, sl]
        outs.append(_dot(pooled.astype(BF16), w_pool_ref[g]))
    return jnp.concatenate(outs, axis=-1) * scale


def _mlp(x1, g_pre, g_post, w_up_ref, w_down_ref):
    hm = _rms(x1, g_pre).astype(BF16)
    acc = jnp.zeros(x1.shape, F32)
    for c in range(D_FF // FF_CHUNK):
        cs = slice(c * FF_CHUNK, (c + 1) * FF_CHUNK)
        a = jnp.square(jnp.maximum(_dot(hm, w_up_ref[:, cs]), 0.0)).astype(BF16)
        acc = acc + _dot(a, w_down_ref[cs, :])
    return x1 + _rms(acc, g_post)


def _mem_project_kernel(mem_ref, g_ref, w_ref, mk_ref, mv_ref):
    kv = _dot(_rms(mem_ref[0], g_ref[0]).astype(BF16), w_ref[0])
    mk_ref[0, 0] = kv[:, :MEM_W]
    mv_ref[0, 0] = kv[:, MEM_W:]


def _mem_project(mem, g_mem, w_mem_kv):
    b, n, _ = mem.shape
    depth = w_mem_kv.shape[0]
    out = jax.ShapeDtypeStruct((depth, b, n, MEM_W), F32)
    return pl.pallas_call(
        _mem_project_kernel,
        grid=(depth, b),
        in_specs=[pl.BlockSpec((1, n, D_MODEL), lambda l, i: (i, 0, 0)),
                  pl.BlockSpec((1, 1, D_MODEL), lambda l, i: (l, 0, 0)),
                  pl.BlockSpec((1, D_MODEL, 2 * MEM_W), lambda l, i: (l, 0, 0))],
        out_specs=[pl.BlockSpec((1, 1, n, MEM_W), lambda l, i: (l, i, 0, 0))] * 2,
        out_shape=[out, out],
        compiler_params=pltpu.CompilerParams(dimension_semantics=("arbitrary", "arbitrary")),
        name="mem_project",
    )(mem, g_mem.reshape(depth, 1, D_MODEL), w_mem_kv)


def _layer0_kernel(x_ref, prev_ref, mk_ref, mv_ref, g_pre, g_post, g_mpre, g_mpost,
                   w_in, w_out, w_pool, pscale, w_up, w_down,
                   y_ref, state_ref, halo_ref, *, nb, tt, pos_base):
    t = pl.program_id(1)

    @pl.when(t == 0)
    def _():
        halo_ref[...] = prev_ref[...]

    x = x_ref[...].reshape(nb * tt, D_MODEL)
    proj = _dot(_rms(x, g_pre[...]).astype(BF16), w_in[...])
    cats = []
    for b in range(nb):
        rows = slice(b * tt, (b + 1) * tt)
        u = proj[rows, :MIX_W]
        mix = _pool_mixer(u, halo_ref[b], pos_base + t * tt, w_pool, pscale[...])
        halo_ref[b] = u[tt - HALO:, :]

        @pl.when(t == pl.num_programs(1) - 1)
        def _():
            state_ref[0, b] = u[tt - POOL_BUF:, :]

        mem = _mem_attention(proj[rows, MIX_W:], mk_ref[b].astype(BF16), mv_ref[b].astype(BF16))
        cats.append(jnp.concatenate([mix, mem], axis=-1))
    cat = cats[0] if nb == 1 else jnp.concatenate(cats, axis=0)
    x1 = x + _rms(_dot(cat.astype(BF16), w_out[...]), g_post[...])
    y = _mlp(x1, g_mpre[...], g_mpost[...], w_up, w_down)
    y_ref[...] = y.reshape(nb, tt, D_MODEL)


def _layer0(x, prev, mem_k, mem_v, g_pre, g_post, g_mpre, g_mpost, w_in, w_out, w_pool, pscale,
            w_up, w_down, *, nb, tt, pos_base):
    b, t, _ = x.shape
    n_mem = mem_k.shape[1]
    grid = (b // nb, t // tt)
    row = lambda a: a.reshape(1, -1)
    tile = pl.BlockSpec((nb, tt, D_MODEL), lambda i, j: (i, j, 0))
    per_b = lambda shape: pl.BlockSpec((nb,) + shape, lambda i, j: (i, 0, 0))
    return pl.pallas_call(
        functools.partial(_layer0_kernel, nb=nb, tt=tt, pos_base=pos_base),
        grid=grid,
        in_specs=[tile, per_b((HALO, MIX_W)), per_b((n_mem, MEM_W)), per_b((n_mem, MEM_W)),
                  _const_spec((1, D_MODEL)), _const_spec((1, D_MODEL)),
                  _const_spec((1, D_MODEL)), _const_spec((1, D_MODEL)),
                  _const_spec((D_MODEL, D_MODEL)), _const_spec((D_MODEL, D_MODEL)),
                  _const_spec(w_pool.shape), _const_spec((1, MIX_W)),
                  _const_spec((D_MODEL, D_FF)), _const_spec((D_FF, D_MODEL))],
        out_specs=[tile, pl.BlockSpec((1, nb, POOL_BUF, MIX_W), lambda i, j: (0, i, 0, 0))],
        out_shape=[jax.ShapeDtypeStruct((b, t, D_MODEL), F32),
                   jax.ShapeDtypeStruct((1, b, POOL_BUF, MIX_W), F32)],
        scratch_shapes=[pltpu.VMEM((nb, HALO, MIX_W), F32)],
        compiler_params=pltpu.CompilerParams(dimension_semantics=("arbitrary", "arbitrary"),
                                             vmem_limit_bytes=VMEM_LIMIT),
        name="layer0",
    )(x, prev, mem_k, mem_v, row(g_pre), row(g_post), row(g_mpre), row(g_mpost),
      w_in, w_out, w_pool, row(pscale), w_up, w_down)


def _bf16_part(x):
    return x.astype(BF16).astype(F32)


def _fold_operands(q, k, v, c):
    cl = c * LOG2E
    hi = _bf16_part(cl)
    mid = _bf16_part(cl - hi)
    lo = _bf16_part(cl - hi - mid)
    lane = lax.broadcasted_iota(jnp.int32, (1, V7X_LANES), 1)
    piece = lax.rem(lane + (3 - FOLD_SRC % 3), 3)
    d = jnp.where(piece == 0, hi, jnp.where(piece == 1, mid, lo))
    qs, ks, vs = [], [], []
    for h in range(FOX_HEADS):
        own = (h % 2) * FOX_HD
        e0 = FOX_HD - own
        cols = slice((h // 2) * PAIR_W, (h // 2 + 1) * PAIR_W)
        dh = pltpu.roll(d, (e0 - (FOLD_SRC + FOLD_W * h)) % V7X_LANES, axis=1)
        in_head = (lane >= own) & (lane < own + FOX_HD)
        first = (lane >= e0) & (lane < e0 + 3)
        second = (lane >= e0 + 3) & (lane < e0 + 6)
        xq = jnp.where(first, dh, jnp.where(second, 1.0, 0.0))
        xk = jnp.where(first, 1.0, jnp.where(second, -dh, 0.0))
        xv = jnp.where(lane == e0, 1.0, 0.0)
        qs.append(jnp.where(in_head, q[:, cols] * (FOX_HD ** -0.5 * LOG2E), xq).T.astype(BF16))
        ks.append(jnp.where(in_head, k[:, cols], xk).astype(BF16))
        vs.append(jnp.where(in_head, v[:, cols], xv).T.astype(BF16))
    return jnp.concatenate(qs, axis=0), jnp.concatenate(ks, axis=-1), jnp.concatenate(vs, axis=0)


def _layer1_front_kernel(x_ref, mk_ref, mv_ref, g_kv, g_pre, w_k, w_v, w_f, w_ft, b_f, b_ft, w_in, *refs,
                         nb, tt, fold):
    if fold:
        k_ref, v_ref, logf_ref, mem_ref, qa_ref, ka_ref, va_ref, carry_ref = refs
    else:
        k_ref, v_ref, logf_ref, mem_ref, q_ref, kb_ref, vb_ref, c_ref, ct_ref = refs

    x = x_ref[...].reshape(nb * tt, D_MODEL)
    hk = _rms(x, g_kv[...]).astype(BF16)
    k = _dot(hk, w_k[...])
    v = _dot(hk, w_v[...])
    k_ref[...] = k.reshape(nb, tt, MIX_W)
    v_ref[...] = v.reshape(nb, tt, MIX_W)
    logf = _log_sigmoid(_dot(hk, w_f[...]) + b_f[...])
    proj = _dot(_rms(x, g_pre[...]).astype(BF16), w_in[...])
    q = proj[:, :MIX_W]
    c = _cumsum(logf, 0, tt)
    if fold:
        @pl.when(pl.program_id(1) == 0)
        def _():
            carry_ref[...] = jnp.zeros_like(carry_ref)

        c = c + carry_ref[...]
        carry_ref[...] = c[tt - 1:, :]
        qa, ka, va = _fold_operands(q, k, v, c)
        qa_ref[0, 0] = qa
        ka_ref[0] = ka
        va_ref[0, 0] = va
    else:
        q_ref[...] = (q * (FOX_HD ** -0.5)).astype(BF16).reshape(nb, tt, MIX_W)
        kb_ref[...] = k.astype(BF16).reshape(nb, tt, MIX_W)
        vb_ref[...] = v.astype(BF16).reshape(nb, tt, MIX_W)
        logf_t = _log_sigmoid(_dot_nt(w_ft[...], hk) + b_ft[...])
        ct_ref[0, 0] = _cumsum(logf_t, 1, tt)
    for b in range(nb):
        rows = slice(b * tt, (b + 1) * tt)
        logf_ref[b] = logf[rows, :FOX_HEADS]
        if not fold:
            c_ref[b] = c[rows, :FOX_HEADS]
        mem = _mem_attention(proj[rows, MIX_W:], mk_ref[b].astype(BF16), mv_ref[b].astype(BF16))
        mem_ref[b] = mem.astype(BF16)


def _layer1_front(x, mem_k, mem_v, g_kv, g_pre, w_k, w_v, w_f, w_ft, b_f, b_ft, w_in, *, nb, tt, fold):
    b, t, _ = x.shape
    assert (nb == 1) if fold else (t == tt)
    n_mem = mem_k.shape[1]
    row = lambda a: a.reshape(1, -1)
    tile = lambda w: pl.BlockSpec((nb, tt, w), lambda i, j: (i, j, 0))
    per_b = lambda shape: pl.BlockSpec((nb,) + shape, lambda i, j: (i, 0, 0))
    act = lambda w, dt: jax.ShapeDtypeStruct((b, t, w), dt)
    out_specs = [tile(MIX_W), tile(MIX_W), tile(FOX_HEADS), tile(MEM_W)]
    out_shape = [act(MIX_W, F32), act(MIX_W, F32), act(FOX_HEADS, F32), act(MEM_W, BF16)]
    if fold:
        wf = FOX_HEADS * PAIR_W
        tile_t = pl.BlockSpec((1, 1, wf, tt), lambda i, j: (i, j, 0, 0))
        act_t = jax.ShapeDtypeStruct((b, t // tt, wf, tt), BF16)
        out_specs += [tile_t, tile(wf), tile_t]
        out_shape += [act_t, act(wf, BF16), act_t]
        scratch = [pltpu.VMEM((1, V7X_LANES), F32)]
    else:
        out_specs += [tile(MIX_W)] * 3 + [tile(FOX_HEADS),
                                          pl.BlockSpec((1, 1, FOX_HEADS, nb * tt), lambda i, j: (i, j, 0, 0))]
        out_shape += [act(MIX_W, BF16)] * 3 + [act(FOX_HEADS, F32),
                                               jax.ShapeDtypeStruct((b // nb, t // tt, FOX_HEADS, nb * tt), F32)]
        scratch = []
    return pl.pallas_call(
        functools.partial(_layer1_front_kernel, nb=nb, tt=tt, fold=fold),
        grid=(b // nb, t // tt),
        in_specs=[tile(D_MODEL), per_b((n_mem, MEM_W)), per_b((n_mem, MEM_W)),
                  _const_spec((1, D_MODEL)), _const_spec((1, D_MODEL)),
                  _const_spec((D_MODEL, MIX_W)), _const_spec((D_MODEL, MIX_W)),
                  _const_spec((D_MODEL, V7X_LANES)), _const_spec((FOX_HEADS, D_MODEL)),
                  _const_spec((1, V7X_LANES)), _const_spec((FOX_HEADS, 1)),
                  _const_spec((D_MODEL, D_MODEL))],
        out_specs=out_specs,
        out_shape=out_shape,
        scratch_shapes=scratch,
        compiler_params=pltpu.CompilerParams(dimension_semantics=("arbitrary", "arbitrary"),
                                             vmem_limit_bytes=VMEM_LIMIT),
        name="layer1_front",
    )(x, mem_k, mem_v, row(g_kv), row(g_pre), w_k, w_v, w_f, w_ft, b_f, b_ft, w_in)


def _fox_tile(qm, cq, k_pair, v_pair, ck, carry, mask):
    m, l, acc = carry
    s = _dot_nt(qm, k_pair) + cq - ck
    if mask is not None:
        s = jnp.where(mask, s, NEG_BIG)
    m_new = jnp.maximum(m, jnp.max(s, axis=-1, keepdims=True))
    alpha = jnp.exp(m - m_new)
    p = jnp.exp(s - m_new)
    l = alpha * l + jnp.sum(p, axis=-1, keepdims=True)
    acc = alpha * acc + _dot(p.astype(BF16), v_pair)
    return m_new, l, acc


def _fox_prompt_kernel(qt_ref, k_ref, vt_ref, o_ref, *, tq, tk):
    i = pl.program_id(1)
    n_full = (i * tq) // tk
    sub = lax.broadcasted_iota(jnp.int32, (PAIR_W, 1), 0)
    causal = (n_full * tk + lax.broadcasted_iota(jnp.int32, (tk, tq), 0)
              <= i * tq + lax.broadcasted_iota(jnp.int32, (tk, tq), 1))
    for g in range(FOX_HEADS // FOX_GROUP):
        heads = range(g * FOX_GROUP, (g + 1) * FOX_GROUP)
        col = [slice(h * PAIR_W, (h + 1) * PAIR_W) for h in heads]
        qt = [qt_ref[0, 0, c, :] for c in col]

        def tile(j, carry, mask):
            rows = pl.ds(pl.multiple_of(j * tk, tk), tk)
            out = []
            logits = lambda n: _dot(k_ref[0, rows, col[n]], qt[n])
            ahead = [logits(n) for n in range(min(FOX_AHEAD, FOX_GROUP))]
            for n in range(FOX_GROUP):
                m, acc = carry[n]
                s = ahead.pop(0)
                if n + FOX_AHEAD < FOX_GROUP:
                    ahead.append(logits(n + FOX_AHEAD))
                if mask is not None:
                    s = jnp.where(mask, s, NEG_BIG)
                m_new = jnp.maximum(m, jnp.max(s, axis=0, keepdims=True))
                p = jnp.exp2(s - m_new).astype(BF16)
                out.append((m_new, jnp.exp2(m - m_new) * acc + _dot(vt_ref[0, j, col[n], :], p)))
            return tuple(out)

        init = ((jnp.full((1, tq), NEG_BIG, F32), jnp.zeros((PAIR_W, tq), F32)),) * FOX_GROUP
        carry = lax.fori_loop(0, n_full, lambda j, c: tile(j, c, None), init)
        carry = tile(n_full, carry, causal)
        outs = []
        for h, (_, acc) in zip(heads, carry):
            denom = FOX_HD * (1 - h % 2)
            outs.append(acc / acc[denom:denom + 1, :])
        for n in range(0, FOX_GROUP, 2):
            pair = (g * FOX_GROUP + n) // 2
            o_ref[0, :, pair * PAIR_W:(pair + 1) * PAIR_W] = (
                jnp.where(sub < FOX_HD, outs[n], outs[n + 1]).T.astype(BF16))


def _fox_prompt(qt, ka, vt, *, tq):
    b, t, w = ka.shape
    tk = qt.shape[3]
    assert tk % tq == 0
    per_stream = lambda shape: pl.BlockSpec((1,) + shape, lambda i, j: (i,) + (0,) * len(shape),
                                            pipeline_mode=pl.Buffered(1))
    return pl.pallas_call(
        functools.partial(_fox_prompt_kernel, tq=tq, tk=tk),
        grid=(b, t // tq),
        in_specs=[pl.BlockSpec((1, 1, w, tq), lambda i, j: (i, (j * tq) // tk, 0, j % (tk // tq))),
                  per_stream((t, w)), per_stream((t // tk, w, tk))],
        out_specs=pl.BlockSpec((1, tq, MIX_W), lambda i, j: (i, j, 0)),
        out_shape=jax.ShapeDtypeStruct((b, t, MIX_W), BF16),
        compiler_params=pltpu.CompilerParams(dimension_semantics=("arbitrary", "arbitrary"),
                                             vmem_limit_bytes=VMEM_LIMIT),
        name="fox_prompt",
    )(qt, ka, vt)


def _fox_sample_kernel(q_ref, kc_ref, vc_ref, kn_ref, vn_ref, lfc_t_ref, c_ref, ct_ref, o_ref):
    tn = q_ref.shape[1]
    p = lfc_t_ref.shape[2]
    cc = _cumsum(lfc_t_ref[0], 1, p)
    ck_cache = cc - cc[:, p - 1:]
    lane = lax.broadcasted_iota(jnp.int32, (1, PAIR_W), 1)
    causal = (lax.broadcasted_iota(jnp.int32, (tn, V7X_LANES), 1)
              <= lax.broadcasted_iota(jnp.int32, (tn, V7X_LANES), 0))
    pad = jnp.zeros((V7X_LANES - tn, PAIR_W), BF16)
    for pair in range(FOX_HEADS // 2):
        cols = slice(pair * PAIR_W, (pair + 1) * PAIR_W)
        q_pair = q_ref[0, :, cols]
        kc = kc_ref[0, :, cols].astype(BF16)
        vc = vc_ref[0, :, cols].astype(BF16)
        kn = jnp.concatenate([kn_ref[0, :, cols], pad], axis=0)
        vn = jnp.concatenate([vn_ref[0, :, cols], pad], axis=0)
        outs = []
        for hh in range(2):
            h = 2 * pair + hh
            in_head = (lane >= hh * FOX_HD) & (lane < (hh + 1) * FOX_HD)
            qm = jnp.where(in_head, q_pair, jnp.zeros_like(q_pair))
            cq = c_ref[0, :, h:h + 1]
            init = (jnp.full((tn, 1), NEG_BIG, F32), jnp.zeros((tn, 1), F32), jnp.zeros((tn, PAIR_W), F32))
            carry = _fox_tile(qm, cq, kc, vc, ck_cache[h:h + 1, :], init, None)
            _, l, acc = _fox_tile(qm, cq, kn, vn, ct_ref[0, h:h + 1, :], carry, causal)
            outs.append(acc / l)
        o_ref[0, :, cols] = jnp.where(lane < FOX_HD, outs[0], outs[1]).astype(BF16)


def _fox_sample(q, cache_k, cache_v, kb, vb, cache_logf_t, c, ct):
    b, tn, _ = q.shape
    p = cache_k.shape[1]
    new = lambda w: pl.BlockSpec((1, tn, w), lambda i: (i, 0, 0))
    old = pl.BlockSpec((1, p, MIX_W), lambda i: (i, 0, 0))
    return pl.pallas_call(
        _fox_sample_kernel,
        grid=(b,),
        in_specs=[new(MIX_W), old, old, new(MIX_W), new(MIX_W),
                  pl.BlockSpec((1, FOX_HEADS, p), lambda i: (i, 0, 0)),
                  new(FOX_HEADS), pl.BlockSpec((1, FOX_HEADS, V7X_LANES), lambda i: (i, 0, 0))],
        out_specs=new(MIX_W),
        out_shape=jax.ShapeDtypeStruct((b, tn, MIX_W), BF16),
        compiler_params=pltpu.CompilerParams(dimension_semantics=("arbitrary",), vmem_limit_bytes=VMEM_LIMIT),
        name="fox_sample",
    )(q, cache_k, cache_v, kb, vb, cache_logf_t, c, ct)


def _layer1_back_kernel(x_ref, mix_ref, mem_ref, g_post, g_mpre, g_mpost, w_out_mix, w_out_mem, w_up, w_down,
                        y_ref, *, nb, tt):
    m = nb * tt
    x = x_ref[...].reshape(m, D_MODEL)
    o = (_dot(mix_ref[...].reshape(m, MIX_W), w_out_mix[...])
         + _dot(mem_ref[...].reshape(m, MEM_W), w_out_mem[...]))
    x1 = x + _rms(o, g_post[...])
    y_ref[...] = _mlp(x1, g_mpre[...], g_mpost[...], w_up, w_down).reshape(nb, tt, D_MODEL)


def _layer1_back(x, mix, mem, g_post, g_mpre, g_mpost, w_out, w_up, w_down, *, nb, tt):
    b, t, _ = x.shape
    row = lambda a: a.reshape(1, -1)
    tile = lambda w: pl.BlockSpec((nb, tt, w), lambda i, j: (i, j, 0))
    return pl.pallas_call(
        functools.partial(_layer1_back_kernel, nb=nb, tt=tt),
        grid=(b // nb, t // tt),
        in_specs=[tile(D_MODEL), tile(MIX_W), tile(MEM_W),
                  _const_spec((1, D_MODEL)), _const_spec((1, D_MODEL)), _const_spec((1, D_MODEL)),
                  _const_spec((MIX_W, D_MODEL)), _const_spec((MEM_W, D_MODEL)),
                  _const_spec((D_MODEL, D_FF)), _const_spec((D_FF, D_MODEL))],
        out_specs=tile(D_MODEL),
        out_shape=jax.ShapeDtypeStruct((b, t, D_MODEL), F32),
        compiler_params=pltpu.CompilerParams(dimension_semantics=("arbitrary", "arbitrary"),
                                             vmem_limit_bytes=VMEM_LIMIT),
        name="layer1_back",
    )(x, mix, mem, row(g_post), row(g_mpre), row(g_mpost), w_out[:MIX_W], w_out[MIX_W:], w_up, w_down)


def _token_tile(b, t):
    if t >= 512:
        return 1, 512
    return b, t


def _trunk(x, pool_prev, cache, mem_k, mem_v, p):
    b, t, _ = x.shape
    nb, tt = _token_tile(b, t)
    pos_base = 0 if cache is None else POOL_BUF
    y0, pool_state = _layer0(x, pool_prev, mem_k[0], mem_v[0], p["g_mix_pre"][0], p["g_mix_post"][0],
                             p["g_mlp_pre"][0], p["g_mlp_post"][0], p["w_in"][0], p["w_out"][0], p["w_pool"][0],
                             p["pool_scale"][0], p["w_up"][0], p["w_down"][0], nb=nb, tt=tt, pos_base=pos_base)
    k, v, logf, mem, *att = _layer1_front(
        y0, mem_k[1], mem_v[1], p["g_kv"], p["g_mix_pre"][1], p["w_k"], p["w_v"], p["w_f"], p["w_ft"],
        p["b_f"], p["b_ft"], p["w_in"][1], nb=nb, tt=tt, fold=cache is None)
    if cache is None:
        mix = _fox_prompt(*att, tq=FOX_TQ)
    else:
        q, kb, vb, c, ct = att
        cache_k, cache_v, cache_logf = cache
        past = cache_k.shape[1]
        ct_new = jnp.swapaxes(ct.reshape(FOX_HEADS, b, t), 0, 1)
        ct_new = jnp.pad(ct_new, ((0, 0), (0, 0), (0, V7X_LANES - t)))
        mix = _fox_sample(q, cache_k.reshape(b, past, MIX_W), cache_v.reshape(b, past, MIX_W), kb, vb,
                          jnp.swapaxes(cache_logf, 1, 2), c, ct_new)
    y = _layer1_back(y0, mix, mem, p["g_mix_post"][1], p["g_mlp_pre"][1], p["g_mlp_post"][1],
                     p["w_out"][1], p["w_up"][1], p["w_down"][1], nb=nb, tt=tt)
    return (y, pool_state, k.reshape(b, t, FOX_HEADS, FOX_HD), v.reshape(b, t, FOX_HEADS, FOX_HD), logf)


def kernel(x_prompt, x_sample, cache_pool, cache_k, cache_v, cache_logf, cache_mem_k, cache_mem_v, mem_prompt,
           g_mix_pre, g_mix_post, g_mlp_pre, g_mlp_post, w_in, w_out, w_pool, pool_scale, g_kv, w_kvf, b_f,
           g_mem, w_mem_kv, w_up, w_down):
    depth = w_in.shape[0]
    assert depth == 2 and w_pool.shape[0] == 1, "one pooling layer followed by one forgetting-attention layer"
    bp = x_prompt.shape[0]
    bs = x_sample.shape[0]
    n_mem = mem_prompt.shape[1]
    w_f = w_kvf[:, 2 * MIX_W:]
    fold_cols = lambda a: jnp.pad(jnp.concatenate([a, jnp.repeat(a, FOLD_W, axis=-1)], axis=-1),
                                  [(0, 0)] * (a.ndim - 1) + [(0, V7X_LANES - FOLD_SRC - FOLD_W * FOX_HEADS)])
    p = dict(
        g_mix_pre=g_mix_pre, g_mix_post=g_mix_post, g_mlp_pre=g_mlp_pre, g_mlp_post=g_mlp_post, g_kv=g_kv,
        w_in=w_in.astype(BF16), w_out=w_out.astype(BF16), w_pool=w_pool.astype(BF16), pool_scale=pool_scale,
        w_up=w_up.astype(BF16), w_down=w_down.astype(BF16),
        w_k=w_kvf[:, :MIX_W].astype(BF16), w_v=w_kvf[:, MIX_W:2 * MIX_W].astype(BF16),
        w_f=fold_cols(w_f).astype(BF16), w_ft=w_f.T.astype(BF16),
        b_f=fold_cols(b_f).reshape(1, V7X_LANES), b_ft=b_f.reshape(FOX_HEADS, 1),
    )

    mk, mv = _mem_project(mem_prompt, g_mem, w_mem_kv.astype(BF16))
    mem_k_prompt = mk.reshape(depth, bp, n_mem, MEM_HEADS, MEM_HD)
    mem_v_prompt = mv.reshape(depth, bp, n_mem, MEM_HEADS, MEM_HD)

    y_p, pool_p, k_p, v_p, logf_p = _trunk(x_prompt, jnp.zeros((bp, HALO, MIX_W), F32), None, mk, mv, p)

    pool_prev = jnp.pad(cache_pool[0], ((0, 0), (HALO - POOL_BUF, 0), (0, 0)))
    y_s, pool_s, k_s, v_s, logf_s = _trunk(
        x_sample, pool_prev, (cache_k, cache_v, cache_logf),
        cache_mem_k.reshape(depth, bs, n_mem, MEM_W), cache_mem_v.reshape(depth, bs, n_mem, MEM_W), p)

    return (y_p, y_s, pool_p, pool_s, k_p, v_p, logf_p, k_s, v_s, logf_s, mem_k_prompt, mem_v_prompt)
```

```python
import functools

import jax
import jax.numpy as jnp
from jax import lax
from jax.experimental import pallas as pl
from jax.experimental.pallas import tpu as pltpu

F32 = jnp.float32
BF16 = jnp.bfloat16

D_MODEL = 1024
MIX_W = D_MODEL // 2
MEM_W = D_MODEL - MIX_W
POOL_WINDOWS = (2, 4, 8, 16)
POOL_GROUP = MIX_W // len(POOL_WINDOWS)
POOL_BUF = max(POOL_WINDOWS) - 1
FOX_HEADS = 8
FOX_HD = MIX_W // FOX_HEADS
MEM_HEADS = 4
MEM_HD = MEM_W // MEM_HEADS
D_FF = 4 * D_MODEL
EPS = 1e-6

V7X_LANES = 128
V7X_SUBLANES_F32 = 8
V7X_VMEM_BYTES = 64 * 1024 * 1024

HALO = 2 * V7X_SUBLANES_F32
FF_CHUNK = 1024
PAIR_W = 2 * FOX_HD
NEG_BIG = -1e30
LOG2E = 1.4426950408889634
FOLD_SRC = FOX_HEADS
FOLD_W = 6
FOX_AHEAD = 2
FOX_TQ = 512
FOX_GROUP = 8
assert FOLD_SRC + FOLD_W * FOX_HEADS <= V7X_LANES
VMEM_LIMIT = V7X_VMEM_BYTES - 8 * 1024 * 1024

assert HALO >= POOL_BUF and PAIR_W == V7X_LANES and POOL_GROUP == V7X_LANES and MEM_HD == V7X_LANES


def _const_spec(shape):
    zeros = (0,) * len(shape)
    return pl.BlockSpec(shape, lambda *_: zeros, pipeline_mode=pl.Buffered(1))


def _layer_spec(stacked, layer):
    index = (layer,) + (0,) * (stacked.ndim - 1)
    return pl.BlockSpec((None,) + stacked.shape[1:], lambda *_: index, pipeline_mode=pl.Buffered(1))


def _mem_spec(mem, layer, nb):
    return pl.BlockSpec((None, nb) + mem.shape[2:], lambda i, j: (layer, i, 0, 0, 0), pipeline_mode=pl.Buffered(1))


def _rms(x, g):
    return x * lax.rsqrt(jnp.mean(x * x, axis=-1, keepdims=True) + EPS) * g


def _dot(a, b):
    return jnp.dot(a, b, preferred_element_type=F32)


def _dot_nt(a, b):
    return lax.dot_general(a, b, (((1,), (1,)), ((), ())), preferred_element_type=F32)


def _log_sigmoid(x):
    return jnp.minimum(x, 0.0) - jnp.log1p(jnp.exp(-jnp.abs(x)))


def _cumsum(x, axis, seg):
    assert seg & (seg - 1) == 0 and x.shape[axis] % seg == 0
    idx = lax.broadcasted_iota(jnp.int32, x.shape, axis) & (seg - 1)
    k = 1
    while k < seg:
        x = x + jnp.where(idx >= k, pltpu.roll(x, k, axis=axis), 0.0)
        k *= 2
    return x


def _mem_attention(q, mk_ref, mv_ref, b):
    outs = []
    for h in range(MEM_HEADS):
        sl = slice(h * MEM_HD, (h + 1) * MEM_HD)
        s = _dot_nt(q[:, sl].astype(BF16), mk_ref[b, :, h, :].astype(BF16)) * (MEM_HD ** -0.5)
        m = jnp.max(s, axis=-1, keepdims=True)
        p = jnp.exp(s - m)
        l = jnp.sum(p, axis=-1, keepdims=True)
        outs.append(_dot(p.astype(BF16), mv_ref[b, :, h, :].astype(BF16)) / l)
    return jnp.concatenate(outs, axis=-1)


def _pool_mixer(u, halo, pos0, w_pool_ref, scale):
    tt = u.shape[0]
    ext = jnp.concatenate([halo, u], axis=0)
    pos = pos0 + lax.broadcasted_iota(jnp.int32, (tt, 1), 0)
    outs = []
    for g, w in enumerate(POOL_WINDOWS):
        sl = slice(g * POOL_GROUP, (g + 1) * POOL_GROUP)
        s = ext[:, sl]
        k = 1
        while k < w:
            s = s + pltpu.roll(s, k, axis=0)
            k *= 2
        cnt = jnp.minimum(pos + 1, w).astype(F32)
        pooled = s[HALO:] / cnt - u[:, sl]
        outs.append(_dot(pooled.astype(BF16), w_pool_ref[g]))
    return jnp.concatenate(outs, axis=-1) * scale


def _mlp(x1, g_pre, g_post, w_up_ref, w_down_ref):
    hm = _rms(x1, g_pre).astype(BF16)
    acc = jnp.zeros(x1.shape, F32)
    for c in range(D_FF // FF_CHUNK):
        cs = slice(c * FF_CHUNK, (c + 1) * FF_CHUNK)
        a = jnp.square(jnp.maximum(_dot(hm, w_up_ref[:, cs]), 0.0)).astype(BF16)
        acc = acc + _dot(a, w_down_ref[cs, :])
    return x1 + _rms(acc, g_post)


def _mem_project_kernel(mem_ref, g_ref, w_ref, mk_ref, mv_ref):
    kv = _dot(_rms(mem_ref[0], g_ref[0]).astype(BF16), w_ref[0])
    for h in range(MEM_HEADS):
        mk_ref[0, 0, :, h, :] = kv[:, h * MEM_HD:(h + 1) * MEM_HD]
        mv_ref[0, 0, :, h, :] = kv[:, MEM_W + h * MEM_HD:MEM_W + (h + 1) * MEM_HD]


def _mem_project(mem, g_mem, w_mem_kv):
    b, n, _ = mem.shape
    depth = w_mem_kv.shape[0]
    out = jax.ShapeDtypeStruct((depth, b, n, MEM_HEADS, MEM_HD), F32)
    return pl.pallas_call(
        _mem_project_kernel,
        grid=(depth, b),
        in_specs=[pl.BlockSpec((1, n, D_MODEL), lambda l, i: (i, 0, 0)),
                  pl.BlockSpec((1, 1, D_MODEL), lambda l, i: (l, 0, 0)),
                  pl.BlockSpec((1, D_MODEL, 2 * MEM_W), lambda l, i: (l, 0, 0))],
        out_specs=[pl.BlockSpec((1, 1, n, MEM_HEADS, MEM_HD), lambda l, i: (l, i, 0, 0, 0))] * 2,
        out_shape=[out, out],
        compiler_params=pltpu.CompilerParams(dimension_semantics=("arbitrary", "arbitrary")),
        name="mem_project",
    )(mem, g_mem.reshape(depth, 1, D_MODEL), w_mem_kv)


def _layer0_kernel(x_ref, prev_ref, mk_ref, mv_ref, g_pre, g_post, g_mpre, g_mpost,
                   w_in, w_out, w_pool, pscale, w_up, w_down,
                   y_ref, state_ref, halo_ref, *, nb, tt, pos_base):
    t = pl.program_id(1)

    @pl.when(t == 0)
    def _():
        halo_ref[...] = prev_ref[...]

    x = x_ref[...].reshape(nb * tt, D_MODEL)
    proj = _dot(_rms(x, g_pre[...]).astype(BF16), w_in[...])
    cats = []
    for b in range(nb):
        rows = slice(b * tt, (b + 1) * tt)
        u = proj[rows, :MIX_W]
        mix = _pool_mixer(u, halo_ref[b], pos_base + t * tt, w_pool, pscale[...])
        halo_ref[b] = u[tt - HALO:, :]

        @pl.when(t == pl.num_programs(1) - 1)
        def _():
            state_ref[0, b] = u[tt - POOL_BUF:, :]

        mem = _mem_attention(proj[rows, MIX_W:], mk_ref, mv_ref, b)
        cats.append(jnp.concatenate([mix, mem], axis=-1))
    cat = cats[0] if nb == 1 else jnp.concatenate(cats, axis=0)
    x1 = x + _rms(_dot(cat.astype(BF16), w_out[...]), g_post[...])
    y = _mlp(x1, g_mpre[...], g_mpost[...], w_up, w_down)
    y_ref[...] = y.reshape(nb, tt, D_MODEL)


def _layer0(x, prev, mem_k, mem_v, p, *, nb, tt, pos_base):
    b, t, _ = x.shape
    tile = pl.BlockSpec((nb, tt, D_MODEL), lambda i, j: (i, j, 0))
    return pl.pallas_call(
        functools.partial(_layer0_kernel, nb=nb, tt=tt, pos_base=pos_base),
        grid=(b // nb, t // tt),
        in_specs=[tile, pl.BlockSpec((nb, HALO, MIX_W), lambda i, j: (i, 0, 0)),
                  _mem_spec(mem_k, 0, nb), _mem_spec(mem_v, 0, nb),
                  _layer_spec(p["g_mix_pre"], 0), _layer_spec(p["g_mix_post"], 0),
                  _layer_spec(p["g_mlp_pre"], 0), _layer_spec(p["g_mlp_post"], 0),
                  _layer_spec(p["w_in"], 0), _layer_spec(p["w_out"], 0),
                  _layer_spec(p["w_pool"], 0), _layer_spec(p["pool_scale"], 0),
                  _layer_spec(p["w_up"], 0), _layer_spec(p["w_down"], 0)],
        out_specs=[tile, pl.BlockSpec((1, nb, POOL_BUF, MIX_W), lambda i, j: (0, i, 0, 0))],
        out_shape=[jax.ShapeDtypeStruct((b, t, D_MODEL), F32),
                   jax.ShapeDtypeStruct((1, b, POOL_BUF, MIX_W), F32)],
        scratch_shapes=[pltpu.VMEM((nb, HALO, MIX_W), F32)],
        compiler_params=pltpu.CompilerParams(dimension_semantics=("arbitrary", "arbitrary"),
                                             vmem_limit_bytes=VMEM_LIMIT),
        name="layer0",
    )(x, prev, mem_k, mem_v, p["g_mix_pre"], p["g_mix_post"], p["g_mlp_pre"], p["g_mlp_post"],
      p["w_in"], p["w_out"], p["w_pool"], p["pool_scale"], p["w_up"], p["w_down"])


def _bf16_part(x):
    return x.astype(BF16).astype(F32)


def _fold_operands(q, k, v, c):
    cl = c * LOG2E
    hi = _bf16_part(cl)
    mid = _bf16_part(cl - hi)
    lo = _bf16_part(cl - hi - mid)
    lane = lax.broadcasted_iota(jnp.int32, (1, V7X_LANES), 1)
    piece = lax.rem(lane + (3 - FOLD_SRC % 3), 3)
    d = jnp.where(piece == 0, hi, jnp.where(piece == 1, mid, lo))
    qs, ks, vs = [], [], []
    for h in range(FOX_HEADS):
        own = (h % 2) * FOX_HD
        e0 = FOX_HD - own
        cols = slice((h // 2) * PAIR_W, (h // 2 + 1) * PAIR_W)
        dh = pltpu.roll(d, (e0 - (FOLD_SRC + FOLD_W * h)) % V7X_LANES, axis=1)
        in_head = (lane >= own) & (lane < own + FOX_HD)
        first = (lane >= e0) & (lane < e0 + 3)
        second = (lane >= e0 + 3) & (lane < e0 + 6)
        xq = jnp.where(first, dh, jnp.where(second, 1.0, 0.0))
        xk = jnp.where(first, 1.0, jnp.where(second, -dh, 0.0))
        xv = jnp.where(lane == e0, 1.0, 0.0)
        qs.append(jnp.where(in_head, q[:, cols] * (FOX_HD ** -0.5 * LOG2E), xq).T.astype(BF16))
        ks.append(jnp.where(in_head, k[:, cols], xk).astype(BF16))
        vs.append(jnp.where(in_head, v[:, cols], xv).T.astype(BF16))
    return jnp.concatenate(qs, axis=0), jnp.concatenate(ks, axis=-1), jnp.concatenate(vs, axis=0)


def _layer1_front_kernel(x_ref, mk_ref, mv_ref, g_kv, g_pre, w_kv, w_f, w_ft, b_f, b_ft, w_in, *refs,
                         nb, tt, fold):
    if fold:
        k_ref, v_ref, logft_ref, mem_ref, qa_ref, ka_ref, va_ref, carry_ref = refs
    else:
        k_ref, v_ref, logft_ref, mem_ref, q_ref, kb_ref, vb_ref, c_ref, ct_ref = refs

    x = x_ref[...].reshape(nb * tt, D_MODEL)
    hk = _rms(x, g_kv[...]).astype(BF16)
    kv = _dot(hk, w_kv[...])
    k = kv[:, :MIX_W]
    v = kv[:, MIX_W:]
    k_ref[...] = k.reshape(nb, tt, MIX_W)
    v_ref[...] = v.reshape(nb, tt, MIX_W)
    logf = _log_sigmoid(_dot(hk, w_f[...]) + b_f[...])
    proj = _dot(_rms(x, g_pre[...]).astype(BF16), w_in[...])
    q = proj[:, :MIX_W]
    c = _cumsum(logf, 0, tt)
    if fold:
        @pl.when(pl.program_id(1) == 0)
        def _():
            carry_ref[...] = jnp.zeros_like(carry_ref)

        c = c + carry_ref[...]
        carry_ref[...] = c[tt - 1:, :]
        qa, ka, va = _fold_operands(q, k, v, c)
        qa_ref[0, 0] = qa
        ka_ref[0] = ka
        va_ref[0, 0] = va
        logft_ref[0] = logf.T[:FOX_HEADS, :]
    else:
        q_ref[...] = (q * (FOX_HD ** -0.5)).astype(BF16).reshape(nb, tt, MIX_W)
        kb_ref[...] = k.astype(BF16).reshape(nb, tt, MIX_W)
        vb_ref[...] = v.astype(BF16).reshape(nb, tt, MIX_W)
        logf_t = _log_sigmoid(_dot_nt(w_ft[...], hk) + b_ft[...])
        logft_ref[0] = logf_t
        ct_ref[0, 0] = _cumsum(logf_t, 1, tt)
    for b in range(nb):
        rows = slice(b * tt, (b + 1) * tt)
        if not fold:
            c_ref[b] = c[rows, :FOX_HEADS]
        mem_ref[b] = _mem_attention(proj[rows, MIX_W:], mk_ref, mv_ref, b).astype(BF16)


def _layer1_front(x, mem_k, mem_v, p, *, nb, tt, fold):
    b, t, _ = x.shape
    assert (nb == 1) if fold else (t == tt)
    tile = lambda w: pl.BlockSpec((nb, tt, w), lambda i, j: (i, j, 0))
    act = lambda w, dt: jax.ShapeDtypeStruct((b, t, w), dt)
    out_specs = [tile(MIX_W), tile(MIX_W), pl.BlockSpec((1, FOX_HEADS, nb * tt), lambda i, j: (i, 0, j)), tile(MEM_W)]
    out_shape = [act(MIX_W, F32), act(MIX_W, F32), jax.ShapeDtypeStruct((b // nb, FOX_HEADS, nb * t), F32),
                 act(MEM_W, BF16)]
    if fold:
        wf = FOX_HEADS * PAIR_W
        tile_t = pl.BlockSpec((1, 1, wf, tt), lambda i, j: (i, j, 0, 0))
        act_t = jax.ShapeDtypeStruct((b, t // tt, wf, tt), BF16)
        out_specs += [tile_t, tile(wf), tile_t]
        out_shape += [act_t, act(wf, BF16), act_t]
        scratch = [pltpu.VMEM((1, V7X_LANES), F32)]
    else:
        out_specs += [tile(MIX_W)] * 3 + [tile(FOX_HEADS),
                                          pl.BlockSpec((1, 1, FOX_HEADS, nb * tt), lambda i, j: (i, j, 0, 0))]
        out_shape += [act(MIX_W, BF16)] * 3 + [act(FOX_HEADS, F32),
                                               jax.ShapeDtypeStruct((b // nb, t // tt, FOX_HEADS, nb * tt), F32)]
        scratch = []
    return pl.pallas_call(
        functools.partial(_layer1_front_kernel, nb=nb, tt=tt, fold=fold),
        grid=(b // nb, t // tt),
        in_specs=[tile(D_MODEL), _mem_spec(mem_k, 1, nb), _mem_spec(mem_v, 1, nb),
                  _const_spec(p["g_kv"].shape), _layer_spec(p["g_mix_pre"], 1),
                  _const_spec(p["w_kv"].shape), _const_spec(p["w_f"].shape), _const_spec(p["w_ft"].shape),
                  _const_spec(p["b_f"].shape), _const_spec(p["b_ft"].shape), _layer_spec(p["w_in"], 1)],
        out_specs=out_specs,
        out_shape=out_shape,
        scratch_shapes=scratch,
        compiler_params=pltpu.CompilerParams(dimension_semantics=("arbitrary", "arbitrary"),
                                             vmem_limit_bytes=VMEM_LIMIT),
        name="layer1_front",
    )(x, mem_k, mem_v, p["g_kv"], p["g_mix_pre"], p["w_kv"], p["w_f"], p["w_ft"], p["b_f"], p["b_ft"], p["w_in"])


def _fox_tile(qm, cq, k, v, ck, carry, mask, *, keys_on_lanes):
    m, l, acc = carry
    s = (_dot(qm, k) if keys_on_lanes else _dot_nt(qm, k)) + cq - ck
    if mask is not None:
        s = jnp.where(mask, s, NEG_BIG)
    m_new = jnp.maximum(m, jnp.max(s, axis=-1, keepdims=True))
    alpha = jnp.exp(m - m_new)
    p = jnp.exp(s - m_new)
    l = alpha * l + jnp.sum(p, axis=-1, keepdims=True)
    pv = _dot_nt(p.astype(BF16), v) if keys_on_lanes else _dot(p.astype(BF16), v)
    return m_new, l, alpha * acc + pv


def _fox_prompt_kernel(qt_ref, k_ref, vt_ref, o_ref, *, tq, tk):
    i = pl.program_id(1)
    n_full = (i * tq) // tk
    sub = lax.broadcasted_iota(jnp.int32, (PAIR_W, 1), 0)
    causal = (n_full * tk + lax.broadcasted_iota(jnp.int32, (tk, tq), 0)
              <= i * tq + lax.broadcasted_iota(jnp.int32, (tk, tq), 1))
    for g in range(FOX_HEADS // FOX_GROUP):
        heads = range(g * FOX_GROUP, (g + 1) * FOX_GROUP)
        col = [slice(h * PAIR_W, (h + 1) * PAIR_W) for h in heads]
        qt = [qt_ref[0, 0, c, :] for c in col]

        def tile(j, carry, mask):
            rows = pl.ds(pl.multiple_of(j * tk, tk), tk)
            out = []
            logits = lambda n: _dot(k_ref[0, rows, col[n]], qt[n])
            ahead = [logits(n) for n in range(min(FOX_AHEAD, FOX_GROUP))]
            for n in range(FOX_GROUP):
                m, acc = carry[n]
                s = ahead.pop(0)
                if n + FOX_AHEAD < FOX_GROUP:
                    ahead.append(logits(n + FOX_AHEAD))
                if mask is not None:
                    s = jnp.where(mask, s, NEG_BIG)
                m_new = jnp.maximum(m, jnp.max(s, axis=0, keepdims=True))
                p = jnp.exp2(s - m_new).astype(BF16)
                out.append((m_new, jnp.exp2(m - m_new) * acc + _dot(vt_ref[0, j, col[n], :], p)))
            return tuple(out)

        init = ((jnp.full((1, tq), NEG_BIG, F32), jnp.zeros((PAIR_W, tq), F32)),) * FOX_GROUP
        carry = lax.fori_loop(0, n_full, lambda j, c: tile(j, c, None), init)
        carry = tile(n_full, carry, causal)
        outs = []
        for h, (_, acc) in zip(heads, carry):
            denom = FOX_HD * (1 - h % 2)
            outs.append(acc / acc[denom:denom + 1, :])
        for n in range(0, FOX_GROUP, 2):
            pair = (g * FOX_GROUP + n) // 2
            o_ref[0, :, pair * PAIR_W:(pair + 1) * PAIR_W] = (
                jnp.where(sub < FOX_HD, outs[n], outs[n + 1]).T.astype(BF16))


def _fox_prompt(qt, ka, vt, *, tq):
    b, t, w = ka.shape
    tk = qt.shape[3]
    assert tk % tq == 0
    per_stream = lambda shape: pl.BlockSpec((1,) + shape, lambda i, j: (i,) + (0,) * len(shape),
                                            pipeline_mode=pl.Buffered(1))
    return pl.pallas_call(
        functools.partial(_fox_prompt_kernel, tq=tq, tk=tk),
        grid=(b, t // tq),
        in_specs=[pl.BlockSpec((1, 1, w, tq), lambda i, j: (i, (j * tq) // tk, 0, j % (tk // tq))),
                  per_stream((t, w)), per_stream((t // tk, w, tk))],
        out_specs=pl.BlockSpec((1, tq, MIX_W), lambda i, j: (i, j, 0)),
        out_shape=jax.ShapeDtypeStruct((b, t, MIX_W), BF16),
        compiler_params=pltpu.CompilerParams(dimension_semantics=("arbitrary", "arbitrary"),
                                             vmem_limit_bytes=VMEM_LIMIT),
        name="fox_prompt",
    )(qt, ka, vt)


def _fox_sample_kernel(q_ref, kc_ref, vc_ref, kn_ref, vn_ref, lfc_t_ref, c_ref, ct_ref, o_ref):
    tn = q_ref.shape[1]
    p = lfc_t_ref.shape[2]
    cc = _cumsum(lfc_t_ref[0], 1, p)
    ck_cache = cc - cc[:, p - 1:]
    causal = (lax.broadcasted_iota(jnp.int32, (tn, V7X_LANES), 1)
              <= lax.broadcasted_iota(jnp.int32, (tn, V7X_LANES), 0))
    pad = jnp.zeros((V7X_LANES - tn, FOX_HD), BF16)
    outs = []
    for h in range(FOX_HEADS):
        cols = slice(h * FOX_HD, (h + 1) * FOX_HD)
        qh = q_ref[0, :, cols]
        kn = jnp.concatenate([kn_ref[0, :, cols], pad], axis=0)
        vn = jnp.concatenate([vn_ref[0, :, cols], pad], axis=0)
        cq = c_ref[0, :, h:h + 1]
        init = (jnp.full((tn, 1), NEG_BIG, F32), jnp.zeros((tn, 1), F32), jnp.zeros((tn, FOX_HD), F32))
        carry = _fox_tile(qh, cq, kc_ref[0, h].astype(BF16), vc_ref[0, h].astype(BF16),
                          ck_cache[h:h + 1, :], init, None, keys_on_lanes=True)
        _, l, acc = _fox_tile(qh, cq, kn, vn, ct_ref[0, h:h + 1, :], carry, causal, keys_on_lanes=False)
        outs.append(acc / l)
    o_ref[0] = jnp.concatenate(outs, axis=-1).astype(BF16)


def _fox_sample(q, cache_kt, cache_vt, kb, vb, cache_logf_t, c, ct):
    b, tn, _ = q.shape
    p = cache_kt.shape[3]
    new = lambda w: pl.BlockSpec((1, tn, w), lambda i: (i, 0, 0))
    old = pl.BlockSpec((1, FOX_HEADS, FOX_HD, p), lambda i: (i, 0, 0, 0))
    return pl.pallas_call(
        _fox_sample_kernel,
        grid=(b,),
        in_specs=[new(MIX_W), old, old, new(MIX_W), new(MIX_W),
                  pl.BlockSpec((1, FOX_HEADS, p), lambda i: (i, 0, 0)),
                  new(FOX_HEADS), pl.BlockSpec((1, FOX_HEADS, V7X_LANES), lambda i: (i, 0, 0))],
        out_specs=new(MIX_W),
        out_shape=jax.ShapeDtypeStruct((b, tn, MIX_W), BF16),
        compiler_params=pltpu.CompilerParams(dimension_semantics=("arbitrary",), vmem_limit_bytes=VMEM_LIMIT),
        name="fox_sample",
    )(q, cache_kt, cache_vt, kb, vb, cache_logf_t, c, ct)


def _layer1_back_kernel(x_ref, mix_ref, mem_ref, g_post, g_mpre, g_mpost, w_out, w_up, w_down, y_ref, *, nb, tt):
    m = nb * tt
    x = x_ref[...].reshape(m, D_MODEL)
    o = (_dot(mix_ref[...].reshape(m, MIX_W), w_out[:MIX_W, :])
         + _dot(mem_ref[...].reshape(m, MEM_W), w_out[MIX_W:, :]))
    x1 = x + _rms(o, g_post[...])
    y_ref[...] = _mlp(x1, g_mpre[...], g_mpost[...], w_up, w_down).reshape(nb, tt, D_MODEL)


def _layer1_back(x, mix, mem, p, *, nb, tt):
    b, t, _ = x.shape
    tile = lambda w: pl.BlockSpec((nb, tt, w), lambda i, j: (i, j, 0))
    return pl.pallas_call(
        functools.partial(_layer1_back_kernel, nb=nb, tt=tt),
        grid=(b // nb, t // tt),
        in_specs=[tile(D_MODEL), tile(MIX_W), tile(MEM_W),
                  _layer_spec(p["g_mix_post"], 1), _layer_spec(p["g_mlp_pre"], 1), _layer_spec(p["g_mlp_post"], 1),
                  _layer_spec(p["w_out"], 1), _layer_spec(p["w_up"], 1), _layer_spec(p["w_down"], 1)],
        out_specs=tile(D_MODEL),
        out_shape=jax.ShapeDtypeStruct((b, t, D_MODEL), F32),
        compiler_params=pltpu.CompilerParams(dimension_semantics=("arbitrary", "arbitrary"),
                                             vmem_limit_bytes=VMEM_LIMIT),
        name="layer1_back",
    )(x, mix, mem, p["g_mix_post"], p["g_mlp_pre"], p["g_mlp_post"], p["w_out"], p["w_up"], p["w_down"])


def _token_tile(b, t):
    if t >= 512:
        return 1, 512
    return b, t


def _trunk(x, pool_prev, cache, mem_k, mem_v, p):
    b, t, _ = x.shape
    nb, tt = _token_tile(b, t)
    pos_base = 0 if cache is None else POOL_BUF
    y0, pool_state = _layer0(x, pool_prev, mem_k, mem_v, p, nb=nb, tt=tt, pos_base=pos_base)
    k, v, logf_t, mem, *att = _layer1_front(y0, mem_k, mem_v, p, nb=nb, tt=tt, fold=cache is None)
    logf = jnp.transpose(logf_t.reshape(b // nb, FOX_HEADS, nb, t), (0, 2, 3, 1)).reshape(b, t, FOX_HEADS)
    if cache is None:
        mix = _fox_prompt(*att, tq=FOX_TQ)
    else:
        q, kb, vb, c, ct = att
        cache_k, cache_v, cache_logf = cache
        ct_new = jnp.swapaxes(ct.reshape(FOX_HEADS, b, t), 0, 1)
        ct_new = jnp.pad(ct_new, ((0, 0), (0, 0), (0, V7X_LANES - t)))
        to_lanes = lambda a: jnp.transpose(a, (0, 2, 3, 1))
        mix = _fox_sample(q, to_lanes(cache_k), to_lanes(cache_v), kb, vb, jnp.swapaxes(cache_logf, 1, 2), c, ct_new)
    y = _layer1_back(y0, mix, mem, p, nb=nb, tt=tt)
    return (y, pool_state, k.reshape(b, t, FOX_HEADS, FOX_HD), v.reshape(b, t, FOX_HEADS, FOX_HD), logf)


def kernel(x_prompt, x_sample, cache_pool, cache_k, cache_v, cache_logf, cache_mem_k, cache_mem_v, mem_prompt,
           g_mix_pre, g_mix_post, g_mlp_pre, g_mlp_post, w_in, w_out, w_pool, pool_scale, g_kv, w_kvf, b_f,
           g_mem, w_mem_kv, w_up, w_down):
    depth = w_in.shape[0]
    assert depth == 2 and w_pool.shape[0] == 1, "one pooling layer followed by one forgetting-attention layer"
    bp = x_prompt.shape[0]
    w_f = w_kvf[:, 2 * MIX_W:]
    fold_cols = lambda a: jnp.pad(jnp.concatenate([a, jnp.repeat(a, FOLD_W, axis=-1)], axis=-1),
                                  [(0, 0)] * (a.ndim - 1) + [(0, V7X_LANES - FOLD_SRC - FOLD_W * FOX_HEADS)])
    rows = lambda g: g.reshape(g.shape[0], 1, g.shape[1])
    p = dict(
        g_mix_pre=rows(g_mix_pre), g_mix_post=rows(g_mix_post), g_mlp_pre=rows(g_mlp_pre),
        g_mlp_post=rows(g_mlp_post), pool_scale=rows(pool_scale), g_kv=g_kv.reshape(1, D_MODEL),
        w_in=w_in.astype(BF16), w_out=w_out.astype(BF16), w_pool=w_pool.astype(BF16),
        w_up=w_up.astype(BF16), w_down=w_down.astype(BF16), w_kv=w_kvf[:, :2 * MIX_W].astype(BF16),
        w_f=fold_cols(w_f).astype(BF16), w_ft=w_f.T.astype(BF16),
        b_f=fold_cols(b_f).reshape(1, V7X_LANES), b_ft=b_f.reshape(FOX_HEADS, 1),
    )

    mem_k_prompt, mem_v_prompt = _mem_project(mem_prompt, g_mem, w_mem_kv.astype(BF16))

    y_p, pool_p, k_p, v_p, logf_p = _trunk(x_prompt, jnp.zeros((bp, HALO, MIX_W), F32), None,
                                           mem_k_prompt, mem_v_prompt, p)

    pool_prev = jnp.pad(cache_pool[0], ((0, 0), (HALO - POOL_BUF, 0), (0, 0)))
    y_s, pool_s, k_s, v_s, logf_s = _trunk(x_sample, pool_prev, (cache_k, cache_v, cache_logf),
                                           cache_mem_k, cache_mem_v, p)

    return (y_p, y_s, pool_p, pool_s, k_p, v_p, logf_p, k_s, v_s, logf_s, mem_k_prompt, mem_v_prompt)
```

```python
import functools

import jax
import jax.numpy as jnp
from jax import lax
from jax.experimental import pallas as pl
from jax.experimental.pallas import tpu as pltpu

F32 = jnp.float32
BF16 = jnp.bfloat16

D_MODEL = 1024
MIX_W = D_MODEL // 2
MEM_W = D_MODEL - MIX_W
POOL_WINDOWS = (2, 4, 8, 16)
POOL_GROUP = MIX_W // len(POOL_WINDOWS)
POOL_BUF = max(POOL_WINDOWS) - 1
FOX_HEADS = 8
FOX_HD = MIX_W // FOX_HEADS
MEM_HEADS = 4
MEM_HD = MEM_W // MEM_HEADS
D_FF = 4 * D_MODEL
EPS = 1e-6

V7X_LANES = 128
V7X_SUBLANES_F32 = 8
V7X_VMEM_BYTES = 64 * 1024 * 1024

HALO = 2 * V7X_SUBLANES_F32
FF_CHUNK = 1024
PAIR_W = 2 * FOX_HD
NEG_BIG = -1e30
LOG2E = 1.4426950408889634
FOLD_SRC = FOX_HEADS
FOLD_W = 6
FOX_AHEAD = 2
FOX_TQ = 512
FOX_GROUP = 8
assert FOLD_SRC + FOLD_W * FOX_HEADS <= V7X_LANES
VMEM_LIMIT = V7X_VMEM_BYTES - 8 * 1024 * 1024

assert HALO >= POOL_BUF and PAIR_W == V7X_LANES and POOL_GROUP == V7X_LANES and MEM_HD == V7X_LANES


def _const_spec(shape):
    zeros = (0,) * len(shape)
    return pl.BlockSpec(shape, lambda *_: zeros, pipeline_mode=pl.Buffered(1))


def _layer_spec(stacked, layer):
    index = (layer,) + (0,) * (stacked.ndim - 1)
    return pl.BlockSpec((None,) + stacked.shape[1:], lambda *_: index, pipeline_mode=pl.Buffered(1))


def _mem_spec(mem, layer, nb):
    return pl.BlockSpec((None, nb) + mem.shape[2:], lambda i, j: (layer, i, 0, 0, 0))


def _rms(x, g):
    return x * lax.rsqrt(jnp.mean(x * x, axis=-1, keepdims=True) + EPS) * g


def _dot(a, b):
    return jnp.dot(a, b, preferred_element_type=F32)


def _dot_nt(a, b):
    return lax.dot_general(a, b, (((1,), (1,)), ((), ())), preferred_element_type=F32)


def _log_sigmoid(x):
    return jnp.minimum(x, 0.0) - jnp.log1p(jnp.exp(-jnp.abs(x)))


def _cumsum(x, axis, seg):
    assert seg & (seg - 1) == 0 and x.shape[axis] % seg == 0
    idx = lax.broadcasted_iota(jnp.int32, x.shape, axis) & (seg - 1)
    k = 1
    while k < seg:
        x = x + jnp.where(idx >= k, pltpu.roll(x, k, axis=axis), 0.0)
        k *= 2
    return x


def _mem_attention(q, mk_ref, mv_ref, b):
    outs = []
    for h in range(MEM_HEADS):
        sl = slice(h * MEM_HD, (h + 1) * MEM_HD)
        s = _dot_nt(q[:, sl].astype(BF16), mk_ref[b, h]) * (MEM_HD ** -0.5)
        m = jnp.max(s, axis=-1, keepdims=True)
        p = jnp.exp(s - m)
        l = jnp.sum(p, axis=-1, keepdims=True)
        outs.append(_dot(p.astype(BF16), mv_ref[b, h]) / l)
    return jnp.concatenate(outs, axis=-1)


def _pool_mixer(u, halo, pos0, w_pool_ref, scale):
    tt = u.shape[0]
    ext = jnp.concatenate([halo, u], axis=0)
    pos = pos0 + lax.broadcasted_iota(jnp.int32, (tt, 1), 0)
    outs = []
    for g, w in enumerate(POOL_WINDOWS):
        sl = slice(g * POOL_GROUP, (g + 1) * POOL_GROUP)
        s = ext[:, sl]
        k = 1
        while k < w:
            s = s + pltpu.roll(s, k, axis=0)
            k *= 2
        cnt = jnp.minimum(pos + 1, w).astype(F32)
        pooled = s[HALO:] / cnt - u[:, sl]
        outs.append(_dot(pooled.astype(BF16), w_pool_ref[g]))
    return jnp.concatenate(outs, axis=-1) * scale


def _mlp(x1, g_pre, g_post, w_up_ref, w_down_ref):
    hm = _rms(x1, g_pre).astype(BF16)
    acc = jnp.zeros(x1.shape, F32)
    for c in range(D_FF // FF_CHUNK):
        cs = slice(c * FF_CHUNK, (c + 1) * FF_CHUNK)
        a = jnp.square(jnp.maximum(_dot(hm, w_up_ref[:, cs]), 0.0)).astype(BF16)
        acc = acc + _dot(a, w_down_ref[cs, :])
    return x1 + _rms(acc, g_post)


def _mem_project_kernel(mem_ref, g_ref, w_ref, mk_ref, mv_ref, mkh_ref, mvh_ref):
    kv = _dot(_rms(mem_ref[0], g_ref[0]).astype(BF16), w_ref[0])
    for h in range(MEM_HEADS):
        mk = kv[:, h * MEM_HD:(h + 1) * MEM_HD]
        mv = kv[:, MEM_W + h * MEM_HD:MEM_W + (h + 1) * MEM_HD]
        mk_ref[0, 0, :, h, :] = mk
        mv_ref[0, 0, :, h, :] = mv
        mkh_ref[0, 0, h] = mk.astype(BF16)
        mvh_ref[0, 0, h] = mv.astype(BF16)


def _mem_project(mem, g_mem, w_mem_kv):
    b, n, _ = mem.shape
    depth = w_mem_kv.shape[0]
    out = jax.ShapeDtypeStruct((depth, b, n, MEM_HEADS, MEM_HD), F32)
    out_h = jax.ShapeDtypeStruct((depth, b, MEM_HEADS, n, MEM_HD), BF16)
    return pl.pallas_call(
        _mem_project_kernel,
        grid=(depth, b),
        in_specs=[pl.BlockSpec((1, n, D_MODEL), lambda l, i: (i, 0, 0)),
                  pl.BlockSpec((1, 1, D_MODEL), lambda l, i: (l, 0, 0)),
                  pl.BlockSpec((1, D_MODEL, 2 * MEM_W), lambda l, i: (l, 0, 0))],
        out_specs=([pl.BlockSpec((1, 1, n, MEM_HEADS, MEM_HD), lambda l, i: (l, i, 0, 0, 0))] * 2
                   + [pl.BlockSpec((1, 1, MEM_HEADS, n, MEM_HD), lambda l, i: (l, i, 0, 0, 0))] * 2),
        out_shape=[out, out, out_h, out_h],
        compiler_params=pltpu.CompilerParams(dimension_semantics=("arbitrary", "arbitrary")),
        name="mem_project",
    )(mem, g_mem.reshape(depth, 1, D_MODEL), w_mem_kv)


def _layer0_kernel(x_ref, prev_ref, mk_ref, mv_ref, g_pre, g_post, g_mpre, g_mpost,
                   w_in, w_out, w_pool, pscale, w_up, w_down,
                   y_ref, state_ref, halo_ref, *, nb, tt, pos_base):
    t = pl.program_id(1)

    @pl.when(t == 0)
    def _():
        halo_ref[...] = prev_ref[...]

    x = x_ref[...].reshape(nb * tt, D_MODEL)
    proj = _dot(_rms(x, g_pre[...]).astype(BF16), w_in[...])
    cats = []
    for b in range(nb):
        rows = slice(b * tt, (b + 1) * tt)
        u = proj[rows, :MIX_W]
        mix = _pool_mixer(u, halo_ref[b], pos_base + t * tt, w_pool, pscale[...])
        halo_ref[b] = u[tt - HALO:, :]

        @pl.when(t == pl.num_programs(1) - 1)
        def _():
            state_ref[0, b] = u[tt - POOL_BUF:, :]

        mem = _mem_attention(proj[rows, MIX_W:], mk_ref, mv_ref, b)
        cats.append(jnp.concatenate([mix, mem], axis=-1))
    cat = cats[0] if nb == 1 else jnp.concatenate(cats, axis=0)
    x1 = x + _rms(_dot(cat.astype(BF16), w_out[...]), g_post[...])
    y = _mlp(x1, g_mpre[...], g_mpost[...], w_up, w_down)
    y_ref[...] = y.reshape(nb, tt, D_MODEL)


def _layer0(x, prev, mem_k, mem_v, p, *, nb, tt, pos_base):
    b, t, _ = x.shape
    tile = pl.BlockSpec((nb, tt, D_MODEL), lambda i, j: (i, j, 0))
    return pl.pallas_call(
        functools.partial(_layer0_kernel, nb=nb, tt=tt, pos_base=pos_base),
        grid=(b // nb, t // tt),
        in_specs=[tile, pl.BlockSpec((nb, HALO, MIX_W), lambda i, j: (i, 0, 0)),
                  _mem_spec(mem_k, 0, nb), _mem_spec(mem_v, 0, nb),
                  _layer_spec(p["g_mix_pre"], 0), _layer_spec(p["g_mix_post"], 0),
                  _layer_spec(p["g_mlp_pre"], 0), _layer_spec(p["g_mlp_post"], 0),
                  _layer_spec(p["w_in"], 0), _layer_spec(p["w_out"], 0),
                  _layer_spec(p["w_pool"], 0), _layer_spec(p["pool_scale"], 0),
                  _layer_spec(p["w_up"], 0), _layer_spec(p["w_down"], 0)],
        out_specs=[tile, pl.BlockSpec((1, nb, POOL_BUF, MIX_W), lambda i, j: (0, i, 0, 0))],
        out_shape=[jax.ShapeDtypeStruct((b, t, D_MODEL), F32),
                   jax.ShapeDtypeStruct((1, b, POOL_BUF, MIX_W), F32)],
        scratch_shapes=[pltpu.VMEM((nb, HALO, MIX_W), F32)],
        compiler_params=pltpu.CompilerParams(dimension_semantics=("arbitrary", "arbitrary"),
                                             vmem_limit_bytes=VMEM_LIMIT),
        name="layer0",
    )(x, prev, mem_k, mem_v, p["g_mix_pre"], p["g_mix_post"], p["g_mlp_pre"], p["g_mlp_post"],
      p["w_in"], p["w_out"], p["w_pool"], p["pool_scale"], p["w_up"], p["w_down"])


def _bf16_part(x):
    return x.astype(BF16).astype(F32)


def _fold_operands(q, k, v, c):
    cl = c * LOG2E
    hi = _bf16_part(cl)
    mid = _bf16_part(cl - hi)
    lo = _bf16_part(cl - hi - mid)
    lane = lax.broadcasted_iota(jnp.int32, (1, V7X_LANES), 1)
    piece = lax.rem(lane + (3 - FOLD_SRC % 3), 3)
    d = jnp.where(piece == 0, hi, jnp.where(piece == 1, mid, lo))
    qs, ks, vs = [], [], []
    for h in range(FOX_HEADS):
        own = (h % 2) * FOX_HD
        e0 = FOX_HD - own
        cols = slice((h // 2) * PAIR_W, (h // 2 + 1) * PAIR_W)
        dh = pltpu.roll(d, (e0 - (FOLD_SRC + FOLD_W * h)) % V7X_LANES, axis=1)
        in_head = (lane >= own) & (lane < own + FOX_HD)
        first = (lane >= e0) & (lane < e0 + 3)
        second = (lane >= e0 + 3) & (lane < e0 + 6)
        xq = jnp.where(first, dh, jnp.where(second, 1.0, 0.0))
        xk = jnp.where(first, 1.0, jnp.where(second, -dh, 0.0))
        xv = jnp.where(lane == e0, 1.0, 0.0)
        qs.append(jnp.where(in_head, q[:, cols] * (FOX_HD ** -0.5 * LOG2E), xq).T.astype(BF16))
        ks.append(jnp.where(in_head, k[:, cols], xk).astype(BF16))
        vs.append(jnp.where(in_head, v[:, cols], xv).T.astype(BF16))
    return jnp.concatenate(qs, axis=0), jnp.concatenate(ks, axis=-1), jnp.concatenate(vs, axis=0)


def _layer1_front_kernel(x_ref, mk_ref, mv_ref, g_kv, g_pre, w_kv, w_f, w_ft, b_f, b_ft, w_in, *refs,
                         nb, tt, fold):
    if fold:
        k_ref, v_ref, logft_ref, mem_ref, qa_ref, ka_ref, va_ref, carry_ref = refs
    else:
        k_ref, v_ref, logft_ref, mem_ref, q_ref, kb_ref, vb_ref, c_ref, ct_ref = refs

    x = x_ref[...].reshape(nb * tt, D_MODEL)
    hk = _rms(x, g_kv[...]).astype(BF16)
    kv = _dot(hk, w_kv[...])
    k = kv[:, :MIX_W]
    v = kv[:, MIX_W:]
    k_ref[...] = k.reshape(nb, tt, MIX_W)
    v_ref[...] = v.reshape(nb, tt, MIX_W)
    logf = _log_sigmoid(_dot(hk, w_f[...]) + b_f[...])
    proj = _dot(_rms(x, g_pre[...]).astype(BF16), w_in[...])
    q = proj[:, :MIX_W]
    c = _cumsum(logf, 0, tt)
    if fold:
        @pl.when(pl.program_id(1) == 0)
        def _():
            carry_ref[...] = jnp.zeros_like(carry_ref)

        c = c + carry_ref[...]
        carry_ref[...] = c[tt - 1:, :]
        qa, ka, va = _fold_operands(q, k, v, c)
        qa_ref[0, 0] = qa
        ka_ref[0] = ka
        va_ref[0, 0] = va
        logft_ref[0] = logf.T[:FOX_HEADS, :]
    else:
        q_ref[...] = (q * (FOX_HD ** -0.5)).astype(BF16).reshape(nb, tt, MIX_W)
        kb_ref[...] = k.astype(BF16).reshape(nb, tt, MIX_W)
        vb_ref[...] = v.astype(BF16).reshape(nb, tt, MIX_W)
        logf_t = _log_sigmoid(_dot_nt(w_ft[...], hk) + b_ft[...])
        logft_ref[0] = logf_t
        ct_ref[0, 0] = _cumsum(logf_t, 1, tt)
    for b in range(nb):
        rows = slice(b * tt, (b + 1) * tt)
        if not fold:
            c_ref[b] = c[rows, :FOX_HEADS]
        mem_ref[b] = _mem_attention(proj[rows, MIX_W:], mk_ref, mv_ref, b).astype(BF16)


def _layer1_front(x, mem_k, mem_v, p, *, nb, tt, fold):
    b, t, _ = x.shape
    assert (nb == 1) if fold else (t == tt)
    tile = lambda w: pl.BlockSpec((nb, tt, w), lambda i, j: (i, j, 0))
    act = lambda w, dt: jax.ShapeDtypeStruct((b, t, w), dt)
    out_specs = [tile(MIX_W), tile(MIX_W), pl.BlockSpec((1, FOX_HEADS, nb * tt), lambda i, j: (i, 0, j)), tile(MEM_W)]
    out_shape = [act(MIX_W, F32), act(MIX_W, F32), jax.ShapeDtypeStruct((b // nb, FOX_HEADS, nb * t), F32),
                 act(MEM_W, BF16)]
    if fold:
        wf = FOX_HEADS * PAIR_W
        tile_t = pl.BlockSpec((1, 1, wf, tt), lambda i, j: (i, j, 0, 0))
        act_t = jax.ShapeDtypeStruct((b, t // tt, wf, tt), BF16)
        out_specs += [tile_t, tile(wf), tile_t]
        out_shape += [act_t, act(wf, BF16), act_t]
        scratch = [pltpu.VMEM((1, V7X_LANES), F32)]
    else:
        out_specs += [tile(MIX_W)] * 3 + [tile(FOX_HEADS),
                                          pl.BlockSpec((1, 1, FOX_HEADS, nb * tt), lambda i, j: (i, j, 0, 0))]
        out_shape += [act(MIX_W, BF16)] * 3 + [act(FOX_HEADS, F32),
                                               jax.ShapeDtypeStruct((b // nb, t // tt, FOX_HEADS, nb * tt), F32)]
        scratch = []
    return pl.pallas_call(
        functools.partial(_layer1_front_kernel, nb=nb, tt=tt, fold=fold),
        grid=(b // nb, t // tt),
        in_specs=[tile(D_MODEL), _mem_spec(mem_k, 1, nb), _mem_spec(mem_v, 1, nb),
                  _const_spec(p["g_kv"].shape), _layer_spec(p["g_mix_pre"], 1),
                  _const_spec(p["w_kv"].shape), _const_spec(p["w_f"].shape), _const_spec(p["w_ft"].shape),
                  _const_spec(p["b_f"].shape), _const_spec(p["b_ft"].shape), _layer_spec(p["w_in"], 1)],
        out_specs=out_specs,
        out_shape=out_shape,
        scratch_shapes=scratch,
        compiler_params=pltpu.CompilerParams(dimension_semantics=("arbitrary", "arbitrary"),
                                             vmem_limit_bytes=VMEM_LIMIT),
        name="layer1_front",
    )(x, mem_k, mem_v, p["g_kv"], p["g_mix_pre"], p["w_kv"], p["w_f"], p["w_ft"], p["b_f"], p["b_ft"], p["w_in"])


def _fox_tile(qm, cq, k, v, ck, carry, mask, *, keys_on_lanes):
    m, l, acc = carry
    s = (_dot(qm, k) if keys_on_lanes else _dot_nt(qm, k)) + cq - ck
    if mask is not None:
        s = jnp.where(mask, s, NEG_BIG)
    m_new = jnp.maximum(m, jnp.max(s, axis=-1, keepdims=True))
    alpha = jnp.exp(m - m_new)
    p = jnp.exp(s - m_new)
    l = alpha * l + jnp.sum(p, axis=-1, keepdims=True)
    pv = _dot_nt(p.astype(BF16), v) if keys_on_lanes else _dot(p.astype(BF16), v)
    return m_new, l, alpha * acc + pv


def _fox_prompt_kernel(qt_ref, k_ref, vt_ref, o_ref, *, tq, tk):
    i = pl.program_id(1)
    n_full = (i * tq) // tk
    sub = lax.broadcasted_iota(jnp.int32, (PAIR_W, 1), 0)
    causal = (n_full * tk + lax.broadcasted_iota(jnp.int32, (tk, tq), 0)
              <= i * tq + lax.broadcasted_iota(jnp.int32, (tk, tq), 1))
    for g in range(FOX_HEADS // FOX_GROUP):
        heads = range(g * FOX_GROUP, (g + 1) * FOX_GROUP)
        col = [slice(h * PAIR_W, (h + 1) * PAIR_W) for h in heads]
        qt = [qt_ref[0, 0, c, :] for c in col]

        def tile(j, carry, mask):
            rows = pl.ds(pl.multiple_of(j * tk, tk), tk)
            out = []
            logits = lambda n: _dot(k_ref[0, rows, col[n]], qt[n])
            ahead = [logits(n) for n in range(min(FOX_AHEAD, FOX_GROUP))]
            for n in range(FOX_GROUP):
                m, acc = carry[n]
                s = ahead.pop(0)
                if n + FOX_AHEAD < FOX_GROUP:
                    ahead.append(logits(n + FOX_AHEAD))
                if mask is not None:
                    s = jnp.where(mask, s, NEG_BIG)
                m_new = jnp.maximum(m, jnp.max(s, axis=0, keepdims=True))
                p = jnp.exp2(s - m_new).astype(BF16)
                out.append((m_new, jnp.exp2(m - m_new) * acc + _dot(vt_ref[0, j, col[n], :], p)))
            return tuple(out)

        init = ((jnp.full((1, tq), NEG_BIG, F32), jnp.zeros((PAIR_W, tq), F32)),) * FOX_GROUP
        carry = lax.fori_loop(0, n_full, lambda j, c: tile(j, c, None), init)
        carry = tile(n_full, carry, causal)
        outs = []
        for h, (_, acc) in zip(heads, carry):
            denom = FOX_HD * (1 - h % 2)
            outs.append(acc / acc[denom:denom + 1, :])
        for n in range(0, FOX_GROUP, 2):
            pair = (g * FOX_GROUP + n) // 2
            o_ref[0, :, pair * PAIR_W:(pair + 1) * PAIR_W] = (
                jnp.where(sub < FOX_HD, outs[n], outs[n + 1]).T.astype(BF16))


def _fox_prompt(qt, ka, vt, *, tq):
    b, t, w = ka.shape
    tk = qt.shape[3]
    assert tk % tq == 0
    per_stream = lambda shape: pl.BlockSpec((1,) + shape, lambda i, j: (i,) + (0,) * len(shape),
                                            pipeline_mode=pl.Buffered(1))
    return pl.pallas_call(
        functools.partial(_fox_prompt_kernel, tq=tq, tk=tk),
        grid=(b, t // tq),
        in_specs=[pl.BlockSpec((1, 1, w, tq), lambda i, j: (i, (j * tq) // tk, 0, j % (tk // tq))),
                  per_stream((t, w)), per_stream((t // tk, w, tk))],
        out_specs=pl.BlockSpec((1, tq, MIX_W), lambda i, j: (i, j, 0)),
        out_shape=jax.ShapeDtypeStruct((b, t, MIX_W), BF16),
        compiler_params=pltpu.CompilerParams(dimension_semantics=("arbitrary", "arbitrary"),
                                             vmem_limit_bytes=VMEM_LIMIT),
        name="fox_prompt",
    )(qt, ka, vt)


def _fox_sample_kernel(q_ref, kc_ref, vc_ref, kn_ref, vn_ref, lfc_t_ref, c_ref, ct_ref, o_ref):
    tn = q_ref.shape[1]
    p = lfc_t_ref.shape[2]
    cc = _cumsum(lfc_t_ref[0], 1, p)
    ck_cache = cc - cc[:, p - 1:]
    causal = (lax.broadcasted_iota(jnp.int32, (tn, V7X_LANES), 1)
              <= lax.broadcasted_iota(jnp.int32, (tn, V7X_LANES), 0))
    pad = jnp.zeros((V7X_LANES - tn, FOX_HD), BF16)
    outs = []
    for h in range(FOX_HEADS):
        cols = slice(h * FOX_HD, (h + 1) * FOX_HD)
        qh = q_ref[0, :, cols]
        kn = jnp.concatenate([kn_ref[0, :, cols], pad], axis=0)
        vn = jnp.concatenate([vn_ref[0, :, cols], pad], axis=0)
        cq = c_ref[0, :, h:h + 1]
        init = (jnp.full((tn, 1), NEG_BIG, F32), jnp.zeros((tn, 1), F32), jnp.zeros((tn, FOX_HD), F32))
        carry = _fox_tile(qh, cq, kc_ref[0, h].astype(BF16), vc_ref[0, h].astype(BF16),
                          ck_cache[h:h + 1, :], init, None, keys_on_lanes=True)
        _, l, acc = _fox_tile(qh, cq, kn, vn, ct_ref[0, h:h + 1, :], carry, causal, keys_on_lanes=False)
        outs.append(acc / l)
    o_ref[0] = jnp.concatenate(outs, axis=-1).astype(BF16)


def _fox_sample(q, cache_kt, cache_vt, kb, vb, cache_logf_t, c, ct):
    b, tn, _ = q.shape
    p = cache_kt.shape[3]
    new = lambda w: pl.BlockSpec((1, tn, w), lambda i: (i, 0, 0))
    old = pl.BlockSpec((1, FOX_HEADS, FOX_HD, p), lambda i: (i, 0, 0, 0))
    return pl.pallas_call(
        _fox_sample_kernel,
        grid=(b,),
        in_specs=[new(MIX_W), old, old, new(MIX_W), new(MIX_W),
                  pl.BlockSpec((1, FOX_HEADS, p), lambda i: (i, 0, 0)),
                  new(FOX_HEADS), pl.BlockSpec((1, FOX_HEADS, V7X_LANES), lambda i: (i, 0, 0))],
        out_specs=new(MIX_W),
        out_shape=jax.ShapeDtypeStruct((b, tn, MIX_W), BF16),
        compiler_params=pltpu.CompilerParams(dimension_semantics=("arbitrary",), vmem_limit_bytes=VMEM_LIMIT),
        name="fox_sample",
    )(q, cache_kt, cache_vt, kb, vb, cache_logf_t, c, ct)


def _layer1_back_kernel(x_ref, mix_ref, mem_ref, g_post, g_mpre, g_mpost, w_out, w_up, w_down, y_ref, *, nb, tt):
    m = nb * tt
    x = x_ref[...].reshape(m, D_MODEL)
    o = (_dot(mix_ref[...].reshape(m, MIX_W), w_out[:MIX_W, :])
         + _dot(mem_ref[...].reshape(m, MEM_W), w_out[MIX_W:, :]))
    x1 = x + _rms(o, g_post[...])
    y_ref[...] = _mlp(x1, g_mpre[...], g_mpost[...], w_up, w_down).reshape(nb, tt, D_MODEL)


def _layer1_back(x, mix, mem, p, *, nb, tt):
    b, t, _ = x.shape
    tile = lambda w: pl.BlockSpec((nb, tt, w), lambda i, j: (i, j, 0))
    return pl.pallas_call(
        functools.partial(_layer1_back_kernel, nb=nb, tt=tt),
        grid=(b // nb, t // tt),
        in_specs=[tile(D_MODEL), tile(MIX_W), tile(MEM_W),
                  _layer_spec(p["g_mix_post"], 1), _layer_spec(p["g_mlp_pre"], 1), _layer_spec(p["g_mlp_post"], 1),
                  _layer_spec(p["w_out"], 1), _layer_spec(p["w_up"], 1), _layer_spec(p["w_down"], 1)],
        out_specs=tile(D_MODEL),
        out_shape=jax.ShapeDtypeStruct((b, t, D_MODEL), F32),
        compiler_params=pltpu.CompilerParams(dimension_semantics=("arbitrary", "arbitrary"),
                                             vmem_limit_bytes=VMEM_LIMIT),
        name="layer1_back",
    )(x, mix, mem, p["g_mix_post"], p["g_mlp_pre"], p["g_mlp_post"], p["w_out"], p["w_up"], p["w_down"])


def _token_tile(b, t):
    if t >= 512:
        return 1, 512
    return b, t


def _trunk(x, pool_prev, cache, mem_k, mem_v, p):
    b, t, _ = x.shape
    nb, tt = _token_tile(b, t)
    pos_base = 0 if cache is None else POOL_BUF
    y0, pool_state = _layer0(x, pool_prev, mem_k, mem_v, p, nb=nb, tt=tt, pos_base=pos_base)
    k, v, logf_t, mem, *att = _layer1_front(y0, mem_k, mem_v, p, nb=nb, tt=tt, fold=cache is None)
    logf = jnp.transpose(logf_t.reshape(b // nb, FOX_HEADS, nb, t), (0, 2, 3, 1)).reshape(b, t, FOX_HEADS)
    if cache is None:
        mix = _fox_prompt(*att, tq=FOX_TQ)
    else:
        q, kb, vb, c, ct = att
        cache_k, cache_v, cache_logf = cache
        ct_new = jnp.swapaxes(ct.reshape(FOX_HEADS, b, t), 0, 1)
        ct_new = jnp.pad(ct_new, ((0, 0), (0, 0), (0, V7X_LANES - t)))
        to_lanes = lambda a: jnp.transpose(a, (0, 2, 3, 1))
        mix = _fox_sample(q, to_lanes(cache_k), to_lanes(cache_v), kb, vb, jnp.swapaxes(cache_logf, 1, 2), c, ct_new)
    y = _layer1_back(y0, mix, mem, p, nb=nb, tt=tt)
    return (y, pool_state, k.reshape(b, t, FOX_HEADS, FOX_HD), v.reshape(b, t, FOX_HEADS, FOX_HD), logf)


def kernel(x_prompt, x_sample, cache_pool, cache_k, cache_v, cache_logf, cache_mem_k, cache_mem_v, mem_prompt,
           g_mix_pre, g_mix_post, g_mlp_pre, g_mlp_post, w_in, w_out, w_pool, pool_scale, g_kv, w_kvf, b_f,
           g_mem, w_mem_kv, w_up, w_down):
    depth = w_in.shape[0]
    assert depth == 2 and w_pool.shape[0] == 1, "one pooling layer followed by one forgetting-attention layer"
    bp = x_prompt.shape[0]
    w_f = w_kvf[:, 2 * MIX_W:]
    fold_cols = lambda a: jnp.pad(jnp.concatenate([a, jnp.repeat(a, FOLD_W, axis=-1)], axis=-1),
                                  [(0, 0)] * (a.ndim - 1) + [(0, V7X_LANES - FOLD_SRC - FOLD_W * FOX_HEADS)])
    rows = lambda g: g.reshape(g.shape[0], 1, g.shape[1])
    p = dict(
        g_mix_pre=rows(g_mix_pre), g_mix_post=rows(g_mix_post), g_mlp_pre=rows(g_mlp_pre),
        g_mlp_post=rows(g_mlp_post), pool_scale=rows(pool_scale), g_kv=g_kv.reshape(1, D_MODEL),
        w_in=w_in.astype(BF16), w_out=w_out.astype(BF16), w_pool=w_pool.astype(BF16),
        w_up=w_up.astype(BF16), w_down=w_down.astype(BF16), w_kv=w_kvf[:, :2 * MIX_W].astype(BF16),
        w_f=fold_cols(w_f).astype(BF16), w_ft=w_f.T.astype(BF16),
        b_f=fold_cols(b_f).reshape(1, V7X_LANES), b_ft=b_f.reshape(FOX_HEADS, 1),
    )

    mem_k_prompt, mem_v_prompt, mkh, mvh = _mem_project(mem_prompt, g_mem, w_mem_kv.astype(BF16))

    y_p, pool_p, k_p, v_p, logf_p = _trunk(x_prompt, jnp.zeros((bp, HALO, MIX_W), F32), None, mkh, mvh, p)

    pool_prev = jnp.pad(cache_pool[0], ((0, 0), (HALO - POOL_BUF, 0), (0, 0)))
    head_major = lambda a: jnp.swapaxes(a.astype(BF16), 2, 3)
    y_s, pool_s, k_s, v_s, logf_s = _trunk(x_sample, pool_prev, (cache_k, cache_v, cache_logf),
                                           head_major(cache_mem_k), head_major(cache_mem_v), p)

    return (y_p, y_s, pool_p, pool_s, k_p, v_p, logf_p, k_s, v_s, logf_s, mem_k_prompt, mem_v_prompt)
```

```python
import functools

import jax
import jax.numpy as jnp
from jax import lax
from jax.experimental import pallas as pl
from jax.experimental.pallas import tpu as pltpu

F32 = jnp.float32
BF16 = jnp.bfloat16

D_MODEL = 1024
MIX_W = D_MODEL // 2
MEM_W = D_MODEL - MIX_W
POOL_WINDOWS = (2, 4, 8, 16)
POOL_GROUP = MIX_W // len(POOL_WINDOWS)
POOL_BUF = max(POOL_WINDOWS) - 1
FOX_HEADS = 8
FOX_HD = MIX_W // FOX_HEADS
MEM_HEADS = 4
MEM_HD = MEM_W // MEM_HEADS
D_FF = 4 * D_MODEL
EPS = 1e-6

V7X_LANES = 128
V7X_SUBLANES_F32 = 8
V7X_VMEM_BYTES = 64 * 1024 * 1024

HALO = 2 * V7X_SUBLANES_F32
FF_CHUNK = 1024
PAIR_W = 2 * FOX_HD
NEG_BIG = -1e30
LOG2E = 1.4426950408889634
FOLD_SRC = FOX_HEADS
FOLD_W = 6
FOX_AHEAD = 2
FOX_TQ = 512
assert FOLD_SRC + FOLD_W * FOX_HEADS <= V7X_LANES
VMEM_LIMIT = V7X_VMEM_BYTES - 8 * 1024 * 1024

assert HALO >= POOL_BUF and PAIR_W == V7X_LANES and POOL_GROUP == V7X_LANES and MEM_HD == V7X_LANES


def _const_spec(shape):
    zeros = (0,) * len(shape)
    return pl.BlockSpec(shape, lambda *_: zeros, pipeline_mode=pl.Buffered(1))


def _layer_spec(stacked, layer):
    index = (layer,) + (0,) * (stacked.ndim - 1)
    return pl.BlockSpec((None,) + stacked.shape[1:], lambda *_: index, pipeline_mode=pl.Buffered(1))


def _mem_spec(mem, layer, nb):
    return pl.BlockSpec((None, nb) + mem.shape[2:], lambda i, j: (layer, i, 0, 0, 0))


def _rms(x, g):
    return x * lax.rsqrt(jnp.mean(x * x, axis=-1, keepdims=True) + EPS) * g


def _dot(a, b):
    return jnp.dot(a, b, preferred_element_type=F32)


def _dot_nt(a, b):
    return lax.dot_general(a, b, (((1,), (1,)), ((), ())), preferred_element_type=F32)


def _log_sigmoid(x):
    return jnp.minimum(x, 0.0) - jnp.log1p(jnp.exp(-jnp.abs(x)))


def _cumsum(x, axis, seg):
    assert seg & (seg - 1) == 0 and x.shape[axis] % seg == 0
    idx = lax.broadcasted_iota(jnp.int32, x.shape, axis) & (seg - 1)
    k = 1
    while k < seg:
        x = x + jnp.where(idx >= k, pltpu.roll(x, k, axis=axis), 0.0)
        k *= 2
    return x


def _mem_attention(q, mk_ref, mv_ref, b):
    outs = []
    for h in range(MEM_HEADS):
        sl = slice(h * MEM_HD, (h + 1) * MEM_HD)
        s = _dot_nt(q[:, sl].astype(BF16), mk_ref[b, h]) * (MEM_HD ** -0.5)
        m = jnp.max(s, axis=-1, keepdims=True)
        p = jnp.exp(s - m)
        l = jnp.sum(p, axis=-1, keepdims=True)
        outs.append(_dot(p.astype(BF16), mv_ref[b, h]) / l)
    return jnp.concatenate(outs, axis=-1)


def _pool_mixer(u, halo, pos0, w_pool_ref, scale):
    tt = u.shape[0]
    ext = jnp.concatenate([halo, u], axis=0)
    pos = pos0 + lax.broadcasted_iota(jnp.int32, (tt, 1), 0)
    outs = []
    for g, w in enumerate(POOL_WINDOWS):
        sl = slice(g * POOL_GROUP, (g + 1) * POOL_GROUP)
        s = ext[:, sl]
        k = 1
        while k < w:
            s = s + pltpu.roll(s, k, axis=0)
            k *= 2
        cnt = jnp.minimum(pos + 1, w).astype(F32)
        pooled = s[HALO:] / cnt - u[:, sl]
        outs.append(_dot(pooled.astype(BF16), w_pool_ref[g]))
    return jnp.concatenate(outs, axis=-1) * scale


def _mlp(x1, g_pre, g_post, w_up_ref, w_down_ref):
    hm = _rms(x1, g_pre).astype(BF16)
    acc = jnp.zeros(x1.shape, F32)
    for c in range(D_FF // FF_CHUNK):
        cs = slice(c * FF_CHUNK, (c + 1) * FF_CHUNK)
        a = jnp.square(jnp.maximum(_dot(hm, w_up_ref[:, cs]), 0.0)).astype(BF16)
        acc = acc + _dot(a, w_down_ref[cs, :])
    return x1 + _rms(acc, g_post)


def _mem_project_kernel(mem_ref, g_ref, w_ref, mk_ref, mv_ref, mkh_ref, mvh_ref):
    kv = _dot(_rms(mem_ref[0], g_ref[0]).astype(BF16), w_ref[0])
    for h in range(MEM_HEADS):
        mk = kv[:, h * MEM_HD:(h + 1) * MEM_HD]
        mv = kv[:, MEM_W + h * MEM_HD:MEM_W + (h + 1) * MEM_HD]
        mk_ref[0, 0, :, h, :] = mk
        mv_ref[0, 0, :, h, :] = mv
        mkh_ref[0, 0, h] = mk.astype(BF16)
        mvh_ref[0, 0, h] = mv.astype(BF16)


def _mem_project(mem, g_mem, w_mem_kv):
    b, n, _ = mem.shape
    depth = w_mem_kv.shape[0]
    out = jax.ShapeDtypeStruct((depth, b, n, MEM_HEADS, MEM_HD), F32)
    out_h = jax.ShapeDtypeStruct((depth, b, MEM_HEADS, n, MEM_HD), BF16)
    return pl.pallas_call(
        _mem_project_kernel,
        grid=(depth, b),
        in_specs=[pl.BlockSpec((1, n, D_MODEL), lambda l, i: (i, 0, 0)),
                  pl.BlockSpec((1, 1, D_MODEL), lambda l, i: (l, 0, 0)),
                  pl.BlockSpec((1, D_MODEL, 2 * MEM_W), lambda l, i: (l, 0, 0))],
        out_specs=([pl.BlockSpec((1, 1, n, MEM_HEADS, MEM_HD), lambda l, i: (l, i, 0, 0, 0))] * 2
                   + [pl.BlockSpec((1, 1, MEM_HEADS, n, MEM_HD), lambda l, i: (l, i, 0, 0, 0))] * 2),
        out_shape=[out, out, out_h, out_h],
        compiler_params=pltpu.CompilerParams(dimension_semantics=("arbitrary", "arbitrary")),
        name="mem_project",
    )(mem, g_mem.reshape(depth, 1, D_MODEL), w_mem_kv)


def _layer0_kernel(x_ref, prev_ref, mk_ref, mv_ref, g_pre, g_post, g_mpre, g_mpost,
                   w_in, w_out, w_pool, pscale, w_up, w_down,
                   y_ref, state_ref, halo_ref, *, nb, tt, pos_base):
    t = pl.program_id(1)

    @pl.when(t == 0)
    def _():
        halo_ref[...] = prev_ref[...]

    x = x_ref[...].reshape(nb * tt, D_MODEL)
    proj = _dot(_rms(x, g_pre[...]).astype(BF16), w_in[...])
    cats = []
    for b in range(nb):
        rows = slice(b * tt, (b + 1) * tt)
        u = proj[rows, :MIX_W]
        mix = _pool_mixer(u, halo_ref[b], pos_base + t * tt, w_pool, pscale[...])
        halo_ref[b] = u[tt - HALO:, :]

        @pl.when(t == pl.num_programs(1) - 1)
        def _():
            state_ref[0, b] = u[tt - POOL_BUF:, :]

        mem = _mem_attention(proj[rows, MIX_W:], mk_ref, mv_ref, b)
        cats.append(jnp.concatenate([mix, mem], axis=-1))
    cat = cats[0] if nb == 1 else jnp.concatenate(cats, axis=0)
    x1 = x + _rms(_dot(cat.astype(BF16), w_out[...]), g_post[...])
    y = _mlp(x1, g_mpre[...], g_mpost[...], w_up, w_down)
    y_ref[...] = y.reshape(nb, tt, D_MODEL)


def _layer0(x, prev, mem_k, mem_v, p, *, nb, tt, pos_base):
    b, t, _ = x.shape
    tile = pl.BlockSpec((nb, tt, D_MODEL), lambda i, j: (i, j, 0))
    return pl.pallas_call(
        functools.partial(_layer0_kernel, nb=nb, tt=tt, pos_base=pos_base),
        grid=(b // nb, t // tt),
        in_specs=[tile, pl.BlockSpec((nb, HALO, MIX_W), lambda i, j: (i, 0, 0)),
                  _mem_spec(mem_k, 0, nb), _mem_spec(mem_v, 0, nb),
                  _layer_spec(p["g_mix_pre"], 0), _layer_spec(p["g_mix_post"], 0),
                  _layer_spec(p["g_mlp_pre"], 0), _layer_spec(p["g_mlp_post"], 0),
                  _layer_spec(p["w_in"], 0), _layer_spec(p["w_out"], 0),
                  _layer_spec(p["w_pool"], 0), _layer_spec(p["pool_scale"], 0),
                  _layer_spec(p["w_up"], 0), _layer_spec(p["w_down"], 0)],
        out_specs=[tile, pl.BlockSpec((1, nb, POOL_BUF, MIX_W), lambda i, j: (0, i, 0, 0))],
        out_shape=[jax.ShapeDtypeStruct((b, t, D_MODEL), F32),
                   jax.ShapeDtypeStruct((1, b, POOL_BUF, MIX_W), F32)],
        scratch_shapes=[pltpu.VMEM((nb, HALO, MIX_W), F32)],
        compiler_params=pltpu.CompilerParams(dimension_semantics=("arbitrary", "arbitrary"),
                                             vmem_limit_bytes=VMEM_LIMIT),
        name="layer0",
    )(x, prev, mem_k, mem_v, p["g_mix_pre"], p["g_mix_post"], p["g_mlp_pre"], p["g_mlp_post"],
      p["w_in"], p["w_out"], p["w_pool"], p["pool_scale"], p["w_up"], p["w_down"])


def _bf16_part(x):
    return x.astype(BF16).astype(F32)


def _fold_operands(q, k, v, c):
    cl = c * LOG2E
    hi = _bf16_part(cl)
    mid = _bf16_part(cl - hi)
    lo = _bf16_part(cl - hi - mid)
    lane = lax.broadcasted_iota(jnp.int32, (1, V7X_LANES), 1)
    piece = lax.rem(lane + (3 - FOLD_SRC % 3), 3)
    d = jnp.where(piece == 0, hi, jnp.where(piece == 1, mid, lo))
    qs, ks, vs = [], [], []
    for h in range(FOX_HEADS):
        own = (h % 2) * FOX_HD
        e0 = FOX_HD - own
        cols = slice((h // 2) * PAIR_W, (h // 2 + 1) * PAIR_W)
        dh = pltpu.roll(d, (e0 - (FOLD_SRC + FOLD_W * h)) % V7X_LANES, axis=1)
        in_head = (lane >= own) & (lane < own + FOX_HD)
        first = (lane >= e0) & (lane < e0 + 3)
        second = (lane >= e0 + 3) & (lane < e0 + 6)
        xq = jnp.where(first, dh, jnp.where(second, 1.0, 0.0))
        xk = jnp.where(first, 1.0, jnp.where(second, -dh, 0.0))
        xv = jnp.where(lane == e0, 1.0, 0.0)
        qs.append(jnp.where(in_head, q[:, cols] * (FOX_HD ** -0.5 * LOG2E), xq).T.astype(BF16))
        ks.append(jnp.where(in_head, k[:, cols], xk).astype(BF16))
        vs.append(jnp.where(in_head, v[:, cols], xv).T.astype(BF16))
    return jnp.concatenate(qs, axis=0), jnp.concatenate(ks, axis=-1), jnp.concatenate(vs, axis=0)


def _layer1_front_kernel(x_ref, mk_ref, mv_ref, g_kv, g_pre, w_kv, w_f, w_ft, b_f, b_ft, w_in, *refs,
                         nb, tt, fold):
    if fold:
        k_ref, v_ref, logft_ref, mem_ref, qa_ref, ka_ref, va_ref, carry_ref = refs
    else:
        k_ref, v_ref, logft_ref, mem_ref, q_ref, kb_ref, vb_ref, c_ref, ct_ref = refs

    x = x_ref[...].reshape(nb * tt, D_MODEL)
    hk = _rms(x, g_kv[...]).astype(BF16)
    kv = _dot(hk, w_kv[...])
    k = kv[:, :MIX_W]
    v = kv[:, MIX_W:]
    k_ref[...] = k.reshape(nb, tt, MIX_W)
    v_ref[...] = v.reshape(nb, tt, MIX_W)
    logf = _log_sigmoid(_dot(hk, w_f[...]) + b_f[...])
    proj = _dot(_rms(x, g_pre[...]).astype(BF16), w_in[...])
    q = proj[:, :MIX_W]
    c = _cumsum(logf, 0, tt)
    if fold:
        @pl.when(pl.program_id(1) == 0)
        def _():
            carry_ref[...] = jnp.zeros_like(carry_ref)

        c = c + carry_ref[...]
        carry_ref[...] = c[tt - 1:, :]
        qa, ka, va = _fold_operands(q, k, v, c)
        qa_ref[0, 0] = qa
        ka_ref[0] = ka
        va_ref[0, 0] = va
        logft_ref[0] = logf.T[:FOX_HEADS, :]
    else:
        q_ref[...] = (q * (FOX_HD ** -0.5)).astype(BF16).reshape(nb, tt, MIX_W)
        kb_ref[...] = k.astype(BF16).reshape(nb, tt, MIX_W)
        vb_ref[...] = v.astype(BF16).reshape(nb, tt, MIX_W)
        logf_t = _log_sigmoid(_dot_nt(w_ft[...], hk) + b_ft[...])
        logft_ref[0] = logf_t
        ct_ref[0, 0] = _cumsum(logf_t, 1, tt)
    for b in range(nb):
        rows = slice(b * tt, (b + 1) * tt)
        if not fold:
            c_ref[b] = c[rows, :FOX_HEADS]
        mem_ref[b] = _mem_attention(proj[rows, MIX_W:], mk_ref, mv_ref, b).astype(BF16)


def _layer1_front(x, mem_k, mem_v, p, *, nb, tt, fold):
    b, t, _ = x.shape
    assert (nb == 1) if fold else (t == tt)
    tile = lambda w: pl.BlockSpec((nb, tt, w), lambda i, j: (i, j, 0))
    act = lambda w, dt: jax.ShapeDtypeStruct((b, t, w), dt)
    out_specs = [tile(MIX_W), tile(MIX_W), pl.BlockSpec((1, FOX_HEADS, nb * tt), lambda i, j: (i, 0, j)), tile(MEM_W)]
    out_shape = [act(MIX_W, F32), act(MIX_W, F32), jax.ShapeDtypeStruct((b // nb, FOX_HEADS, nb * t), F32),
                 act(MEM_W, BF16)]
    if fold:
        wf = FOX_HEADS * PAIR_W
        tile_t = pl.BlockSpec((1, 1, wf, tt), lambda i, j: (i, j, 0, 0))
        act_t = jax.ShapeDtypeStruct((b, t // tt, wf, tt), BF16)
        out_specs += [tile_t, tile(wf), tile_t]
        out_shape += [act_t, act(wf, BF16), act_t]
        scratch = [pltpu.VMEM((1, V7X_LANES), F32)]
    else:
        out_specs += [tile(MIX_W)] * 3 + [tile(FOX_HEADS),
                                          pl.BlockSpec((1, 1, FOX_HEADS, nb * tt), lambda i, j: (i, j, 0, 0))]
        out_shape += [act(MIX_W, BF16)] * 3 + [act(FOX_HEADS, F32),
                                               jax.ShapeDtypeStruct((b // nb, t // tt, FOX_HEADS, nb * tt), F32)]
        scratch = []
    return pl.pallas_call(
        functools.partial(_layer1_front_kernel, nb=nb, tt=tt, fold=fold),
        grid=(b // nb, t // tt),
        in_specs=[tile(D_MODEL), _mem_spec(mem_k, 1, nb), _mem_spec(mem_v, 1, nb),
                  _const_spec(p["g_kv"].shape), _layer_spec(p["g_mix_pre"], 1),
                  _const_spec(p["w_kv"].shape), _const_spec(p["w_f"].shape), _const_spec(p["w_ft"].shape),
                  _const_spec(p["b_f"].shape), _const_spec(p["b_ft"].shape), _layer_spec(p["w_in"], 1)],
        out_specs=out_specs,
        out_shape=out_shape,
        scratch_shapes=scratch,
        compiler_params=pltpu.CompilerParams(dimension_semantics=("arbitrary", "arbitrary"),
                                             vmem_limit_bytes=VMEM_LIMIT),
        name="layer1_front",
    )(x, mem_k, mem_v, p["g_kv"], p["g_mix_pre"], p["w_kv"], p["w_f"], p["w_ft"], p["b_f"], p["b_ft"], p["w_in"])


def _fox_tile(qm, cq, k, v, ck, carry, mask, *, keys_on_lanes):
    m, l, acc = carry
    s = (_dot(qm, k) if keys_on_lanes else _dot_nt(qm, k)) + cq - ck
    if mask is not None:
        s = jnp.where(mask, s, NEG_BIG)
    m_new = jnp.maximum(m, jnp.max(s, axis=-1, keepdims=True))
    alpha = jnp.exp(m - m_new)
    p = jnp.exp(s - m_new)
    l = alpha * l + jnp.sum(p, axis=-1, keepdims=True)
    pv = _dot_nt(p.astype(BF16), v) if keys_on_lanes else _dot(p.astype(BF16), v)
    return m_new, l, alpha * acc + pv


def _fox_prompt_kernel(qt_ref, k_ref, vt_ref, o_ref, m_ref, acc_ref, *s_refs, tq, tk):
    i = pl.program_id(1)
    n_full = (i * tq) // tk
    sub = lax.broadcasted_iota(jnp.int32, (PAIR_W, 1), 0)
    causal = (n_full * tk + lax.broadcasted_iota(jnp.int32, (tk, tq), 0)
              <= i * tq + lax.broadcasted_iota(jnp.int32, (tk, tq), 1))
    col = [slice(h * PAIR_W, (h + 1) * PAIR_W) for h in range(FOX_HEADS)]
    qt = [qt_ref[0, 0, c, :] for c in col]
    m_ref[...] = jnp.full(m_ref.shape, NEG_BIG, F32)
    acc_ref[...] = jnp.zeros(acc_ref.shape, F32)

    def tile(j, mask):
        rows = pl.ds(pl.multiple_of(j * tk, tk), tk)

        def logits(h):
            s = _dot(k_ref[0, rows, col[h]], qt[h])
            if mask is not None:
                s = jnp.where(mask, s, NEG_BIG)
            s_refs[h % len(s_refs)][...] = s
            return jnp.max(s, axis=0, keepdims=True)

        top = [logits(h) for h in range(FOX_AHEAD)]
        for h in range(FOX_HEADS):
            if h + FOX_AHEAD < FOX_HEADS:
                top.append(logits(h + FOX_AHEAD))
            m = m_ref[h]
            m_new = jnp.maximum(m, top[h])
            p = jnp.exp2(s_refs[h % len(s_refs)][...] - m_new).astype(BF16)
            acc_ref[h] = jnp.exp2(m - m_new) * acc_ref[h] + _dot(vt_ref[0, j, col[h], :], p)
            m_ref[h] = m_new

    lax.fori_loop(0, n_full, lambda j, c: tile(j, None), None)
    tile(n_full, causal)
    for pair in range(FOX_HEADS // 2):
        outs = []
        for h in (2 * pair, 2 * pair + 1):
            acc = acc_ref[h]
            denom = FOX_HD * (1 - h % 2)
            outs.append(acc / acc[denom:denom + 1, :])
        o_ref[0, :, pair * PAIR_W:(pair + 1) * PAIR_W] = jnp.where(sub < FOX_HD, outs[0], outs[1]).T.astype(BF16)


def _fox_prompt(qt, ka, vt, *, tq):
    b, t, w = ka.shape
    tk = qt.shape[3]
    assert tk % tq == 0
    per_stream = lambda shape: pl.BlockSpec((1,) + shape, lambda i, j: (i,) + (0,) * len(shape),
                                            pipeline_mode=pl.Buffered(1))
    return pl.pallas_call(
        functools.partial(_fox_prompt_kernel, tq=tq, tk=tk),
        grid=(b, t // tq),
        in_specs=[pl.BlockSpec((1, 1, w, tq), lambda i, j: (i, (j * tq) // tk, 0, j % (tk // tq))),
                  per_stream((t, w)), per_stream((t // tk, w, tk))],
        out_specs=pl.BlockSpec((1, tq, MIX_W), lambda i, j: (i, j, 0)),
        out_shape=jax.ShapeDtypeStruct((b, t, MIX_W), BF16),
        scratch_shapes=([pltpu.VMEM((FOX_HEADS, 1, tq), F32), pltpu.VMEM((FOX_HEADS, PAIR_W, tq), F32)]
                        + [pltpu.VMEM((tk, tq), F32)] * (FOX_AHEAD + 1)),
        compiler_params=pltpu.CompilerParams(dimension_semantics=("arbitrary", "arbitrary"),
                                             vmem_limit_bytes=VMEM_LIMIT),
        name="fox_prompt",
    )(qt, ka, vt)


def _fox_sample_kernel(q_ref, kc_ref, vc_ref, kn_ref, vn_ref, lfc_t_ref, c_ref, ct_ref, o_ref):
    tn = q_ref.shape[1]
    p = lfc_t_ref.shape[2]
    cc = _cumsum(lfc_t_ref[0], 1, p)
    ck_cache = cc - cc[:, p - 1:]
    causal = (lax.broadcasted_iota(jnp.int32, (tn, V7X_LANES), 1)
              <= lax.broadcasted_iota(jnp.int32, (tn, V7X_LANES), 0))
    pad = jnp.zeros((V7X_LANES - tn, FOX_HD), BF16)
    outs = []
    for h in range(FOX_HEADS):
        cols = slice(h * FOX_HD, (h + 1) * FOX_HD)
        qh = q_ref[0, :, cols]
        kn = jnp.concatenate([kn_ref[0, :, cols], pad], axis=0)
        vn = jnp.concatenate([vn_ref[0, :, cols], pad], axis=0)
        cq = c_ref[0, :, h:h + 1]
        init = (jnp.full((tn, 1), NEG_BIG, F32), jnp.zeros((tn, 1), F32), jnp.zeros((tn, FOX_HD), F32))
        carry = _fox_tile(qh, cq, kc_ref[0, h].astype(BF16), vc_ref[0, h].astype(BF16),
                          ck_cache[h:h + 1, :], init, None, keys_on_lanes=True)
        _, l, acc = _fox_tile(qh, cq, kn, vn, ct_ref[0, h:h + 1, :], carry, causal, keys_on_lanes=False)
        outs.append(acc / l)
    o_ref[0] = jnp.concatenate(outs, axis=-1).astype(BF16)


def _fox_sample(q, cache_kt, cache_vt, kb, vb, cache_logf_t, c, ct):
    b, tn, _ = q.shape
    p = cache_kt.shape[3]
    new = lambda w: pl.BlockSpec((1, tn, w), lambda i: (i, 0, 0))
    old = pl.BlockSpec((1, FOX_HEADS, FOX_HD, p), lambda i: (i, 0, 0, 0))
    return pl.pallas_call(
        _fox_sample_kernel,
        grid=(b,),
        in_specs=[new(MIX_W), old, old, new(MIX_W), new(MIX_W),
                  pl.BlockSpec((1, FOX_HEADS, p), lambda i: (i, 0, 0)),
                  new(FOX_HEADS), pl.BlockSpec((1, FOX_HEADS, V7X_LANES), lambda i: (i, 0, 0))],
        out_specs=new(MIX_W),
        out_shape=jax.ShapeDtypeStruct((b, tn, MIX_W), BF16),
        compiler_params=pltpu.CompilerParams(dimension_semantics=("arbitrary",), vmem_limit_bytes=VMEM_LIMIT),
        name="fox_sample",
    )(q, cache_kt, cache_vt, kb, vb, cache_logf_t, c, ct)


def _layer1_back_kernel(x_ref, mix_ref, mem_ref, g_post, g_mpre, g_mpost, w_out, w_up, w_down, y_ref, *, nb, tt):
    m = nb * tt
    x = x_ref[...].reshape(m, D_MODEL)
    o = (_dot(mix_ref[...].reshape(m, MIX_W), w_out[:MIX_W, :])
         + _dot(mem_ref[...].reshape(m, MEM_W), w_out[MIX_W:, :]))
    x1 = x + _rms(o, g_post[...])
    y_ref[...] = _mlp(x1, g_mpre[...], g_mpost[...], w_up, w_down).reshape(nb, tt, D_MODEL)


def _layer1_back(x, mix, mem, p, *, nb, tt):
    b, t, _ = x.shape
    tile = lambda w: pl.BlockSpec((nb, tt, w), lambda i, j: (i, j, 0))
    return pl.pallas_call(
        functools.partial(_layer1_back_kernel, nb=nb, tt=tt),
        grid=(b // nb, t // tt),
        in_specs=[tile(D_MODEL), tile(MIX_W), tile(MEM_W),
                  _layer_spec(p["g_mix_post"], 1), _layer_spec(p["g_mlp_pre"], 1), _layer_spec(p["g_mlp_post"], 1),
                  _layer_spec(p["w_out"], 1), _layer_spec(p["w_up"], 1), _layer_spec(p["w_down"], 1)],
        out_specs=tile(D_MODEL),
        out_shape=jax.ShapeDtypeStruct((b, t, D_MODEL), F32),
        compiler_params=pltpu.CompilerParams(dimension_semantics=("arbitrary", "arbitrary"),
                                             vmem_limit_bytes=VMEM_LIMIT),
        name="layer1_back",
    )(x, mix, mem, p["g_mix_post"], p["g_mlp_pre"], p["g_mlp_post"], p["w_out"], p["w_up"], p["w_down"])


def _token_tile(b, t):
    if t >= 512:
        return 1, 512
    return b, t


def _trunk(x, pool_prev, cache, mem_k, mem_v, p):
    b, t, _ = x.shape
    nb, tt = _token_tile(b, t)
    pos_base = 0 if cache is None else POOL_BUF
    y0, pool_state = _layer0(x, pool_prev, mem_k, mem_v, p, nb=nb, tt=tt, pos_base=pos_base)
    k, v, logf_t, mem, *att = _layer1_front(y0, mem_k, mem_v, p, nb=nb, tt=tt, fold=cache is None)
    logf = jnp.transpose(logf_t.reshape(b // nb, FOX_HEADS, nb, t), (0, 2, 3, 1)).reshape(b, t, FOX_HEADS)
    if cache is None:
        mix = _fox_prompt(*att, tq=FOX_TQ)
    else:
        q, kb, vb, c, ct = att
        cache_k, cache_v, cache_logf = cache
        ct_new = jnp.swapaxes(ct.reshape(FOX_HEADS, b, t), 0, 1)
        ct_new = jnp.pad(ct_new, ((0, 0), (0, 0), (0, V7X_LANES - t)))
        to_lanes = lambda a: jnp.transpose(a, (0, 2, 3, 1))
        mix = _fox_sample(q, to_lanes(cache_k), to_lanes(cache_v), kb, vb, jnp.swapaxes(cache_logf, 1, 2), c, ct_new)
    y = _layer1_back(y0, mix, mem, p, nb=nb, tt=tt)
    return (y, pool_state, k.reshape(b, t, FOX_HEADS, FOX_HD), v.reshape(b, t, FOX_HEADS, FOX_HD), logf)


def kernel(x_prompt, x_sample, cache_pool, cache_k, cache_v, cache_logf, cache_mem_k, cache_mem_v, mem_prompt,
           g_mix_pre, g_mix_post, g_mlp_pre, g_mlp_post, w_in, w_out, w_pool, pool_scale, g_kv, w_kvf, b_f,
           g_mem, w_mem_kv, w_up, w_down):
    depth = w_in.shape[0]
    assert depth == 2 and w_pool.shape[0] == 1, "one pooling layer followed by one forgetting-attention layer"
    bp = x_prompt.shape[0]
    w_f = w_kvf[:, 2 * MIX_W:]
    fold_cols = lambda a: jnp.pad(jnp.concatenate([a, jnp.repeat(a, FOLD_W, axis=-1)], axis=-1),
                                  [(0, 0)] * (a.ndim - 1) + [(0, V7X_LANES - FOLD_SRC - FOLD_W * FOX_HEADS)])
    rows = lambda g: g.reshape(g.shape[0], 1, g.shape[1])
    p = dict(
        g_mix_pre=rows(g_mix_pre), g_mix_post=rows(g_mix_post), g_mlp_pre=rows(g_mlp_pre),
        g_mlp_post=rows(g_mlp_post), pool_scale=rows(pool_scale), g_kv=g_kv.reshape(1, D_MODEL),
        w_in=w_in.astype(BF16), w_out=w_out.astype(BF16), w_pool=w_pool.astype(BF16),
        w_up=w_up.astype(BF16), w_down=w_down.astype(BF16), w_kv=w_kvf[:, :2 * MIX_W].astype(BF16),
        w_f=fold_cols(w_f).astype(BF16), w_ft=w_f.T.astype(BF16),
        b_f=fold_cols(b_f).reshape(1, V7X_LANES), b_ft=b_f.reshape(FOX_HEADS, 1),
    )

    mem_k_prompt, mem_v_prompt, mkh, mvh = _mem_project(mem_prompt, g_mem, w_mem_kv.astype(BF16))

    y_p, pool_p, k_p, v_p, logf_p = _trunk(x_prompt, jnp.zeros((bp, HALO, MIX_W), F32), None, mkh, mvh, p)

    pool_prev = jnp.pad(cache_pool[0], ((0, 0), (HALO - POOL_BUF, 0), (0, 0)))
    head_major = lambda a: jnp.swapaxes(a.astype(BF16), 2, 3)
    y_s, pool_s, k_s, v_s, logf_s = _trunk(x_sample, pool_prev, (cache_k, cache_v, cache_logf),
                                           head_major(cache_mem_k), head_major(cache_mem_v), p)

    return (y_p, y_s, pool_p, pool_s, k_p, v_p, logf_p, k_s, v_s, logf_s, mem_k_prompt, mem_v_prompt)
```

```python
import functools

import jax
import jax.numpy as jnp
from jax import lax
from jax.experimental import pallas as pl
from jax.experimental.pallas import tpu as pltpu

F32 = jnp.float32
BF16 = jnp.bfloat16

D_MODEL = 1024
MIX_W = D_MODEL // 2
MEM_W = D_MODEL - MIX_W
POOL_WINDOWS = (2, 4, 8, 16)
POOL_GROUP = MIX_W // len(POOL_WINDOWS)
POOL_BUF = max(POOL_WINDOWS) - 1
FOX_HEADS = 8
FOX_HD = MIX_W // FOX_HEADS
MEM_HEADS = 4
MEM_HD = MEM_W // MEM_HEADS
D_FF = 4 * D_MODEL
EPS = 1e-6

V7X_LANES = 128
V7X_SUBLANES_F32 = 8
V7X_VMEM_BYTES = 64 * 1024 * 1024

HALO = 2 * V7X_SUBLANES_F32
FF_CHUNK = 1024
PAIR_W = 2 * FOX_HD
NEG_BIG = -1e30
LOG2E = 1.4426950408889634
FOLD_SRC = FOX_HEADS
FOLD_W = 6
FOX_AHEAD = 2
FOX_TQ = 512
assert FOLD_SRC + FOLD_W * FOX_HEADS <= V7X_LANES
VMEM_LIMIT = V7X_VMEM_BYTES - 8 * 1024 * 1024

assert HALO >= POOL_BUF and PAIR_W == V7X_LANES and POOL_GROUP == V7X_LANES and MEM_HD == V7X_LANES


def _const_spec(shape):
    zeros = (0,) * len(shape)
    return pl.BlockSpec(shape, lambda *_: zeros, pipeline_mode=pl.Buffered(1))


def _layer_spec(stacked, layer):
    index = (layer,) + (0,) * (stacked.ndim - 1)
    return pl.BlockSpec((None,) + stacked.shape[1:], lambda *_: index, pipeline_mode=pl.Buffered(1))


def _mem_spec(mem, layer, nb):
    return pl.BlockSpec((None, nb) + mem.shape[2:], lambda i, j: (layer, i, 0, 0, 0))


def _unit_rms(x):
    return x * lax.rsqrt(jnp.mean(x * x, axis=-1, keepdims=True) + EPS)


def _rms(x, g):
    return _unit_rms(x) * g


def _dot(a, b):
    return jnp.dot(a, b, preferred_element_type=F32)


def _dot_nt(a, b):
    return lax.dot_general(a, b, (((1,), (1,)), ((), ())), preferred_element_type=F32)


def _log_sigmoid(x):
    return jnp.minimum(x, 0.0) - jnp.log1p(jnp.exp(-jnp.abs(x)))


def _cumsum(x, axis, seg):
    assert seg & (seg - 1) == 0 and x.shape[axis] % seg == 0
    idx = lax.broadcasted_iota(jnp.int32, x.shape, axis) & (seg - 1)
    k = 1
    while k < seg:
        x = x + jnp.where(idx >= k, pltpu.roll(x, k, axis=axis), 0.0)
        k *= 2
    return x


def _mem_attention(q, mk_ref, mv_ref, b):
    logits = [_dot_nt(q[:, h * MEM_HD:(h + 1) * MEM_HD].astype(BF16), mk_ref[b, h]) * (MEM_HD ** -0.5)
              for h in range(MEM_HEADS)]
    probs = []
    for s in logits:
        p = jnp.exp(s - jnp.max(s, axis=-1, keepdims=True))
        probs.append((p.astype(BF16), jnp.sum(p, axis=-1, keepdims=True)))
    return jnp.concatenate([_dot(p, mv_ref[b, h]) / l for h, (p, l) in enumerate(probs)], axis=-1)


def _pool_mixer(u, halo, pos0, w_pool_ref, scale):
    tt = u.shape[0]
    ext = jnp.concatenate([halo, u], axis=0)
    pos = pos0 + lax.broadcasted_iota(jnp.int32, (tt, 1), 0)
    outs = []
    for g, w in enumerate(POOL_WINDOWS):
        sl = slice(g * POOL_GROUP, (g + 1) * POOL_GROUP)
        s = ext[:, sl]
        k = 1
        while k < w:
            s = s + pltpu.roll(s, k, axis=0)
            k *= 2
        cnt = jnp.minimum(pos + 1, w).astype(F32)
        pooled = s[HALO:] / cnt - u[:, sl]
        outs.append(_dot(pooled.astype(BF16), w_pool_ref[g]))
    return jnp.concatenate(outs, axis=-1) * scale


def _mlp(x1, g_pre, g_post, w_up_ref, w_down_ref):
    hm = _rms(x1, g_pre).astype(BF16)
    acc = jnp.zeros(x1.shape, F32)
    for c in range(D_FF // FF_CHUNK):
        cs = slice(c * FF_CHUNK, (c + 1) * FF_CHUNK)
        a = jnp.square(jnp.maximum(_dot(hm, w_up_ref[:, cs]), 0.0)).astype(BF16)
        acc = acc + _dot(a, w_down_ref[cs, :])
    return x1 + _rms(acc, g_post)


def _mem_project_kernel(mem_ref, g_ref, w_ref, mk_ref, mv_ref, mkh_ref, mvh_ref):
    kv = _dot(_rms(mem_ref[0], g_ref[0]).astype(BF16), w_ref[0])
    for h in range(MEM_HEADS):
        mk = kv[:, h * MEM_HD:(h + 1) * MEM_HD]
        mv = kv[:, MEM_W + h * MEM_HD:MEM_W + (h + 1) * MEM_HD]
        mk_ref[0, 0, :, h, :] = mk
        mv_ref[0, 0, :, h, :] = mv
        mkh_ref[0, 0, h] = mk.astype(BF16)
        mvh_ref[0, 0, h] = mv.astype(BF16)


def _mem_project(mem, g_mem, w_mem_kv):
    b, n, _ = mem.shape
    depth = w_mem_kv.shape[0]
    out = jax.ShapeDtypeStruct((depth, b, n, MEM_HEADS, MEM_HD), F32)
    out_h = jax.ShapeDtypeStruct((depth, b, MEM_HEADS, n, MEM_HD), BF16)
    return pl.pallas_call(
        _mem_project_kernel,
        grid=(depth, b),
        in_specs=[pl.BlockSpec((1, n, D_MODEL), lambda l, i: (i, 0, 0)),
                  pl.BlockSpec((1, 1, D_MODEL), lambda l, i: (l, 0, 0)),
                  pl.BlockSpec((1, D_MODEL, 2 * MEM_W), lambda l, i: (l, 0, 0))],
        out_specs=([pl.BlockSpec((1, 1, n, MEM_HEADS, MEM_HD), lambda l, i: (l, i, 0, 0, 0))] * 2
                   + [pl.BlockSpec((1, 1, MEM_HEADS, n, MEM_HD), lambda l, i: (l, i, 0, 0, 0))] * 2),
        out_shape=[out, out, out_h, out_h],
        compiler_params=pltpu.CompilerParams(dimension_semantics=("arbitrary", "arbitrary")),
        name="mem_project",
    )(mem, g_mem.reshape(depth, 1, D_MODEL), w_mem_kv)


def _layer0_kernel(x_ref, prev_ref, mk_ref, mv_ref, g_pre, g_post, g_mpre, g_mpost,
                   w_in, w_out, w_pool, pscale, w_up, w_down,
                   y_ref, state_ref, halo_ref, *, nb, tt, pos_base):
    t = pl.program_id(1)

    @pl.when(t == 0)
    def _():
        halo_ref[...] = prev_ref[...]

    x = x_ref[...].reshape(nb * tt, D_MODEL)
    proj = _dot(_rms(x, g_pre[...]).astype(BF16), w_in[...])
    cats = []
    for b in range(nb):
        rows = slice(b * tt, (b + 1) * tt)
        u = proj[rows, :MIX_W]
        mix = _pool_mixer(u, halo_ref[b], pos_base + t * tt, w_pool, pscale[...])
        halo_ref[b] = u[tt - HALO:, :]

        @pl.when(t == pl.num_programs(1) - 1)
        def _():
            state_ref[0, b] = u[tt - POOL_BUF:, :]

        mem = _mem_attention(proj[rows, MIX_W:], mk_ref, mv_ref, b)
        cats.append(jnp.concatenate([mix, mem], axis=-1))
    cat = cats[0] if nb == 1 else jnp.concatenate(cats, axis=0)
    x1 = x + _rms(_dot(cat.astype(BF16), w_out[...]), g_post[...])
    y = _mlp(x1, g_mpre[...], g_mpost[...], w_up, w_down)
    y_ref[...] = y.reshape(nb, tt, D_MODEL)


def _layer0(x, prev, mem_k, mem_v, p, *, nb, tt, pos_base):
    b, t, _ = x.shape
    tile = pl.BlockSpec((nb, tt, D_MODEL), lambda i, j: (i, j, 0))
    return pl.pallas_call(
        functools.partial(_layer0_kernel, nb=nb, tt=tt, pos_base=pos_base),
        grid=(b // nb, t // tt),
        in_specs=[tile, pl.BlockSpec((nb, HALO, MIX_W), lambda i, j: (i, 0, 0)),
                  _mem_spec(mem_k, 0, nb), _mem_spec(mem_v, 0, nb),
                  _layer_spec(p["g_mix_pre"], 0), _layer_spec(p["g_mix_post"], 0),
                  _layer_spec(p["g_mlp_pre"], 0), _layer_spec(p["g_mlp_post"], 0),
                  _layer_spec(p["w_in"], 0), _layer_spec(p["w_out"], 0),
                  _layer_spec(p["w_pool"], 0), _layer_spec(p["pool_scale"], 0),
                  _layer_spec(p["w_up"], 0), _layer_spec(p["w_down"], 0)],
        out_specs=[tile, pl.BlockSpec((1, nb, POOL_BUF, MIX_W), lambda i, j: (0, i, 0, 0))],
        out_shape=[jax.ShapeDtypeStruct((b, t, D_MODEL), F32),
                   jax.ShapeDtypeStruct((1, b, POOL_BUF, MIX_W), F32)],
        scratch_shapes=[pltpu.VMEM((nb, HALO, MIX_W), F32)],
        compiler_params=pltpu.CompilerParams(dimension_semantics=("arbitrary", "arbitrary"),
                                             vmem_limit_bytes=VMEM_LIMIT),
        name="layer0",
    )(x, prev, mem_k, mem_v, p["g_mix_pre"], p["g_mix_post"], p["g_mlp_pre"], p["g_mlp_post"],
      p["w_in"], p["w_out"], p["w_pool"], p["pool_scale"], p["w_up"], p["w_down"])


def _bf16_part(x):
    return x.astype(BF16).astype(F32)


def _fold_operands(q, k, v, c):
    cl = c * LOG2E
    hi = _bf16_part(cl)
    mid = _bf16_part(cl - hi)
    lo = _bf16_part(cl - hi - mid)
    lane = lax.broadcasted_iota(jnp.int32, (1, V7X_LANES), 1)
    piece = lax.rem(lane + (3 - FOLD_SRC % 3), 3)
    d = jnp.where(piece == 0, hi, jnp.where(piece == 1, mid, lo))
    qs, ks, vs = [], [], []
    for h in range(FOX_HEADS):
        own = (h % 2) * FOX_HD
        e0 = FOX_HD - own
        cols = slice((h // 2) * PAIR_W, (h // 2 + 1) * PAIR_W)
        dh = pltpu.roll(d, (e0 - (FOLD_SRC + FOLD_W * h)) % V7X_LANES, axis=1)
        in_head = (lane >= own) & (lane < own + FOX_HD)
        first = ((lane >= e0) & (lane < e0 + 3)).astype(F32)
        second = ((lane >= e0 + 3) & (lane < e0 + 6)).astype(F32)
        xq = dh * first + second
        xk = first - dh * second
        xv = (lane == e0).astype(F32)
        qs.append(jnp.where(in_head, q[:, cols] * (FOX_HD ** -0.5 * LOG2E), xq).T.astype(BF16))
        ks.append(jnp.where(in_head, k[:, cols], xk).astype(BF16))
        vs.append(jnp.where(in_head, v[:, cols], xv).T.astype(BF16))
    return jnp.concatenate(qs, axis=0), jnp.concatenate(ks, axis=-1), jnp.concatenate(vs, axis=0)


def _layer1_front_kernel(x_ref, mk_ref, mv_ref, g_kv, g_pre, w_kv, w_f, w_ft, b_f, b_ft, w_in, *refs,
                         nb, tt, fold):
    if fold:
        k_ref, v_ref, logft_ref, mem_ref, qa_ref, ka_ref, va_ref, carry_ref = refs
    else:
        k_ref, v_ref, logft_ref, mem_ref, q_ref, kb_ref, vb_ref, c_ref, ct_ref = refs

    xn = _unit_rms(x_ref[...].reshape(nb * tt, D_MODEL))
    hk = (xn * g_kv[...]).astype(BF16)
    kv = _dot(hk, w_kv[...])
    k = kv[:, :MIX_W]
    v = kv[:, MIX_W:]
    k_ref[...] = k.reshape(nb, tt, MIX_W)
    v_ref[...] = v.reshape(nb, tt, MIX_W)
    logf = _log_sigmoid(_dot(hk, w_f[...]) + b_f[...])
    proj = _dot((xn * g_pre[...]).astype(BF16), w_in[...])
    q = proj[:, :MIX_W]
    c = _cumsum(logf, 0, tt)
    if fold:
        @pl.when(pl.program_id(1) == 0)
        def _():
            carry_ref[...] = jnp.zeros_like(carry_ref)

        c = c + carry_ref[...]
        carry_ref[...] = c[tt - 1:, :]
        qa, ka, va = _fold_operands(q, k, v, c)
        qa_ref[0, 0] = qa
        ka_ref[0] = ka
        va_ref[0, 0] = va
        logft_ref[0] = logf.T[:FOX_HEADS, :]
    else:
        q_ref[...] = (q * (FOX_HD ** -0.5)).astype(BF16).reshape(nb, tt, MIX_W)
        kb_ref[...] = k.astype(BF16).reshape(nb, tt, MIX_W)
        vb_ref[...] = v.astype(BF16).reshape(nb, tt, MIX_W)
        logf_t = _log_sigmoid(_dot_nt(w_ft[...], hk) + b_ft[...])
        logft_ref[0] = logf_t
        ct_ref[0, 0] = _cumsum(logf_t, 1, tt)
    for b in range(nb):
        rows = slice(b * tt, (b + 1) * tt)
        if not fold:
            c_ref[b] = c[rows, :FOX_HEADS]
        mem_ref[b] = _mem_attention(proj[rows, MIX_W:], mk_ref, mv_ref, b).astype(BF16)


def _layer1_front(x, mem_k, mem_v, p, *, nb, tt, fold):
    b, t, _ = x.shape
    assert (nb == 1) if fold else (t == tt)
    tile = lambda w: pl.BlockSpec((nb, tt, w), lambda i, j: (i, j, 0))
    act = lambda w, dt: jax.ShapeDtypeStruct((b, t, w), dt)
    out_specs = [tile(MIX_W), tile(MIX_W), pl.BlockSpec((1, FOX_HEADS, nb * tt), lambda i, j: (i, 0, j)), tile(MEM_W)]
    out_shape = [act(MIX_W, F32), act(MIX_W, F32), jax.ShapeDtypeStruct((b // nb, FOX_HEADS, nb * t), F32),
                 act(MEM_W, BF16)]
    if fold:
        wf = FOX_HEADS * PAIR_W
        tile_t = pl.BlockSpec((1, 1, wf, tt), lambda i, j: (i, j, 0, 0))
        act_t = jax.ShapeDtypeStruct((b, t // tt, wf, tt), BF16)
        out_specs += [tile_t, tile(wf), tile_t]
        out_shape += [act_t, act(wf, BF16), act_t]
        scratch = [pltpu.VMEM((1, V7X_LANES), F32)]
    else:
        out_specs += [tile(MIX_W)] * 3 + [tile(FOX_HEADS),
                                          pl.BlockSpec((1, 1, FOX_HEADS, nb * tt), lambda i, j: (i, j, 0, 0))]
        out_shape += [act(MIX_W, BF16)] * 3 + [act(FOX_HEADS, F32),
                                               jax.ShapeDtypeStruct((b // nb, t // tt, FOX_HEADS, nb * tt), F32)]
        scratch = []
    return pl.pallas_call(
        functools.partial(_layer1_front_kernel, nb=nb, tt=tt, fold=fold),
        grid=(b // nb, t // tt),
        in_specs=[tile(D_MODEL), _mem_spec(mem_k, 1, nb), _mem_spec(mem_v, 1, nb),
                  _const_spec(p["g_kv"].shape), _layer_spec(p["g_mix_pre"], 1),
                  _const_spec(p["w_kv"].shape), _const_spec(p["w_f"].shape), _const_spec(p["w_ft"].shape),
                  _const_spec(p["b_f"].shape), _const_spec(p["b_ft"].shape), _layer_spec(p["w_in"], 1)],
        out_specs=out_specs,
        out_shape=out_shape,
        scratch_shapes=scratch,
        compiler_params=pltpu.CompilerParams(dimension_semantics=("arbitrary", "arbitrary"),
                                             vmem_limit_bytes=VMEM_LIMIT),
        name="layer1_front",
    )(x, mem_k, mem_v, p["g_kv"], p["g_mix_pre"], p["w_kv"], p["w_f"], p["w_ft"], p["b_f"], p["b_ft"], p["w_in"])


def _fox_prompt_kernel(qt_ref, k_ref, vt_ref, o_ref, m_ref, acc_ref, *s_refs, tq, tk):
    i = pl.program_id(1)
    n_full = (i * tq) // tk
    sub = lax.broadcasted_iota(jnp.int32, (PAIR_W, 1), 0)
    causal = (n_full * tk + lax.broadcasted_iota(jnp.int32, (tk, tq), 0)
              <= i * tq + lax.broadcasted_iota(jnp.int32, (tk, tq), 1))
    col = [slice(h * PAIR_W, (h + 1) * PAIR_W) for h in range(FOX_HEADS)]
    qt = [qt_ref[0, 0, c, :] for c in col]
    m_ref[...] = jnp.full(m_ref.shape, NEG_BIG, F32)
    acc_ref[...] = jnp.zeros(acc_ref.shape, F32)

    def tile(j, mask):
        rows = pl.ds(pl.multiple_of(j * tk, tk), tk)

        def logits(h):
            s = _dot(k_ref[0, rows, col[h]], qt[h])
            if mask is not None:
                s = jnp.where(mask, s, NEG_BIG)
            s_refs[h % len(s_refs)][...] = s
            return jnp.max(s, axis=0, keepdims=True)

        top = [logits(h) for h in range(FOX_AHEAD)]
        for h in range(FOX_HEADS):
            if h + FOX_AHEAD < FOX_HEADS:
                top.append(logits(h + FOX_AHEAD))
            m = m_ref[h]
            m_new = jnp.maximum(m, top[h])
            p = jnp.exp2(s_refs[h % len(s_refs)][...] - m_new).astype(BF16)
            acc_ref[h] = jnp.exp2(m - m_new) * acc_ref[h] + _dot(vt_ref[0, j, col[h], :], p)
            m_ref[h] = m_new

    lax.fori_loop(0, n_full, lambda j, c: tile(j, None), None)
    tile(n_full, causal)
    for pair in range(FOX_HEADS // 2):
        outs = []
        for h in (2 * pair, 2 * pair + 1):
            acc = acc_ref[h]
            denom = FOX_HD * (1 - h % 2)
            outs.append(acc / acc[denom:denom + 1, :])
        o_ref[0, :, pair * PAIR_W:(pair + 1) * PAIR_W] = jnp.where(sub < FOX_HD, outs[0], outs[1]).T.astype(BF16)


def _fox_prompt(qt, ka, vt, *, tq):
    b, t, w = ka.shape
    tk = qt.shape[3]
    assert tk % tq == 0
    per_stream = lambda shape: pl.BlockSpec((1,) + shape, lambda i, j: (i,) + (0,) * len(shape))
    return pl.pallas_call(
        functools.partial(_fox_prompt_kernel, tq=tq, tk=tk),
        grid=(b, t // tq),
        in_specs=[pl.BlockSpec((1, 1, w, tq), lambda i, j: (i, (j * tq) // tk, 0, j % (tk // tq))),
                  per_stream((t, w)), per_stream((t // tk, w, tk))],
        out_specs=pl.BlockSpec((1, tq, MIX_W), lambda i, j: (i, j, 0)),
        out_shape=jax.ShapeDtypeStruct((b, t, MIX_W), BF16),
        scratch_shapes=([pltpu.VMEM((FOX_HEADS, 1, tq), F32), pltpu.VMEM((FOX_HEADS, PAIR_W, tq), F32)]
                        + [pltpu.VMEM((tk, tq), F32)] * (FOX_AHEAD + 1)),
        compiler_params=pltpu.CompilerParams(dimension_semantics=("arbitrary", "arbitrary"),
                                             vmem_limit_bytes=VMEM_LIMIT),
        name="fox_prompt",
    )(qt, ka, vt)


def _fox_sample_kernel(q_ref, kc_ref, vc_ref, kn_ref, vn_ref, lfc_t_ref, c_ref, ct_ref, o_ref):
    tn = q_ref.shape[1]
    p = lfc_t_ref.shape[2]
    cc = _cumsum(lfc_t_ref[0], 1, p)
    ck_cache = cc - cc[:, p - 1:]
    causal = (lax.broadcasted_iota(jnp.int32, (tn, V7X_LANES), 1)
              <= lax.broadcasted_iota(jnp.int32, (tn, V7X_LANES), 0))
    pad = jnp.zeros((V7X_LANES - tn, FOX_HD), BF16)
    col = [slice(h * FOX_HD, (h + 1) * FOX_HD) for h in range(FOX_HEADS)]
    logits = []
    for h in range(FOX_HEADS):
        qh = q_ref[0, :, col[h]]
        cq = c_ref[0, :, h:h + 1]
        kn = jnp.concatenate([kn_ref[0, :, col[h]], pad], axis=0)
        s_old = _dot(qh, kc_ref[0, h].astype(BF16)) + cq - ck_cache[h:h + 1, :]
        s_new = jnp.where(causal, _dot_nt(qh, kn) + cq - ct_ref[0, h:h + 1, :], NEG_BIG)
        logits.append((s_old, s_new))
    probs = []
    for s_old, s_new in logits:
        m = jnp.maximum(jnp.max(s_old, axis=-1, keepdims=True), jnp.max(s_new, axis=-1, keepdims=True))
        p_old = jnp.exp(s_old - m)
        p_new = jnp.exp(s_new - m)
        l = jnp.sum(p_old, axis=-1, keepdims=True) + jnp.sum(p_new, axis=-1, keepdims=True)
        probs.append((p_old.astype(BF16), p_new.astype(BF16), l))
    outs = []
    for h, (p_old, p_new, l) in enumerate(probs):
        vn = jnp.concatenate([vn_ref[0, :, col[h]], pad], axis=0)
        outs.append((_dot_nt(p_old, vc_ref[0, h].astype(BF16)) + _dot(p_new, vn)) / l)
    o_ref[0] = jnp.concatenate(outs, axis=-1).astype(BF16)


def _fox_sample(q, cache_kt, cache_vt, kb, vb, cache_logf_t, c, ct):
    b, tn, _ = q.shape
    p = cache_kt.shape[3]
    new = lambda w: pl.BlockSpec((1, tn, w), lambda i: (i, 0, 0))
    old = pl.BlockSpec((1, FOX_HEADS, FOX_HD, p), lambda i: (i, 0, 0, 0))
    return pl.pallas_call(
        _fox_sample_kernel,
        grid=(b,),
        in_specs=[new(MIX_W), old, old, new(MIX_W), new(MIX_W),
                  pl.BlockSpec((1, FOX_HEADS, p), lambda i: (i, 0, 0)),
                  new(FOX_HEADS), pl.BlockSpec((1, FOX_HEADS, V7X_LANES), lambda i: (i, 0, 0))],
        out_specs=new(MIX_W),
        out_shape=jax.ShapeDtypeStruct((b, tn, MIX_W), BF16),
        compiler_params=pltpu.CompilerParams(dimension_semantics=("arbitrary",), vmem_limit_bytes=VMEM_LIMIT),
        name="fox_sample",
    )(q, cache_kt, cache_vt, kb, vb, cache_logf_t, c, ct)


def _layer1_back_kernel(x_ref, mix_ref, mem_ref, g_post, g_mpre, g_mpost, w_out, w_up, w_down, y_ref, *, nb, tt):
    m = nb * tt
    x = x_ref[...].reshape(m, D_MODEL)
    o = (_dot(mix_ref[...].reshape(m, MIX_W), w_out[:MIX_W, :])
         + _dot(mem_ref[...].reshape(m, MEM_W), w_out[MIX_W:, :]))
    x1 = x + _rms(o, g_post[...])
    y_ref[...] = _mlp(x1, g_mpre[...], g_mpost[...], w_up, w_down).reshape(nb, tt, D_MODEL)


def _layer1_back(x, mix, mem, p, *, nb, tt):
    b, t, _ = x.shape
    tile = lambda w: pl.BlockSpec((nb, tt, w), lambda i, j: (i, j, 0))
    return pl.pallas_call(
        functools.partial(_layer1_back_kernel, nb=nb, tt=tt),
        grid=(b // nb, t // tt),
        in_specs=[tile(D_MODEL), tile(MIX_W), tile(MEM_W),
                  _layer_spec(p["g_mix_post"], 1), _layer_spec(p["g_mlp_pre"], 1), _layer_spec(p["g_mlp_post"], 1),
                  _layer_spec(p["w_out"], 1), _layer_spec(p["w_up"], 1), _layer_spec(p["w_down"], 1)],
        out_specs=tile(D_MODEL),
        out_shape=jax.ShapeDtypeStruct((b, t, D_MODEL), F32),
        compiler_params=pltpu.CompilerParams(dimension_semantics=("arbitrary", "arbitrary"),
                                             vmem_limit_bytes=VMEM_LIMIT),
        name="layer1_back",
    )(x, mix, mem, p["g_mix_post"], p["g_mlp_pre"], p["g_mlp_post"], p["w_out"], p["w_up"], p["w_down"])


def _token_tile(b, t):
    if t >= 512:
        return 1, 512
    return b, t


def _trunk(x, pool_prev, cache, mem_k, mem_v, p):
    b, t, _ = x.shape
    nb, tt = _token_tile(b, t)
    pos_base = 0 if cache is None else POOL_BUF
    y0, pool_state = _layer0(x, pool_prev, mem_k, mem_v, p, nb=nb, tt=tt, pos_base=pos_base)
    k, v, logf_t, mem, *att = _layer1_front(y0, mem_k, mem_v, p, nb=nb, tt=tt, fold=cache is None)
    logf = jnp.transpose(logf_t.reshape(b // nb, FOX_HEADS, nb, t), (0, 2, 3, 1)).reshape(b, t, FOX_HEADS)
    if cache is None:
        mix = _fox_prompt(*att, tq=FOX_TQ)
    else:
        q, kb, vb, c, ct = att
        cache_k, cache_v, cache_logf = cache
        ct_new = jnp.swapaxes(ct.reshape(FOX_HEADS, b, t), 0, 1)
        ct_new = jnp.pad(ct_new, ((0, 0), (0, 0), (0, V7X_LANES - t)))
        to_lanes = lambda a: jnp.transpose(a, (0, 2, 3, 1))
        mix = _fox_sample(q, to_lanes(cache_k), to_lanes(cache_v), kb, vb, jnp.swapaxes(cache_logf, 1, 2), c, ct_new)
    y = _layer1_back(y0, mix, mem, p, nb=nb, tt=tt)
    return (y, pool_state, k.reshape(b, t, FOX_HEADS, FOX_HD), v.reshape(b, t, FOX_HEADS, FOX_HD), logf)


def kernel(x_prompt, x_sample, cache_pool, cache_k, cache_v, cache_logf, cache_mem_k, cache_mem_v, mem_prompt,
           g_mix_pre, g_mix_post, g_mlp_pre, g_mlp_post, w_in, w_out, w_pool, pool_scale, g_kv, w_kvf, b_f,
           g_mem, w_mem_kv, w_up, w_down):
    depth = w_in.shape[0]
    assert depth == 2 and w_pool.shape[0] == 1, "one pooling layer followed by one forgetting-attention layer"
    bp = x_prompt.shape[0]
    w_f = w_kvf[:, 2 * MIX_W:]
    fold_cols = lambda a: jnp.pad(jnp.concatenate([a, jnp.repeat(a, FOLD_W, axis=-1)], axis=-1),
                                  [(0, 0)] * (a.ndim - 1) + [(0, V7X_LANES - FOLD_SRC - FOLD_W * FOX_HEADS)])
    rows = lambda g: g.reshape(g.shape[0], 1, g.shape[1])
    p = dict(
        g_mix_pre=rows(g_mix_pre), g_mix_post=rows(g_mix_post), g_mlp_pre=rows(g_mlp_pre),
        g_mlp_post=rows(g_mlp_post), pool_scale=rows(pool_scale), g_kv=g_kv.reshape(1, D_MODEL),
        w_in=w_in.astype(BF16), w_out=w_out.astype(BF16), w_pool=w_pool.astype(BF16),
        w_up=w_up.astype(BF16), w_down=w_down.astype(BF16), w_kv=w_kvf[:, :2 * MIX_W].astype(BF16),
        w_f=fold_cols(w_f).astype(BF16), w_ft=w_f.T.astype(BF16),
        b_f=fold_cols(b_f).reshape(1, V7X_LANES), b_ft=b_f.reshape(FOX_HEADS, 1),
    )

    mem_k_prompt, mem_v_prompt, mkh, mvh = _mem_project(mem_prompt, g_mem, w_mem_kv.astype(BF16))

    y_p, pool_p, k_p, v_p, logf_p = _trunk(x_prompt, jnp.zeros((bp, HALO, MIX_W), F32), None, mkh, mvh, p)

    pool_prev = jnp.pad(cache_pool[0], ((0, 0), (HALO - POOL_BUF, 0), (0, 0)))
    head_major = lambda a: jnp.swapaxes(a.astype(BF16), 2, 3)
    y_s, pool_s, k_s, v_s, logf_s = _trunk(x_sample, pool_prev, (cache_k, cache_v, cache_logf),
                                           head_major(cache_mem_k), head_major(cache_mem_v), p)

    return (y_p, y_s, pool_p, pool_s, k_p, v_p, logf_p, k_s, v_s, logf_s, mem_k_prompt, mem_v_prompt)
```

```python
import functools

import jax
import jax.numpy as jnp
from jax import lax
from jax.experimental import pallas as pl
from jax.experimental.pallas import tpu as pltpu

F32 = jnp.float32
BF16 = jnp.bfloat16

D_MODEL = 1024
MIX_W = D_MODEL // 2
MEM_W = D_MODEL - MIX_W
POOL_WINDOWS = (2, 4, 8, 16)
POOL_GROUP = MIX_W // len(POOL_WINDOWS)
POOL_BUF = max(POOL_WINDOWS) - 1
FOX_HEADS = 8
FOX_HD = MIX_W // FOX_HEADS
MEM_HEADS = 4
MEM_HD = MEM_W // MEM_HEADS
D_FF = 4 * D_MODEL
EPS = 1e-6

V7X_LANES = 128
V7X_SUBLANES_F32 = 8
V7X_VMEM_BYTES = 64 * 1024 * 1024

HALO = 2 * V7X_SUBLANES_F32
FF_CHUNK = 1024
ROW_PARTS = 2
PAIR_W = 2 * FOX_HD
NEG_BIG = -1e30
LOG2E = 1.4426950408889634
FOLD_SRC = FOX_HEADS
FOLD_W = 6
FOX_AHEAD = 2
FOX_TQ = 512
assert FOLD_SRC + FOLD_W * FOX_HEADS <= V7X_LANES
VMEM_LIMIT = V7X_VMEM_BYTES - 8 * 1024 * 1024

assert HALO >= POOL_BUF and PAIR_W == V7X_LANES and POOL_GROUP == V7X_LANES and MEM_HD == V7X_LANES


def _const_spec(shape):
    zeros = (0,) * len(shape)
    return pl.BlockSpec(shape, lambda *_: zeros, pipeline_mode=pl.Buffered(1))


def _layer_spec(stacked, layer):
    index = (layer,) + (0,) * (stacked.ndim - 1)
    return pl.BlockSpec((None,) + stacked.shape[1:], lambda *_: index, pipeline_mode=pl.Buffered(1))


def _mem_spec(mem, layer, nb):
    return pl.BlockSpec((None, nb) + mem.shape[2:], lambda i, j: (layer, i, 0, 0, 0))


def _unit_rms(x):
    return x * lax.rsqrt(jnp.mean(x * x, axis=-1, keepdims=True) + EPS)


def _rms(x, g):
    return _unit_rms(x) * g


def _dot(a, b):
    return jnp.dot(a, b, preferred_element_type=F32)


def _dot_nt(a, b):
    return lax.dot_general(a, b, (((1,), (1,)), ((), ())), preferred_element_type=F32)


def _log_sigmoid(x):
    return jnp.minimum(x, 0.0) - jnp.log1p(jnp.exp(-jnp.abs(x)))


def _cumsum(x, axis, seg):
    assert seg & (seg - 1) == 0 and x.shape[axis] % seg == 0
    idx = lax.broadcasted_iota(jnp.int32, x.shape, axis) & (seg - 1)
    k = 1
    while k < seg:
        x = x + jnp.where(idx >= k, pltpu.roll(x, k, axis=axis), 0.0)
        k *= 2
    return x


def _mem_attention(q, mk_ref, mv_ref, b):
    logits = [_dot_nt(q[:, h * MEM_HD:(h + 1) * MEM_HD].astype(BF16), mk_ref[b, h]) * (MEM_HD ** -0.5)
              for h in range(MEM_HEADS)]
    probs = []
    for s in logits:
        p = jnp.exp(s - jnp.max(s, axis=-1, keepdims=True))
        probs.append((p.astype(BF16), jnp.sum(p, axis=-1, keepdims=True)))
    return jnp.concatenate([_dot(p, mv_ref[b, h]) / l for h, (p, l) in enumerate(probs)], axis=-1)


def _pool_mixer(u, halo, pos0, w_pool_ref, scale):
    tt = u.shape[0]
    ext = jnp.concatenate([halo, u], axis=0)
    pos = pos0 + lax.broadcasted_iota(jnp.int32, (tt, 1), 0)
    outs = []
    for g, w in enumerate(POOL_WINDOWS):
        sl = slice(g * POOL_GROUP, (g + 1) * POOL_GROUP)
        s = ext[:, sl]
        k = 1
        while k < w:
            s = s + pltpu.roll(s, k, axis=0)
            k *= 2
        cnt = jnp.minimum(pos + 1, w).astype(F32)
        pooled = s[HALO:] / cnt - u[:, sl]
        outs.append(_dot(pooled.astype(BF16), w_pool_ref[g]))
    return jnp.concatenate(outs, axis=-1) * scale


def _row_parts(m):
    n = ROW_PARTS if m >= 512 else 1
    return [slice(i * m // n, (i + 1) * m // n) for i in range(n)]


def _mlp(x1_parts, g_pre, g_post, w_up_ref, w_down_ref):
    n_chunks = D_FF // FF_CHUNK
    chunk = lambda c: slice(c * FF_CHUNK, (c + 1) * FF_CHUNK)
    act = lambda up: jnp.square(jnp.maximum(up, 0.0)).astype(BF16)
    hm_parts = [_rms(x1, g_pre).astype(BF16) for x1 in x1_parts]
    hm = jnp.concatenate(hm_parts, axis=0)
    parts = _row_parts(hm.shape[0])
    acc = None
    for c in range(n_chunks):
        if c == 0:
            a = jnp.concatenate([act(_dot(h, w_up_ref[:, chunk(c)])) for h in hm_parts], axis=0)
        else:
            a = act(_dot(hm, w_up_ref[:, chunk(c)]))
        if c < n_chunks - 1:
            d = _dot(a, w_down_ref[chunk(c), :])
            acc = d if acc is None else acc + d
    return [x1 + _rms(acc[rows] + _dot(a[rows], w_down_ref[chunk(n_chunks - 1), :]), g_post)
            for x1, rows in zip(x1_parts, parts)]


def _mem_project_kernel(mem_ref, g_ref, w_ref, mk_ref, mv_ref, mkh_ref, mvh_ref):
    kv = _dot(_rms(mem_ref[0], g_ref[0]).astype(BF16), w_ref[0])
    for h in range(MEM_HEADS):
        mk = kv[:, h * MEM_HD:(h + 1) * MEM_HD]
        mv = kv[:, MEM_W + h * MEM_HD:MEM_W + (h + 1) * MEM_HD]
        mk_ref[0, 0, :, h, :] = mk
        mv_ref[0, 0, :, h, :] = mv
        mkh_ref[0, 0, h] = mk.astype(BF16)
        mvh_ref[0, 0, h] = mv.astype(BF16)


def _mem_project(mem, g_mem, w_mem_kv):
    b, n, _ = mem.shape
    depth = w_mem_kv.shape[0]
    out = jax.ShapeDtypeStruct((depth, b, n, MEM_HEADS, MEM_HD), F32)
    out_h = jax.ShapeDtypeStruct((depth, b, MEM_HEADS, n, MEM_HD), BF16)
    return pl.pallas_call(
        _mem_project_kernel,
        grid=(depth, b),
        in_specs=[pl.BlockSpec((1, n, D_MODEL), lambda l, i: (i, 0, 0)),
                  pl.BlockSpec((1, 1, D_MODEL), lambda l, i: (l, 0, 0)),
                  pl.BlockSpec((1, D_MODEL, 2 * MEM_W), lambda l, i: (l, 0, 0))],
        out_specs=([pl.BlockSpec((1, 1, n, MEM_HEADS, MEM_HD), lambda l, i: (l, i, 0, 0, 0))] * 2
                   + [pl.BlockSpec((1, 1, MEM_HEADS, n, MEM_HD), lambda l, i: (l, i, 0, 0, 0))] * 2),
        out_shape=[out, out, out_h, out_h],
        compiler_params=pltpu.CompilerParams(dimension_semantics=("arbitrary", "arbitrary")),
        name="mem_project",
    )(mem, g_mem.reshape(depth, 1, D_MODEL), w_mem_kv)


def _layer0_kernel(x_ref, prev_ref, mk_ref, mv_ref, g_pre, g_post, g_mpre, g_mpost,
                   w_in, w_out, w_pool, pscale, w_up, w_down,
                   y_ref, state_ref, halo_ref, *, nb, tt, pos_base):
    t = pl.program_id(1)

    @pl.when(t == 0)
    def _():
        halo_ref[...] = prev_ref[...]

    x = x_ref[...].reshape(nb * tt, D_MODEL)
    proj = jnp.concatenate([_dot(_rms(x[rows], g_pre[...]).astype(BF16), w_in[...])
                            for rows in _row_parts(nb * tt)], axis=0)
    cats = []
    for b in range(nb):
        rows = slice(b * tt, (b + 1) * tt)
        u = proj[rows, :MIX_W]
        mix = _pool_mixer(u, halo_ref[b], pos_base + t * tt, w_pool, pscale[...])
        halo_ref[b] = u[tt - HALO:, :]

        @pl.when(t == pl.num_programs(1) - 1)
        def _():
            state_ref[0, b] = u[tt - POOL_BUF:, :]

        mem = _mem_attention(proj[rows, MIX_W:], mk_ref, mv_ref, b)
        cats.append(jnp.concatenate([mix, mem], axis=-1))
    cat = (cats[0] if nb == 1 else jnp.concatenate(cats, axis=0)).astype(BF16)
    x1 = [x[rows] + _rms(_dot(cat[rows], w_out[...]), g_post[...]) for rows in _row_parts(nb * tt)]
    y = jnp.concatenate(_mlp(x1, g_mpre[...], g_mpost[...], w_up, w_down), axis=0)
    y_ref[...] = y.reshape(nb, tt, D_MODEL)


def _layer0(x, prev, mem_k, mem_v, p, *, nb, tt, pos_base):
    b, t, _ = x.shape
    tile = pl.BlockSpec((nb, tt, D_MODEL), lambda i, j: (i, j, 0))
    return pl.pallas_call(
        functools.partial(_layer0_kernel, nb=nb, tt=tt, pos_base=pos_base),
        grid=(b // nb, t // tt),
        in_specs=[tile, pl.BlockSpec((nb, HALO, MIX_W), lambda i, j: (i, 0, 0)),
                  _mem_spec(mem_k, 0, nb), _mem_spec(mem_v, 0, nb),
                  _layer_spec(p["g_mix_pre"], 0), _layer_spec(p["g_mix_post"], 0),
                  _layer_spec(p["g_mlp_pre"], 0), _layer_spec(p["g_mlp_post"], 0),
                  _layer_spec(p["w_in"], 0), _layer_spec(p["w_out"], 0),
                  _layer_spec(p["w_pool"], 0), _layer_spec(p["pool_scale"], 0),
                  _layer_spec(p["w_up"], 0), _layer_spec(p["w_down"], 0)],
        out_specs=[tile, pl.BlockSpec((1, nb, POOL_BUF, MIX_W), lambda i, j: (0, i, 0, 0))],
        out_shape=[jax.ShapeDtypeStruct((b, t, D_MODEL), F32),
                   jax.ShapeDtypeStruct((1, b, POOL_BUF, MIX_W), F32)],
        scratch_shapes=[pltpu.VMEM((nb, HALO, MIX_W), F32)],
        compiler_params=pltpu.CompilerParams(dimension_semantics=("arbitrary", "arbitrary"),
                                             vmem_limit_bytes=VMEM_LIMIT),
        name="layer0",
    )(x, prev, mem_k, mem_v, p["g_mix_pre"], p["g_mix_post"], p["g_mlp_pre"], p["g_mlp_post"],
      p["w_in"], p["w_out"], p["w_pool"], p["pool_scale"], p["w_up"], p["w_down"])


def _bf16_part(x):
    return x.astype(BF16).astype(F32)


def _fold_operands(q, k, v, c):
    cl = c * LOG2E
    hi = _bf16_part(cl)
    mid = _bf16_part(cl - hi)
    lo = _bf16_part(cl - hi - mid)
    lane = lax.broadcasted_iota(jnp.int32, (1, V7X_LANES), 1)
    piece = lax.rem(lane + (3 - FOLD_SRC % 3), 3)
    d = jnp.where(piece == 0, hi, jnp.where(piece == 1, mid, lo))
    qs, ks, vs = [], [], []
    for h in range(FOX_HEADS):
        own = (h % 2) * FOX_HD
        e0 = FOX_HD - own
        cols = slice((h // 2) * PAIR_W, (h // 2 + 1) * PAIR_W)
        dh = pltpu.roll(d, (e0 - (FOLD_SRC + FOLD_W * h)) % V7X_LANES, axis=1)
        in_head = (lane >= own) & (lane < own + FOX_HD)
        first = ((lane >= e0) & (lane < e0 + 3)).astype(F32)
        second = ((lane >= e0 + 3) & (lane < e0 + 6)).astype(F32)
        xq = dh * first + second
        xk = first - dh * second
        xv = (lane == e0).astype(F32)
        qs.append(jnp.where(in_head, q[:, cols] * (FOX_HD ** -0.5 * LOG2E), xq).T.astype(BF16))
        ks.append(jnp.where(in_head, k[:, cols], xk).astype(BF16))
        vs.append(jnp.where(in_head, v[:, cols], xv).T.astype(BF16))
    return jnp.concatenate(qs, axis=0), jnp.concatenate(ks, axis=-1), jnp.concatenate(vs, axis=0)


def _layer1_front_kernel(x_ref, mk_ref, mv_ref, g_kv, g_pre, w_kv, w_f, w_ft, b_f, b_ft, w_in, *refs,
                         nb, tt, fold):
    if fold:
        k_ref, v_ref, logft_ref, mem_ref, qa_ref, ka_ref, va_ref, carry_ref = refs
    else:
        k_ref, v_ref, logft_ref, mem_ref, q_ref, kb_ref, vb_ref, c_ref, ct_ref = refs

    xn = _unit_rms(x_ref[...].reshape(nb * tt, D_MODEL))
    hk = (xn * g_kv[...]).astype(BF16)
    kv = _dot(hk, w_kv[...])
    k = kv[:, :MIX_W]
    v = kv[:, MIX_W:]
    k_ref[...] = k.reshape(nb, tt, MIX_W)
    v_ref[...] = v.reshape(nb, tt, MIX_W)
    logf = _log_sigmoid(_dot(hk, w_f[...]) + b_f[...])
    proj = _dot((xn * g_pre[...]).astype(BF16), w_in[...])
    q = proj[:, :MIX_W]
    c = _cumsum(logf, 0, tt)
    if fold:
        @pl.when(pl.program_id(1) == 0)
        def _():
            carry_ref[...] = jnp.zeros_like(carry_ref)

        c = c + carry_ref[...]
        carry_ref[...] = c[tt - 1:, :]
        qa, ka, va = _fold_operands(q, k, v, c)
        qa_ref[0, 0] = qa
        ka_ref[0] = ka
        va_ref[0, 0] = va
        logft_ref[0] = logf.T[:FOX_HEADS, :]
    else:
        q_ref[...] = (q * (FOX_HD ** -0.5)).astype(BF16).reshape(nb, tt, MIX_W)
        kb_ref[...] = k.astype(BF16).reshape(nb, tt, MIX_W)
        vb_ref[...] = v.astype(BF16).reshape(nb, tt, MIX_W)
        logf_t = _log_sigmoid(_dot_nt(w_ft[...], hk) + b_ft[...])
        logft_ref[0] = logf_t
        ct_ref[0, 0] = _cumsum(logf_t, 1, tt)
    for b in range(nb):
        rows = slice(b * tt, (b + 1) * tt)
        if not fold:
            c_ref[b] = c[rows, :FOX_HEADS]
        mem_ref[b] = _mem_attention(proj[rows, MIX_W:], mk_ref, mv_ref, b).astype(BF16)


def _layer1_front(x, mem_k, mem_v, p, *, nb, tt, fold):
    b, t, _ = x.shape
    assert (nb == 1) if fold else (t == tt)
    tile = lambda w: pl.BlockSpec((nb, tt, w), lambda i, j: (i, j, 0))
    act = lambda w, dt: jax.ShapeDtypeStruct((b, t, w), dt)
    out_specs = [tile(MIX_W), tile(MIX_W), pl.BlockSpec((1, FOX_HEADS, nb * tt), lambda i, j: (i, 0, j)), tile(MEM_W)]
    out_shape = [act(MIX_W, F32), act(MIX_W, F32), jax.ShapeDtypeStruct((b // nb, FOX_HEADS, nb * t), F32),
                 act(MEM_W, BF16)]
    if fold:
        wf = FOX_HEADS * PAIR_W
        tile_t = pl.BlockSpec((1, 1, wf, tt), lambda i, j: (i, j, 0, 0))
        act_t = jax.ShapeDtypeStruct((b, t // tt, wf, tt), BF16)
        out_specs += [tile_t, tile(wf), tile_t]
        out_shape += [act_t, act(wf, BF16), act_t]
        scratch = [pltpu.VMEM((1, V7X_LANES), F32)]
    else:
        out_specs += [tile(MIX_W)] * 3 + [tile(FOX_HEADS),
                                          pl.BlockSpec((1, 1, FOX_HEADS, nb * tt), lambda i, j: (i, j, 0, 0))]
        out_shape += [act(MIX_W, BF16)] * 3 + [act(FOX_HEADS, F32),
                                               jax.ShapeDtypeStruct((b // nb, t // tt, FOX_HEADS, nb * tt), F32)]
        scratch = []
    return pl.pallas_call(
        functools.partial(_layer1_front_kernel, nb=nb, tt=tt, fold=fold),
        grid=(b // nb, t // tt),
        in_specs=[tile(D_MODEL), _mem_spec(mem_k, 1, nb), _mem_spec(mem_v, 1, nb),
                  _const_spec(p["g_kv"].shape), _layer_spec(p["g_mix_pre"], 1),
                  _const_spec(p["w_kv"].shape), _const_spec(p["w_f"].shape), _const_spec(p["w_ft"].shape),
                  _const_spec(p["b_f"].shape), _const_spec(p["b_ft"].shape), _layer_spec(p["w_in"], 1)],
        out_specs=out_specs,
        out_shape=out_shape,
        scratch_shapes=scratch,
        compiler_params=pltpu.CompilerParams(dimension_semantics=("arbitrary", "arbitrary"),
                                             vmem_limit_bytes=VMEM_LIMIT),
        name="layer1_front",
    )(x, mem_k, mem_v, p["g_kv"], p["g_mix_pre"], p["w_kv"], p["w_f"], p["w_ft"], p["b_f"], p["b_ft"], p["w_in"])


def _fox_prompt_kernel(qt_ref, k_ref, vt_ref, o_ref, m_ref, acc_ref, *s_refs, tq, tk):
    i = pl.program_id(1)
    n_full = (i * tq) // tk
    sub = lax.broadcasted_iota(jnp.int32, (PAIR_W, 1), 0)
    causal = (n_full * tk + lax.broadcasted_iota(jnp.int32, (tk, tq), 0)
              <= i * tq + lax.broadcasted_iota(jnp.int32, (tk, tq), 1))
    col = [slice(h * PAIR_W, (h + 1) * PAIR_W) for h in range(FOX_HEADS)]
    qt = [qt_ref[0, 0, c, :] for c in col]
    m_ref[...] = jnp.full(m_ref.shape, NEG_BIG, F32)
    acc_ref[...] = jnp.zeros(acc_ref.shape, F32)

    def tile(j, mask):
        rows = pl.ds(pl.multiple_of(j * tk, tk), tk)

        def logits(h):
            s = _dot(k_ref[0, rows, col[h]], qt[h])
            if mask is not None:
                s = jnp.where(mask, s, NEG_BIG)
            s_refs[h % len(s_refs)][...] = s
            return jnp.max(s, axis=0, keepdims=True)

        top = [logits(h) for h in range(FOX_AHEAD)]
        for h in range(FOX_HEADS):
            if h + FOX_AHEAD < FOX_HEADS:
                top.append(logits(h + FOX_AHEAD))
            m = m_ref[h]
            m_new = jnp.maximum(m, top[h])
            p = jnp.exp2(s_refs[h % len(s_refs)][...] - m_new).astype(BF16)
            acc_ref[h] = jnp.exp2(m - m_new) * acc_ref[h] + _dot(vt_ref[0, j, col[h], :], p)
            m_ref[h] = m_new

    lax.fori_loop(0, n_full, lambda j, c: tile(j, None), None)
    tile(n_full, causal)
    for pair in range(FOX_HEADS // 2):
        outs = []
        for h in (2 * pair, 2 * pair + 1):
            acc = acc_ref[h]
            denom = FOX_HD * (1 - h % 2)
            outs.append(acc / acc[denom:denom + 1, :])
        o_ref[0, :, pair * PAIR_W:(pair + 1) * PAIR_W] = jnp.where(sub < FOX_HD, outs[0], outs[1]).T.astype(BF16)


def _fox_prompt(qt, ka, vt, *, tq):
    b, t, w = ka.shape
    tk = qt.shape[3]
    assert tk % tq == 0
    per_stream = lambda shape: pl.BlockSpec((1,) + shape, lambda i, j: (i,) + (0,) * len(shape))
    return pl.pallas_call(
        functools.partial(_fox_prompt_kernel, tq=tq, tk=tk),
        grid=(b, t // tq),
        in_specs=[pl.BlockSpec((1, 1, w, tq), lambda i, j: (i, (j * tq) // tk, 0, j % (tk // tq))),
                  per_stream((t, w)), per_stream((t // tk, w, tk))],
        out_specs=pl.BlockSpec((1, tq, MIX_W), lambda i, j: (i, j, 0)),
        out_shape=jax.ShapeDtypeStruct((b, t, MIX_W), BF16),
        scratch_shapes=([pltpu.VMEM((FOX_HEADS, 1, tq), F32), pltpu.VMEM((FOX_HEADS, PAIR_W, tq), F32)]
                        + [pltpu.VMEM((tk, tq), F32)] * (FOX_AHEAD + 1)),
        compiler_params=pltpu.CompilerParams(dimension_semantics=("arbitrary", "arbitrary"),
                                             vmem_limit_bytes=VMEM_LIMIT),
        name="fox_prompt",
    )(qt, ka, vt)


def _fox_sample_kernel(q_ref, kc_ref, vc_ref, kn_ref, vn_ref, lfc_t_ref, c_ref, ct_ref, o_ref):
    tn = q_ref.shape[1]
    p = lfc_t_ref.shape[2]
    cc = _cumsum(lfc_t_ref[0], 1, p)
    ck_cache = cc - cc[:, p - 1:]
    causal = (lax.broadcasted_iota(jnp.int32, (tn, V7X_LANES), 1)
              <= lax.broadcasted_iota(jnp.int32, (tn, V7X_LANES), 0))
    pad = jnp.zeros((V7X_LANES - tn, FOX_HD), BF16)
    col = [slice(h * FOX_HD, (h + 1) * FOX_HD) for h in range(FOX_HEADS)]
    logits = []
    for h in range(FOX_HEADS):
        qh = q_ref[0, :, col[h]]
        cq = c_ref[0, :, h:h + 1]
        kn = jnp.concatenate([kn_ref[0, :, col[h]], pad], axis=0)
        s_old = _dot(qh, kc_ref[0, h].astype(BF16)) + cq - ck_cache[h:h + 1, :]
        s_new = jnp.where(causal, _dot_nt(qh, kn) + cq - ct_ref[0, h:h + 1, :], NEG_BIG)
        logits.append((s_old, s_new))
    probs = []
    for s_old, s_new in logits:
        m = jnp.maximum(jnp.max(s_old, axis=-1, keepdims=True), jnp.max(s_new, axis=-1, keepdims=True))
        p_old = jnp.exp(s_old - m)
        p_new = jnp.exp(s_new - m)
        l = jnp.sum(p_old, axis=-1, keepdims=True) + jnp.sum(p_new, axis=-1, keepdims=True)
        probs.append((p_old.astype(BF16), p_new.astype(BF16), l))
    outs = []
    for h, (p_old, p_new, l) in enumerate(probs):
        vn = jnp.concatenate([vn_ref[0, :, col[h]], pad], axis=0)
        outs.append((_dot_nt(p_old, vc_ref[0, h].astype(BF16)) + _dot(p_new, vn)) / l)
    o_ref[0] = jnp.concatenate(outs, axis=-1).astype(BF16)


def _fox_sample(q, cache_kt, cache_vt, kb, vb, cache_logf_t, c, ct):
    b, tn, _ = q.shape
    p = cache_kt.shape[3]
    new = lambda w: pl.BlockSpec((1, tn, w), lambda i: (i, 0, 0))
    old = pl.BlockSpec((1, FOX_HEADS, FOX_HD, p), lambda i: (i, 0, 0, 0))
    return pl.pallas_call(
        _fox_sample_kernel,
        grid=(b,),
        in_specs=[new(MIX_W), old, old, new(MIX_W), new(MIX_W),
                  pl.BlockSpec((1, FOX_HEADS, p), lambda i: (i, 0, 0)),
                  new(FOX_HEADS), pl.BlockSpec((1, FOX_HEADS, V7X_LANES), lambda i: (i, 0, 0))],
        out_specs=new(MIX_W),
        out_shape=jax.ShapeDtypeStruct((b, tn, MIX_W), BF16),
        compiler_params=pltpu.CompilerParams(dimension_semantics=("arbitrary",), vmem_limit_bytes=VMEM_LIMIT),
        name="fox_sample",
    )(q, cache_kt, cache_vt, kb, vb, cache_logf_t, c, ct)


def _layer1_back_kernel(x_ref, mix_ref, mem_ref, g_post, g_mpre, g_mpost, w_out, w_up, w_down, y_ref, *, nb, tt):
    m = nb * tt
    x = x_ref[...].reshape(m, D_MODEL)
    mix = mix_ref[...].reshape(m, MIX_W)
    mem = mem_ref[...].reshape(m, MEM_W)
    x1 = [x[rows] + _rms(_dot(mix[rows], w_out[:MIX_W, :]) + _dot(mem[rows], w_out[MIX_W:, :]), g_post[...])
          for rows in _row_parts(m)]
    y = jnp.concatenate(_mlp(x1, g_mpre[...], g_mpost[...], w_up, w_down), axis=0)
    y_ref[...] = y.reshape(nb, tt, D_MODEL)


def _layer1_back(x, mix, mem, p, *, nb, tt):
    b, t, _ = x.shape
    tile = lambda w: pl.BlockSpec((nb, tt, w), lambda i, j: (i, j, 0))
    return pl.pallas_call(
        functools.partial(_layer1_back_kernel, nb=nb, tt=tt),
        grid=(b // nb, t // tt),
        in_specs=[tile(D_MODEL), tile(MIX_W), tile(MEM_W),
                  _layer_spec(p["g_mix_post"], 1), _layer_spec(p["g_mlp_pre"], 1), _layer_spec(p["g_mlp_post"], 1),
                  _layer_spec(p["w_out"], 1), _layer_spec(p["w_up"], 1), _layer_spec(p["w_down"], 1)],
        out_specs=tile(D_MODEL),
        out_shape=jax.ShapeDtypeStruct((b, t, D_MODEL), F32),
        compiler_params=pltpu.CompilerParams(dimension_semantics=("arbitrary", "arbitrary"),
                                             vmem_limit_bytes=VMEM_LIMIT),
        name="layer1_back",
    )(x, mix, mem, p["g_mix_post"], p["g_mlp_pre"], p["g_mlp_post"], p["w_out"], p["w_up"], p["w_down"])


def _token_tile(b, t):
    if t >= 512:
        return 1, 512
    return b, t


def _trunk(x, pool_prev, cache, mem_k, mem_v, p):
    b, t, _ = x.shape
    nb, tt = _token_tile(b, t)
    pos_base = 0 if cache is None else POOL_BUF
    y0, pool_state = _layer0(x, pool_prev, mem_k, mem_v, p, nb=nb, tt=tt, pos_base=pos_base)
    k, v, logf_t, mem, *att = _layer1_front(y0, mem_k, mem_v, p, nb=nb, tt=tt, fold=cache is None)
    logf = jnp.transpose(logf_t.reshape(b // nb, FOX_HEADS, nb, t), (0, 2, 3, 1)).reshape(b, t, FOX_HEADS)
    if cache is None:
        mix = _fox_prompt(*att, tq=FOX_TQ)
    else:
        q, kb, vb, c, ct = att
        cache_k, cache_v, cache_logf = cache
        ct_new = jnp.swapaxes(ct.reshape(FOX_HEADS, b, t), 0, 1)
        ct_new = jnp.pad(ct_new, ((0, 0), (0, 0), (0, V7X_LANES - t)))
        to_lanes = lambda a: jnp.transpose(a, (0, 2, 3, 1))
        mix = _fox_sample(q, to_lanes(cache_k), to_lanes(cache_v), kb, vb, jnp.swapaxes(cache_logf, 1, 2), c, ct_new)
    y = _layer1_back(y0, mix, mem, p, nb=nb, tt=tt)
    return (y, pool_state, k.reshape(b, t, FOX_HEADS, FOX_HD), v.reshape(b, t, FOX_HEADS, FOX_HD), logf)


def kernel(x_prompt, x_sample, cache_pool, cache_k, cache_v, cache_logf, cache_mem_k, cache_mem_v, mem_prompt,
           g_mix_pre, g_mix_post, g_mlp_pre, g_mlp_post, w_in, w_out, w_pool, pool_scale, g_kv, w_kvf, b_f,
           g_mem, w_mem_kv, w_up, w_down):
    depth = w_in.shape[0]
    assert depth == 2 and w_pool.shape[0] == 1, "one pooling layer followed by one forgetting-attention layer"
    bp = x_prompt.shape[0]
    w_f = w_kvf[:, 2 * MIX_W:]
    fold_cols = lambda a: jnp.pad(jnp.concatenate([a, jnp.repeat(a, FOLD_W, axis=-1)], axis=-1),
                                  [(0, 0)] * (a.ndim - 1) + [(0, V7X_LANES - FOLD_SRC - FOLD_W * FOX_HEADS)])
    rows = lambda g: g.reshape(g.shape[0], 1, g.shape[1])
    p = dict(
        g_mix_pre=rows(g_mix_pre), g_mix_post=rows(g_mix_post), g_mlp_pre=rows(g_mlp_pre),
        g_mlp_post=rows(g_mlp_post), pool_scale=rows(pool_scale), g_kv=g_kv.reshape(1, D_MODEL),
        w_in=w_in.astype(BF16), w_out=w_out.astype(BF16), w_pool=w_pool.astype(BF16),
        w_up=w_up.astype(BF16), w_down=w_down.astype(BF16), w_kv=w_kvf[:, :2 * MIX_W].astype(BF16),
        w_f=fold_cols(w_f).astype(BF16), w_ft=w_f.T.astype(BF16),
        b_f=fold_cols(b_f).reshape(1, V7X_LANES), b_ft=b_f.reshape(FOX_HEADS, 1),
    )

    mem_k_prompt, mem_v_prompt, mkh, mvh = _mem_project(mem_prompt, g_mem, w_mem_kv.astype(BF16))

    y_p, pool_p, k_p, v_p, logf_p = _trunk(x_prompt, jnp.zeros((bp, HALO, MIX_W), F32), None, mkh, mvh, p)

    pool_prev = jnp.pad(cache_pool[0], ((0, 0), (HALO - POOL_BUF, 0), (0, 0)))
    head_major = lambda a: jnp.swapaxes(a.astype(BF16), 2, 3)
    y_s, pool_s, k_s, v_s, logf_s = _trunk(x_sample, pool_prev, (cache_k, cache_v, cache_logf),
                                           head_major(cache_mem_k), head_major(cache_mem_v), p)

    return (y_p, y_s, pool_p, pool_s, k_p, v_p, logf_p, k_s, v_s, logf_s, mem_k_prompt, mem_v_prompt)
```

```python
import functools

import jax
import jax.numpy as jnp
from jax import lax
from jax.experimental import pallas as pl
from jax.experimental.pallas import tpu as pltpu

F32 = jnp.float32
BF16 = jnp.bfloat16

D_MODEL = 1024
MIX_W = D_MODEL // 2
MEM_W = D_MODEL - MIX_W
POOL_WINDOWS = (2, 4, 8, 16)
POOL_GROUP = MIX_W // len(POOL_WINDOWS)
POOL_BUF = max(POOL_WINDOWS) - 1
FOX_HEADS = 8
FOX_HD = MIX_W // FOX_HEADS
MEM_HEADS = 4
MEM_HD = MEM_W // MEM_HEADS
D_FF = 4 * D_MODEL
EPS = 1e-6

V7X_LANES = 128
V7X_SUBLANES_F32 = 8
V7X_VMEM_BYTES = 64 * 1024 * 1024

HALO = 2 * V7X_SUBLANES_F32
FF_CHUNK = 1024
ROW_PARTS = 2
PAIR_W = 2 * FOX_HD
NEG_BIG = -1e30
LOG2E = 1.4426950408889634
FOLD_SRC = FOX_HEADS
FOLD_W = 6
FOX_AHEAD = 2
FOX_TQ = 512
assert FOLD_SRC + FOLD_W * FOX_HEADS <= V7X_LANES
VMEM_LIMIT = V7X_VMEM_BYTES - 8 * 1024 * 1024

assert HALO >= POOL_BUF and PAIR_W == V7X_LANES and POOL_GROUP == V7X_LANES and MEM_HD == V7X_LANES


def _const_spec(shape):
    zeros = (0,) * len(shape)
    return pl.BlockSpec(shape, lambda *_: zeros, pipeline_mode=pl.Buffered(1))


def _layer_spec(stacked, layer):
    index = (layer,) + (0,) * (stacked.ndim - 1)
    return pl.BlockSpec((None,) + stacked.shape[1:], lambda *_: index, pipeline_mode=pl.Buffered(1))


def _mem_spec(mem, layer, nb):
    return pl.BlockSpec((None, nb) + mem.shape[2:], lambda i, j: (layer, i, 0, 0, 0))


def _unit_rms(x):
    return x * lax.rsqrt(jnp.mean(x * x, axis=-1, keepdims=True) + EPS)


def _rms(x, g):
    return _unit_rms(x) * g


def _dot(a, b):
    return jnp.dot(a, b, preferred_element_type=F32)


def _dot_nt(a, b):
    return lax.dot_general(a, b, (((1,), (1,)), ((), ())), preferred_element_type=F32)


def _log_sigmoid(x):
    return jnp.minimum(x, 0.0) - jnp.log1p(jnp.exp(-jnp.abs(x)))


def _cumsum(x, axis, seg):
    assert seg & (seg - 1) == 0 and x.shape[axis] % seg == 0
    n = x.shape[axis]
    vreg = (V7X_SUBLANES_F32, V7X_LANES)[axis]
    idx = lax.broadcasted_iota(jnp.int32, x.shape, axis) & (seg - 1)
    k = 1
    while k < seg:
        if seg == n and k % vreg == 0:
            zeros = jnp.zeros(x.shape[:axis] + (k,) + x.shape[axis + 1:], x.dtype)
            x = x + jnp.concatenate([zeros, lax.slice_in_dim(x, 0, n - k, axis=axis)], axis=axis)
        else:
            x = x + jnp.where(idx >= k, pltpu.roll(x, k, axis=axis), 0.0)
        k *= 2
    return x


def _mem_logits(q, mk_ref, b):
    qs = (q * (MEM_HD ** -0.5 * LOG2E)).astype(BF16)
    return [_dot_nt(qs[:, h * MEM_HD:(h + 1) * MEM_HD], mk_ref[b, h]) for h in range(MEM_HEADS)]


def _mem_probs(logits):
    probs = []
    for s in logits:
        p = jnp.exp2(s - jnp.max(s, axis=-1, keepdims=True))
        probs.append((p.astype(BF16), jnp.sum(p, axis=-1, keepdims=True)))
    return probs


def _mem_values(probs, mv_ref, b):
    return jnp.concatenate([_dot(p, mv_ref[b, h]) / l for h, (p, l) in enumerate(probs)], axis=-1)


def _pool_mixer(u, halo, pos0, w_pool_ref, scale):
    tt = u.shape[0]
    ext = jnp.concatenate([halo, u], axis=0)
    pos = pos0 + lax.broadcasted_iota(jnp.int32, (tt, 1), 0)
    outs = []
    for g, w in enumerate(POOL_WINDOWS):
        sl = slice(g * POOL_GROUP, (g + 1) * POOL_GROUP)
        s = ext[:, sl]
        k = 1
        while k < w:
            s = s + pltpu.roll(s, k, axis=0)
            k *= 2
        cnt = jnp.minimum(pos + 1, w).astype(F32)
        pooled = s[HALO:] / cnt - u[:, sl]
        outs.append(_dot(pooled.astype(BF16), w_pool_ref[g]))
    return jnp.concatenate(outs, axis=-1) * scale


def _row_parts(m):
    n = ROW_PARTS if m >= 512 else 1
    return [slice(i * m // n, (i + 1) * m // n) for i in range(n)]


def _mlp(x1_parts, g_pre, g_post, w_up_ref, w_down_ref):
    n_chunks = D_FF // FF_CHUNK
    chunk = lambda c: slice(c * FF_CHUNK, (c + 1) * FF_CHUNK)
    act = lambda up: jnp.square(jnp.maximum(up, 0.0)).astype(BF16)
    hm_parts = [_rms(x1, g_pre).astype(BF16) for x1 in x1_parts]
    hm = jnp.concatenate(hm_parts, axis=0)
    parts = _row_parts(hm.shape[0])
    acc = None
    for c in range(n_chunks):
        if c == 0:
            a = jnp.concatenate([act(_dot(h, w_up_ref[:, chunk(c)])) for h in hm_parts], axis=0)
        else:
            a = act(_dot(hm, w_up_ref[:, chunk(c)]))
        if c < n_chunks - 1:
            d = _dot(a, w_down_ref[chunk(c), :])
            acc = d if acc is None else acc + d
    return [x1 + _rms(acc[rows] + _dot(a[rows], w_down_ref[chunk(n_chunks - 1), :]), g_post)
            for x1, rows in zip(x1_parts, parts)]


def _mem_project_kernel(mem_ref, g_ref, w_ref, mk_ref, mv_ref, mkh_ref, mvh_ref):
    kv = _dot(_rms(mem_ref[0], g_ref[0]).astype(BF16), w_ref[0])
    for h in range(MEM_HEADS):
        mk = kv[:, h * MEM_HD:(h + 1) * MEM_HD]
        mv = kv[:, MEM_W + h * MEM_HD:MEM_W + (h + 1) * MEM_HD]
        mk_ref[0, 0, :, h, :] = mk
        mv_ref[0, 0, :, h, :] = mv
        mkh_ref[0, 0, h] = mk.astype(BF16)
        mvh_ref[0, 0, h] = mv.astype(BF16)


def _mem_project(mem, g_mem, w_mem_kv):
    b, n, _ = mem.shape
    depth = w_mem_kv.shape[0]
    out = jax.ShapeDtypeStruct((depth, b, n, MEM_HEADS, MEM_HD), F32)
    out_h = jax.ShapeDtypeStruct((depth, b, MEM_HEADS, n, MEM_HD), BF16)
    return pl.pallas_call(
        _mem_project_kernel,
        grid=(depth, b),
        in_specs=[pl.BlockSpec((1, n, D_MODEL), lambda l, i: (i, 0, 0)),
                  pl.BlockSpec((1, 1, D_MODEL), lambda l, i: (l, 0, 0)),
                  pl.BlockSpec((1, D_MODEL, 2 * MEM_W), lambda l, i: (l, 0, 0))],
        out_specs=([pl.BlockSpec((1, 1, n, MEM_HEADS, MEM_HD), lambda l, i: (l, i, 0, 0, 0))] * 2
                   + [pl.BlockSpec((1, 1, MEM_HEADS, n, MEM_HD), lambda l, i: (l, i, 0, 0, 0))] * 2),
        out_shape=[out, out, out_h, out_h],
        compiler_params=pltpu.CompilerParams(dimension_semantics=("arbitrary", "arbitrary")),
        name="mem_project",
    )(mem, g_mem.reshape(depth, 1, D_MODEL), w_mem_kv)


def _layer0_kernel(x_ref, prev_ref, mk_ref, mv_ref, g_pre, g_post, g_mpre, g_mpost,
                   w_in, w_out, w_pool, pscale, w_up, w_down,
                   y_ref, state_ref, halo_ref, *, nb, tt, pos_base):
    t = pl.program_id(1)

    @pl.when(t == 0)
    def _():
        halo_ref[...] = prev_ref[...]

    x = x_ref[...].reshape(nb * tt, D_MODEL)
    proj = jnp.concatenate([_dot(_rms(x[rows], g_pre[...]).astype(BF16), w_in[...])
                            for rows in _row_parts(nb * tt)], axis=0)
    cats = []
    for b in range(nb):
        rows = slice(b * tt, (b + 1) * tt)
        u = proj[rows, :MIX_W]
        logits = _mem_logits(proj[rows, MIX_W:], mk_ref, b)
        mix = _pool_mixer(u, halo_ref[b], pos_base + t * tt, w_pool, pscale[...])
        halo_ref[b] = u[tt - HALO:, :]
        state_ref[0, b] = u[tt - POOL_BUF:, :]
        mem = _mem_values(_mem_probs(logits), mv_ref, b)
        cats.append(jnp.concatenate([mix, mem], axis=-1))
    cat = (cats[0] if nb == 1 else jnp.concatenate(cats, axis=0)).astype(BF16)
    x1 = [x[rows] + _rms(_dot(cat[rows], w_out[...]), g_post[...]) for rows in _row_parts(nb * tt)]
    y = jnp.concatenate(_mlp(x1, g_mpre[...], g_mpost[...], w_up, w_down), axis=0)
    y_ref[...] = y.reshape(nb, tt, D_MODEL)


def _layer0(x, prev, mem_k, mem_v, p, *, nb, tt, pos_base):
    b, t, _ = x.shape
    tile = pl.BlockSpec((nb, tt, D_MODEL), lambda i, j: (i, j, 0))
    return pl.pallas_call(
        functools.partial(_layer0_kernel, nb=nb, tt=tt, pos_base=pos_base),
        grid=(b // nb, t // tt),
        in_specs=[tile, pl.BlockSpec((nb, HALO, MIX_W), lambda i, j: (i, 0, 0)),
                  _mem_spec(mem_k, 0, nb), _mem_spec(mem_v, 0, nb),
                  _layer_spec(p["g_mix_pre"], 0), _layer_spec(p["g_mix_post"], 0),
                  _layer_spec(p["g_mlp_pre"], 0), _layer_spec(p["g_mlp_post"], 0),
                  _layer_spec(p["w_in"], 0), _layer_spec(p["w_out"], 0),
                  _layer_spec(p["w_pool"], 0), _layer_spec(p["pool_scale"], 0),
                  _layer_spec(p["w_up"], 0), _layer_spec(p["w_down"], 0)],
        out_specs=[tile, pl.BlockSpec((1, nb, POOL_BUF, MIX_W), lambda i, j: (0, i, 0, 0))],
        out_shape=[jax.ShapeDtypeStruct((b, t, D_MODEL), F32),
                   jax.ShapeDtypeStruct((1, b, POOL_BUF, MIX_W), F32)],
        scratch_shapes=[pltpu.VMEM((nb, HALO, MIX_W), F32)],
        compiler_params=pltpu.CompilerParams(dimension_semantics=("arbitrary", "arbitrary"),
                                             vmem_limit_bytes=VMEM_LIMIT),
        name="layer0",
    )(x, prev, mem_k, mem_v, p["g_mix_pre"], p["g_mix_post"], p["g_mlp_pre"], p["g_mlp_post"],
      p["w_in"], p["w_out"], p["w_pool"], p["pool_scale"], p["w_up"], p["w_down"])


def _bf16_part(x):
    return x.astype(BF16).astype(F32)


def _fold_operands(q, k, v, c):
    cl = c * LOG2E
    hi = _bf16_part(cl)
    mid = _bf16_part(cl - hi)
    lo = _bf16_part(cl - hi - mid)
    lane = lax.broadcasted_iota(jnp.int32, (1, V7X_LANES), 1)
    piece = lax.rem(lane + (3 - FOLD_SRC % 3), 3)
    d = jnp.where(piece == 0, hi, jnp.where(piece == 1, mid, lo))
    qs, ks, vs = [], [], []
    for h in range(FOX_HEADS):
        own = (h % 2) * FOX_HD
        e0 = FOX_HD - own
        cols = slice((h // 2) * PAIR_W, (h // 2 + 1) * PAIR_W)
        dh = pltpu.roll(d, (e0 - (FOLD_SRC + FOLD_W * h)) % V7X_LANES, axis=1)
        in_head = (lane >= own) & (lane < own + FOX_HD)
        first = ((lane >= e0) & (lane < e0 + 3)).astype(F32)
        second = ((lane >= e0 + 3) & (lane < e0 + 6)).astype(F32)
        xq = dh * first + second
        xk = first - dh * second
        xv = (lane == e0).astype(F32)
        qs.append(jnp.where(in_head, q[:, cols] * (FOX_HD ** -0.5 * LOG2E), xq).T.astype(BF16))
        ks.append(jnp.where(in_head, k[:, cols], xk).astype(BF16))
        vs.append(jnp.where(in_head, v[:, cols], xv).T.astype(BF16))
    return jnp.concatenate(qs, axis=0), jnp.concatenate(ks, axis=-1), jnp.concatenate(vs, axis=0)


def _layer1_front_kernel(x_ref, mk_ref, mv_ref, g_kv, g_pre, w_kv, w_f, w_ft, b_f, b_ft, w_in, *refs,
                         nb, tt, fold):
    if fold:
        k_ref, v_ref, logft_ref, mem_ref, qa_ref, ka_ref, va_ref, carry_ref = refs
    else:
        k_ref, v_ref, logft_ref, mem_ref, q_ref, kb_ref, vb_ref, c_ref, ct_ref = refs
    if fold:
        @pl.when(pl.program_id(1) == 0)
        def _():
            carry_ref[...] = jnp.zeros_like(carry_ref)

    xn = _unit_rms(x_ref[...].reshape(nb * tt, D_MODEL))
    hk = (xn * g_kv[...]).astype(BF16)
    kv = _dot(hk, w_kv[...])
    k = kv[:, :MIX_W]
    v = kv[:, MIX_W:]
    k_ref[...] = k.reshape(nb, tt, MIX_W)
    v_ref[...] = v.reshape(nb, tt, MIX_W)
    logf = _log_sigmoid(_dot(hk, w_f[...]) + b_f[...])
    proj = _dot((xn * g_pre[...]).astype(BF16), w_in[...])
    q = proj[:, :MIX_W]
    mem_logits = [_mem_logits(proj[b * tt:(b + 1) * tt, MIX_W:], mk_ref, b) for b in range(nb)]
    c = _cumsum(logf, 0, tt)
    if fold:
        c = c + carry_ref[...]
        carry_ref[...] = c[tt - 1:, :]
        qa, ka, va = _fold_operands(q, k, v, c)
        qa_ref[0, 0] = qa
        ka_ref[0] = ka
        va_ref[0, 0] = va
        logft_ref[0] = logf.T[:FOX_HEADS, :]
    else:
        q_ref[...] = (q * (FOX_HD ** -0.5)).astype(BF16).reshape(nb, tt, MIX_W)
        kb_ref[...] = k.astype(BF16).reshape(nb, tt, MIX_W)
        vb_ref[...] = v.astype(BF16).reshape(nb, tt, MIX_W)
        logf_t = _log_sigmoid(_dot_nt(w_ft[...], hk) + b_ft[...])
        logft_ref[0] = logf_t
        ct_ref[0, 0] = _cumsum(logf_t, 1, tt)
    for b in range(nb):
        rows = slice(b * tt, (b + 1) * tt)
        if not fold:
            c_ref[b] = c[rows, :FOX_HEADS]
        mem_ref[b] = _mem_values(_mem_probs(mem_logits[b]), mv_ref, b).astype(BF16)


def _layer1_front(x, mem_k, mem_v, p, *, nb, tt, fold):
    b, t, _ = x.shape
    assert (nb == 1) if fold else (t == tt)
    tile = lambda w: pl.BlockSpec((nb, tt, w), lambda i, j: (i, j, 0))
    act = lambda w, dt: jax.ShapeDtypeStruct((b, t, w), dt)
    out_specs = [tile(MIX_W), tile(MIX_W), pl.BlockSpec((1, FOX_HEADS, nb * tt), lambda i, j: (i, 0, j)), tile(MEM_W)]
    out_shape = [act(MIX_W, F32), act(MIX_W, F32), jax.ShapeDtypeStruct((b // nb, FOX_HEADS, nb * t), F32),
                 act(MEM_W, BF16)]
    if fold:
        wf = FOX_HEADS * PAIR_W
        tile_t = pl.BlockSpec((1, 1, wf, tt), lambda i, j: (i, j, 0, 0))
        act_t = jax.ShapeDtypeStruct((b, t // tt, wf, tt), BF16)
        out_specs += [tile_t, tile(wf), tile_t]
        out_shape += [act_t, act(wf, BF16), act_t]
        scratch = [pltpu.VMEM((1, V7X_LANES), F32)]
    else:
        out_specs += [tile(MIX_W)] * 3 + [tile(FOX_HEADS),
                                          pl.BlockSpec((1, 1, FOX_HEADS, nb * tt), lambda i, j: (i, j, 0, 0))]
        out_shape += [act(MIX_W, BF16)] * 3 + [act(FOX_HEADS, F32),
                                               jax.ShapeDtypeStruct((b // nb, t // tt, FOX_HEADS, nb * tt), F32)]
        scratch = []
    return pl.pallas_call(
        functools.partial(_layer1_front_kernel, nb=nb, tt=tt, fold=fold),
        grid=(b // nb, t // tt),
        in_specs=[tile(D_MODEL), _mem_spec(mem_k, 1, nb), _mem_spec(mem_v, 1, nb),
                  _const_spec(p["g_kv"].shape), _layer_spec(p["g_mix_pre"], 1),
                  _const_spec(p["w_kv"].shape), _const_spec(p["w_f"].shape), _const_spec(p["w_ft"].shape),
                  _const_spec(p["b_f"].shape), _const_spec(p["b_ft"].shape), _layer_spec(p["w_in"], 1)],
        out_specs=out_specs,
        out_shape=out_shape,
        scratch_shapes=scratch,
        compiler_params=pltpu.CompilerParams(dimension_semantics=("arbitrary", "arbitrary"),
                                             vmem_limit_bytes=VMEM_LIMIT),
        name="layer1_front",
    )(x, mem_k, mem_v, p["g_kv"], p["g_mix_pre"], p["w_kv"], p["w_f"], p["w_ft"], p["b_f"], p["b_ft"], p["w_in"])


def _fox_prompt_kernel(qt_ref, k_ref, vt_ref, o_ref, m_ref, acc_ref, *s_refs, tq, tk):
    i = pl.program_id(1)
    n_full = (i * tq) // tk
    sub = lax.broadcasted_iota(jnp.int32, (PAIR_W, 1), 0)
    causal = (n_full * tk + lax.broadcasted_iota(jnp.int32, (tk, tq), 0)
              <= i * tq + lax.broadcasted_iota(jnp.int32, (tk, tq), 1))
    col = [slice(h * PAIR_W, (h + 1) * PAIR_W) for h in range(FOX_HEADS)]
    qt = [qt_ref[0, 0, c, :] for c in col]
    m_ref[...] = jnp.full(m_ref.shape, NEG_BIG, F32)
    acc_ref[...] = jnp.zeros(acc_ref.shape, F32)

    def tile(j, mask):
        rows = pl.ds(pl.multiple_of(j * tk, tk), tk)

        def logits(h):
            s = _dot(k_ref[0, rows, col[h]], qt[h])
            if mask is not None:
                s = jnp.where(mask, s, NEG_BIG)
            s_refs[h % len(s_refs)][...] = s
            return jnp.max(s, axis=0, keepdims=True)

        top = [logits(h) for h in range(FOX_AHEAD)]
        for h in range(FOX_HEADS):
            if h + FOX_AHEAD < FOX_HEADS:
                top.append(logits(h + FOX_AHEAD))
            m = m_ref[h]
            m_new = jnp.maximum(m, top[h])
            p = jnp.exp2(s_refs[h % len(s_refs)][...] - m_new).astype(BF16)
            acc_ref[h] = jnp.exp2(m - m_new) * acc_ref[h] + _dot(vt_ref[0, j, col[h], :], p)
            m_ref[h] = m_new

    lax.fori_loop(0, n_full, lambda j, c: tile(j, None), None)
    tile(n_full, causal)
    for pair in range(FOX_HEADS // 2):
        outs = []
        for h in (2 * pair, 2 * pair + 1):
            acc = acc_ref[h]
            denom = FOX_HD * (1 - h % 2)
            outs.append(acc / acc[denom:denom + 1, :])
        o_ref[0, :, pair * PAIR_W:(pair + 1) * PAIR_W] = jnp.where(sub < FOX_HD, outs[0], outs[1]).T.astype(BF16)


def _fox_prompt(qt, ka, vt, *, tq):
    b, t, w = ka.shape
    tk = qt.shape[3]
    assert tk % tq == 0
    per_stream = lambda shape: pl.BlockSpec((1,) + shape, lambda i, j: (i,) + (0,) * len(shape))
    return pl.pallas_call(
        functools.partial(_fox_prompt_kernel, tq=tq, tk=tk),
        grid=(b, t // tq),
        in_specs=[pl.BlockSpec((1, 1, w, tq), lambda i, j: (i, (j * tq) // tk, 0, j % (tk // tq))),
                  per_stream((t, w)), per_stream((t // tk, w, tk))],
        out_specs=pl.BlockSpec((1, tq, MIX_W), lambda i, j: (i, j, 0)),
        out_shape=jax.ShapeDtypeStruct((b, t, MIX_W), BF16),
        scratch_shapes=([pltpu.VMEM((FOX_HEADS, 1, tq), F32), pltpu.VMEM((FOX_HEADS, PAIR_W, tq), F32)]
                        + [pltpu.VMEM((tk, tq), F32)] * (FOX_AHEAD + 1)),
        compiler_params=pltpu.CompilerParams(dimension_semantics=("arbitrary", "arbitrary"),
                                             vmem_limit_bytes=VMEM_LIMIT),
        name="fox_prompt",
    )(qt, ka, vt)


def _fox_sample_kernel(q_ref, kc_ref, vc_ref, kn_ref, vn_ref, lfc_t_ref, c_ref, ct_ref, o_ref):
    tn = q_ref.shape[1]
    p = lfc_t_ref.shape[2]
    cc = _cumsum(lfc_t_ref[0], 1, p)
    ck_cache = cc - cc[:, p - 1:]
    causal = (lax.broadcasted_iota(jnp.int32, (tn, V7X_LANES), 1)
              <= lax.broadcasted_iota(jnp.int32, (tn, V7X_LANES), 0))
    pad = jnp.zeros((V7X_LANES - tn, FOX_HD), BF16)
    col = [slice(h * FOX_HD, (h + 1) * FOX_HD) for h in range(FOX_HEADS)]
    logits = []
    for h in range(FOX_HEADS):
        qh = q_ref[0, :, col[h]]
        cq = c_ref[0, :, h:h + 1]
        kn = jnp.concatenate([kn_ref[0, :, col[h]], pad], axis=0)
        s_old = _dot(qh, kc_ref[0, h].astype(BF16)) + cq - ck_cache[h:h + 1, :]
        s_new = jnp.where(causal, _dot_nt(qh, kn) + cq - ct_ref[0, h:h + 1, :], NEG_BIG)
        logits.append((s_old, s_new))
    probs = []
    for s_old, s_new in logits:
        m = jnp.maximum(jnp.max(s_old, axis=-1, keepdims=True), jnp.max(s_new, axis=-1, keepdims=True))
        p_old = jnp.exp(s_old - m)
        p_new = jnp.exp(s_new - m)
        l = jnp.sum(p_old, axis=-1, keepdims=True) + jnp.sum(p_new, axis=-1, keepdims=True)
        probs.append((p_old.astype(BF16), p_new.astype(BF16), l))
    outs = []
    for h, (p_old, p_new, l) in enumerate(probs):
        vn = jnp.concatenate([vn_ref[0, :, col[h]], pad], axis=0)
        outs.append((_dot_nt(p_old, vc_ref[0, h].astype(BF16)) + _dot(p_new, vn)) / l)
    o_ref[0] = jnp.concatenate(outs, axis=-1).astype(BF16)


def _fox_sample(q, cache_kt, cache_vt, kb, vb, cache_logf_t, c, ct):
    b, tn, _ = q.shape
    p = cache_kt.shape[3]
    new = lambda w: pl.BlockSpec((1, tn, w), lambda i: (i, 0, 0))
    old = pl.BlockSpec((1, FOX_HEADS, FOX_HD, p), lambda i: (i, 0, 0, 0))
    return pl.pallas_call(
        _fox_sample_kernel,
        grid=(b,),
        in_specs=[new(MIX_W), old, old, new(MIX_W), new(MIX_W),
                  pl.BlockSpec((1, FOX_HEADS, p), lambda i: (i, 0, 0)),
                  new(FOX_HEADS), pl.BlockSpec((1, FOX_HEADS, V7X_LANES), lambda i: (i, 0, 0))],
        out_specs=new(MIX_W),
        out_shape=jax.ShapeDtypeStruct((b, tn, MIX_W), BF16),
        compiler_params=pltpu.CompilerParams(dimension_semantics=("arbitrary",), vmem_limit_bytes=VMEM_LIMIT),
        name="fox_sample",
    )(q, cache_kt, cache_vt, kb, vb, cache_logf_t, c, ct)


def _layer1_back_kernel(x_ref, mix_ref, mem_ref, g_post, g_mpre, g_mpost, w_out, w_up, w_down, y_ref, *, nb, tt):
    m = nb * tt
    x = x_ref[...].reshape(m, D_MODEL)
    mix = mix_ref[...].reshape(m, MIX_W)
    mem = mem_ref[...].reshape(m, MEM_W)
    x1 = [x[rows] + _rms(_dot(mix[rows], w_out[:MIX_W, :]) + _dot(mem[rows], w_out[MIX_W:, :]), g_post[...])
          for rows in _row_parts(m)]
    y = jnp.concatenate(_mlp(x1, g_mpre[...], g_mpost[...], w_up, w_down), axis=0)
    y_ref[...] = y.reshape(nb, tt, D_MODEL)


def _layer1_back(x, mix, mem, p, *, nb, tt):
    b, t, _ = x.shape
    tile = lambda w: pl.BlockSpec((nb, tt, w), lambda i, j: (i, j, 0))
    return pl.pallas_call(
        functools.partial(_layer1_back_kernel, nb=nb, tt=tt),
        grid=(b // nb, t // tt),
        in_specs=[tile(D_MODEL), tile(MIX_W), tile(MEM_W),
                  _layer_spec(p["g_mix_post"], 1), _layer_spec(p["g_mlp_pre"], 1), _layer_spec(p["g_mlp_post"], 1),
                  _layer_spec(p["w_out"], 1), _layer_spec(p["w_up"], 1), _layer_spec(p["w_down"], 1)],
        out_specs=tile(D_MODEL),
        out_shape=jax.ShapeDtypeStruct((b, t, D_MODEL), F32),
        compiler_params=pltpu.CompilerParams(dimension_semantics=("arbitrary", "arbitrary"),
                                             vmem_limit_bytes=VMEM_LIMIT),
        name="layer1_back",
    )(x, mix, mem, p["g_mix_post"], p["g_mlp_pre"], p["g_mlp_post"], p["w_out"], p["w_up"], p["w_down"])


def _token_tile(b, t):
    if t >= 512:
        return 1, 512
    return b, t


def _trunk(x, pool_prev, cache, mem_k, mem_v, p):
    b, t, _ = x.shape
    nb, tt = _token_tile(b, t)
    pos_base = 0 if cache is None else POOL_BUF
    y0, pool_state = _layer0(x, pool_prev, mem_k, mem_v, p, nb=nb, tt=tt, pos_base=pos_base)
    k, v, logf_t, mem, *att = _layer1_front(y0, mem_k, mem_v, p, nb=nb, tt=tt, fold=cache is None)
    logf = jnp.transpose(logf_t.reshape(b // nb, FOX_HEADS, nb, t), (0, 2, 3, 1)).reshape(b, t, FOX_HEADS)
    if cache is None:
        mix = _fox_prompt(*att, tq=FOX_TQ)
    else:
        q, kb, vb, c, ct = att
        cache_k, cache_v, cache_logf = cache
        ct_new = jnp.swapaxes(ct.reshape(FOX_HEADS, b, t), 0, 1)
        ct_new = jnp.pad(ct_new, ((0, 0), (0, 0), (0, V7X_LANES - t)))
        to_lanes = lambda a: jnp.transpose(a, (0, 2, 3, 1))
        mix = _fox_sample(q, to_lanes(cache_k), to_lanes(cache_v), kb, vb, jnp.swapaxes(cache_logf, 1, 2), c, ct_new)
    y = _layer1_back(y0, mix, mem, p, nb=nb, tt=tt)
    return (y, pool_state, k.reshape(b, t, FOX_HEADS, FOX_HD), v.reshape(b, t, FOX_HEADS, FOX_HD), logf)


def kernel(x_prompt, x_sample, cache_pool, cache_k, cache_v, cache_logf, cache_mem_k, cache_mem_v, mem_prompt,
           g_mix_pre, g_mix_post, g_mlp_pre, g_mlp_post, w_in, w_out, w_pool, pool_scale, g_kv, w_kvf, b_f,
           g_mem, w_mem_kv, w_up, w_down):
    depth = w_in.shape[0]
    assert depth == 2 and w_pool.shape[0] == 1, "one pooling layer followed by one forgetting-attention layer"
    bp = x_prompt.shape[0]
    w_f = w_kvf[:, 2 * MIX_W:]
    fold_cols = lambda a: jnp.pad(jnp.concatenate([a, jnp.repeat(a, FOLD_W, axis=-1)], axis=-1),
                                  [(0, 0)] * (a.ndim - 1) + [(0, V7X_LANES - FOLD_SRC - FOLD_W * FOX_HEADS)])
    rows = lambda g: g.reshape(g.shape[0], 1, g.shape[1])
    p = dict(
        g_mix_pre=rows(g_mix_pre), g_mix_post=rows(g_mix_post), g_mlp_pre=rows(g_mlp_pre),
        g_mlp_post=rows(g_mlp_post), pool_scale=rows(pool_scale), g_kv=g_kv.reshape(1, D_MODEL),
        w_in=w_in.astype(BF16), w_out=w_out.astype(BF16), w_pool=w_pool.astype(BF16),
        w_up=w_up.astype(BF16), w_down=w_down.astype(BF16), w_kv=w_kvf[:, :2 * MIX_W].astype(BF16),
        w_f=fold_cols(w_f).astype(BF16), w_ft=w_f.T.astype(BF16),
        b_f=fold_cols(b_f).reshape(1, V7X_LANES), b_ft=b_f.reshape(FOX_HEADS, 1),
    )

    mem_k_prompt, mem_v_prompt, mkh, mvh = _mem_project(mem_prompt, g_mem, w_mem_kv.astype(BF16))

    y_p, pool_p, k_p, v_p, logf_p = _trunk(x_prompt, jnp.zeros((bp, HALO, MIX_W), F32), None, mkh, mvh, p)

    pool_prev = jnp.pad(cache_pool[0], ((0, 0), (HALO - POOL_BUF, 0), (0, 0)))
    head_major = lambda a: jnp.swapaxes(a.astype(BF16), 2, 3)
    y_s, pool_s, k_s, v_s, logf_s = _trunk(x_sample, pool_prev, (cache_k, cache_v, cache_logf),
                                           head_major(cache_mem_k), head_major(cache_mem_v), p)

    return (y_p, y_s, pool_p, pool_s, k_p, v_p, logf_p, k_s, v_s, logf_s, mem_k_prompt, mem_v_prompt)
```

```python
import functools

import jax
import jax.numpy as jnp
from jax import lax
from jax.experimental import pallas as pl
from jax.experimental.pallas import tpu as pltpu

F32 = jnp.float32
BF16 = jnp.bfloat16

D_MODEL = 1024
MIX_W = D_MODEL // 2
MEM_W = D_MODEL - MIX_W
POOL_WINDOWS = (2, 4, 8, 16)
POOL_GROUP = MIX_W // len(POOL_WINDOWS)
POOL_BUF = max(POOL_WINDOWS) - 1
FOX_HEADS = 8
FOX_HD = MIX_W // FOX_HEADS
MEM_HEADS = 4
MEM_HD = MEM_W // MEM_HEADS
D_FF = 4 * D_MODEL
EPS = 1e-6

V7X_LANES = 128
V7X_SUBLANES_F32 = 8
V7X_VMEM_BYTES = 64 * 1024 * 1024

HALO = 2 * V7X_SUBLANES_F32
FF_CHUNK = 1024
ROW_PARTS = 2
PAIR_W = 2 * FOX_HD
NEG_BIG = -1e30
LOG2E = 1.4426950408889634
FOLD_SRC = FOX_HEADS
FOLD_W = 6
FOX_AHEAD = 2
FOX_TQ = 512
assert FOLD_SRC + FOLD_W * FOX_HEADS <= V7X_LANES
VMEM_LIMIT = V7X_VMEM_BYTES - 8 * 1024 * 1024

assert HALO >= POOL_BUF and PAIR_W == V7X_LANES and POOL_GROUP == V7X_LANES and MEM_HD == V7X_LANES


def _const_spec(shape):
    zeros = (0,) * len(shape)
    return pl.BlockSpec(shape, lambda *_: zeros, pipeline_mode=pl.Buffered(1))


def _layer_spec(stacked, layer):
    index = (layer,) + (0,) * (stacked.ndim - 1)
    return pl.BlockSpec((None,) + stacked.shape[1:], lambda *_: index, pipeline_mode=pl.Buffered(1))


def _mem_spec(mem, layer, nb):
    return pl.BlockSpec((None, nb) + mem.shape[2:], lambda i, j: (layer, i, 0, 0, 0))


def _unit_rms(x):
    return x * lax.rsqrt(jnp.mean(x * x, axis=-1, keepdims=True) + EPS)


def _rms(x, g):
    return _unit_rms(x) * g


def _dot(a, b):
    return jnp.dot(a, b, preferred_element_type=F32)


def _dot_nt(a, b):
    return lax.dot_general(a, b, (((1,), (1,)), ((), ())), preferred_element_type=F32)


def _log_sigmoid(x):
    return jnp.minimum(x, 0.0) - jnp.log1p(jnp.exp(-jnp.abs(x)))


def _cumsum(x, axis, seg):
    assert seg & (seg - 1) == 0 and x.shape[axis] % seg == 0
    n = x.shape[axis]
    vreg = (V7X_SUBLANES_F32, V7X_LANES)[axis]
    idx = lax.broadcasted_iota(jnp.int32, x.shape, axis) & (seg - 1)
    k = 1
    while k < seg:
        if seg == n and k % vreg == 0:
            zeros = jnp.zeros(x.shape[:axis] + (k,) + x.shape[axis + 1:], x.dtype)
            x = x + jnp.concatenate([zeros, lax.slice_in_dim(x, 0, n - k, axis=axis)], axis=axis)
        else:
            x = x + jnp.where(idx >= k, pltpu.roll(x, k, axis=axis), 0.0)
        k *= 2
    return x


def _mem_logits(q, mk_ref, b):
    qs = (q * (MEM_HD ** -0.5 * LOG2E)).astype(BF16)
    return [_dot_nt(qs[:, h * MEM_HD:(h + 1) * MEM_HD], mk_ref[b, h]) for h in range(MEM_HEADS)]


def _mem_probs(logits):
    probs = []
    for s in logits:
        p = jnp.exp2(s - jnp.max(s, axis=-1, keepdims=True))
        probs.append((p.astype(BF16), jnp.sum(p, axis=-1, keepdims=True)))
    return probs


def _mem_values(probs, mv_ref, b):
    return jnp.concatenate([_dot(p, mv_ref[b, h]) / l for h, (p, l) in enumerate(probs)], axis=-1)


def _pool_mixer(u, halo, pos0, w_pool_ref, scale):
    tt = u.shape[0]
    ext = jnp.concatenate([halo, u], axis=0)
    pos = pos0 + lax.broadcasted_iota(jnp.int32, (tt, 1), 0)
    outs = []
    for g, w in enumerate(POOL_WINDOWS):
        sl = slice(g * POOL_GROUP, (g + 1) * POOL_GROUP)
        s = ext[:, sl]
        k = 1
        while k < w:
            s = s + pltpu.roll(s, k, axis=0)
            k *= 2
        cnt = jnp.minimum(pos + 1, w).astype(F32)
        pooled = s[HALO:] / cnt - u[:, sl]
        outs.append(_dot(pooled.astype(BF16), w_pool_ref[g]))
    return jnp.concatenate(outs, axis=-1) * scale


def _row_parts(m):
    n = ROW_PARTS if m >= 512 else 1
    return [slice(i * m // n, (i + 1) * m // n) for i in range(n)]


def _mlp(x1_parts, g_pre, g_post, w_up_ref, w_down_ref):
    n_chunks = D_FF // FF_CHUNK
    chunk = lambda c: slice(c * FF_CHUNK, (c + 1) * FF_CHUNK)
    act = lambda up: jnp.square(jnp.maximum(up, 0.0)).astype(BF16)
    hm_parts = [_rms(x1, g_pre).astype(BF16) for x1 in x1_parts]
    hm = jnp.concatenate(hm_parts, axis=0)
    parts = _row_parts(hm.shape[0])
    acc = None
    for c in range(n_chunks):
        if c == 0:
            a = jnp.concatenate([act(_dot(h, w_up_ref[:, chunk(c)])) for h in hm_parts], axis=0)
        else:
            a = act(_dot(hm, w_up_ref[:, chunk(c)]))
        if c < n_chunks - 1:
            d = _dot(a, w_down_ref[chunk(c), :])
            acc = d if acc is None else acc + d
    return [x1 + _rms(acc[rows] + _dot(a[rows], w_down_ref[chunk(n_chunks - 1), :]), g_post)
            for x1, rows in zip(x1_parts, parts)]


def _mem_project_kernel(mem_ref, g_ref, w_ref, mk_ref, mv_ref, mkh_ref, mvh_ref):
    xn = _unit_rms(mem_ref[0])
    for l in range(w_ref.shape[0]):
        kv = _dot((xn * g_ref[l]).astype(BF16), w_ref[l].astype(BF16))
        for h in range(MEM_HEADS):
            mk = kv[:, h * MEM_HD:(h + 1) * MEM_HD]
            mv = kv[:, MEM_W + h * MEM_HD:MEM_W + (h + 1) * MEM_HD]
            mk_ref[l, 0, :, h, :] = mk
            mv_ref[l, 0, :, h, :] = mv
            mkh_ref[l, 0, h] = mk.astype(BF16)
            mvh_ref[l, 0, h] = mv.astype(BF16)


def _mem_project(mem, g_mem, w_mem_kv):
    b, n, _ = mem.shape
    depth = w_mem_kv.shape[0]
    out = jax.ShapeDtypeStruct((depth, b, n, MEM_HEADS, MEM_HD), F32)
    out_h = jax.ShapeDtypeStruct((depth, b, MEM_HEADS, n, MEM_HD), BF16)
    g_mem = g_mem.reshape(depth, 1, D_MODEL)
    return pl.pallas_call(
        _mem_project_kernel,
        grid=(b,),
        in_specs=[pl.BlockSpec((1, n, D_MODEL), lambda i: (i, 0, 0)),
                  _const_spec(g_mem.shape), _const_spec(w_mem_kv.shape)],
        out_specs=([pl.BlockSpec((depth, 1, n, MEM_HEADS, MEM_HD), lambda i: (0, i, 0, 0, 0))] * 2
                   + [pl.BlockSpec((depth, 1, MEM_HEADS, n, MEM_HD), lambda i: (0, i, 0, 0, 0))] * 2),
        out_shape=[out, out, out_h, out_h],
        compiler_params=pltpu.CompilerParams(dimension_semantics=("arbitrary",), vmem_limit_bytes=VMEM_LIMIT),
        name="mem_project",
    )(mem, g_mem, w_mem_kv)


def _layer0_kernel(x_ref, prev_ref, mk_ref, mv_ref, g_pre, g_post, g_mpre, g_mpost,
                   w_in, w_out, w_pool, pscale, w_up, w_down, *refs, nb, tt, pos_base, n_cast):
    cast_in, (y_ref, state_ref), cast_out, (halo_ref,) = (
        refs[:n_cast], refs[n_cast:n_cast + 2], refs[n_cast + 2:2 * n_cast + 2], refs[2 * n_cast + 2:])
    t = pl.program_id(1)

    @pl.when(t == 0)
    def _():
        halo_ref[...] = prev_ref[...]

    for src, dst in zip(cast_in, cast_out):
        dst[...] = src[...].astype(BF16)

    x = x_ref[...].reshape(nb * tt, D_MODEL)
    proj = jnp.concatenate([_dot(_rms(x[rows], g_pre[...]).astype(BF16), w_in[...])
                            for rows in _row_parts(nb * tt)], axis=0)
    cats = []
    for b in range(nb):
        rows = slice(b * tt, (b + 1) * tt)
        u = proj[rows, :MIX_W]
        logits = _mem_logits(proj[rows, MIX_W:], mk_ref, b)
        mix = _pool_mixer(u, halo_ref[b], pos_base + t * tt, w_pool, pscale[...])
        halo_ref[b] = u[tt - HALO:, :]
        state_ref[0, b] = u[tt - POOL_BUF:, :]
        mem = _mem_values(_mem_probs(logits), mv_ref, b)
        cats.append(jnp.concatenate([mix, mem], axis=-1))
    cat = (cats[0] if nb == 1 else jnp.concatenate(cats, axis=0)).astype(BF16)
    x1 = [x[rows] + _rms(_dot(cat[rows], w_out[...]), g_post[...]) for rows in _row_parts(nb * tt)]
    y = jnp.concatenate(_mlp(x1, g_mpre[...], g_mpost[...], w_up, w_down), axis=0)
    y_ref[...] = y.reshape(nb, tt, D_MODEL)


def _layer0(x, prev, mem_k, mem_v, p, *, nb, tt, pos_base, cast=()):
    b, t, _ = x.shape
    grid = (b // nb, t // tt)
    steps = grid[0] * grid[1]
    tile = pl.BlockSpec((nb, tt, D_MODEL), lambda i, j: (i, j, 0))
    slab = lambda w, l: pl.BlockSpec((1, w.shape[1] // steps, w.shape[2]), lambda i, j: (l, i * grid[1] + j, 0))
    return pl.pallas_call(
        functools.partial(_layer0_kernel, nb=nb, tt=tt, pos_base=pos_base, n_cast=len(cast)),
        grid=grid,
        in_specs=[tile, pl.BlockSpec((nb, HALO, MIX_W), lambda i, j: (i, 0, 0)),
                  _mem_spec(mem_k, 0, nb), _mem_spec(mem_v, 0, nb),
                  _layer_spec(p["g_mix_pre"], 0), _layer_spec(p["g_mix_post"], 0),
                  _layer_spec(p["g_mlp_pre"], 0), _layer_spec(p["g_mlp_post"], 0),
                  _layer_spec(p["w_in"][0], 0), _layer_spec(p["w_out"][0], 0),
                  _layer_spec(p["w_pool"], 0), _layer_spec(p["pool_scale"], 0),
                  _layer_spec(p["w_up"][0], 0), _layer_spec(p["w_down"][0], 0)]
                 + [slab(w, l) for w, l in cast],
        out_specs=[tile, pl.BlockSpec((1, nb, POOL_BUF, MIX_W), lambda i, j: (0, i, 0, 0))]
                  + [slab(w, 0) for w, _ in cast],
        out_shape=[jax.ShapeDtypeStruct((b, t, D_MODEL), F32),
                   jax.ShapeDtypeStruct((1, b, POOL_BUF, MIX_W), F32)]
                  + [jax.ShapeDtypeStruct((1,) + w.shape[1:], BF16) for w, _ in cast],
        scratch_shapes=[pltpu.VMEM((nb, HALO, MIX_W), F32)],
        compiler_params=pltpu.CompilerParams(dimension_semantics=("arbitrary", "arbitrary"),
                                             vmem_limit_bytes=VMEM_LIMIT),
        name="layer0",
    )(x, prev, mem_k, mem_v, p["g_mix_pre"], p["g_mix_post"], p["g_mlp_pre"], p["g_mlp_post"],
      p["w_in"][0], p["w_out"][0], p["w_pool"], p["pool_scale"], p["w_up"][0], p["w_down"][0],
      *[w for w, _ in cast])


def _bf16_part(x):
    return x.astype(BF16).astype(F32)


def _fold_operands(q, k, v, c):
    cl = c * LOG2E
    hi = _bf16_part(cl)
    mid = _bf16_part(cl - hi)
    lo = _bf16_part(cl - hi - mid)
    lane = lax.broadcasted_iota(jnp.int32, (1, V7X_LANES), 1)
    piece = lax.rem(lane + (3 - FOLD_SRC % 3), 3)
    d = jnp.where(piece == 0, hi, jnp.where(piece == 1, mid, lo))
    qs, ks, vs = [], [], []
    for h in range(FOX_HEADS):
        own = (h % 2) * FOX_HD
        e0 = FOX_HD - own
        cols = slice((h // 2) * PAIR_W, (h // 2 + 1) * PAIR_W)
        dh = pltpu.roll(d, (e0 - (FOLD_SRC + FOLD_W * h)) % V7X_LANES, axis=1)
        in_head = (lane >= own) & (lane < own + FOX_HD)
        first = ((lane >= e0) & (lane < e0 + 3)).astype(F32)
        second = ((lane >= e0 + 3) & (lane < e0 + 6)).astype(F32)
        xq = dh * first + second
        xk = first - dh * second
        xv = (lane == e0).astype(F32)
        qs.append(jnp.where(in_head, q[:, cols] * (FOX_HD ** -0.5 * LOG2E), xq).T.astype(BF16))
        ks.append(jnp.where(in_head, k[:, cols], xk).astype(BF16))
        vs.append(jnp.where(in_head, v[:, cols], xv).T.astype(BF16))
    return jnp.concatenate(qs, axis=0), jnp.concatenate(ks, axis=-1), jnp.concatenate(vs, axis=0)


def _layer1_front_kernel(x_ref, mk_ref, mv_ref, g_kv, g_pre, w_kv, w_f, w_ft, b_f, b_ft, w_in, *refs,
                         nb, tt, fold):
    if fold:
        k_ref, v_ref, logft_ref, mem_ref, qa_ref, ka_ref, va_ref, carry_ref = refs
    else:
        k_ref, v_ref, logft_ref, mem_ref, q_ref, kb_ref, vb_ref, c_ref, ct_ref = refs
    if fold:
        @pl.when(pl.program_id(1) == 0)
        def _():
            carry_ref[...] = jnp.zeros_like(carry_ref)

    xn = _unit_rms(x_ref[...].reshape(nb * tt, D_MODEL))
    hk = (xn * g_kv[...]).astype(BF16)
    kv = _dot(hk, w_kv[...])
    k = kv[:, :MIX_W]
    v = kv[:, MIX_W:]
    k_ref[...] = k.reshape(nb, tt, MIX_W)
    v_ref[...] = v.reshape(nb, tt, MIX_W)
    logf = _log_sigmoid(_dot(hk, w_f[...]) + b_f[...])
    proj = _dot((xn * g_pre[...]).astype(BF16), w_in[...])
    q = proj[:, :MIX_W]
    mem_logits = [_mem_logits(proj[b * tt:(b + 1) * tt, MIX_W:], mk_ref, b) for b in range(nb)]
    c = _cumsum(logf, 0, tt)
    if fold:
        c = c + carry_ref[...]
        carry_ref[...] = c[tt - 1:, :]
        qa, ka, va = _fold_operands(q, k, v, c)
        qa_ref[0, 0] = qa
        ka_ref[0] = ka
        va_ref[0, 0] = va
        logft_ref[0] = logf.T[:FOX_HEADS, :]
    else:
        q_ref[...] = (q * (FOX_HD ** -0.5)).astype(BF16).reshape(nb, tt, MIX_W)
        kb_ref[...] = k.astype(BF16).reshape(nb, tt, MIX_W)
        vb_ref[...] = v.astype(BF16).reshape(nb, tt, MIX_W)
        logf_t = _log_sigmoid(_dot_nt(w_ft[...], hk) + b_ft[...])
        logft_ref[0] = logf_t
        ct_ref[0, 0] = _cumsum(logf_t, 1, tt)
    for b in range(nb):
        rows = slice(b * tt, (b + 1) * tt)
        if not fold:
            c_ref[b] = c[rows, :FOX_HEADS]
        mem_ref[b] = _mem_values(_mem_probs(mem_logits[b]), mv_ref, b).astype(BF16)


def _layer1_front(x, mem_k, mem_v, p, *, nb, tt, fold):
    b, t, _ = x.shape
    assert (nb == 1) if fold else (t == tt)
    tile = lambda w: pl.BlockSpec((nb, tt, w), lambda i, j: (i, j, 0))
    act = lambda w, dt: jax.ShapeDtypeStruct((b, t, w), dt)
    out_specs = [tile(MIX_W), tile(MIX_W), pl.BlockSpec((1, FOX_HEADS, nb * tt), lambda i, j: (i, 0, j)), tile(MEM_W)]
    out_shape = [act(MIX_W, F32), act(MIX_W, F32), jax.ShapeDtypeStruct((b // nb, FOX_HEADS, nb * t), F32),
                 act(MEM_W, BF16)]
    if fold:
        wf = FOX_HEADS * PAIR_W
        tile_t = pl.BlockSpec((1, 1, wf, tt), lambda i, j: (i, j, 0, 0))
        act_t = jax.ShapeDtypeStruct((b, t // tt, wf, tt), BF16)
        out_specs += [tile_t, tile(wf), tile_t]
        out_shape += [act_t, act(wf, BF16), act_t]
        scratch = [pltpu.VMEM((1, V7X_LANES), F32)]
    else:
        out_specs += [tile(MIX_W)] * 3 + [tile(FOX_HEADS),
                                          pl.BlockSpec((1, 1, FOX_HEADS, nb * tt), lambda i, j: (i, j, 0, 0))]
        out_shape += [act(MIX_W, BF16)] * 3 + [act(FOX_HEADS, F32),
                                               jax.ShapeDtypeStruct((b // nb, t // tt, FOX_HEADS, nb * tt), F32)]
        scratch = []
    return pl.pallas_call(
        functools.partial(_layer1_front_kernel, nb=nb, tt=tt, fold=fold),
        grid=(b // nb, t // tt),
        in_specs=[tile(D_MODEL), _mem_spec(mem_k, 1, nb), _mem_spec(mem_v, 1, nb),
                  _const_spec(p["g_kv"].shape), _layer_spec(p["g_mix_pre"], 1),
                  _const_spec(p["w_kv"].shape), _const_spec(p["w_f"].shape), _const_spec(p["w_ft"].shape),
                  _const_spec(p["b_f"].shape), _const_spec(p["b_ft"].shape), _layer_spec(p["w_in"][1], 0)],
        out_specs=out_specs,
        out_shape=out_shape,
        scratch_shapes=scratch,
        compiler_params=pltpu.CompilerParams(dimension_semantics=("arbitrary", "arbitrary"),
                                             vmem_limit_bytes=VMEM_LIMIT),
        name="layer1_front",
    )(x, mem_k, mem_v, p["g_kv"], p["g_mix_pre"], p["w_kv"], p["w_f"], p["w_ft"], p["b_f"], p["b_ft"], p["w_in"][1])


def _fox_prompt_kernel(qt_ref, k_ref, vt_ref, o_ref, m_ref, acc_ref, *s_refs, tq, tk):
    i = pl.program_id(1)
    n_full = (i * tq) // tk
    sub = lax.broadcasted_iota(jnp.int32, (PAIR_W, 1), 0)
    causal = (n_full * tk + lax.broadcasted_iota(jnp.int32, (tk, tq), 0)
              <= i * tq + lax.broadcasted_iota(jnp.int32, (tk, tq), 1))
    col = [slice(h * PAIR_W, (h + 1) * PAIR_W) for h in range(FOX_HEADS)]
    qt = [qt_ref[0, 0, c, :] for c in col]
    m_ref[...] = jnp.full(m_ref.shape, NEG_BIG, F32)
    acc_ref[...] = jnp.zeros(acc_ref.shape, F32)

    def tile(j, mask):
        rows = pl.ds(pl.multiple_of(j * tk, tk), tk)

        def logits(h):
            s = _dot(k_ref[0, rows, col[h]], qt[h])
            if mask is not None:
                s = jnp.where(mask, s, NEG_BIG)
            s_refs[h % len(s_refs)][...] = s
            return jnp.max(s, axis=0, keepdims=True)

        top = [logits(h) for h in range(FOX_AHEAD)]
        for h in range(FOX_HEADS):
            if h + FOX_AHEAD < FOX_HEADS:
                top.append(logits(h + FOX_AHEAD))
            m = m_ref[h]
            m_new = jnp.maximum(m, top[h])
            p = jnp.exp2(s_refs[h % len(s_refs)][...] - m_new).astype(BF16)
            acc_ref[h] = jnp.exp2(m - m_new) * acc_ref[h] + _dot(vt_ref[0, j, col[h], :], p)
            m_ref[h] = m_new

    lax.fori_loop(0, n_full, lambda j, c: tile(j, None), None)
    tile(n_full, causal)
    for pair in range(FOX_HEADS // 2):
        outs = []
        for h in (2 * pair, 2 * pair + 1):
            acc = acc_ref[h]
            denom = FOX_HD * (1 - h % 2)
            outs.append(acc / acc[denom:denom + 1, :])
        o_ref[0, :, pair * PAIR_W:(pair + 1) * PAIR_W] = jnp.where(sub < FOX_HD, outs[0], outs[1]).T.astype(BF16)


def _fox_prompt(qt, ka, vt, *, tq):
    b, t, w = ka.shape
    tk = qt.shape[3]
    assert tk % tq == 0
    per_stream = lambda shape: pl.BlockSpec((1,) + shape, lambda i, j: (i,) + (0,) * len(shape))
    return pl.pallas_call(
        functools.partial(_fox_prompt_kernel, tq=tq, tk=tk),
        grid=(b, t // tq),
        in_specs=[pl.BlockSpec((1, 1, w, tq), lambda i, j: (i, (j * tq) // tk, 0, j % (tk // tq))),
                  per_stream((t, w)), per_stream((t // tk, w, tk))],
        out_specs=pl.BlockSpec((1, tq, MIX_W), lambda i, j: (i, j, 0)),
        out_shape=jax.ShapeDtypeStruct((b, t, MIX_W), BF16),
        scratch_shapes=([pltpu.VMEM((FOX_HEADS, 1, tq), F32), pltpu.VMEM((FOX_HEADS, PAIR_W, tq), F32)]
                        + [pltpu.VMEM((tk, tq), F32)] * (FOX_AHEAD + 1)),
        compiler_params=pltpu.CompilerParams(dimension_semantics=("arbitrary", "arbitrary"),
                                             vmem_limit_bytes=VMEM_LIMIT),
        name="fox_prompt",
    )(qt, ka, vt)


def _fox_sample_kernel(q_ref, kc_ref, vc_ref, kn_ref, vn_ref, lfc_t_ref, c_ref, ct_ref, o_ref):
    tn = q_ref.shape[1]
    p = lfc_t_ref.shape[2]
    cc = _cumsum(lfc_t_ref[0], 1, p)
    ck_cache = cc - cc[:, p - 1:]
    causal = (lax.broadcasted_iota(jnp.int32, (tn, V7X_LANES), 1)
              <= lax.broadcasted_iota(jnp.int32, (tn, V7X_LANES), 0))
    pad = jnp.zeros((V7X_LANES - tn, FOX_HD), BF16)
    col = [slice(h * FOX_HD, (h + 1) * FOX_HD) for h in range(FOX_HEADS)]
    logits = []
    for h in range(FOX_HEADS):
        qh = q_ref[0, :, col[h]]
        cq = c_ref[0, :, h:h + 1]
        kn = jnp.concatenate([kn_ref[0, :, col[h]], pad], axis=0)
        s_old = _dot(qh, kc_ref[0, h].astype(BF16)) + cq - ck_cache[h:h + 1, :]
        s_new = jnp.where(causal, _dot_nt(qh, kn) + cq - ct_ref[0, h:h + 1, :], NEG_BIG)
        logits.append((s_old, s_new))
    probs = []
    for s_old, s_new in logits:
        m = jnp.maximum(jnp.max(s_old, axis=-1, keepdims=True), jnp.max(s_new, axis=-1, keepdims=True))
        p_old = jnp.exp(s_old - m)
        p_new = jnp.exp(s_new - m)
        l = jnp.sum(p_old, axis=-1, keepdims=True) + jnp.sum(p_new, axis=-1, keepdims=True)
        probs.append((p_old.astype(BF16), p_new.astype(BF16), l))
    outs = []
    for h, (p_old, p_new, l) in enumerate(probs):
        vn = jnp.concatenate([vn_ref[0, :, col[h]], pad], axis=0)
        outs.append((_dot_nt(p_old, vc_ref[0, h].astype(BF16)) + _dot(p_new, vn)) / l)
    o_ref[0] = jnp.concatenate(outs, axis=-1).astype(BF16)


def _fox_sample(q, cache_kt, cache_vt, kb, vb, cache_logf_t, c, ct):
    b, tn, _ = q.shape
    p = cache_kt.shape[3]
    new = lambda w: pl.BlockSpec((1, tn, w), lambda i: (i, 0, 0))
    old = pl.BlockSpec((1, FOX_HEADS, FOX_HD, p), lambda i: (i, 0, 0, 0))
    return pl.pallas_call(
        _fox_sample_kernel,
        grid=(b,),
        in_specs=[new(MIX_W), old, old, new(MIX_W), new(MIX_W),
                  pl.BlockSpec((1, FOX_HEADS, p), lambda i: (i, 0, 0)),
                  new(FOX_HEADS), pl.BlockSpec((1, FOX_HEADS, V7X_LANES), lambda i: (i, 0, 0))],
        out_specs=new(MIX_W),
        out_shape=jax.ShapeDtypeStruct((b, tn, MIX_W), BF16),
        compiler_params=pltpu.CompilerParams(dimension_semantics=("arbitrary",), vmem_limit_bytes=VMEM_LIMIT),
        name="fox_sample",
    )(q, cache_kt, cache_vt, kb, vb, cache_logf_t, c, ct)


def _layer1_back_kernel(x_ref, mix_ref, mem_ref, g_post, g_mpre, g_mpost, w_out, w_up, w_down, y_ref, *, nb, tt):
    m = nb * tt
    x = x_ref[...].reshape(m, D_MODEL)
    mix = mix_ref[...].reshape(m, MIX_W)
    mem = mem_ref[...].reshape(m, MEM_W)
    x1 = [x[rows] + _rms(_dot(mix[rows], w_out[:MIX_W, :]) + _dot(mem[rows], w_out[MIX_W:, :]), g_post[...])
          for rows in _row_parts(m)]
    y = jnp.concatenate(_mlp(x1, g_mpre[...], g_mpost[...], w_up, w_down), axis=0)
    y_ref[...] = y.reshape(nb, tt, D_MODEL)


def _layer1_back(x, mix, mem, p, *, nb, tt):
    b, t, _ = x.shape
    tile = lambda w: pl.BlockSpec((nb, tt, w), lambda i, j: (i, j, 0))
    return pl.pallas_call(
        functools.partial(_layer1_back_kernel, nb=nb, tt=tt),
        grid=(b // nb, t // tt),
        in_specs=[tile(D_MODEL), tile(MIX_W), tile(MEM_W),
                  _layer_spec(p["g_mix_post"], 1), _layer_spec(p["g_mlp_pre"], 1), _layer_spec(p["g_mlp_post"], 1),
                  _layer_spec(p["w_out"][1], 0), _layer_spec(p["w_up"][1], 0), _layer_spec(p["w_down"][1], 0)],
        out_specs=tile(D_MODEL),
        out_shape=jax.ShapeDtypeStruct((b, t, D_MODEL), F32),
        compiler_params=pltpu.CompilerParams(dimension_semantics=("arbitrary", "arbitrary"),
                                             vmem_limit_bytes=VMEM_LIMIT),
        name="layer1_back",
    )(x, mix, mem, p["g_mix_post"], p["g_mlp_pre"], p["g_mlp_post"], p["w_out"][1], p["w_up"][1], p["w_down"][1])


def _token_tile(b, t):
    if t >= 512:
        return 1, 512
    return b, t


def _trunk(x, pool_prev, cache, mem_k, mem_v, p, later_f32=None):
    b, t, _ = x.shape
    nb, tt = _token_tile(b, t)
    pos_base = 0 if cache is None else POOL_BUF
    names = sorted(later_f32) if later_f32 else []
    y0, pool_state, *cast = _layer0(x, pool_prev, mem_k, mem_v, p, nb=nb, tt=tt, pos_base=pos_base,
                                    cast=[(later_f32[n], 1) for n in names])
    p = {**p, **{n: [p[n][0], w] for n, w in zip(names, cast)}}
    k, v, logf_t, mem, *att = _layer1_front(y0, mem_k, mem_v, p, nb=nb, tt=tt, fold=cache is None)
    logf = jnp.transpose(logf_t.reshape(b // nb, FOX_HEADS, nb, t), (0, 2, 3, 1)).reshape(b, t, FOX_HEADS)
    if cache is None:
        mix = _fox_prompt(*att, tq=FOX_TQ)
    else:
        q, kb, vb, c, ct = att
        cache_k, cache_v, cache_logf = cache
        ct_new = jnp.swapaxes(ct.reshape(FOX_HEADS, b, t), 0, 1)
        ct_new = jnp.pad(ct_new, ((0, 0), (0, 0), (0, V7X_LANES - t)))
        to_lanes = lambda a: jnp.transpose(a, (0, 2, 3, 1))
        mix = _fox_sample(q, to_lanes(cache_k), to_lanes(cache_v), kb, vb, jnp.swapaxes(cache_logf, 1, 2), c, ct_new)
    y = _layer1_back(y0, mix, mem, p, nb=nb, tt=tt)
    return (y, pool_state, k.reshape(b, t, FOX_HEADS, FOX_HD), v.reshape(b, t, FOX_HEADS, FOX_HD), logf), p


def kernel(x_prompt, x_sample, cache_pool, cache_k, cache_v, cache_logf, cache_mem_k, cache_mem_v, mem_prompt,
           g_mix_pre, g_mix_post, g_mlp_pre, g_mlp_post, w_in, w_out, w_pool, pool_scale, g_kv, w_kvf, b_f,
           g_mem, w_mem_kv, w_up, w_down):
    depth = w_in.shape[0]
    assert depth == 2 and w_pool.shape[0] == 1, "one pooling layer followed by one forgetting-attention layer"
    bp = x_prompt.shape[0]
    w_f = w_kvf[:, 2 * MIX_W:]
    fold_cols = lambda a: jnp.pad(jnp.concatenate([a, jnp.repeat(a, FOLD_W, axis=-1)], axis=-1),
                                  [(0, 0)] * (a.ndim - 1) + [(0, V7X_LANES - FOLD_SRC - FOLD_W * FOX_HEADS)])
    rows = lambda g: g.reshape(g.shape[0], 1, g.shape[1])
    p = dict(
        g_mix_pre=rows(g_mix_pre), g_mix_post=rows(g_mix_post), g_mlp_pre=rows(g_mlp_pre),
        g_mlp_post=rows(g_mlp_post), pool_scale=rows(pool_scale), g_kv=g_kv.reshape(1, D_MODEL),
        w_pool=w_pool.astype(BF16), w_kv=w_kvf[:, :2 * MIX_W].astype(BF16),
        w_f=fold_cols(w_f).astype(BF16), w_ft=w_f.T.astype(BF16),
        b_f=fold_cols(b_f).reshape(1, V7X_LANES), b_ft=b_f.reshape(FOX_HEADS, 1),
    )
    later_f32 = dict(w_in=w_in, w_out=w_out, w_up=w_up, w_down=w_down)
    p.update({n: [w[:1].astype(BF16), None] for n, w in later_f32.items()})

    mem_k_prompt, mem_v_prompt, mkh, mvh = _mem_project(mem_prompt, g_mem, w_mem_kv)

    (y_p, pool_p, k_p, v_p, logf_p), p = _trunk(x_prompt, jnp.zeros((bp, HALO, MIX_W), F32), None, mkh, mvh, p,
                                                later_f32)

    pool_prev = jnp.pad(cache_pool[0], ((0, 0), (HALO - POOL_BUF, 0), (0, 0)))
    head_major = lambda a: jnp.swapaxes(a.astype(BF16), 2, 3)
    (y_s, pool_s, k_s, v_s, logf_s), _ = _trunk(x_sample, pool_prev, (cache_k, cache_v, cache_logf),
                                                head_major(cache_mem_k), head_major(cache_mem_v), p)

    return (y_p, y_s, pool_p, pool_s, k_p, v_p, logf_p, k_s, v_s, logf_s, mem_k_prompt, mem_v_prompt)
```

```python
import functools

import jax
import jax.numpy as jnp
from jax import lax
from jax.experimental import pallas as pl
from jax.experimental.pallas import tpu as pltpu

F32 = jnp.float32
BF16 = jnp.bfloat16

D_MODEL = 1024
MIX_W = D_MODEL // 2
MEM_W = D_MODEL - MIX_W
POOL_WINDOWS = (2, 4, 8, 16)
POOL_GROUP = MIX_W // len(POOL_WINDOWS)
POOL_BUF = max(POOL_WINDOWS) - 1
FOX_HEADS = 8
FOX_HD = MIX_W // FOX_HEADS
MEM_HEADS = 4
MEM_HD = MEM_W // MEM_HEADS
D_FF = 4 * D_MODEL
EPS = 1e-6

V7X_LANES = 128
V7X_SUBLANES_F32 = 8
V7X_VMEM_BYTES = 64 * 1024 * 1024

HALO = 2 * V7X_SUBLANES_F32
FF_CHUNK = 1024
ROW_PARTS = 2
PAIR_W = 2 * FOX_HD
NEG_BIG = -1e30
LOG2E = 1.4426950408889634
FOLD_SRC = FOX_HEADS
FOLD_W = 6
FOX_AHEAD = 2
assert FOLD_SRC + FOLD_W * FOX_HEADS <= V7X_LANES
VMEM_LIMIT = V7X_VMEM_BYTES - 8 * 1024 * 1024

assert HALO >= POOL_BUF and PAIR_W == V7X_LANES and POOL_GROUP == V7X_LANES and MEM_HD == V7X_LANES


def _const_spec(shape):
    zeros = (0,) * len(shape)
    return pl.BlockSpec(shape, lambda *_: zeros, pipeline_mode=pl.Buffered(1))


def _layer_spec(stacked, layer):
    index = (layer,) + (0,) * (stacked.ndim - 1)
    return pl.BlockSpec((None,) + stacked.shape[1:], lambda *_: index, pipeline_mode=pl.Buffered(1))


def _mem_spec(mem, layer, nb):
    return pl.BlockSpec((None, nb) + mem.shape[2:], lambda i, j: (layer, i, 0, 0, 0))


def _unit_rms(x):
    return x * lax.rsqrt(jnp.mean(x * x, axis=-1, keepdims=True) + EPS)


def _rms(x, g):
    return _unit_rms(x) * g


def _dot(a, b):
    return jnp.dot(a, b, preferred_element_type=F32)


def _dot_nt(a, b):
    return lax.dot_general(a, b, (((1,), (1,)), ((), ())), preferred_element_type=F32)


def _log_sigmoid(x):
    return jnp.minimum(x, 0.0) - jnp.log1p(jnp.exp(-jnp.abs(x)))


def _cumsum(x, axis, seg):
    assert seg & (seg - 1) == 0 and x.shape[axis] % seg == 0
    n = x.shape[axis]
    vreg = (V7X_SUBLANES_F32, V7X_LANES)[axis]
    idx = lax.broadcasted_iota(jnp.int32, x.shape, axis) & (seg - 1)
    k = 1
    while k < seg:
        if seg == n and k % vreg == 0:
            zeros = jnp.zeros(x.shape[:axis] + (k,) + x.shape[axis + 1:], x.dtype)
            x = x + jnp.concatenate([zeros, lax.slice_in_dim(x, 0, n - k, axis=axis)], axis=axis)
        else:
            x = x + jnp.where(idx >= k, pltpu.roll(x, k, axis=axis), 0.0)
        k *= 2
    return x


def _mem_logits(q, mk_ref, b):
    qs = (q * (MEM_HD ** -0.5 * LOG2E)).astype(BF16)
    return [_dot_nt(qs[:, h * MEM_HD:(h + 1) * MEM_HD], mk_ref[b, h]) for h in range(MEM_HEADS)]


def _mem_probs(logits):
    probs = []
    for s in logits:
        p = jnp.exp2(s - jnp.max(s, axis=-1, keepdims=True))
        probs.append((p.astype(BF16), jnp.sum(p, axis=-1, keepdims=True)))
    return probs


def _mem_values(probs, mv_ref, b):
    return jnp.concatenate([_dot(p, mv_ref[b, h]) / l for h, (p, l) in enumerate(probs)], axis=-1)


def _pool_mixer(u, halo, pos0, w_pool_ref, scale):
    tt = u.shape[0]
    ext = jnp.concatenate([halo, u], axis=0)
    pos = pos0 + lax.broadcasted_iota(jnp.int32, (tt, 1), 0)
    outs = []
    for g, w in enumerate(POOL_WINDOWS):
        sl = slice(g * POOL_GROUP, (g + 1) * POOL_GROUP)
        s = ext[:, sl]
        k = 1
        while k < w:
            s = s + pltpu.roll(s, k, axis=0)
            k *= 2
        cnt = jnp.minimum(pos + 1, w).astype(F32)
        pooled = s[HALO:] / cnt - u[:, sl]
        outs.append(_dot(pooled.astype(BF16), w_pool_ref[g]))
    return jnp.concatenate(outs, axis=-1) * scale


def _row_parts(m):
    n = ROW_PARTS if m >= 512 else 1
    return [slice(i * m // n, (i + 1) * m // n) for i in range(n)]


def _mlp(x1_parts, g_pre, g_post, w_up_ref, w_down_ref):
    n_chunks = D_FF // FF_CHUNK
    chunk = lambda c: slice(c * FF_CHUNK, (c + 1) * FF_CHUNK)
    act = lambda up: jnp.square(jnp.maximum(up, 0.0)).astype(BF16)
    hm_parts = [_rms(x1, g_pre).astype(BF16) for x1 in x1_parts]
    hm = jnp.concatenate(hm_parts, axis=0)
    parts = _row_parts(hm.shape[0])
    acc = None
    for c in range(n_chunks):
        if c == 0:
            a = jnp.concatenate([act(_dot(h, w_up_ref[:, chunk(c)])) for h in hm_parts], axis=0)
        else:
            a = act(_dot(hm, w_up_ref[:, chunk(c)]))
        if c < n_chunks - 1:
            d = _dot(a, w_down_ref[chunk(c), :])
            acc = d if acc is None else acc + d
    return [x1 + _rms(acc[rows] + _dot(a[rows], w_down_ref[chunk(n_chunks - 1), :]), g_post)
            for x1, rows in zip(x1_parts, parts)]


def _mem_project_kernel(mem_ref, g_ref, w_ref, mk_ref, mv_ref, mkh_ref, mvh_ref):
    xn = _unit_rms(mem_ref[0])
    for l in range(w_ref.shape[0]):
        kv = _dot((xn * g_ref[l]).astype(BF16), w_ref[l].astype(BF16))
        for h in range(MEM_HEADS):
            mk = kv[:, h * MEM_HD:(h + 1) * MEM_HD]
            mv = kv[:, MEM_W + h * MEM_HD:MEM_W + (h + 1) * MEM_HD]
            mk_ref[l, 0, :, h, :] = mk
            mv_ref[l, 0, :, h, :] = mv
            mkh_ref[l, 0, h] = mk.astype(BF16)
            mvh_ref[l, 0, h] = mv.astype(BF16)


def _mem_project(mem, g_mem, w_mem_kv):
    b, n, _ = mem.shape
    depth = w_mem_kv.shape[0]
    out = jax.ShapeDtypeStruct((depth, b, n, MEM_HEADS, MEM_HD), F32)
    out_h = jax.ShapeDtypeStruct((depth, b, MEM_HEADS, n, MEM_HD), BF16)
    g_mem = g_mem.reshape(depth, 1, D_MODEL)
    return pl.pallas_call(
        _mem_project_kernel,
        grid=(b,),
        in_specs=[pl.BlockSpec((1, n, D_MODEL), lambda i: (i, 0, 0)),
                  _const_spec(g_mem.shape), _const_spec(w_mem_kv.shape)],
        out_specs=([pl.BlockSpec((depth, 1, n, MEM_HEADS, MEM_HD), lambda i: (0, i, 0, 0, 0))] * 2
                   + [pl.BlockSpec((depth, 1, MEM_HEADS, n, MEM_HD), lambda i: (0, i, 0, 0, 0))] * 2),
        out_shape=[out, out, out_h, out_h],
        compiler_params=pltpu.CompilerParams(dimension_semantics=("arbitrary",), vmem_limit_bytes=VMEM_LIMIT),
        name="mem_project",
    )(mem, g_mem, w_mem_kv)


def _layer0_kernel(x_ref, prev_ref, mk_ref, mv_ref, g_pre, g_post, g_mpre, g_mpost,
                   w_in, w_out, w_pool, pscale, w_up, w_down, *refs, nb, tt, pos_base, n_cast):
    cast_in, (y_ref, state_ref), cast_out, (halo_ref,) = (
        refs[:n_cast], refs[n_cast:n_cast + 2], refs[n_cast + 2:2 * n_cast + 2], refs[2 * n_cast + 2:])
    t = pl.program_id(1)

    @pl.when(t == 0)
    def _():
        halo_ref[...] = prev_ref[...]

    for src, dst in zip(cast_in, cast_out):
        dst[...] = src[...].astype(BF16)

    x = x_ref[...].reshape(nb * tt, D_MODEL)
    proj = jnp.concatenate([_dot(_rms(x[rows], g_pre[...]).astype(BF16), w_in[...])
                            for rows in _row_parts(nb * tt)], axis=0)
    cats = []
    for b in range(nb):
        rows = slice(b * tt, (b + 1) * tt)
        u = proj[rows, :MIX_W]
        logits = _mem_logits(proj[rows, MIX_W:], mk_ref, b)
        mix = _pool_mixer(u, halo_ref[b], pos_base + t * tt, w_pool, pscale[...])
        halo_ref[b] = u[tt - HALO:, :]
        state_ref[0, b] = u[tt - POOL_BUF:, :]
        mem = _mem_values(_mem_probs(logits), mv_ref, b)
        cats.append(jnp.concatenate([mix, mem], axis=-1))
    cat = (cats[0] if nb == 1 else jnp.concatenate(cats, axis=0)).astype(BF16)
    x1 = [x[rows] + _rms(_dot(cat[rows], w_out[...]), g_post[...]) for rows in _row_parts(nb * tt)]
    y = jnp.concatenate(_mlp(x1, g_mpre[...], g_mpost[...], w_up, w_down), axis=0)
    y_ref[...] = y.reshape(nb, tt, D_MODEL)


def _layer0(x, prev, mem_k, mem_v, p, *, nb, tt, pos_base, cast=()):
    b, t, _ = x.shape
    grid = (b // nb, t // tt)
    steps = grid[0] * grid[1]
    tile = pl.BlockSpec((nb, tt, D_MODEL), lambda i, j: (i, j, 0))
    slab = lambda w, l: pl.BlockSpec((1, w.shape[1] // steps, w.shape[2]), lambda i, j: (l, i * grid[1] + j, 0))
    return pl.pallas_call(
        functools.partial(_layer0_kernel, nb=nb, tt=tt, pos_base=pos_base, n_cast=len(cast)),
        grid=grid,
        in_specs=[tile, pl.BlockSpec((nb, HALO, MIX_W), lambda i, j: (i, 0, 0)),
                  _mem_spec(mem_k, 0, nb), _mem_spec(mem_v, 0, nb),
                  _layer_spec(p["g_mix_pre"], 0), _layer_spec(p["g_mix_post"], 0),
                  _layer_spec(p["g_mlp_pre"], 0), _layer_spec(p["g_mlp_post"], 0),
                  _layer_spec(p["w_in"][0], 0), _layer_spec(p["w_out"][0], 0),
                  _layer_spec(p["w_pool"], 0), _layer_spec(p["pool_scale"], 0),
                  _layer_spec(p["w_up"][0], 0), _layer_spec(p["w_down"][0], 0)]
                 + [slab(w, l) for w, l in cast],
        out_specs=[tile, pl.BlockSpec((1, nb, POOL_BUF, MIX_W), lambda i, j: (0, i, 0, 0))]
                  + [slab(w, 0) for w, _ in cast],
        out_shape=[jax.ShapeDtypeStruct((b, t, D_MODEL), F32),
                   jax.ShapeDtypeStruct((1, b, POOL_BUF, MIX_W), F32)]
                  + [jax.ShapeDtypeStruct((1,) + w.shape[1:], BF16) for w, _ in cast],
        scratch_shapes=[pltpu.VMEM((nb, HALO, MIX_W), F32)],
        compiler_params=pltpu.CompilerParams(dimension_semantics=("arbitrary", "arbitrary"),
                                             vmem_limit_bytes=VMEM_LIMIT),
        name="layer0",
    )(x, prev, mem_k, mem_v, p["g_mix_pre"], p["g_mix_post"], p["g_mlp_pre"], p["g_mlp_post"],
      p["w_in"][0], p["w_out"][0], p["w_pool"], p["pool_scale"], p["w_up"][0], p["w_down"][0],
      *[w for w, _ in cast])


def _bf16_part(x):
    return x.astype(BF16).astype(F32)


def _fold_operands(q, k, v, c):
    cl = c * LOG2E
    hi = _bf16_part(cl)
    mid = _bf16_part(cl - hi)
    lo = _bf16_part(cl - hi - mid)
    lane = lax.broadcasted_iota(jnp.int32, (1, V7X_LANES), 1)
    piece = lax.rem(lane + (3 - FOLD_SRC % 3), 3)
    d = jnp.where(piece == 0, hi, jnp.where(piece == 1, mid, lo))
    qs, ks, vs = [], [], []
    for h in range(FOX_HEADS):
        own = (h % 2) * FOX_HD
        e0 = FOX_HD - own
        cols = slice((h // 2) * PAIR_W, (h // 2 + 1) * PAIR_W)
        dh = pltpu.roll(d, (e0 - (FOLD_SRC + FOLD_W * h)) % V7X_LANES, axis=1)
        in_head = (lane >= own) & (lane < own + FOX_HD)
        first = ((lane >= e0) & (lane < e0 + 3)).astype(F32)
        second = ((lane >= e0 + 3) & (lane < e0 + 6)).astype(F32)
        xq = dh * first + second
        xk = first - dh * second
        xv = (lane == e0).astype(F32)
        qs.append(jnp.where(in_head, q[:, cols] * (FOX_HD ** -0.5 * LOG2E), xq).T.astype(BF16))
        ks.append(jnp.where(in_head, k[:, cols], xk).astype(BF16))
        vs.append(jnp.where(in_head, v[:, cols], xv).T.astype(BF16))
    return jnp.concatenate(qs, axis=0), jnp.concatenate(ks, axis=-1), jnp.concatenate(vs, axis=0)


def _layer1_front_kernel(x_ref, mk_ref, mv_ref, g_kv, g_pre, w_kv, w_f, w_ft, b_f, b_ft, w_in, *refs,
                         nb, tt, fold):
    if fold:
        k_ref, v_ref, logft_ref, mem_ref, qa_ref, ka_ref, va_ref, carry_ref = refs
    else:
        k_ref, v_ref, logft_ref, mem_ref, q_ref, kb_ref, vb_ref, c_ref, ct_ref = refs
    if fold:
        @pl.when(pl.program_id(1) == 0)
        def _():
            carry_ref[...] = jnp.zeros_like(carry_ref)

    xn = _unit_rms(x_ref[...].reshape(nb * tt, D_MODEL))
    hk = (xn * g_kv[...]).astype(BF16)
    kv = _dot(hk, w_kv[...])
    k = kv[:, :MIX_W]
    v = kv[:, MIX_W:]
    k_ref[...] = k.reshape(nb, tt, MIX_W)
    v_ref[...] = v.reshape(nb, tt, MIX_W)
    logf = _log_sigmoid(_dot(hk, w_f[...]) + b_f[...])
    proj = _dot((xn * g_pre[...]).astype(BF16), w_in[...])
    q = proj[:, :MIX_W]
    mem_logits = [_mem_logits(proj[b * tt:(b + 1) * tt, MIX_W:], mk_ref, b) for b in range(nb)]
    c = _cumsum(logf, 0, tt)
    if fold:
        c = c + carry_ref[...]
        carry_ref[...] = c[tt - 1:, :]
        qa, ka, va = _fold_operands(q, k, v, c)
        qa_ref[0, 0] = qa
        ka_ref[0] = ka
        va_ref[0, 0] = va
        logft_ref[0] = logf.T[:FOX_HEADS, :]
    else:
        q_ref[...] = (q * (FOX_HD ** -0.5)).astype(BF16).reshape(nb, tt, MIX_W)
        kb_ref[...] = k.astype(BF16).reshape(nb, tt, MIX_W)
        vb_ref[...] = v.astype(BF16).reshape(nb, tt, MIX_W)
        logf_t = _log_sigmoid(_dot_nt(w_ft[...], hk) + b_ft[...])
        logft_ref[0] = logf_t
        ct_ref[0, 0] = _cumsum(logf_t, 1, tt)
    for b in range(nb):
        rows = slice(b * tt, (b + 1) * tt)
        if not fold:
            c_ref[b] = c[rows, :FOX_HEADS]
        mem_ref[b] = _mem_values(_mem_probs(mem_logits[b]), mv_ref, b).astype(BF16)


def _layer1_front(x, mem_k, mem_v, p, *, nb, tt, fold):
    b, t, _ = x.shape
    assert (nb == 1) if fold else (t == tt)
    tile = lambda w: pl.BlockSpec((nb, tt, w), lambda i, j: (i, j, 0))
    act = lambda w, dt: jax.ShapeDtypeStruct((b, t, w), dt)
    out_specs = [tile(MIX_W), tile(MIX_W), pl.BlockSpec((1, FOX_HEADS, nb * tt), lambda i, j: (i, 0, j)), tile(MEM_W)]
    out_shape = [act(MIX_W, F32), act(MIX_W, F32), jax.ShapeDtypeStruct((b // nb, FOX_HEADS, nb * t), F32),
                 act(MEM_W, BF16)]
    if fold:
        wf = FOX_HEADS * PAIR_W
        tile_t = pl.BlockSpec((1, 1, wf, tt), lambda i, j: (i, j, 0, 0))
        act_t = jax.ShapeDtypeStruct((b, t // tt, wf, tt), BF16)
        out_specs += [tile_t, tile(wf), tile_t]
        out_shape += [act_t, act(wf, BF16), act_t]
        scratch = [pltpu.VMEM((1, V7X_LANES), F32)]
    else:
        out_specs += [tile(MIX_W)] * 3 + [tile(FOX_HEADS),
                                          pl.BlockSpec((1, 1, FOX_HEADS, nb * tt), lambda i, j: (i, j, 0, 0))]
        out_shape += [act(MIX_W, BF16)] * 3 + [act(FOX_HEADS, F32),
                                               jax.ShapeDtypeStruct((b // nb, t // tt, FOX_HEADS, nb * tt), F32)]
        scratch = []
    return pl.pallas_call(
        functools.partial(_layer1_front_kernel, nb=nb, tt=tt, fold=fold),
        grid=(b // nb, t // tt),
        in_specs=[tile(D_MODEL), _mem_spec(mem_k, 1, nb), _mem_spec(mem_v, 1, nb),
                  _const_spec(p["g_kv"].shape), _layer_spec(p["g_mix_pre"], 1),
                  _const_spec(p["w_kv"].shape), _const_spec(p["w_f"].shape), _const_spec(p["w_ft"].shape),
                  _const_spec(p["b_f"].shape), _const_spec(p["b_ft"].shape), _layer_spec(p["w_in"][1], 0)],
        out_specs=out_specs,
        out_shape=out_shape,
        scratch_shapes=scratch,
        compiler_params=pltpu.CompilerParams(dimension_semantics=("arbitrary", "arbitrary"),
                                             vmem_limit_bytes=VMEM_LIMIT),
        name="layer1_front",
    )(x, mem_k, mem_v, p["g_kv"], p["g_mix_pre"], p["w_kv"], p["w_f"], p["w_ft"], p["b_f"], p["b_ft"], p["w_in"][1])


def _fox_prompt_kernel(qt_ref, k_ref, vt_ref, o_ref, m_ref, acc_ref, *s_refs, t):
    i = pl.program_id(1)
    half = t // 2
    sub = lax.broadcasted_iota(jnp.int32, (PAIR_W, 1), 0)
    col = [slice(h * PAIR_W, (h + 1) * PAIR_W) for h in range(FOX_HEADS)]
    qt = [qt_ref[0, 0, c, :] for c in col]
    slot = lambda h: s_refs[h % len(s_refs)]

    def heads(logits, update):
        top = [logits(h) for h in range(FOX_AHEAD)]
        for h in range(FOX_HEADS):
            if h + FOX_AHEAD < FOX_HEADS:
                top.append(logits(h + FOX_AHEAD))
            update(h, top[h])

    tri = (lax.broadcasted_iota(jnp.int32, (half, half), 0) <= lax.broadcasted_iota(jnp.int32, (half, half), 1))
    lo = pl.ds(pl.multiple_of(i * t, t), half)
    hi = pl.ds(pl.multiple_of(i * t + half, half), half)

    def diag_logits(h):
        s_lo = _dot(k_ref[0, lo, col[h]], qt[h])
        s_hi = _dot(k_ref[0, hi, col[h]], qt[h][:, half:])
        s_lo = jnp.concatenate([jnp.where(tri, s_lo[:, :half], NEG_BIG), s_lo[:, half:]], axis=1)
        s_hi = jnp.where(tri, s_hi, NEG_BIG)
        slot(h)[:half, :] = s_lo
        slot(h)[half:, half:] = s_hi
        top = jnp.max(s_lo, axis=0, keepdims=True)
        return jnp.concatenate([top[:, :half], jnp.maximum(top[:, half:], jnp.max(s_hi, axis=0, keepdims=True))],
                               axis=1)

    def diag_update(h, m):
        p_lo = jnp.exp2(slot(h)[:half, :] - m).astype(BF16)
        p_hi = jnp.exp2(slot(h)[half:, half:] - m[:, half:]).astype(BF16)
        acc = _dot(vt_ref[0, i, col[h], :half], p_lo)
        acc_ref[h] = jnp.concatenate([acc[:, :half], acc[:, half:] + _dot(vt_ref[0, i, col[h], half:], p_hi)], axis=1)
        m_ref[h] = m

    heads(diag_logits, diag_update)

    def full_tile(j, carry):
        rows = pl.ds(pl.multiple_of(j * t, t), t)

        def logits(h):
            s = _dot(k_ref[0, rows, col[h]], qt[h])
            slot(h)[...] = s
            return jnp.max(s, axis=0, keepdims=True)

        def update(h, top):
            m = m_ref[h]
            m_new = jnp.maximum(m, top)
            p = jnp.exp2(slot(h)[...] - m_new).astype(BF16)
            acc_ref[h] = jnp.exp2(m - m_new) * acc_ref[h] + _dot(vt_ref[0, j, col[h], :], p)
            m_ref[h] = m_new

        heads(logits, update)
        return carry

    lax.fori_loop(0, i, full_tile, None)
    for pair in range(FOX_HEADS // 2):
        outs = []
        for h in (2 * pair, 2 * pair + 1):
            acc = acc_ref[h]
            denom = FOX_HD * (1 - h % 2)
            outs.append(acc / acc[denom:denom + 1, :])
        o_ref[0, :, pair * PAIR_W:(pair + 1) * PAIR_W] = jnp.where(sub < FOX_HD, outs[0], outs[1]).T.astype(BF16)


def _fox_prompt(qt, ka, vt):
    b, n_tiles, w, tile = qt.shape
    per_stream = lambda shape: pl.BlockSpec((1,) + shape, lambda i, j: (i,) + (0,) * len(shape))
    return pl.pallas_call(
        functools.partial(_fox_prompt_kernel, t=tile),
        grid=(b, n_tiles),
        in_specs=[pl.BlockSpec((1, 1, w, tile), lambda i, j: (i, j, 0, 0)),
                  per_stream((n_tiles * tile, w)), per_stream((n_tiles, w, tile))],
        out_specs=pl.BlockSpec((1, tile, MIX_W), lambda i, j: (i, j, 0)),
        out_shape=jax.ShapeDtypeStruct((b, n_tiles * tile, MIX_W), BF16),
        scratch_shapes=([pltpu.VMEM((FOX_HEADS, 1, tile), F32), pltpu.VMEM((FOX_HEADS, PAIR_W, tile), F32)]
                        + [pltpu.VMEM((tile, tile), F32)] * (FOX_AHEAD + 1)),
        compiler_params=pltpu.CompilerParams(dimension_semantics=("arbitrary", "arbitrary"),
                                             vmem_limit_bytes=VMEM_LIMIT),
        name="fox_prompt",
    )(qt, ka, vt)


def _fox_sample_kernel(q_ref, kc_ref, vc_ref, kn_ref, vn_ref, lfc_t_ref, c_ref, ct_ref, o_ref):
    tn = q_ref.shape[1]
    p = lfc_t_ref.shape[2]
    cc = _cumsum(lfc_t_ref[0], 1, p)
    ck_cache = cc - cc[:, p - 1:]
    causal = (lax.broadcasted_iota(jnp.int32, (tn, V7X_LANES), 1)
              <= lax.broadcasted_iota(jnp.int32, (tn, V7X_LANES), 0))
    pad = jnp.zeros((V7X_LANES - tn, FOX_HD), BF16)
    col = [slice(h * FOX_HD, (h + 1) * FOX_HD) for h in range(FOX_HEADS)]
    logits = []
    for h in range(FOX_HEADS):
        qh = q_ref[0, :, col[h]]
        cq = c_ref[0, :, h:h + 1]
        kn = jnp.concatenate([kn_ref[0, :, col[h]], pad], axis=0)
        s_old = _dot(qh, kc_ref[0, h].astype(BF16)) + cq - ck_cache[h:h + 1, :]
        s_new = jnp.where(causal, _dot_nt(qh, kn) + cq - ct_ref[0, h:h + 1, :], NEG_BIG)
        logits.append((s_old, s_new))
    probs = []
    for s_old, s_new in logits:
        m = jnp.maximum(jnp.max(s_old, axis=-1, keepdims=True), jnp.max(s_new, axis=-1, keepdims=True))
        p_old = jnp.exp(s_old - m)
        p_new = jnp.exp(s_new - m)
        l = jnp.sum(p_old, axis=-1, keepdims=True) + jnp.sum(p_new, axis=-1, keepdims=True)
        probs.append((p_old.astype(BF16), p_new.astype(BF16), l))
    outs = []
    for h, (p_old, p_new, l) in enumerate(probs):
        vn = jnp.concatenate([vn_ref[0, :, col[h]], pad], axis=0)
        outs.append((_dot_nt(p_old, vc_ref[0, h].astype(BF16)) + _dot(p_new, vn)) / l)
    o_ref[0] = jnp.concatenate(outs, axis=-1).astype(BF16)


def _fox_sample(q, cache_kt, cache_vt, kb, vb, cache_logf_t, c, ct):
    b, tn, _ = q.shape
    p = cache_kt.shape[3]
    new = lambda w: pl.BlockSpec((1, tn, w), lambda i: (i, 0, 0))
    old = pl.BlockSpec((1, FOX_HEADS, FOX_HD, p), lambda i: (i, 0, 0, 0))
    return pl.pallas_call(
        _fox_sample_kernel,
        grid=(b,),
        in_specs=[new(MIX_W), old, old, new(MIX_W), new(MIX_W),
                  pl.BlockSpec((1, FOX_HEADS, p), lambda i: (i, 0, 0)),
                  new(FOX_HEADS), pl.BlockSpec((1, FOX_HEADS, V7X_LANES), lambda i: (i, 0, 0))],
        out_specs=new(MIX_W),
        out_shape=jax.ShapeDtypeStruct((b, tn, MIX_W), BF16),
        compiler_params=pltpu.CompilerParams(dimension_semantics=("arbitrary",), vmem_limit_bytes=VMEM_LIMIT),
        name="fox_sample",
    )(q, cache_kt, cache_vt, kb, vb, cache_logf_t, c, ct)


def _layer1_back_kernel(x_ref, mix_ref, mem_ref, g_post, g_mpre, g_mpost, w_out, w_up, w_down, y_ref, *, nb, tt):
    m = nb * tt
    x = x_ref[...].reshape(m, D_MODEL)
    mix = mix_ref[...].reshape(m, MIX_W)
    mem = mem_ref[...].reshape(m, MEM_W)
    x1 = [x[rows] + _rms(_dot(mix[rows], w_out[:MIX_W, :]) + _dot(mem[rows], w_out[MIX_W:, :]), g_post[...])
          for rows in _row_parts(m)]
    y = jnp.concatenate(_mlp(x1, g_mpre[...], g_mpost[...], w_up, w_down), axis=0)
    y_ref[...] = y.reshape(nb, tt, D_MODEL)


def _layer1_back(x, mix, mem, p, *, nb, tt):
    b, t, _ = x.shape
    tile = lambda w: pl.BlockSpec((nb, tt, w), lambda i, j: (i, j, 0))
    return pl.pallas_call(
        functools.partial(_layer1_back_kernel, nb=nb, tt=tt),
        grid=(b // nb, t // tt),
        in_specs=[tile(D_MODEL), tile(MIX_W), tile(MEM_W),
                  _layer_spec(p["g_mix_post"], 1), _layer_spec(p["g_mlp_pre"], 1), _layer_spec(p["g_mlp_post"], 1),
                  _layer_spec(p["w_out"][1], 0), _layer_spec(p["w_up"][1], 0), _layer_spec(p["w_down"][1], 0)],
        out_specs=tile(D_MODEL),
        out_shape=jax.ShapeDtypeStruct((b, t, D_MODEL), F32),
        compiler_params=pltpu.CompilerParams(dimension_semantics=("arbitrary", "arbitrary"),
                                             vmem_limit_bytes=VMEM_LIMIT),
        name="layer1_back",
    )(x, mix, mem, p["g_mix_post"], p["g_mlp_pre"], p["g_mlp_post"], p["w_out"][1], p["w_up"][1], p["w_down"][1])


def _token_tile(b, t):
    if t >= 512:
        return 1, 512
    return b, t


def _trunk(x, pool_prev, cache, mem_k, mem_v, p, later_f32=None):
    b, t, _ = x.shape
    nb, tt = _token_tile(b, t)
    pos_base = 0 if cache is None else POOL_BUF
    names = sorted(later_f32) if later_f32 else []
    y0, pool_state, *cast = _layer0(x, pool_prev, mem_k, mem_v, p, nb=nb, tt=tt, pos_base=pos_base,
                                    cast=[(later_f32[n], 1) for n in names])
    p = {**p, **{n: [p[n][0], w] for n, w in zip(names, cast)}}
    k, v, logf_t, mem, *att = _layer1_front(y0, mem_k, mem_v, p, nb=nb, tt=tt, fold=cache is None)
    logf = jnp.transpose(logf_t.reshape(b // nb, FOX_HEADS, nb, t), (0, 2, 3, 1)).reshape(b, t, FOX_HEADS)
    if cache is None:
        mix = _fox_prompt(*att)
    else:
        q, kb, vb, c, ct = att
        cache_k, cache_v, cache_logf = cache
        ct_new = jnp.swapaxes(ct.reshape(FOX_HEADS, b, t), 0, 1)
        ct_new = jnp.pad(ct_new, ((0, 0), (0, 0), (0, V7X_LANES - t)))
        to_lanes = lambda a: jnp.transpose(a, (0, 2, 3, 1))
        mix = _fox_sample(q, to_lanes(cache_k), to_lanes(cache_v), kb, vb, jnp.swapaxes(cache_logf, 1, 2), c, ct_new)
    y = _layer1_back(y0, mix, mem, p, nb=nb, tt=tt)
    return (y, pool_state, k.reshape(b, t, FOX_HEADS, FOX_HD), v.reshape(b, t, FOX_HEADS, FOX_HD), logf), p


def kernel(x_prompt, x_sample, cache_pool, cache_k, cache_v, cache_logf, cache_mem_k, cache_mem_v, mem_prompt,
           g_mix_pre, g_mix_post, g_mlp_pre, g_mlp_post, w_in, w_out, w_pool, pool_scale, g_kv, w_kvf, b_f,
           g_mem, w_mem_kv, w_up, w_down):
    depth = w_in.shape[0]
    assert depth == 2 and w_pool.shape[0] == 1, "one pooling layer followed by one forgetting-attention layer"
    bp = x_prompt.shape[0]
    w_f = w_kvf[:, 2 * MIX_W:]
    fold_cols = lambda a: jnp.pad(jnp.concatenate([a, jnp.repeat(a, FOLD_W, axis=-1)], axis=-1),
                                  [(0, 0)] * (a.ndim - 1) + [(0, V7X_LANES - FOLD_SRC - FOLD_W * FOX_HEADS)])
    rows = lambda g: g.reshape(g.shape[0], 1, g.shape[1])
    p = dict(
        g_mix_pre=rows(g_mix_pre), g_mix_post=rows(g_mix_post), g_mlp_pre=rows(g_mlp_pre),
        g_mlp_post=rows(g_mlp_post), pool_scale=rows(pool_scale), g_kv=g_kv.reshape(1, D_MODEL),
        w_pool=w_pool.astype(BF16), w_kv=w_kvf[:, :2 * MIX_W].astype(BF16),
        w_f=fold_cols(w_f).astype(BF16), w_ft=w_f.T.astype(BF16),
        b_f=fold_cols(b_f).reshape(1, V7X_LANES), b_ft=b_f.reshape(FOX_HEADS, 1),
    )
    later_f32 = dict(w_in=w_in, w_out=w_out, w_up=w_up, w_down=w_down)
    p.update({n: [w[:1].astype(BF16), None] for n, w in later_f32.items()})

    mem_k_prompt, mem_v_prompt, mkh, mvh = _mem_project(mem_prompt, g_mem, w_mem_kv)

    (y_p, pool_p, k_p, v_p, logf_p), p = _trunk(x_prompt, jnp.zeros((bp, HALO, MIX_W), F32), None, mkh, mvh, p,
                                                later_f32)

    pool_prev = jnp.pad(cache_pool[0], ((0, 0), (HALO - POOL_BUF, 0), (0, 0)))
    head_major = lambda a: jnp.swapaxes(a.astype(BF16), 2, 3)
    (y_s, pool_s, k_s, v_s, logf_s), _ = _trunk(x_sample, pool_prev, (cache_k, cache_v, cache_logf),
                                                head_major(cache_mem_k), head_major(cache_mem_v), p)

    return (y_p, y_s, pool_p, pool_s, k_p, v_p, logf_p, k_s, v_s, logf_s, mem_k_prompt, mem_v_prompt)
```

```python
import functools

import jax
import jax.numpy as jnp
from jax import lax
from jax.experimental import pallas as pl
from jax.experimental.pallas import tpu as pltpu

F32 = jnp.float32
BF16 = jnp.bfloat16

D_MODEL = 1024
MIX_W = D_MODEL // 2
MEM_W = D_MODEL - MIX_W
POOL_WINDOWS = (2, 4, 8, 16)
POOL_GROUP = MIX_W // len(POOL_WINDOWS)
POOL_BUF = max(POOL_WINDOWS) - 1
FOX_HEADS = 8
FOX_HD = MIX_W // FOX_HEADS
MEM_HEADS = 4
MEM_HD = MEM_W // MEM_HEADS
D_FF = 4 * D_MODEL
EPS = 1e-6

V7X_LANES = 128
V7X_SUBLANES_F32 = 8
V7X_VMEM_BYTES = 64 * 1024 * 1024

HALO = 2 * V7X_SUBLANES_F32
FF_CHUNK = 1024
ROW_PARTS = 2
PAIR_W = 2 * FOX_HD
NEG_BIG = -1e30
LOG2E = 1.4426950408889634
FOLD_SRC = FOX_HEADS
FOLD_W = 6
FOX_AHEAD = 2
assert FOLD_SRC + FOLD_W * FOX_HEADS <= V7X_LANES
VMEM_LIMIT = V7X_VMEM_BYTES - 8 * 1024 * 1024

assert HALO >= POOL_BUF and PAIR_W == V7X_LANES and POOL_GROUP == V7X_LANES and MEM_HD == V7X_LANES


def _const_spec(shape):
    zeros = (0,) * len(shape)
    return pl.BlockSpec(shape, lambda *_: zeros, pipeline_mode=pl.Buffered(1))


def _layer_spec(stacked, layer):
    index = (layer,) + (0,) * (stacked.ndim - 1)
    return pl.BlockSpec((None,) + stacked.shape[1:], lambda *_: index, pipeline_mode=pl.Buffered(1))


def _mem_spec(mem, layer, nb):
    return pl.BlockSpec((None, nb) + mem.shape[2:], lambda i, j: (layer, i, 0, 0, 0))


def _unit_rms(x):
    return x * lax.rsqrt(jnp.mean(x * x, axis=-1, keepdims=True) + EPS)


def _rms(x, g):
    return _unit_rms(x) * g


def _dot(a, b):
    return jnp.dot(a, b, preferred_element_type=F32)


def _dot_nt(a, b):
    return lax.dot_general(a, b, (((1,), (1,)), ((), ())), preferred_element_type=F32)


def _log_sigmoid(x):
    return jnp.minimum(x, 0.0) - jnp.log1p(jnp.exp(-jnp.abs(x)))


def _cumsum(x, axis, seg):
    assert seg & (seg - 1) == 0 and x.shape[axis] % seg == 0
    n = x.shape[axis]
    vreg = (V7X_SUBLANES_F32, V7X_LANES)[axis]
    idx = lax.broadcasted_iota(jnp.int32, x.shape, axis) & (seg - 1)
    k = 1
    while k < seg:
        if seg == n and k % vreg == 0:
            zeros = jnp.zeros(x.shape[:axis] + (k,) + x.shape[axis + 1:], x.dtype)
            x = x + jnp.concatenate([zeros, lax.slice_in_dim(x, 0, n - k, axis=axis)], axis=axis)
        else:
            x = x + jnp.where(idx >= k, pltpu.roll(x, k, axis=axis), 0.0)
        k *= 2
    return x


def _mem_logits(q, mk_ref, b):
    qs = (q * (MEM_HD ** -0.5 * LOG2E)).astype(BF16)
    return [_dot_nt(qs[:, h * MEM_HD:(h + 1) * MEM_HD], mk_ref[b, h]) for h in range(MEM_HEADS)]


def _mem_probs(logits):
    probs = []
    for s in logits:
        p = jnp.exp2(s - jnp.max(s, axis=-1, keepdims=True))
        probs.append((p.astype(BF16), jnp.sum(p, axis=-1, keepdims=True)))
    return probs


def _mem_values(probs, mv_ref, b):
    return jnp.concatenate([_dot(p, mv_ref[b, h]) / l for h, (p, l) in enumerate(probs)], axis=-1)


def _pool_mixer(u, halo, pos0, w_pool_ref, scale):
    tt = u.shape[0]
    ext = jnp.concatenate([halo, u], axis=0)
    pos = pos0 + lax.broadcasted_iota(jnp.int32, (tt, 1), 0)
    outs = []
    for g, w in enumerate(POOL_WINDOWS):
        sl = slice(g * POOL_GROUP, (g + 1) * POOL_GROUP)
        s = ext[:, sl]
        k = 1
        while k < w:
            s = s + pltpu.roll(s, k, axis=0)
            k *= 2
        cnt = jnp.minimum(pos + 1, w).astype(F32)
        pooled = s[HALO:] / cnt - u[:, sl]
        outs.append(_dot(pooled.astype(BF16), w_pool_ref[g]))
    return jnp.concatenate(outs, axis=-1) * scale


def _row_parts(m):
    n = ROW_PARTS if m >= 512 else 1
    return [slice(i * m // n, (i + 1) * m // n) for i in range(n)]


def _mlp(x1_parts, g_pre, g_post, w_up_ref, w_down_ref):
    n_chunks = D_FF // FF_CHUNK
    chunk = lambda c: slice(c * FF_CHUNK, (c + 1) * FF_CHUNK)
    act = lambda up: jnp.square(jnp.maximum(up, 0.0)).astype(BF16)
    hm_parts = [_rms(x1, g_pre).astype(BF16) for x1 in x1_parts]
    hm = jnp.concatenate(hm_parts, axis=0)
    parts = _row_parts(hm.shape[0])
    acc = None
    for c in range(n_chunks):
        if c == 0:
            a = jnp.concatenate([act(_dot(h, w_up_ref[:, chunk(c)])) for h in hm_parts], axis=0)
        else:
            a = act(_dot(hm, w_up_ref[:, chunk(c)]))
        if c < n_chunks - 1:
            d = _dot(a, w_down_ref[chunk(c), :])
            acc = d if acc is None else acc + d
    return [x1 + _rms(acc[rows] + _dot(a[rows], w_down_ref[chunk(n_chunks - 1), :]), g_post)
            for x1, rows in zip(x1_parts, parts)]


def _mem_project_kernel(mem_ref, g_ref, w_ref, mk_ref, mv_ref, mkh_ref, mvh_ref):
    xn = _unit_rms(mem_ref[0])
    for l in range(w_ref.shape[0]):
        kv = _dot((xn * g_ref[l]).astype(BF16), w_ref[l].astype(BF16))
        for h in range(MEM_HEADS):
            mk = kv[:, h * MEM_HD:(h + 1) * MEM_HD]
            mv = kv[:, MEM_W + h * MEM_HD:MEM_W + (h + 1) * MEM_HD]
            mk_ref[l, 0, :, h, :] = mk
            mv_ref[l, 0, :, h, :] = mv
            mkh_ref[l, 0, h] = mk.astype(BF16)
            mvh_ref[l, 0, h] = mv.astype(BF16)


def _mem_project(mem, g_mem, w_mem_kv):
    b, n, _ = mem.shape
    depth = w_mem_kv.shape[0]
    out = jax.ShapeDtypeStruct((depth, b, n, MEM_HEADS, MEM_HD), F32)
    out_h = jax.ShapeDtypeStruct((depth, b, MEM_HEADS, n, MEM_HD), BF16)
    g_mem = g_mem.reshape(depth, 1, D_MODEL)
    return pl.pallas_call(
        _mem_project_kernel,
        grid=(b,),
        in_specs=[pl.BlockSpec((1, n, D_MODEL), lambda i: (i, 0, 0)),
                  _const_spec(g_mem.shape), _const_spec(w_mem_kv.shape)],
        out_specs=([pl.BlockSpec((depth, 1, n, MEM_HEADS, MEM_HD), lambda i: (0, i, 0, 0, 0))] * 2
                   + [pl.BlockSpec((depth, 1, MEM_HEADS, n, MEM_HD), lambda i: (0, i, 0, 0, 0))] * 2),
        out_shape=[out, out, out_h, out_h],
        compiler_params=pltpu.CompilerParams(dimension_semantics=("arbitrary",), vmem_limit_bytes=VMEM_LIMIT),
        name="mem_project",
    )(mem, g_mem, w_mem_kv)


def _layer0_kernel(x_ref, prev_ref, mk_ref, mv_ref, g_pre, g_post, g_mpre, g_mpost,
                   w_in, w_out, w_pool, pscale, w_up, w_down, *refs, nb, tt, pos_base, n_cast):
    cast_in, (y_ref, state_ref), cast_out, (halo_ref,) = (
        refs[:n_cast], refs[n_cast:n_cast + 2], refs[n_cast + 2:2 * n_cast + 2], refs[2 * n_cast + 2:])
    t = pl.program_id(1)

    @pl.when(t == 0)
    def _():
        halo_ref[...] = prev_ref[...]

    for src, dst in zip(cast_in, cast_out):
        dst[...] = src[...].astype(BF16)

    x = x_ref[...].reshape(nb * tt, D_MODEL)
    proj = jnp.concatenate([_dot(_rms(x[rows], g_pre[...]).astype(BF16), w_in[...])
                            for rows in _row_parts(nb * tt)], axis=0)
    cats = []
    for b in range(nb):
        rows = slice(b * tt, (b + 1) * tt)
        u = proj[rows, :MIX_W]
        logits = _mem_logits(proj[rows, MIX_W:], mk_ref, b)
        mix = _pool_mixer(u, halo_ref[b], pos_base + t * tt, w_pool, pscale[...])
        halo_ref[b] = u[tt - HALO:, :]
        state_ref[0, b] = u[tt - POOL_BUF:, :]
        mem = _mem_values(_mem_probs(logits), mv_ref, b)
        cats.append(jnp.concatenate([mix, mem], axis=-1))
    cat = (cats[0] if nb == 1 else jnp.concatenate(cats, axis=0)).astype(BF16)
    x1 = [x[rows] + _rms(_dot(cat[rows], w_out[...]), g_post[...]) for rows in _row_parts(nb * tt)]
    y = jnp.concatenate(_mlp(x1, g_mpre[...], g_mpost[...], w_up, w_down), axis=0)
    y_ref[...] = y.reshape(nb, tt, D_MODEL)


def _layer0(x, prev, mem_k, mem_v, p, *, nb, tt, pos_base, cast=()):
    b, t, _ = x.shape
    grid = (b // nb, t // tt)
    steps = grid[0] * grid[1]
    tile = pl.BlockSpec((nb, tt, D_MODEL), lambda i, j: (i, j, 0))
    slab = lambda w, l: pl.BlockSpec((1, w.shape[1] // steps, w.shape[2]), lambda i, j: (l, i * grid[1] + j, 0))
    return pl.pallas_call(
        functools.partial(_layer0_kernel, nb=nb, tt=tt, pos_base=pos_base, n_cast=len(cast)),
        grid=grid,
        in_specs=[tile, pl.BlockSpec((nb, HALO, MIX_W), lambda i, j: (i, 0, 0)),
                  _mem_spec(mem_k, 0, nb), _mem_spec(mem_v, 0, nb),
                  _layer_spec(p["g_mix_pre"], 0), _layer_spec(p["g_mix_post"], 0),
                  _layer_spec(p["g_mlp_pre"], 0), _layer_spec(p["g_mlp_post"], 0),
                  _layer_spec(p["w_in"][0], 0), _layer_spec(p["w_out"][0], 0),
                  _layer_spec(p["w_pool"], 0), _layer_spec(p["pool_scale"], 0),
                  _layer_spec(p["w_up"][0], 0), _layer_spec(p["w_down"][0], 0)]
                 + [slab(w, l) for w, l in cast],
        out_specs=[tile, pl.BlockSpec((1, nb, POOL_BUF, MIX_W), lambda i, j: (0, i, 0, 0))]
                  + [slab(w, 0) for w, _ in cast],
        out_shape=[jax.ShapeDtypeStruct((b, t, D_MODEL), F32),
                   jax.ShapeDtypeStruct((1, b, POOL_BUF, MIX_W), F32)]
                  + [jax.ShapeDtypeStruct((1,) + w.shape[1:], BF16) for w, _ in cast],
        scratch_shapes=[pltpu.VMEM((nb, HALO, MIX_W), F32)],
        compiler_params=pltpu.CompilerParams(dimension_semantics=("arbitrary", "arbitrary"),
                                             vmem_limit_bytes=VMEM_LIMIT),
        name="layer0",
    )(x, prev, mem_k, mem_v, p["g_mix_pre"], p["g_mix_post"], p["g_mlp_pre"], p["g_mlp_post"],
      p["w_in"][0], p["w_out"][0], p["w_pool"], p["pool_scale"], p["w_up"][0], p["w_down"][0],
      *[w for w, _ in cast])


def _bf16_part(x):
    return x.astype(BF16).astype(F32)


def _fold_operands(q, k, v, c):
    cl = c * LOG2E
    hi = _bf16_part(cl)
    mid = _bf16_part(cl - hi)
    lo = _bf16_part(cl - hi - mid)
    lane = lax.broadcasted_iota(jnp.int32, (1, V7X_LANES), 1)
    piece = lax.rem(lane + (3 - FOLD_SRC % 3), 3)
    d = jnp.where(piece == 0, hi, jnp.where(piece == 1, mid, lo))
    qs, ks, vs = [], [], []
    for h in range(FOX_HEADS):
        own = (h % 2) * FOX_HD
        e0 = FOX_HD - own
        cols = slice((h // 2) * PAIR_W, (h // 2 + 1) * PAIR_W)
        dh = pltpu.roll(d, (e0 - (FOLD_SRC + FOLD_W * h)) % V7X_LANES, axis=1)
        in_head = (lane >= own) & (lane < own + FOX_HD)
        first = ((lane >= e0) & (lane < e0 + 3)).astype(F32)
        second = ((lane >= e0 + 3) & (lane < e0 + 6)).astype(F32)
        xq = dh * first + second
        xk = first - dh * second
        xv = (lane == e0).astype(F32)
        qs.append(jnp.where(in_head, q[:, cols] * (FOX_HD ** -0.5 * LOG2E), xq).T.astype(BF16))
        ks.append(jnp.where(in_head, k[:, cols], xk).astype(BF16))
        vs.append(jnp.where(in_head, v[:, cols], xv).T.astype(BF16))
    return jnp.concatenate(qs, axis=0), jnp.concatenate(ks, axis=-1), jnp.concatenate(vs, axis=0)


def _layer1_front_kernel(x_ref, mk_ref, mv_ref, g_kv, g_pre, w_kv, w_f, w_ft, b_f, b_ft, w_in, *refs,
                         nb, tt, fold):
    if fold:
        k_ref, v_ref, logft_ref, mem_ref, qa_ref, ka_ref, va_ref, carry_ref = refs
    else:
        k_ref, v_ref, logft_ref, mem_ref, q_ref, kb_ref, vb_ref, c_ref, ct_ref = refs
    if fold:
        @pl.when(pl.program_id(1) == 0)
        def _():
            carry_ref[...] = jnp.zeros_like(carry_ref)

    xn = _unit_rms(x_ref[...].reshape(nb * tt, D_MODEL))
    hk = (xn * g_kv[...]).astype(BF16)
    kv = _dot(hk, w_kv[...])
    k = kv[:, :MIX_W]
    v = kv[:, MIX_W:]
    k_ref[...] = k.reshape(nb, tt, MIX_W)
    v_ref[...] = v.reshape(nb, tt, MIX_W)
    logf = _log_sigmoid(_dot(hk, w_f[...]) + b_f[...])
    proj = _dot((xn * g_pre[...]).astype(BF16), w_in[...])
    q = proj[:, :MIX_W]
    mem_logits = [_mem_logits(proj[b * tt:(b + 1) * tt, MIX_W:], mk_ref, b) for b in range(nb)]
    c = _cumsum(logf, 0, tt)
    if fold:
        c = c + carry_ref[...]
        carry_ref[...] = c[tt - 1:, :]
        qa, ka, va = _fold_operands(q, k, v, c)
        qa_ref[0, 0] = qa
        ka_ref[0] = ka
        va_ref[0, 0] = va
        logft_ref[0] = logf.T[:FOX_HEADS, :]
    else:
        q_ref[...] = (q * (FOX_HD ** -0.5)).astype(BF16).reshape(nb, tt, MIX_W)
        kb_ref[...] = k.astype(BF16).reshape(nb, tt, MIX_W)
        vb_ref[...] = v.astype(BF16).reshape(nb, tt, MIX_W)
        logf_t = _log_sigmoid(_dot_nt(w_ft[...], hk) + b_ft[...])
        logft_ref[0] = logf_t
        ct_ref[0, 0] = _cumsum(logf_t, 1, tt)
    for b in range(nb):
        rows = slice(b * tt, (b + 1) * tt)
        if not fold:
            c_ref[b] = c[rows, :FOX_HEADS]
        mem_ref[b] = _mem_values(_mem_probs(mem_logits[b]), mv_ref, b).astype(BF16)


def _layer1_front(x, mem_k, mem_v, p, *, nb, tt, fold):
    b, t, _ = x.shape
    assert (nb == 1) if fold else (t == tt)
    tile = lambda w: pl.BlockSpec((nb, tt, w), lambda i, j: (i, j, 0))
    act = lambda w, dt: jax.ShapeDtypeStruct((b, t, w), dt)
    out_specs = [tile(MIX_W), tile(MIX_W), pl.BlockSpec((1, FOX_HEADS, nb * tt), lambda i, j: (i, 0, j)), tile(MEM_W)]
    out_shape = [act(MIX_W, F32), act(MIX_W, F32), jax.ShapeDtypeStruct((b // nb, FOX_HEADS, nb * t), F32),
                 act(MEM_W, BF16)]
    if fold:
        wf = FOX_HEADS * PAIR_W
        tile_t = pl.BlockSpec((1, 1, wf, tt), lambda i, j: (i, j, 0, 0))
        act_t = jax.ShapeDtypeStruct((b, t // tt, wf, tt), BF16)
        out_specs += [tile_t, tile(wf), tile_t]
        out_shape += [act_t, act(wf, BF16), act_t]
        scratch = [pltpu.VMEM((1, V7X_LANES), F32)]
    else:
        out_specs += [tile(MIX_W)] * 3 + [tile(FOX_HEADS),
                                          pl.BlockSpec((1, 1, FOX_HEADS, nb * tt), lambda i, j: (i, j, 0, 0))]
        out_shape += [act(MIX_W, BF16)] * 3 + [act(FOX_HEADS, F32),
                                               jax.ShapeDtypeStruct((b // nb, t // tt, FOX_HEADS, nb * tt), F32)]
        scratch = []
    return pl.pallas_call(
        functools.partial(_layer1_front_kernel, nb=nb, tt=tt, fold=fold),
        grid=(b // nb, t // tt),
        in_specs=[tile(D_MODEL), _mem_spec(mem_k, 1, nb), _mem_spec(mem_v, 1, nb),
                  _const_spec(p["g_kv"].shape), _layer_spec(p["g_mix_pre"], 1),
                  _const_spec(p["w_kv"].shape), _const_spec(p["w_f"].shape), _const_spec(p["w_ft"].shape),
                  _const_spec(p["b_f"].shape), _const_spec(p["b_ft"].shape), _layer_spec(p["w_in"][1], 0)],
        out_specs=out_specs,
        out_shape=out_shape,
        scratch_shapes=scratch,
        compiler_params=pltpu.CompilerParams(dimension_semantics=("arbitrary", "arbitrary"),
                                             vmem_limit_bytes=VMEM_LIMIT),
        name="layer1_front",
    )(x, mem_k, mem_v, p["g_kv"], p["g_mix_pre"], p["w_kv"], p["w_f"], p["w_ft"], p["b_f"], p["b_ft"], p["w_in"][1])


def _fox_prompt_kernel(qt_ref, k_ref, vt_ref, o_ref, m_ref, acc_ref, *s_refs, t):
    i = pl.program_id(1)
    half = t // 2
    sub = lax.broadcasted_iota(jnp.int32, (PAIR_W, 1), 0)
    col = [slice(h * PAIR_W, (h + 1) * PAIR_W) for h in range(FOX_HEADS)]
    qt = [qt_ref[0, 0, c, :] for c in col]
    slot = lambda n: s_refs[n % len(s_refs)]

    def chain(n_items, logits, update):
        top = [logits(n) for n in range(FOX_AHEAD)]
        for n in range(n_items):
            if n + FOX_AHEAD < n_items:
                top.append(logits(n + FOX_AHEAD))
            update(n, top[n])

    tri = (lax.broadcasted_iota(jnp.int32, (half, half), 0) <= lax.broadcasted_iota(jnp.int32, (half, half), 1))
    lo = pl.ds(pl.multiple_of(i * t, t), half)
    hi = pl.ds(pl.multiple_of(i * t + half, half), half)

    def diag_logits(h):
        s_lo = _dot(k_ref[0, lo, col[h]], qt[h])
        s_hi = _dot(k_ref[0, hi, col[h]], qt[h][:, half:])
        s_lo = jnp.concatenate([jnp.where(tri, s_lo[:, :half], NEG_BIG), s_lo[:, half:]], axis=1)
        s_hi = jnp.where(tri, s_hi, NEG_BIG)
        slot(h)[:half, :] = s_lo
        slot(h)[half:, half:] = s_hi
        top = jnp.max(s_lo, axis=0, keepdims=True)
        return jnp.concatenate([top[:, :half], jnp.maximum(top[:, half:], jnp.max(s_hi, axis=0, keepdims=True))],
                               axis=1)

    def diag_update(h, m):
        p_lo = jnp.exp2(slot(h)[:half, :] - m).astype(BF16)
        p_hi = jnp.exp2(slot(h)[half:, half:] - m[:, half:]).astype(BF16)
        acc = _dot(vt_ref[0, i, col[h], :half], p_lo)
        acc_ref[h] = jnp.concatenate([acc[:, :half], acc[:, half:] + _dot(vt_ref[0, i, col[h], half:], p_hi)], axis=1)
        m_ref[h] = m

    chain(FOX_HEADS, diag_logits, diag_update)

    def full_tiles(j0, n_tiles):
        def logits(n):
            rows = pl.ds(pl.multiple_of((j0 + n // FOX_HEADS) * t, t), t)
            s = _dot(k_ref[0, rows, col[n % FOX_HEADS]], qt[n % FOX_HEADS])
            slot(n)[...] = s
            return jnp.max(s, axis=0, keepdims=True)

        def update(n, top):
            h = n % FOX_HEADS
            m = m_ref[h]
            m_new = jnp.maximum(m, top)
            p = jnp.exp2(slot(n)[...] - m_new).astype(BF16)
            acc_ref[h] = jnp.exp2(m - m_new) * acc_ref[h] + _dot(vt_ref[0, j0 + n // FOX_HEADS, col[h], :], p)
            m_ref[h] = m_new

        chain(n_tiles * FOX_HEADS, logits, update)

    lax.fori_loop(0, i // 2, lambda jj, c: full_tiles(2 * jj, 2), None)
    lax.fori_loop(2 * (i // 2), i, lambda j, c: full_tiles(j, 1), None)
    for pair in range(FOX_HEADS // 2):
        even, odd = acc_ref[2 * pair], acc_ref[2 * pair + 1]
        own = sub < FOX_HD
        out = jnp.where(own, even, odd) / jnp.where(own, even[FOX_HD:FOX_HD + 1, :], odd[0:1, :])
        o_ref[0, :, pair * PAIR_W:(pair + 1) * PAIR_W] = out.T.astype(BF16)


def _fox_prompt(qt, ka, vt):
    b, n_tiles, w, tile = qt.shape
    per_stream = lambda shape: pl.BlockSpec((1,) + shape, lambda i, j: (i,) + (0,) * len(shape))
    return pl.pallas_call(
        functools.partial(_fox_prompt_kernel, t=tile),
        grid=(b, n_tiles),
        in_specs=[pl.BlockSpec((1, 1, w, tile), lambda i, j: (i, j, 0, 0)),
                  per_stream((n_tiles * tile, w)), per_stream((n_tiles, w, tile))],
        out_specs=pl.BlockSpec((1, tile, MIX_W), lambda i, j: (i, j, 0)),
        out_shape=jax.ShapeDtypeStruct((b, n_tiles * tile, MIX_W), BF16),
        scratch_shapes=([pltpu.VMEM((FOX_HEADS, 1, tile), F32), pltpu.VMEM((FOX_HEADS, PAIR_W, tile), F32)]
                        + [pltpu.VMEM((tile, tile), F32)] * (FOX_AHEAD + 1)),
        compiler_params=pltpu.CompilerParams(dimension_semantics=("arbitrary", "arbitrary"),
                                             vmem_limit_bytes=VMEM_LIMIT),
        name="fox_prompt",
    )(qt, ka, vt)


def _fox_sample_kernel(q_ref, kc_ref, vc_ref, kn_ref, vn_ref, lfc_t_ref, c_ref, ct_ref, o_ref):
    tn = q_ref.shape[1]
    p = lfc_t_ref.shape[2]
    cc = _cumsum(lfc_t_ref[0], 1, p)
    ck_cache = cc - cc[:, p - 1:]
    causal = (lax.broadcasted_iota(jnp.int32, (tn, V7X_LANES), 1)
              <= lax.broadcasted_iota(jnp.int32, (tn, V7X_LANES), 0))
    pad = jnp.zeros((V7X_LANES - tn, FOX_HD), BF16)
    col = [slice(h * FOX_HD, (h + 1) * FOX_HD) for h in range(FOX_HEADS)]
    logits = []
    for h in range(FOX_HEADS):
        qh = q_ref[0, :, col[h]]
        cq = c_ref[0, :, h:h + 1]
        kn = jnp.concatenate([kn_ref[0, :, col[h]], pad], axis=0)
        s_old = _dot(qh, kc_ref[0, h].astype(BF16)) + cq - ck_cache[h:h + 1, :]
        s_new = jnp.where(causal, _dot_nt(qh, kn) + cq - ct_ref[0, h:h + 1, :], NEG_BIG)
        logits.append((s_old, s_new))
    probs = []
    for s_old, s_new in logits:
        m = jnp.maximum(jnp.max(s_old, axis=-1, keepdims=True), jnp.max(s_new, axis=-1, keepdims=True))
        p_old = jnp.exp(s_old - m)
        p_new = jnp.exp(s_new - m)
        l = jnp.sum(p_old, axis=-1, keepdims=True) + jnp.sum(p_new, axis=-1, keepdims=True)
        probs.append((p_old.astype(BF16), p_new.astype(BF16), l))
    outs = []
    for h, (p_old, p_new, l) in enumerate(probs):
        vn = jnp.concatenate([vn_ref[0, :, col[h]], pad], axis=0)
        outs.append((_dot_nt(p_old, vc_ref[0, h].astype(BF16)) + _dot(p_new, vn)) / l)
    o_ref[0] = jnp.concatenate(outs, axis=-1).astype(BF16)


def _fox_sample(q, cache_kt, cache_vt, kb, vb, cache_logf_t, c, ct):
    b, tn, _ = q.shape
    p = cache_kt.shape[3]
    new = lambda w: pl.BlockSpec((1, tn, w), lambda i: (i, 0, 0))
    old = pl.BlockSpec((1, FOX_HEADS, FOX_HD, p), lambda i: (i, 0, 0, 0))
    return pl.pallas_call(
        _fox_sample_kernel,
        grid=(b,),
        in_specs=[new(MIX_W), old, old, new(MIX_W), new(MIX_W),
                  pl.BlockSpec((1, FOX_HEADS, p), lambda i: (i, 0, 0)),
                  new(FOX_HEADS), pl.BlockSpec((1, FOX_HEADS, V7X_LANES), lambda i: (i, 0, 0))],
        out_specs=new(MIX_W),
        out_shape=jax.ShapeDtypeStruct((b, tn, MIX_W), BF16),
        compiler_params=pltpu.CompilerParams(dimension_semantics=("arbitrary",), vmem_limit_bytes=VMEM_LIMIT),
        name="fox_sample",
    )(q, cache_kt, cache_vt, kb, vb, cache_logf_t, c, ct)


def _layer1_back_kernel(x_ref, mix_ref, mem_ref, g_post, g_mpre, g_mpost, w_out, w_up, w_down, y_ref, *, nb, tt):
    m = nb * tt
    x = x_ref[...].reshape(m, D_MODEL)
    mix = mix_ref[...].reshape(m, MIX_W)
    mem = mem_ref[...].reshape(m, MEM_W)
    x1 = [x[rows] + _rms(_dot(mix[rows], w_out[:MIX_W, :]) + _dot(mem[rows], w_out[MIX_W:, :]), g_post[...])
          for rows in _row_parts(m)]
    y = jnp.concatenate(_mlp(x1, g_mpre[...], g_mpost[...], w_up, w_down), axis=0)
    y_ref[...] = y.reshape(nb, tt, D_MODEL)


def _layer1_back(x, mix, mem, p, *, nb, tt):
    b, t, _ = x.shape
    tile = lambda w: pl.BlockSpec((nb, tt, w), lambda i, j: (i, j, 0))
    return pl.pallas_call(
        functools.partial(_layer1_back_kernel, nb=nb, tt=tt),
        grid=(b // nb, t // tt),
        in_specs=[tile(D_MODEL), tile(MIX_W), tile(MEM_W),
                  _layer_spec(p["g_mix_post"], 1), _layer_spec(p["g_mlp_pre"], 1), _layer_spec(p["g_mlp_post"], 1),
                  _layer_spec(p["w_out"][1], 0), _layer_spec(p["w_up"][1], 0), _layer_spec(p["w_down"][1], 0)],
        out_specs=tile(D_MODEL),
        out_shape=jax.ShapeDtypeStruct((b, t, D_MODEL), F32),
        compiler_params=pltpu.CompilerParams(dimension_semantics=("arbitrary", "arbitrary"),
                                             vmem_limit_bytes=VMEM_LIMIT),
        name="layer1_back",
    )(x, mix, mem, p["g_mix_post"], p["g_mlp_pre"], p["g_mlp_post"], p["w_out"][1], p["w_up"][1], p["w_down"][1])


def _token_tile(b, t):
    if t >= 512:
        return 1, 512
    return b, t


def _trunk(x, pool_prev, cache, mem_k, mem_v, p, later_f32=None):
    b, t, _ = x.shape
    nb, tt = _token_tile(b, t)
    pos_base = 0 if cache is None else POOL_BUF
    names = sorted(later_f32) if later_f32 else []
    y0, pool_state, *cast = _layer0(x, pool_prev, mem_k, mem_v, p, nb=nb, tt=tt, pos_base=pos_base,
                                    cast=[(later_f32[n], 1) for n in names])
    p = {**p, **{n: [p[n][0], w] for n, w in zip(names, cast)}}
    k, v, logf_t, mem, *att = _layer1_front(y0, mem_k, mem_v, p, nb=nb, tt=tt, fold=cache is None)
    logf = jnp.transpose(logf_t.reshape(b // nb, FOX_HEADS, nb, t), (0, 2, 3, 1)).reshape(b, t, FOX_HEADS)
    if cache is None:
        mix = _fox_prompt(*att)
    else:
        q, kb, vb, c, ct = att
        cache_k, cache_v, cache_logf = cache
        ct_new = jnp.swapaxes(ct.reshape(FOX_HEADS, b, t), 0, 1)
        ct_new = jnp.pad(ct_new, ((0, 0), (0, 0), (0, V7X_LANES - t)))
        to_lanes = lambda a: jnp.transpose(a, (0, 2, 3, 1))
        mix = _fox_sample(q, to_lanes(cache_k), to_lanes(cache_v), kb, vb, jnp.swapaxes(cache_logf, 1, 2), c, ct_new)
    y = _layer1_back(y0, mix, mem, p, nb=nb, tt=tt)
    return (y, pool_state, k.reshape(b, t, FOX_HEADS, FOX_HD), v.reshape(b, t, FOX_HEADS, FOX_HD), logf), p


def kernel(x_prompt, x_sample, cache_pool, cache_k, cache_v, cache_logf, cache_mem_k, cache_mem_v, mem_prompt,
           g_mix_pre, g_mix_post, g_mlp_pre, g_mlp_post, w_in, w_out, w_pool, pool_scale, g_kv, w_kvf, b_f,
           g_mem, w_mem_kv, w_up, w_down):
    depth = w_in.shape[0]
    assert depth == 2 and w_pool.shape[0] == 1, "one pooling layer followed by one forgetting-attention layer"
    bp = x_prompt.shape[0]
    w_f = w_kvf[:, 2 * MIX_W:]
    fold_cols = lambda a: jnp.pad(jnp.concatenate([a, jnp.repeat(a, FOLD_W, axis=-1)], axis=-1),
                                  [(0, 0)] * (a.ndim - 1) + [(0, V7X_LANES - FOLD_SRC - FOLD_W * FOX_HEADS)])
    rows = lambda g: g.reshape(g.shape[0], 1, g.shape[1])
    p = dict(
        g_mix_pre=rows(g_mix_pre), g_mix_post=rows(g_mix_post), g_mlp_pre=rows(g_mlp_pre),
        g_mlp_post=rows(g_mlp_post), pool_scale=rows(pool_scale), g_kv=g_kv.reshape(1, D_MODEL),
        w_pool=w_pool.astype(BF16), w_kv=w_kvf[:, :2 * MIX_W].astype(BF16),
        w_f=fold_cols(w_f).astype(BF16), w_ft=w_f.T.astype(BF16),
        b_f=fold_cols(b_f).reshape(1, V7X_LANES), b_ft=b_f.reshape(FOX_HEADS, 1),
    )
    later_f32 = dict(w_in=w_in, w_out=w_out, w_up=w_up, w_down=w_down)
    p.update({n: [w[:1].astype(BF16), None] for n, w in later_f32.items()})

    mem_k_prompt, mem_v_prompt, mkh, mvh = _mem_project(mem_prompt, g_mem, w_mem_kv)

    (y_p, pool_p, k_p, v_p, logf_p), p = _trunk(x_prompt, jnp.zeros((bp, HALO, MIX_W), F32), None, mkh, mvh, p,
                                                later_f32)

    pool_prev = jnp.pad(cache_pool[0], ((0, 0), (HALO - POOL_BUF, 0), (0, 0)))
    head_major = lambda a: jnp.swapaxes(a.astype(BF16), 2, 3)
    (y_s, pool_s, k_s, v_s, logf_s), _ = _trunk(x_sample, pool_prev, (cache_k, cache_v, cache_logf),
                                                head_major(cache_mem_k), head_major(cache_mem_v), p)

    return (y_p, y_s, pool_p, pool_s, k_p, v_p, logf_p, k_s, v_s, logf_s, mem_k_prompt, mem_v_prompt)
```

```python
import functools

import jax
import jax.numpy as jnp
from jax import lax
from jax.experimental import pallas as pl
from jax.experimental.pallas import tpu as pltpu

F32 = jnp.float32
BF16 = jnp.bfloat16

D_MODEL = 1024
MIX_W = D_MODEL // 2
MEM_W = D_MODEL - MIX_W
POOL_WINDOWS = (2, 4, 8, 16)
POOL_GROUP = MIX_W // len(POOL_WINDOWS)
POOL_BUF = max(POOL_WINDOWS) - 1
FOX_HEADS = 8
FOX_HD = MIX_W // FOX_HEADS
MEM_HEADS = 4
MEM_HD = MEM_W // MEM_HEADS
D_FF = 4 * D_MODEL
EPS = 1e-6

V7X_LANES = 128
V7X_SUBLANES_F32 = 8
V7X_VMEM_BYTES = 64 * 1024 * 1024

HALO = 2 * V7X_SUBLANES_F32
FF_CHUNK = 1024
ROW_PARTS = 2
PAIR_W = 2 * FOX_HD
NEG_BIG = -1e30
LOG2E = 1.4426950408889634
FOLD_SRC = FOX_HEADS
FOLD_W = 6
FOX_AHEAD = 2
assert FOLD_SRC + FOLD_W * FOX_HEADS <= V7X_LANES
VMEM_LIMIT = V7X_VMEM_BYTES - 8 * 1024 * 1024

assert HALO >= POOL_BUF and PAIR_W == V7X_LANES and POOL_GROUP == V7X_LANES and MEM_HD == V7X_LANES


def _const_spec(shape):
    zeros = (0,) * len(shape)
    return pl.BlockSpec(shape, lambda *_: zeros, pipeline_mode=pl.Buffered(1))


def _layer_spec(stacked, layer):
    index = (layer,) + (0,) * (stacked.ndim - 1)
    return pl.BlockSpec((None,) + stacked.shape[1:], lambda *_: index, pipeline_mode=pl.Buffered(1))


def _mem_spec(mem, layer, nb):
    return pl.BlockSpec((None, nb) + mem.shape[2:], lambda i, j: (layer, i, 0, 0, 0))


def _unit_rms(x):
    return x * lax.rsqrt(jnp.mean(x * x, axis=-1, keepdims=True) + EPS)


def _rms(x, g):
    return _unit_rms(x) * g


def _dot(a, b):
    return jnp.dot(a, b, preferred_element_type=F32)


def _dot_nt(a, b):
    return lax.dot_general(a, b, (((1,), (1,)), ((), ())), preferred_element_type=F32)


def _log_sigmoid(x):
    return jnp.minimum(x, 0.0) - jnp.log1p(jnp.exp(-jnp.abs(x)))


def _cumsum(x, axis, seg):
    assert seg & (seg - 1) == 0 and x.shape[axis] % seg == 0
    n = x.shape[axis]
    vreg = (V7X_SUBLANES_F32, V7X_LANES)[axis]
    idx = lax.broadcasted_iota(jnp.int32, x.shape, axis) & (seg - 1)
    k = 1
    while k < seg:
        if seg == n and k % vreg == 0:
            zeros = jnp.zeros(x.shape[:axis] + (k,) + x.shape[axis + 1:], x.dtype)
            x = x + jnp.concatenate([zeros, lax.slice_in_dim(x, 0, n - k, axis=axis)], axis=axis)
        else:
            x = x + jnp.where(idx >= k, pltpu.roll(x, k, axis=axis), 0.0)
        k *= 2
    return x


def _mem_logits(q, mk_ref, b):
    qs = (q * (MEM_HD ** -0.5 * LOG2E)).astype(BF16)
    return [_dot_nt(qs[:, h * MEM_HD:(h + 1) * MEM_HD], mk_ref[b, h]) for h in range(MEM_HEADS)]


def _mem_probs(logits):
    probs = []
    for s in logits:
        p = jnp.exp2(s - jnp.max(s, axis=-1, keepdims=True))
        probs.append((p.astype(BF16), jnp.sum(p, axis=-1, keepdims=True)))
    return probs


def _mem_values(probs, mv_ref, b):
    return jnp.concatenate([_dot(p, mv_ref[b, h]) / l for h, (p, l) in enumerate(probs)], axis=-1)


def _pool_mixer(u, halo, pos0, w_pool_ref, scale):
    tt = u.shape[0]
    ext = jnp.concatenate([halo, u], axis=0)
    pos = pos0 + lax.broadcasted_iota(jnp.int32, (tt, 1), 0)
    outs = []
    for g, w in enumerate(POOL_WINDOWS):
        sl = slice(g * POOL_GROUP, (g + 1) * POOL_GROUP)
        s = ext[:, sl]
        k = 1
        while k < w:
            s = s + pltpu.roll(s, k, axis=0)
            k *= 2
        cnt = jnp.minimum(pos + 1, w).astype(F32)
        pooled = s[HALO:] / cnt - u[:, sl]
        outs.append(_dot(pooled.astype(BF16), w_pool_ref[g]))
    return jnp.concatenate(outs, axis=-1) * scale


def _row_parts(m):
    n = ROW_PARTS if m >= 512 else 1
    return [slice(i * m // n, (i + 1) * m // n) for i in range(n)]


def _mlp(x1_parts, g_pre, g_post, w_up_ref, w_down_ref):
    n_chunks = D_FF // FF_CHUNK
    chunk = lambda c: slice(c * FF_CHUNK, (c + 1) * FF_CHUNK)
    act = lambda up: jnp.square(jnp.maximum(up, 0.0)).astype(BF16)
    hm_parts = [_rms(x1, g_pre).astype(BF16) for x1 in x1_parts]
    hm = jnp.concatenate(hm_parts, axis=0)
    parts = _row_parts(hm.shape[0])
    acc = None
    for c in range(n_chunks):
        if c == 0:
            a = jnp.concatenate([act(_dot(h, w_up_ref[:, chunk(c)])) for h in hm_parts], axis=0)
        else:
            a = act(_dot(hm, w_up_ref[:, chunk(c)]))
        if c < n_chunks - 1:
            d = _dot(a, w_down_ref[chunk(c), :])
            acc = d if acc is None else acc + d
    return [x1 + _rms(acc[rows] + _dot(a[rows], w_down_ref[chunk(n_chunks - 1), :]), g_post)
            for x1, rows in zip(x1_parts, parts)]


def _mem_project_kernel(mem_ref, g_ref, w_ref, mk_ref, mv_ref, mkh_ref, mvh_ref):
    xn = _unit_rms(mem_ref[0])
    for l in range(w_ref.shape[0]):
        kv = _dot((xn * g_ref[l]).astype(BF16), w_ref[l].astype(BF16))
        for h in range(MEM_HEADS):
            mk = kv[:, h * MEM_HD:(h + 1) * MEM_HD]
            mv = kv[:, MEM_W + h * MEM_HD:MEM_W + (h + 1) * MEM_HD]
            mk_ref[l, 0, :, h, :] = mk
            mv_ref[l, 0, :, h, :] = mv
            mkh_ref[l, 0, h] = mk.astype(BF16)
            mvh_ref[l, 0, h] = mv.astype(BF16)


def _mem_project(mem, g_mem, w_mem_kv):
    b, n, _ = mem.shape
    depth = w_mem_kv.shape[0]
    out = jax.ShapeDtypeStruct((depth, b, n, MEM_HEADS, MEM_HD), F32)
    out_h = jax.ShapeDtypeStruct((depth, b, MEM_HEADS, n, MEM_HD), BF16)
    g_mem = g_mem.reshape(depth, 1, D_MODEL)
    return pl.pallas_call(
        _mem_project_kernel,
        grid=(b,),
        in_specs=[pl.BlockSpec((1, n, D_MODEL), lambda i: (i, 0, 0)),
                  _const_spec(g_mem.shape), _const_spec(w_mem_kv.shape)],
        out_specs=([pl.BlockSpec((depth, 1, n, MEM_HEADS, MEM_HD), lambda i: (0, i, 0, 0, 0))] * 2
                   + [pl.BlockSpec((depth, 1, MEM_HEADS, n, MEM_HD), lambda i: (0, i, 0, 0, 0))] * 2),
        out_shape=[out, out, out_h, out_h],
        compiler_params=pltpu.CompilerParams(dimension_semantics=("arbitrary",), vmem_limit_bytes=VMEM_LIMIT),
        name="mem_project",
    )(mem, g_mem, w_mem_kv)


def _layer0_kernel(x_ref, prev_ref, mk_ref, mv_ref, g_pre, g_post, g_mpre, g_mpost,
                   w_in, w_out, w_pool, pscale, w_up, w_down, *refs, nb, tt, pos_base, n_cast):
    cast_in, (y_ref, state_ref), cast_out, (halo_ref,) = (
        refs[:n_cast], refs[n_cast:n_cast + 2], refs[n_cast + 2:2 * n_cast + 2], refs[2 * n_cast + 2:])
    t = pl.program_id(1)

    @pl.when(t == 0)
    def _():
        halo_ref[...] = prev_ref[...]

    for src, dst in zip(cast_in, cast_out):
        dst[...] = src[...].astype(BF16)

    x = x_ref[...].reshape(nb * tt, D_MODEL)
    proj = jnp.concatenate([_dot(_rms(x[rows], g_pre[...]).astype(BF16), w_in[...])
                            for rows in _row_parts(nb * tt)], axis=0)
    cats = []
    for b in range(nb):
        rows = slice(b * tt, (b + 1) * tt)
        u = proj[rows, :MIX_W]
        logits = _mem_logits(proj[rows, MIX_W:], mk_ref, b)
        mix = _pool_mixer(u, halo_ref[b], pos_base + t * tt, w_pool, pscale[...])
        halo_ref[b] = u[tt - HALO:, :]
        state_ref[0, b] = u[tt - POOL_BUF:, :]
        mem = _mem_values(_mem_probs(logits), mv_ref, b)
        cats.append(jnp.concatenate([mix, mem], axis=-1))
    cat = (cats[0] if nb == 1 else jnp.concatenate(cats, axis=0)).astype(BF16)
    x1 = [x[rows] + _rms(_dot(cat[rows], w_out[...]), g_post[...]) for rows in _row_parts(nb * tt)]
    y = jnp.concatenate(_mlp(x1, g_mpre[...], g_mpost[...], w_up, w_down), axis=0)
    y_ref[...] = y.reshape(nb, tt, D_MODEL)


def _layer0(x, prev, mem_k, mem_v, p, *, nb, tt, pos_base, cast=()):
    b, t, _ = x.shape
    grid = (b // nb, t // tt)
    steps = grid[0] * grid[1]
    tile = pl.BlockSpec((nb, tt, D_MODEL), lambda i, j: (i, j, 0))
    slab = lambda w, l: pl.BlockSpec((1, w.shape[1] // steps, w.shape[2]), lambda i, j: (l, i * grid[1] + j, 0))
    return pl.pallas_call(
        functools.partial(_layer0_kernel, nb=nb, tt=tt, pos_base=pos_base, n_cast=len(cast)),
        grid=grid,
        in_specs=[tile, pl.BlockSpec((nb, HALO, MIX_W), lambda i, j: (i, 0, 0)),
                  _mem_spec(mem_k, 0, nb), _mem_spec(mem_v, 0, nb),
                  _layer_spec(p["g_mix_pre"], 0), _layer_spec(p["g_mix_post"], 0),
                  _layer_spec(p["g_mlp_pre"], 0), _layer_spec(p["g_mlp_post"], 0),
                  _layer_spec(p["w_in"][0], 0), _layer_spec(p["w_out"][0], 0),
                  _layer_spec(p["w_pool"], 0), _layer_spec(p["pool_scale"], 0),
                  _layer_spec(p["w_up"][0], 0), _layer_spec(p["w_down"][0], 0)]
                 + [slab(w, l) for w, l in cast],
        out_specs=[tile, pl.BlockSpec((1, nb, POOL_BUF, MIX_W), lambda i, j: (0, i, 0, 0))]
                  + [slab(w, 0) for w, _ in cast],
        out_shape=[jax.ShapeDtypeStruct((b, t, D_MODEL), F32),
                   jax.ShapeDtypeStruct((1, b, POOL_BUF, MIX_W), F32)]
                  + [jax.ShapeDtypeStruct((1,) + w.shape[1:], BF16) for w, _ in cast],
        scratch_shapes=[pltpu.VMEM((nb, HALO, MIX_W), F32)],
        compiler_params=pltpu.CompilerParams(dimension_semantics=("arbitrary", "arbitrary"),
                                             vmem_limit_bytes=VMEM_LIMIT),
        name="layer0",
    )(x, prev, mem_k, mem_v, p["g_mix_pre"], p["g_mix_post"], p["g_mlp_pre"], p["g_mlp_post"],
      p["w_in"][0], p["w_out"][0], p["w_pool"], p["pool_scale"], p["w_up"][0], p["w_down"][0],
      *[w for w, _ in cast])


def _bf16_part(x):
    return x.astype(BF16).astype(F32)


def _fold_lanes(h):
    lane = lax.broadcasted_iota(jnp.int32, (1, V7X_LANES), 1)
    own = (h % 2) * FOX_HD
    e0 = FOX_HD - own
    in_head = (lane >= own) & (lane < own + FOX_HD)
    first = ((lane >= e0) & (lane < e0 + 3)).astype(F32)
    second = ((lane >= e0 + 3) & (lane < e0 + 6)).astype(F32)
    return in_head, first, second, lane == e0, e0


def _fold_bias(c):
    cl = c * LOG2E
    hi = _bf16_part(cl)
    mid = _bf16_part(cl - hi)
    lo = _bf16_part(cl - hi - mid)
    lane = lax.broadcasted_iota(jnp.int32, (1, V7X_LANES), 1)
    piece = lax.rem(lane + (3 - FOLD_SRC % 3), 3)
    d = jnp.where(piece == 0, hi, jnp.where(piece == 1, mid, lo))
    return [pltpu.roll(d, (_fold_lanes(h)[4] - (FOLD_SRC + FOLD_W * h)) % V7X_LANES, axis=1)
            for h in range(FOX_HEADS)]


def _fold_kv(k, v, bias):
    ks, vs = [], []
    for h, dh in enumerate(bias):
        in_head, first, second, one, _ = _fold_lanes(h)
        cols = slice((h // 2) * PAIR_W, (h // 2 + 1) * PAIR_W)
        ks.append(jnp.where(in_head, k[:, cols], first - dh * second).astype(BF16))
        vs.append(jnp.where(in_head, v[:, cols], one.astype(F32)).T.astype(BF16))
    return jnp.concatenate(ks, axis=-1), jnp.concatenate(vs, axis=0)


def _fold_q(q, bias):
    qs = []
    for h, dh in enumerate(bias):
        in_head, first, second, _, _ = _fold_lanes(h)
        cols = slice((h // 2) * PAIR_W, (h // 2 + 1) * PAIR_W)
        qs.append(jnp.where(in_head, q[:, cols] * (FOX_HD ** -0.5 * LOG2E), dh * first + second).T.astype(BF16))
    return jnp.concatenate(qs, axis=0)


def _layer1_front_kernel(x_ref, mk_ref, mv_ref, g_kv, g_pre, w_kv, w_f, w_ft, b_f, b_ft, w_in, *refs,
                         nb, tt, fold):
    if fold:
        k_ref, v_ref, logft_ref, mem_ref, qa_ref, ka_ref, va_ref, carry_ref = refs
    else:
        k_ref, v_ref, logft_ref, mem_ref, q_ref, kb_ref, vb_ref, c_ref, ct_ref = refs
    if fold:
        @pl.when(pl.program_id(1) == 0)
        def _():
            carry_ref[...] = jnp.zeros_like(carry_ref)

    xn = _unit_rms(x_ref[...].reshape(nb * tt, D_MODEL))
    hk = (xn * g_kv[...]).astype(BF16)
    logf = _log_sigmoid(_dot(hk, w_f[...]) + b_f[...])
    k = _dot(hk, w_kv[:, :MIX_W])
    v = _dot(hk, w_kv[:, MIX_W:])
    k_ref[...] = k.reshape(nb, tt, MIX_W)
    v_ref[...] = v.reshape(nb, tt, MIX_W)
    c = _cumsum(logf, 0, tt)
    if fold:
        c = c + carry_ref[...]
        carry_ref[...] = c[tt - 1:, :]
        bias = _fold_bias(c)
        ka_ref[0], va_ref[0, 0] = _fold_kv(k, v, bias)
        logft_ref[0] = logf.T[:FOX_HEADS, :]
    proj = _dot((xn * g_pre[...]).astype(BF16), w_in[...])
    q = proj[:, :MIX_W]
    mem_logits = [_mem_logits(proj[b * tt:(b + 1) * tt, MIX_W:], mk_ref, b) for b in range(nb)]
    if fold:
        qa_ref[0, 0] = _fold_q(q, bias)
    else:
        q_ref[...] = (q * (FOX_HD ** -0.5)).astype(BF16).reshape(nb, tt, MIX_W)
        kb_ref[...] = k.astype(BF16).reshape(nb, tt, MIX_W)
        vb_ref[...] = v.astype(BF16).reshape(nb, tt, MIX_W)
        logf_t = _log_sigmoid(_dot_nt(w_ft[...], hk) + b_ft[...])
        logft_ref[0] = logf_t
        ct_ref[0, 0] = _cumsum(logf_t, 1, tt)
    for b in range(nb):
        rows = slice(b * tt, (b + 1) * tt)
        if not fold:
            c_ref[b] = c[rows, :FOX_HEADS]
        mem_ref[b] = _mem_values(_mem_probs(mem_logits[b]), mv_ref, b).astype(BF16)


def _layer1_front(x, mem_k, mem_v, p, *, nb, tt, fold):
    b, t, _ = x.shape
    assert (nb == 1) if fold else (t == tt)
    tile = lambda w: pl.BlockSpec((nb, tt, w), lambda i, j: (i, j, 0))
    act = lambda w, dt: jax.ShapeDtypeStruct((b, t, w), dt)
    out_specs = [tile(MIX_W), tile(MIX_W), pl.BlockSpec((1, FOX_HEADS, nb * tt), lambda i, j: (i, 0, j)), tile(MEM_W)]
    out_shape = [act(MIX_W, F32), act(MIX_W, F32), jax.ShapeDtypeStruct((b // nb, FOX_HEADS, nb * t), F32),
                 act(MEM_W, BF16)]
    if fold:
        wf = FOX_HEADS * PAIR_W
        tile_t = pl.BlockSpec((1, 1, wf, tt), lambda i, j: (i, j, 0, 0))
        act_t = jax.ShapeDtypeStruct((b, t // tt, wf, tt), BF16)
        out_specs += [tile_t, tile(wf), tile_t]
        out_shape += [act_t, act(wf, BF16), act_t]
        scratch = [pltpu.VMEM((1, V7X_LANES), F32)]
    else:
        out_specs += [tile(MIX_W)] * 3 + [tile(FOX_HEADS),
                                          pl.BlockSpec((1, 1, FOX_HEADS, nb * tt), lambda i, j: (i, j, 0, 0))]
        out_shape += [act(MIX_W, BF16)] * 3 + [act(FOX_HEADS, F32),
                                               jax.ShapeDtypeStruct((b // nb, t // tt, FOX_HEADS, nb * tt), F32)]
        scratch = []
    return pl.pallas_call(
        functools.partial(_layer1_front_kernel, nb=nb, tt=tt, fold=fold),
        grid=(b // nb, t // tt),
        in_specs=[tile(D_MODEL), _mem_spec(mem_k, 1, nb), _mem_spec(mem_v, 1, nb),
                  _const_spec(p["g_kv"].shape), _layer_spec(p["g_mix_pre"], 1),
                  _const_spec(p["w_kv"].shape), _const_spec(p["w_f"].shape), _const_spec(p["w_ft"].shape),
                  _const_spec(p["b_f"].shape), _const_spec(p["b_ft"].shape), _layer_spec(p["w_in"][1], 0)],
        out_specs=out_specs,
        out_shape=out_shape,
        scratch_shapes=scratch,
        compiler_params=pltpu.CompilerParams(dimension_semantics=("arbitrary", "arbitrary"),
                                             vmem_limit_bytes=VMEM_LIMIT),
        name="layer1_front",
    )(x, mem_k, mem_v, p["g_kv"], p["g_mix_pre"], p["w_kv"], p["w_f"], p["w_ft"], p["b_f"], p["b_ft"], p["w_in"][1])


def _fox_prompt_kernel(qt_ref, k_ref, vt_ref, o_ref, m_ref, acc_ref, *s_refs, t):
    i = pl.program_id(1)
    half = t // 2
    sub = lax.broadcasted_iota(jnp.int32, (PAIR_W, 1), 0)
    col = [slice(h * PAIR_W, (h + 1) * PAIR_W) for h in range(FOX_HEADS)]
    qt = [qt_ref[0, 0, c, :] for c in col]
    slot = lambda n: s_refs[n % len(s_refs)]

    def chain(n_items, logits, update):
        top = [logits(n) for n in range(FOX_AHEAD)]
        for n in range(n_items):
            if n + FOX_AHEAD < n_items:
                top.append(logits(n + FOX_AHEAD))
            update(n, top[n])

    tri = (lax.broadcasted_iota(jnp.int32, (half, half), 0) <= lax.broadcasted_iota(jnp.int32, (half, half), 1))
    lo = pl.ds(pl.multiple_of(i * t, t), half)
    hi = pl.ds(pl.multiple_of(i * t + half, half), half)

    def diag_logits(h):
        s_lo = _dot(k_ref[0, lo, col[h]], qt[h])
        s_hi = _dot(k_ref[0, hi, col[h]], qt[h][:, half:])
        s_lo = jnp.concatenate([jnp.where(tri, s_lo[:, :half], NEG_BIG), s_lo[:, half:]], axis=1)
        s_hi = jnp.where(tri, s_hi, NEG_BIG)
        slot(h)[:half, :] = s_lo
        slot(h)[half:, half:] = s_hi
        top = jnp.max(s_lo, axis=0, keepdims=True)
        return jnp.concatenate([top[:, :half], jnp.maximum(top[:, half:], jnp.max(s_hi, axis=0, keepdims=True))],
                               axis=1)

    def diag_update(h, m):
        p_lo = jnp.exp2(slot(h)[:half, :] - m).astype(BF16)
        p_hi = jnp.exp2(slot(h)[half:, half:] - m[:, half:]).astype(BF16)
        acc = _dot(vt_ref[0, i, col[h], :half], p_lo)
        acc_ref[h] = jnp.concatenate([acc[:, :half], acc[:, half:] + _dot(vt_ref[0, i, col[h], half:], p_hi)], axis=1)
        m_ref[h] = m

    chain(FOX_HEADS, diag_logits, diag_update)

    def full_tiles(j0, n_tiles):
        def logits(n):
            rows = pl.ds(pl.multiple_of((j0 + n // FOX_HEADS) * t, t), t)
            s = _dot(k_ref[0, rows, col[n % FOX_HEADS]], qt[n % FOX_HEADS])
            slot(n)[...] = s
            return jnp.max(s, axis=0, keepdims=True)

        def update(n, top):
            h = n % FOX_HEADS
            m = m_ref[h]
            m_new = jnp.maximum(m, top)
            p = jnp.exp2(slot(n)[...] - m_new).astype(BF16)
            acc_ref[h] = jnp.exp2(m - m_new) * acc_ref[h] + _dot(vt_ref[0, j0 + n // FOX_HEADS, col[h], :], p)
            m_ref[h] = m_new

        chain(n_tiles * FOX_HEADS, logits, update)

    lax.fori_loop(0, i // 2, lambda jj, c: full_tiles(2 * jj, 2), None)
    lax.fori_loop(2 * (i // 2), i, lambda j, c: full_tiles(j, 1), None)
    for pair in range(FOX_HEADS // 2):
        even, odd = acc_ref[2 * pair], acc_ref[2 * pair + 1]
        own = sub < FOX_HD
        out = jnp.where(own, even, odd) / jnp.where(own, even[FOX_HD:FOX_HD + 1, :], odd[0:1, :])
        o_ref[0, :, pair * PAIR_W:(pair + 1) * PAIR_W] = out.T.astype(BF16)


def _fox_prompt(qt, ka, vt):
    b, n_tiles, w, tile = qt.shape
    per_stream = lambda shape: pl.BlockSpec((1,) + shape, lambda i, j: (i,) + (0,) * len(shape))
    return pl.pallas_call(
        functools.partial(_fox_prompt_kernel, t=tile),
        grid=(b, n_tiles),
        in_specs=[pl.BlockSpec((1, 1, w, tile), lambda i, j: (i, j, 0, 0)),
                  per_stream((n_tiles * tile, w)), per_stream((n_tiles, w, tile))],
        out_specs=pl.BlockSpec((1, tile, MIX_W), lambda i, j: (i, j, 0)),
        out_shape=jax.ShapeDtypeStruct((b, n_tiles * tile, MIX_W), BF16),
        scratch_shapes=([pltpu.VMEM((FOX_HEADS, 1, tile), F32), pltpu.VMEM((FOX_HEADS, PAIR_W, tile), F32)]
                        + [pltpu.VMEM((tile, tile), F32)] * (FOX_AHEAD + 1)),
        compiler_params=pltpu.CompilerParams(dimension_semantics=("arbitrary", "arbitrary"),
                                             vmem_limit_bytes=VMEM_LIMIT),
        name="fox_prompt",
    )(qt, ka, vt)


def _fox_sample_kernel(q_ref, kc_ref, vc_ref, kn_ref, vn_ref, lfc_t_ref, c_ref, ct_ref, o_ref):
    tn = q_ref.shape[1]
    p = lfc_t_ref.shape[2]
    cc = _cumsum(lfc_t_ref[0], 1, p)
    ck_cache = cc - cc[:, p - 1:]
    causal = (lax.broadcasted_iota(jnp.int32, (tn, V7X_LANES), 1)
              <= lax.broadcasted_iota(jnp.int32, (tn, V7X_LANES), 0))
    pad = jnp.zeros((V7X_LANES - tn, FOX_HD), BF16)
    col = [slice(h * FOX_HD, (h + 1) * FOX_HD) for h in range(FOX_HEADS)]
    logits = []
    for h in range(FOX_HEADS):
        qh = q_ref[0, :, col[h]]
        cq = c_ref[0, :, h:h + 1]
        kn = jnp.concatenate([kn_ref[0, :, col[h]], pad], axis=0)
        s_old = _dot(qh, kc_ref[0, h].astype(BF16)) + cq - ck_cache[h:h + 1, :]
        s_new = jnp.where(causal, _dot_nt(qh, kn) + cq - ct_ref[0, h:h + 1, :], NEG_BIG)
        logits.append((s_old, s_new))
    probs = []
    for s_old, s_new in logits:
        m = jnp.maximum(jnp.max(s_old, axis=-1, keepdims=True), jnp.max(s_new, axis=-1, keepdims=True))
        p_old = jnp.exp(s_old - m)
        p_new = jnp.exp(s_new - m)
        l = jnp.sum(p_old, axis=-1, keepdims=True) + jnp.sum(p_new, axis=-1, keepdims=True)
        probs.append((p_old.astype(BF16), p_new.astype(BF16), l))
    outs = []
    for h, (p_old, p_new, l) in enumerate(probs):
        vn = jnp.concatenate([vn_ref[0, :, col[h]], pad], axis=0)
        outs.append((_dot_nt(p_old, vc_ref[0, h].astype(BF16)) + _dot(p_new, vn)) / l)
    o_ref[0] = jnp.concatenate(outs, axis=-1).astype(BF16)


def _fox_sample(q, cache_kt, cache_vt, kb, vb, cache_logf_t, c, ct):
    b, tn, _ = q.shape
    p = cache_kt.shape[3]
    new = lambda w: pl.BlockSpec((1, tn, w), lambda i: (i, 0, 0))
    old = pl.BlockSpec((1, FOX_HEADS, FOX_HD, p), lambda i: (i, 0, 0, 0))
    return pl.pallas_call(
        _fox_sample_kernel,
        grid=(b,),
        in_specs=[new(MIX_W), old, old, new(MIX_W), new(MIX_W),
                  pl.BlockSpec((1, FOX_HEADS, p), lambda i: (i, 0, 0)),
                  new(FOX_HEADS), pl.BlockSpec((1, FOX_HEADS, V7X_LANES), lambda i: (i, 0, 0))],
        out_specs=new(MIX_W),
        out_shape=jax.ShapeDtypeStruct((b, tn, MIX_W), BF16),
        compiler_params=pltpu.CompilerParams(dimension_semantics=("arbitrary",), vmem_limit_bytes=VMEM_LIMIT),
        name="fox_sample",
    )(q, cache_kt, cache_vt, kb, vb, cache_logf_t, c, ct)


def _layer1_back_kernel(x_ref, mix_ref, mem_ref, g_post, g_mpre, g_mpost, w_out, w_up, w_down, y_ref, *, nb, tt):
    m = nb * tt
    x = x_ref[...].reshape(m, D_MODEL)
    mix = mix_ref[...].reshape(m, MIX_W)
    mem = mem_ref[...].reshape(m, MEM_W)
    x1 = [x[rows] + _rms(_dot(mix[rows], w_out[:MIX_W, :]) + _dot(mem[rows], w_out[MIX_W:, :]), g_post[...])
          for rows in _row_parts(m)]
    y = jnp.concatenate(_mlp(x1, g_mpre[...], g_mpost[...], w_up, w_down), axis=0)
    y_ref[...] = y.reshape(nb, tt, D_MODEL)


def _layer1_back(x, mix, mem, p, *, nb, tt):
    b, t, _ = x.shape
    tile = lambda w: pl.BlockSpec((nb, tt, w), lambda i, j: (i, j, 0))
    return pl.pallas_call(
        functools.partial(_layer1_back_kernel, nb=nb, tt=tt),
        grid=(b // nb, t // tt),
        in_specs=[tile(D_MODEL), tile(MIX_W), tile(MEM_W),
                  _layer_spec(p["g_mix_post"], 1), _layer_spec(p["g_mlp_pre"], 1), _layer_spec(p["g_mlp_post"], 1),
                  _layer_spec(p["w_out"][1], 0), _layer_spec(p["w_up"][1], 0), _layer_spec(p["w_down"][1], 0)],
        out_specs=tile(D_MODEL),
        out_shape=jax.ShapeDtypeStruct((b, t, D_MODEL), F32),
        compiler_params=pltpu.CompilerParams(dimension_semantics=("arbitrary", "arbitrary"),
                                             vmem_limit_bytes=VMEM_LIMIT),
        name="layer1_back",
    )(x, mix, mem, p["g_mix_post"], p["g_mlp_pre"], p["g_mlp_post"], p["w_out"][1], p["w_up"][1], p["w_down"][1])


def _token_tile(b, t):
    if t >= 512:
        return 1, 512
    return b, t


def _trunk(x, pool_prev, cache, mem_k, mem_v, p, later_f32=None):
    b, t, _ = x.shape
    nb, tt = _token_tile(b, t)
    pos_base = 0 if cache is None else POOL_BUF
    names = sorted(later_f32) if later_f32 else []
    y0, pool_state, *cast = _layer0(x, pool_prev, mem_k, mem_v, p, nb=nb, tt=tt, pos_base=pos_base,
                                    cast=[(later_f32[n], 1) for n in names])
    p = {**p, **{n: [p[n][0], w] for n, w in zip(names, cast)}}
    k, v, logf_t, mem, *att = _layer1_front(y0, mem_k, mem_v, p, nb=nb, tt=tt, fold=cache is None)
    logf = jnp.transpose(logf_t.reshape(b // nb, FOX_HEADS, nb, t), (0, 2, 3, 1)).reshape(b, t, FOX_HEADS)
    if cache is None:
        mix = _fox_prompt(*att)
    else:
        q, kb, vb, c, ct = att
        cache_k, cache_v, cache_logf = cache
        ct_new = jnp.swapaxes(ct.reshape(FOX_HEADS, b, t), 0, 1)
        ct_new = jnp.pad(ct_new, ((0, 0), (0, 0), (0, V7X_LANES - t)))
        to_lanes = lambda a: jnp.transpose(a, (0, 2, 3, 1))
        mix = _fox_sample(q, to_lanes(cache_k), to_lanes(cache_v), kb, vb, jnp.swapaxes(cache_logf, 1, 2), c, ct_new)
    y = _layer1_back(y0, mix, mem, p, nb=nb, tt=tt)
    return (y, pool_state, k.reshape(b, t, FOX_HEADS, FOX_HD), v.reshape(b, t, FOX_HEADS, FOX_HD), logf), p


def kernel(x_prompt, x_sample, cache_pool, cache_k, cache_v, cache_logf, cache_mem_k, cache_mem_v, mem_prompt,
           g_mix_pre, g_mix_post, g_mlp_pre, g_mlp_post, w_in, w_out, w_pool, pool_scale, g_kv, w_kvf, b_f,
           g_mem, w_mem_kv, w_up, w_down):
    depth = w_in.shape[0]
    assert depth == 2 and w_pool.shape[0] == 1, "one pooling layer followed by one forgetting-attention layer"
    bp = x_prompt.shape[0]
    w_f = w_kvf[:, 2 * MIX_W:]
    fold_cols = lambda a: jnp.pad(jnp.concatenate([a, jnp.repeat(a, FOLD_W, axis=-1)], axis=-1),
                                  [(0, 0)] * (a.ndim - 1) + [(0, V7X_LANES - FOLD_SRC - FOLD_W * FOX_HEADS)])
    rows = lambda g: g.reshape(g.shape[0], 1, g.shape[1])
    p = dict(
        g_mix_pre=rows(g_mix_pre), g_mix_post=rows(g_mix_post), g_mlp_pre=rows(g_mlp_pre),
        g_mlp_post=rows(g_mlp_post), pool_scale=rows(pool_scale), g_kv=g_kv.reshape(1, D_MODEL),
        w_pool=w_pool.astype(BF16), w_kv=w_kvf[:, :2 * MIX_W].astype(BF16),
        w_f=fold_cols(w_f).astype(BF16), w_ft=w_f.T.astype(BF16),
        b_f=fold_cols(b_f).reshape(1, V7X_LANES), b_ft=b_f.reshape(FOX_HEADS, 1),
    )
    later_f32 = dict(w_in=w_in, w_out=w_out, w_up=w_up, w_down=w_down)
    p.update({n: [w[:1].astype(BF16), None] for n, w in later_f32.items()})

    mem_k_prompt, mem_v_prompt, mkh, mvh = _mem_project(mem_prompt, g_mem, w_mem_kv)

    (y_p, pool_p, k_p, v_p, logf_p), p = _trunk(x_prompt, jnp.zeros((bp, HALO, MIX_W), F32), None, mkh, mvh, p,
                                                later_f32)

    pool_prev = jnp.pad(cache_pool[0], ((0, 0), (HALO - POOL_BUF, 0), (0, 0)))
    head_major = lambda a: jnp.swapaxes(a.astype(BF16), 2, 3)
    (y_s, pool_s, k_s, v_s, logf_s), _ = _trunk(x_sample, pool_prev, (cache_k, cache_v, cache_logf),
                                                head_major(cache_mem_k), head_major(cache_mem_v), p)

    return (y_p, y_s, pool_p, pool_s, k_p, v_p, logf_p, k_s, v_s, logf_s, mem_k_prompt, mem_v_prompt)
```

```python
import functools

import jax
import jax.numpy as jnp
from jax import lax
from jax.experimental import pallas as pl
from jax.experimental.pallas import tpu as pltpu

F32 = jnp.float32
BF16 = jnp.bfloat16

D_MODEL = 1024
MIX_W = D_MODEL // 2
MEM_W = D_MODEL - MIX_W
POOL_WINDOWS = (2, 4, 8, 16)
POOL_GROUP = MIX_W // len(POOL_WINDOWS)
POOL_BUF = max(POOL_WINDOWS) - 1
FOX_HEADS = 8
FOX_HD = MIX_W // FOX_HEADS
MEM_HEADS = 4
MEM_HD = MEM_W // MEM_HEADS
D_FF = 4 * D_MODEL
EPS = 1e-6

V7X_LANES = 128
V7X_SUBLANES_F32 = 8
V7X_VMEM_BYTES = 64 * 1024 * 1024

HALO = 2 * V7X_SUBLANES_F32
FF_CHUNK = 1024
ROW_PARTS = 2
PAIR_W = 2 * FOX_HD
NEG_BIG = -1e30
LOG2E = 1.4426950408889634
FOLD_SRC = FOX_HEADS
FOLD_W = 6
FOX_AHEAD = 2
assert FOLD_SRC + FOLD_W * FOX_HEADS <= V7X_LANES
VMEM_LIMIT = V7X_VMEM_BYTES - 8 * 1024 * 1024

assert HALO >= POOL_BUF and PAIR_W == V7X_LANES and POOL_GROUP == V7X_LANES and MEM_HD == V7X_LANES


def _const_spec(shape):
    zeros = (0,) * len(shape)
    return pl.BlockSpec(shape, lambda *_: zeros, pipeline_mode=pl.Buffered(1))


def _layer_spec(stacked, layer):
    index = (layer,) + (0,) * (stacked.ndim - 1)
    return pl.BlockSpec((None,) + stacked.shape[1:], lambda *_: index, pipeline_mode=pl.Buffered(1))


def _mem_spec(mem, layer, nb):
    return pl.BlockSpec((None, nb) + mem.shape[2:], lambda i, j: (layer, i, 0, 0, 0))


def _unit_rms(x):
    return x * lax.rsqrt(jnp.mean(x * x, axis=-1, keepdims=True) + EPS)


def _rms(x, g):
    return _unit_rms(x) * g


def _dot(a, b):
    return jnp.dot(a, b, preferred_element_type=F32)


def _dot_nt(a, b):
    return lax.dot_general(a, b, (((1,), (1,)), ((), ())), preferred_element_type=F32)


def _log_sigmoid(x):
    return jnp.minimum(x, 0.0) - jnp.log1p(jnp.exp(-jnp.abs(x)))


def _cumsum(x, axis, seg):
    assert seg & (seg - 1) == 0 and x.shape[axis] % seg == 0
    n = x.shape[axis]
    vreg = (V7X_SUBLANES_F32, V7X_LANES)[axis]
    idx = lax.broadcasted_iota(jnp.int32, x.shape, axis) & (seg - 1)
    k = 1
    while k < seg:
        if seg == n and k % vreg == 0:
            zeros = jnp.zeros(x.shape[:axis] + (k,) + x.shape[axis + 1:], x.dtype)
            x = x + jnp.concatenate([zeros, lax.slice_in_dim(x, 0, n - k, axis=axis)], axis=axis)
        else:
            x = x + jnp.where(idx >= k, pltpu.roll(x, k, axis=axis), 0.0)
        k *= 2
    return x


def _mem_logits(q, mk_ref, b):
    qs = (q * (MEM_HD ** -0.5 * LOG2E)).astype(BF16)
    return [_dot_nt(qs[:, h * MEM_HD:(h + 1) * MEM_HD], mk_ref[b, h]) for h in range(MEM_HEADS)]


def _mem_probs(logits):
    probs = []
    for s in logits:
        p = jnp.exp2(s - jnp.max(s, axis=-1, keepdims=True))
        probs.append((p.astype(BF16), jnp.sum(p, axis=-1, keepdims=True)))
    return probs


def _mem_values(probs, mv_ref, b):
    return jnp.concatenate([_dot(p, mv_ref[b, h]) / l for h, (p, l) in enumerate(probs)], axis=-1)


def _pool_mixer(u, halo, pos0, w_pool_ref, scale):
    tt = u.shape[0]
    ext = jnp.concatenate([halo, u], axis=0)
    pos = pos0 + lax.broadcasted_iota(jnp.int32, (tt, 1), 0)
    outs = []
    for g, w in enumerate(POOL_WINDOWS):
        sl = slice(g * POOL_GROUP, (g + 1) * POOL_GROUP)
        s = ext[:, sl]
        k = 1
        while k < w:
            s = s + pltpu.roll(s, k, axis=0)
            k *= 2
        cnt = jnp.minimum(pos + 1, w).astype(F32)
        pooled = s[HALO:] / cnt - u[:, sl]
        outs.append(_dot(pooled.astype(BF16), w_pool_ref[g]))
    return jnp.concatenate(outs, axis=-1) * scale


def _row_parts(m):
    n = ROW_PARTS if m >= 512 else 1
    return [slice(i * m // n, (i + 1) * m // n) for i in range(n)]


def _mlp(x1_parts, g_pre, g_post, w_up_ref, w_down_ref):
    n_chunks = D_FF // FF_CHUNK
    chunk = lambda c: slice(c * FF_CHUNK, (c + 1) * FF_CHUNK)
    act = lambda up: jnp.square(jnp.maximum(up, 0.0)).astype(BF16)
    hm_parts = [_rms(x1, g_pre).astype(BF16) for x1 in x1_parts]
    hm = jnp.concatenate(hm_parts, axis=0)
    parts = _row_parts(hm.shape[0])
    acc = None
    for c in range(n_chunks):
        if c == 0:
            a = jnp.concatenate([act(_dot(h, w_up_ref[:, chunk(c)])) for h in hm_parts], axis=0)
        else:
            a = act(_dot(hm, w_up_ref[:, chunk(c)]))
        if c < n_chunks - 1:
            d = _dot(a, w_down_ref[chunk(c), :])
            acc = d if acc is None else acc + d
    return [x1 + _rms(acc[rows] + _dot(a[rows], w_down_ref[chunk(n_chunks - 1), :]), g_post)
            for x1, rows in zip(x1_parts, parts)]


def _cast_specs(cast, steps, step_of):
    def slab(w, layer):
        return pl.BlockSpec((1, w.shape[1] // steps, w.shape[2]), lambda *ids: (layer, step_of(*ids), 0))
    return ([slab(w, l) for w, l in cast], [slab(w, 0) for w, _ in cast],
            [jax.ShapeDtypeStruct((1,) + w.shape[1:], BF16) for w, _ in cast])


def _cast_slabs(src_refs, dst_refs):
    for src, dst in zip(src_refs, dst_refs):
        dst[...] = src[...].astype(BF16)


def _mem_project_kernel(mem_ref, g_ref, w_ref, *refs, n_cast):
    mk_ref, mv_ref, mkh_ref, mvh_ref = refs[n_cast:n_cast + 4]
    _cast_slabs(refs[:n_cast], refs[n_cast + 4:])
    xn = _unit_rms(mem_ref[0])
    for l in range(w_ref.shape[0]):
        kv = _dot((xn * g_ref[l]).astype(BF16), w_ref[l].astype(BF16))
        for h in range(MEM_HEADS):
            mk = kv[:, h * MEM_HD:(h + 1) * MEM_HD]
            mv = kv[:, MEM_W + h * MEM_HD:MEM_W + (h + 1) * MEM_HD]
            mk_ref[l, 0, :, h, :] = mk
            mv_ref[l, 0, :, h, :] = mv
            mkh_ref[l, 0, h] = mk.astype(BF16)
            mvh_ref[l, 0, h] = mv.astype(BF16)


def _mem_project(mem, g_mem, w_mem_kv, cast=()):
    b, n, _ = mem.shape
    depth = w_mem_kv.shape[0]
    out = jax.ShapeDtypeStruct((depth, b, n, MEM_HEADS, MEM_HD), F32)
    out_h = jax.ShapeDtypeStruct((depth, b, MEM_HEADS, n, MEM_HD), BF16)
    g_mem = g_mem.reshape(depth, 1, D_MODEL)
    cast_in, cast_out, cast_shape = _cast_specs(cast, b, lambda i: i)
    return pl.pallas_call(
        functools.partial(_mem_project_kernel, n_cast=len(cast)),
        grid=(b,),
        in_specs=[pl.BlockSpec((1, n, D_MODEL), lambda i: (i, 0, 0)),
                  _const_spec(g_mem.shape), _const_spec(w_mem_kv.shape)] + cast_in,
        out_specs=([pl.BlockSpec((depth, 1, n, MEM_HEADS, MEM_HD), lambda i: (0, i, 0, 0, 0))] * 2
                   + [pl.BlockSpec((depth, 1, MEM_HEADS, n, MEM_HD), lambda i: (0, i, 0, 0, 0))] * 2 + cast_out),
        out_shape=[out, out, out_h, out_h] + cast_shape,
        compiler_params=pltpu.CompilerParams(dimension_semantics=("arbitrary",), vmem_limit_bytes=VMEM_LIMIT),
        name="mem_project",
    )(mem, g_mem, w_mem_kv, *[w for w, _ in cast])


def _layer0_kernel(x_ref, prev_ref, mk_ref, mv_ref, g_pre, g_post, g_mpre, g_mpost,
                   w_in, w_out, w_pool, pscale, w_up, w_down, *refs, nb, tt, pos_base, n_cast):
    cast_in, (y_ref, state_ref), cast_out, (halo_ref,) = (
        refs[:n_cast], refs[n_cast:n_cast + 2], refs[n_cast + 2:2 * n_cast + 2], refs[2 * n_cast + 2:])
    t = pl.program_id(1)

    @pl.when(t == 0)
    def _():
        halo_ref[...] = prev_ref[...]

    _cast_slabs(cast_in, cast_out)
    x = x_ref[...].reshape(nb * tt, D_MODEL)
    proj = jnp.concatenate([_dot(_rms(x[rows], g_pre[...]).astype(BF16), w_in[...])
                            for rows in _row_parts(nb * tt)], axis=0)
    cats = []
    for b in range(nb):
        rows = slice(b * tt, (b + 1) * tt)
        u = proj[rows, :MIX_W]
        logits = _mem_logits(proj[rows, MIX_W:], mk_ref, b)
        mix = _pool_mixer(u, halo_ref[b], pos_base + t * tt, w_pool, pscale[...])
        halo_ref[b] = u[tt - HALO:, :]
        state_ref[0, b] = u[tt - POOL_BUF:, :]
        mem = _mem_values(_mem_probs(logits), mv_ref, b)
        cats.append(jnp.concatenate([mix, mem], axis=-1))
    cat = (cats[0] if nb == 1 else jnp.concatenate(cats, axis=0)).astype(BF16)
    x1 = [x[rows] + _rms(_dot(cat[rows], w_out[...]), g_post[...]) for rows in _row_parts(nb * tt)]
    y = jnp.concatenate(_mlp(x1, g_mpre[...], g_mpost[...], w_up, w_down), axis=0)
    y_ref[...] = y.reshape(nb, tt, D_MODEL)


def _layer0(x, prev, mem_k, mem_v, p, *, nb, tt, pos_base, cast=()):
    b, t, _ = x.shape
    grid = (b // nb, t // tt)
    tile = pl.BlockSpec((nb, tt, D_MODEL), lambda i, j: (i, j, 0))
    cast_in, cast_out, cast_shape = _cast_specs(cast, grid[0] * grid[1], lambda i, j: i * grid[1] + j)
    return pl.pallas_call(
        functools.partial(_layer0_kernel, nb=nb, tt=tt, pos_base=pos_base, n_cast=len(cast)),
        grid=grid,
        in_specs=[tile, pl.BlockSpec((nb, HALO, MIX_W), lambda i, j: (i, 0, 0)),
                  _mem_spec(mem_k, 0, nb), _mem_spec(mem_v, 0, nb),
                  _layer_spec(p["g_mix_pre"], 0), _layer_spec(p["g_mix_post"], 0),
                  _layer_spec(p["g_mlp_pre"], 0), _layer_spec(p["g_mlp_post"], 0),
                  _layer_spec(p["w_in"][0], 0), _layer_spec(p["w_out"][0], 0),
                  _layer_spec(p["w_pool"], 0), _layer_spec(p["pool_scale"], 0),
                  _layer_spec(p["w_up"][0], 0), _layer_spec(p["w_down"][0], 0)] + cast_in,
        out_specs=[tile, pl.BlockSpec((1, nb, POOL_BUF, MIX_W), lambda i, j: (0, i, 0, 0))] + cast_out,
        out_shape=[jax.ShapeDtypeStruct((b, t, D_MODEL), F32),
                   jax.ShapeDtypeStruct((1, b, POOL_BUF, MIX_W), F32)] + cast_shape,
        scratch_shapes=[pltpu.VMEM((nb, HALO, MIX_W), F32)],
        compiler_params=pltpu.CompilerParams(dimension_semantics=("arbitrary", "arbitrary"),
                                             vmem_limit_bytes=VMEM_LIMIT),
        name="layer0",
    )(x, prev, mem_k, mem_v, p["g_mix_pre"], p["g_mix_post"], p["g_mlp_pre"], p["g_mlp_post"],
      p["w_in"][0], p["w_out"][0], p["w_pool"], p["pool_scale"], p["w_up"][0], p["w_down"][0],
      *[w for w, _ in cast])


def _bf16_part(x):
    return x.astype(BF16).astype(F32)


def _fold_lanes(h):
    lane = lax.broadcasted_iota(jnp.int32, (1, V7X_LANES), 1)
    own = (h % 2) * FOX_HD
    e0 = FOX_HD - own
    in_head = (lane >= own) & (lane < own + FOX_HD)
    first = ((lane >= e0) & (lane < e0 + 3)).astype(F32)
    second = ((lane >= e0 + 3) & (lane < e0 + 6)).astype(F32)
    return in_head, first, second, lane == e0, e0


def _fold_bias(c):
    cl = c * LOG2E
    hi = _bf16_part(cl)
    mid = _bf16_part(cl - hi)
    lo = _bf16_part(cl - hi - mid)
    lane = lax.broadcasted_iota(jnp.int32, (1, V7X_LANES), 1)
    piece = lax.rem(lane + (3 - FOLD_SRC % 3), 3)
    d = jnp.where(piece == 0, hi, jnp.where(piece == 1, mid, lo))
    return [pltpu.roll(d, (_fold_lanes(h)[4] - (FOLD_SRC + FOLD_W * h)) % V7X_LANES, axis=1)
            for h in range(FOX_HEADS)]


def _fold_kv(k, v, bias):
    ks, vs = [], []
    for h, dh in enumerate(bias):
        in_head, first, second, one, _ = _fold_lanes(h)
        cols = slice((h // 2) * PAIR_W, (h // 2 + 1) * PAIR_W)
        ks.append(jnp.where(in_head, k[:, cols], first - dh * second).astype(BF16))
        vs.append(jnp.where(in_head, v[:, cols], one.astype(F32)).T.astype(BF16))
    return jnp.concatenate(ks, axis=-1), jnp.concatenate(vs, axis=0)


def _fold_q(q, bias):
    qs = []
    for h, dh in enumerate(bias):
        in_head, first, second, _, _ = _fold_lanes(h)
        cols = slice((h // 2) * PAIR_W, (h // 2 + 1) * PAIR_W)
        qs.append(jnp.where(in_head, q[:, cols] * (FOX_HD ** -0.5 * LOG2E), dh * first + second).T.astype(BF16))
    return jnp.concatenate(qs, axis=0)


def _layer1_front_kernel(x_ref, mk_ref, mv_ref, g_kv, g_pre, w_kv, w_f, w_ft, b_f, b_ft, w_in, *refs,
                         nb, tt, fold):
    if fold:
        k_ref, v_ref, logft_ref, mem_ref, qa_ref, ka_ref, va_ref, carry_ref = refs
    else:
        k_ref, v_ref, logft_ref, mem_ref, q_ref, kb_ref, vb_ref, c_ref, ct_ref = refs
    if fold:
        @pl.when(pl.program_id(1) == 0)
        def _():
            carry_ref[...] = jnp.zeros_like(carry_ref)

    xn = _unit_rms(x_ref[...].reshape(nb * tt, D_MODEL))
    hk = (xn * g_kv[...]).astype(BF16)
    logf = _log_sigmoid(_dot(hk, w_f[...]) + b_f[...])
    k = _dot(hk, w_kv[:, :MIX_W])
    v = _dot(hk, w_kv[:, MIX_W:])
    k_ref[...] = k.reshape(nb, tt, MIX_W)
    v_ref[...] = v.reshape(nb, tt, MIX_W)
    c = _cumsum(logf, 0, tt)
    if fold:
        c = c + carry_ref[...]
        carry_ref[...] = c[tt - 1:, :]
        bias = _fold_bias(c)
        ka_ref[0], va_ref[0, 0] = _fold_kv(k, v, bias)
        logft_ref[0] = logf.T[:FOX_HEADS, :]
    proj = _dot((xn * g_pre[...]).astype(BF16), w_in[...])
    q = proj[:, :MIX_W]
    mem_logits = [_mem_logits(proj[b * tt:(b + 1) * tt, MIX_W:], mk_ref, b) for b in range(nb)]
    if fold:
        qa_ref[0, 0] = _fold_q(q, bias)
    else:
        q_ref[...] = (q * (FOX_HD ** -0.5)).astype(BF16).reshape(nb, tt, MIX_W)
        kb_ref[...] = k.astype(BF16).reshape(nb, tt, MIX_W)
        vb_ref[...] = v.astype(BF16).reshape(nb, tt, MIX_W)
        logf_t = _log_sigmoid(_dot_nt(w_ft[...], hk) + b_ft[...])
        logft_ref[0] = logf_t
        ct_ref[0, 0] = _cumsum(logf_t, 1, tt)
    for b in range(nb):
        rows = slice(b * tt, (b + 1) * tt)
        if not fold:
            c_ref[b] = c[rows, :FOX_HEADS]
        mem_ref[b] = _mem_values(_mem_probs(mem_logits[b]), mv_ref, b).astype(BF16)


def _layer1_front(x, mem_k, mem_v, p, *, nb, tt, fold):
    b, t, _ = x.shape
    assert (nb == 1) if fold else (t == tt)
    tile = lambda w: pl.BlockSpec((nb, tt, w), lambda i, j: (i, j, 0))
    act = lambda w, dt: jax.ShapeDtypeStruct((b, t, w), dt)
    out_specs = [tile(MIX_W), tile(MIX_W), pl.BlockSpec((1, FOX_HEADS, nb * tt), lambda i, j: (i, 0, j)), tile(MEM_W)]
    out_shape = [act(MIX_W, F32), act(MIX_W, F32), jax.ShapeDtypeStruct((b // nb, FOX_HEADS, nb * t), F32),
                 act(MEM_W, BF16)]
    if fold:
        wf = FOX_HEADS * PAIR_W
        tile_t = pl.BlockSpec((1, 1, wf, tt), lambda i, j: (i, j, 0, 0))
        act_t = jax.ShapeDtypeStruct((b, t // tt, wf, tt), BF16)
        out_specs += [tile_t, tile(wf), tile_t]
        out_shape += [act_t, act(wf, BF16), act_t]
        scratch = [pltpu.VMEM((1, V7X_LANES), F32)]
    else:
        out_specs += [tile(MIX_W)] * 3 + [tile(FOX_HEADS),
                                          pl.BlockSpec((1, 1, FOX_HEADS, nb * tt), lambda i, j: (i, j, 0, 0))]
        out_shape += [act(MIX_W, BF16)] * 3 + [act(FOX_HEADS, F32),
                                               jax.ShapeDtypeStruct((b // nb, t // tt, FOX_HEADS, nb * tt), F32)]
        scratch = []
    return pl.pallas_call(
        functools.partial(_layer1_front_kernel, nb=nb, tt=tt, fold=fold),
        grid=(b // nb, t // tt),
        in_specs=[tile(D_MODEL), _mem_spec(mem_k, 1, nb), _mem_spec(mem_v, 1, nb),
                  _const_spec(p["g_kv"].shape), _layer_spec(p["g_mix_pre"], 1),
                  _const_spec(p["w_kv"].shape), _const_spec(p["w_f"].shape), _const_spec(p["w_ft"].shape),
                  _const_spec(p["b_f"].shape), _const_spec(p["b_ft"].shape), _layer_spec(p["w_in"][1], 0)],
        out_specs=out_specs,
        out_shape=out_shape,
        scratch_shapes=scratch,
        compiler_params=pltpu.CompilerParams(dimension_semantics=("arbitrary", "arbitrary"),
                                             vmem_limit_bytes=VMEM_LIMIT),
        name="layer1_front",
    )(x, mem_k, mem_v, p["g_kv"], p["g_mix_pre"], p["w_kv"], p["w_f"], p["w_ft"], p["b_f"], p["b_ft"], p["w_in"][1])


def _fox_prompt_kernel(qt_ref, k_ref, vt_ref, o_ref, m_ref, acc_ref, *s_refs, t):
    i = pl.program_id(1)
    half = t // 2
    sub = lax.broadcasted_iota(jnp.int32, (PAIR_W, 1), 0)
    col = [slice(h * PAIR_W, (h + 1) * PAIR_W) for h in range(FOX_HEADS)]
    qt = [qt_ref[0, 0, c, :] for c in col]
    slot = lambda n: s_refs[n % len(s_refs)]

    def chain(n_items, logits, update):
        top = [logits(n) for n in range(FOX_AHEAD)]
        for n in range(n_items):
            if n + FOX_AHEAD < n_items:
                top.append(logits(n + FOX_AHEAD))
            update(n, top[n])

    tri = (lax.broadcasted_iota(jnp.int32, (half, half), 0) <= lax.broadcasted_iota(jnp.int32, (half, half), 1))
    lo = pl.ds(pl.multiple_of(i * t, t), half)
    hi = pl.ds(pl.multiple_of(i * t + half, half), half)

    def diag_logits(h):
        s_lo = _dot(k_ref[0, lo, col[h]], qt[h])
        s_hi = _dot(k_ref[0, hi, col[h]], qt[h][:, half:])
        s_lo = jnp.concatenate([jnp.where(tri, s_lo[:, :half], NEG_BIG), s_lo[:, half:]], axis=1)
        s_hi = jnp.where(tri, s_hi, NEG_BIG)
        slot(h)[:half, :] = s_lo
        slot(h)[half:, half:] = s_hi
        top = jnp.max(s_lo, axis=0, keepdims=True)
        return jnp.concatenate([top[:, :half], jnp.maximum(top[:, half:], jnp.max(s_hi, axis=0, keepdims=True))],
                               axis=1)

    def diag_update(h, m):
        p_lo = jnp.exp2(slot(h)[:half, :] - m).astype(BF16)
        p_hi = jnp.exp2(slot(h)[half:, half:] - m[:, half:]).astype(BF16)
        acc = _dot(vt_ref[0, i, col[h], :half], p_lo)
        acc_ref[h] = jnp.concatenate([acc[:, :half], acc[:, half:] + _dot(vt_ref[0, i, col[h], half:], p_hi)], axis=1)
        m_ref[h] = m

    chain(FOX_HEADS, diag_logits, diag_update)

    def full_tiles(j0, n_tiles):
        def logits(n):
            rows = pl.ds(pl.multiple_of((j0 + n // FOX_HEADS) * t, t), t)
            s = _dot(k_ref[0, rows, col[n % FOX_HEADS]], qt[n % FOX_HEADS])
            slot(n)[...] = s
            return jnp.max(s, axis=0, keepdims=True)

        def update(n, top):
            h = n % FOX_HEADS
            m = m_ref[h]
            m_new = jnp.maximum(m, top)
            p = jnp.exp2(slot(n)[...] - m_new).astype(BF16)
            acc_ref[h] = jnp.exp2(m - m_new) * acc_ref[h] + _dot(vt_ref[0, j0 + n // FOX_HEADS, col[h], :], p)
            m_ref[h] = m_new

        chain(n_tiles * FOX_HEADS, logits, update)

    lax.fori_loop(0, i // 2, lambda jj, c: full_tiles(2 * jj, 2), None)
    lax.fori_loop(2 * (i // 2), i, lambda j, c: full_tiles(j, 1), None)
    for pair in range(FOX_HEADS // 2):
        even, odd = acc_ref[2 * pair], acc_ref[2 * pair + 1]
        own = sub < FOX_HD
        out = jnp.where(own, even, odd) / jnp.where(own, even[FOX_HD:FOX_HD + 1, :], odd[0:1, :])
        o_ref[0, :, pair * PAIR_W:(pair + 1) * PAIR_W] = out.T.astype(BF16)


def _fox_prompt(qt, ka, vt):
    b, n_tiles, w, tile = qt.shape
    per_stream = lambda shape: pl.BlockSpec((1,) + shape, lambda i, j: (i,) + (0,) * len(shape))
    return pl.pallas_call(
        functools.partial(_fox_prompt_kernel, t=tile),
        grid=(b, n_tiles),
        in_specs=[pl.BlockSpec((1, 1, w, tile), lambda i, j: (i, j, 0, 0)),
                  per_stream((n_tiles * tile, w)), per_stream((n_tiles, w, tile))],
        out_specs=pl.BlockSpec((1, tile, MIX_W), lambda i, j: (i, j, 0)),
        out_shape=jax.ShapeDtypeStruct((b, n_tiles * tile, MIX_W), BF16),
        scratch_shapes=([pltpu.VMEM((FOX_HEADS, 1, tile), F32), pltpu.VMEM((FOX_HEADS, PAIR_W, tile), F32)]
                        + [pltpu.VMEM((tile, tile), F32)] * (FOX_AHEAD + 1)),
        compiler_params=pltpu.CompilerParams(dimension_semantics=("arbitrary", "arbitrary"),
                                             vmem_limit_bytes=VMEM_LIMIT),
        name="fox_prompt",
    )(qt, ka, vt)


def _fox_sample_kernel(q_ref, kc_ref, vc_ref, kn_ref, vn_ref, lfc_t_ref, c_ref, ct_ref, o_ref):
    tn = q_ref.shape[1]
    p = lfc_t_ref.shape[2]
    cc = _cumsum(lfc_t_ref[0], 1, p)
    ck_cache = cc - cc[:, p - 1:]
    causal = (lax.broadcasted_iota(jnp.int32, (tn, V7X_LANES), 1)
              <= lax.broadcasted_iota(jnp.int32, (tn, V7X_LANES), 0))
    pad = jnp.zeros((V7X_LANES - tn, FOX_HD), BF16)
    col = [slice(h * FOX_HD, (h + 1) * FOX_HD) for h in range(FOX_HEADS)]
    logits = []
    for h in range(FOX_HEADS):
        qh = q_ref[0, :, col[h]]
        cq = c_ref[0, :, h:h + 1]
        kn = jnp.concatenate([kn_ref[0, :, col[h]], pad], axis=0)
        s_old = _dot(qh, kc_ref[0, h].astype(BF16)) + cq - ck_cache[h:h + 1, :]
        s_new = jnp.where(causal, _dot_nt(qh, kn) + cq - ct_ref[0, h:h + 1, :], NEG_BIG)
        logits.append((s_old, s_new))
    probs = []
    for s_old, s_new in logits:
        m = jnp.maximum(jnp.max(s_old, axis=-1, keepdims=True), jnp.max(s_new, axis=-1, keepdims=True))
        p_old = jnp.exp(s_old - m)
        p_new = jnp.exp(s_new - m)
        l = jnp.sum(p_old, axis=-1, keepdims=True) + jnp.sum(p_new, axis=-1, keepdims=True)
        probs.append((p_old.astype(BF16), p_new.astype(BF16), l))
    outs = []
    for h, (p_old, p_new, l) in enumerate(probs):
        vn = jnp.concatenate([vn_ref[0, :, col[h]], pad], axis=0)
        outs.append((_dot_nt(p_old, vc_ref[0, h].astype(BF16)) + _dot(p_new, vn)) / l)
    o_ref[0] = jnp.concatenate(outs, axis=-1).astype(BF16)


def _fox_sample(q, cache_kt, cache_vt, kb, vb, cache_logf_t, c, ct):
    b, tn, _ = q.shape
    p = cache_kt.shape[3]
    new = lambda w: pl.BlockSpec((1, tn, w), lambda i: (i, 0, 0))
    old = pl.BlockSpec((1, FOX_HEADS, FOX_HD, p), lambda i: (i, 0, 0, 0))
    return pl.pallas_call(
        _fox_sample_kernel,
        grid=(b,),
        in_specs=[new(MIX_W), old, old, new(MIX_W), new(MIX_W),
                  pl.BlockSpec((1, FOX_HEADS, p), lambda i: (i, 0, 0)),
                  new(FOX_HEADS), pl.BlockSpec((1, FOX_HEADS, V7X_LANES), lambda i: (i, 0, 0))],
        out_specs=new(MIX_W),
        out_shape=jax.ShapeDtypeStruct((b, tn, MIX_W), BF16),
        compiler_params=pltpu.CompilerParams(dimension_semantics=("arbitrary",), vmem_limit_bytes=VMEM_LIMIT),
        name="fox_sample",
    )(q, cache_kt, cache_vt, kb, vb, cache_logf_t, c, ct)


def _layer1_back_kernel(x_ref, mix_ref, mem_ref, g_post, g_mpre, g_mpost, w_out, w_up, w_down, y_ref, *, nb, tt):
    m = nb * tt
    x = x_ref[...].reshape(m, D_MODEL)
    mix = mix_ref[...].reshape(m, MIX_W)
    mem = mem_ref[...].reshape(m, MEM_W)
    x1 = [x[rows] + _rms(_dot(mix[rows], w_out[:MIX_W, :]) + _dot(mem[rows], w_out[MIX_W:, :]), g_post[...])
          for rows in _row_parts(m)]
    y = jnp.concatenate(_mlp(x1, g_mpre[...], g_mpost[...], w_up, w_down), axis=0)
    y_ref[...] = y.reshape(nb, tt, D_MODEL)


def _layer1_back(x, mix, mem, p, *, nb, tt):
    b, t, _ = x.shape
    tile = lambda w: pl.BlockSpec((nb, tt, w), lambda i, j: (i, j, 0))
    return pl.pallas_call(
        functools.partial(_layer1_back_kernel, nb=nb, tt=tt),
        grid=(b // nb, t // tt),
        in_specs=[tile(D_MODEL), tile(MIX_W), tile(MEM_W),
                  _layer_spec(p["g_mix_post"], 1), _layer_spec(p["g_mlp_pre"], 1), _layer_spec(p["g_mlp_post"], 1),
                  _layer_spec(p["w_out"][1], 0), _layer_spec(p["w_up"][1], 0), _layer_spec(p["w_down"][1], 0)],
        out_specs=tile(D_MODEL),
        out_shape=jax.ShapeDtypeStruct((b, t, D_MODEL), F32),
        compiler_params=pltpu.CompilerParams(dimension_semantics=("arbitrary", "arbitrary"),
                                             vmem_limit_bytes=VMEM_LIMIT),
        name="layer1_back",
    )(x, mix, mem, p["g_mix_post"], p["g_mlp_pre"], p["g_mlp_post"], p["w_out"][1], p["w_up"][1], p["w_down"][1])


def _token_tile(b, t):
    if t >= 512:
        return 1, 512
    return b, t


def _trunk(x, pool_prev, cache, mem_k, mem_v, p, later_f32=None):
    b, t, _ = x.shape
    nb, tt = _token_tile(b, t)
    pos_base = 0 if cache is None else POOL_BUF
    names = sorted(later_f32) if later_f32 else []
    y0, pool_state, *cast = _layer0(x, pool_prev, mem_k, mem_v, p, nb=nb, tt=tt, pos_base=pos_base,
                                    cast=[(later_f32[n], 1) for n in names])
    p = {**p, **{n: [p[n][0], w] for n, w in zip(names, cast)}}
    k, v, logf_t, mem, *att = _layer1_front(y0, mem_k, mem_v, p, nb=nb, tt=tt, fold=cache is None)
    logf = jnp.transpose(logf_t.reshape(b // nb, FOX_HEADS, nb, t), (0, 2, 3, 1)).reshape(b, t, FOX_HEADS)
    if cache is None:
        mix = _fox_prompt(*att)
    else:
        q, kb, vb, c, ct = att
        cache_k, cache_v, cache_logf = cache
        ct_new = jnp.swapaxes(ct.reshape(FOX_HEADS, b, t), 0, 1)
        ct_new = jnp.pad(ct_new, ((0, 0), (0, 0), (0, V7X_LANES - t)))
        to_lanes = lambda a: jnp.transpose(a, (0, 2, 3, 1))
        mix = _fox_sample(q, to_lanes(cache_k), to_lanes(cache_v), kb, vb, jnp.swapaxes(cache_logf, 1, 2), c, ct_new)
    y = _layer1_back(y0, mix, mem, p, nb=nb, tt=tt)
    return (y, pool_state, k.reshape(b, t, FOX_HEADS, FOX_HD), v.reshape(b, t, FOX_HEADS, FOX_HD), logf), p


def kernel(x_prompt, x_sample, cache_pool, cache_k, cache_v, cache_logf, cache_mem_k, cache_mem_v, mem_prompt,
           g_mix_pre, g_mix_post, g_mlp_pre, g_mlp_post, w_in, w_out, w_pool, pool_scale, g_kv, w_kvf, b_f,
           g_mem, w_mem_kv, w_up, w_down):
    depth = w_in.shape[0]
    assert depth == 2 and w_pool.shape[0] == 1, "one pooling layer followed by one forgetting-attention layer"
    bp = x_prompt.shape[0]
    w_f = w_kvf[:, 2 * MIX_W:]
    fold_cols = lambda a: jnp.pad(jnp.concatenate([a, jnp.repeat(a, FOLD_W, axis=-1)], axis=-1),
                                  [(0, 0)] * (a.ndim - 1) + [(0, V7X_LANES - FOLD_SRC - FOLD_W * FOX_HEADS)])
    rows = lambda g: g.reshape(g.shape[0], 1, g.shape[1])
    p = dict(
        g_mix_pre=rows(g_mix_pre), g_mix_post=rows(g_mix_post), g_mlp_pre=rows(g_mlp_pre),
        g_mlp_post=rows(g_mlp_post), pool_scale=rows(pool_scale), g_kv=g_kv.reshape(1, D_MODEL),
        w_pool=w_pool.astype(BF16), w_kv=w_kvf[:, :2 * MIX_W].astype(BF16),
        w_f=fold_cols(w_f).astype(BF16), w_ft=w_f.T.astype(BF16),
        b_f=fold_cols(b_f).reshape(1, V7X_LANES), b_ft=b_f.reshape(FOX_HEADS, 1),
    )
    later_f32 = dict(w_in=w_in, w_out=w_out, w_up=w_up, w_down=w_down)
    names = sorted(later_f32)
    mem_k_prompt, mem_v_prompt, mkh, mvh, *w0 = _mem_project(mem_prompt, g_mem, w_mem_kv,
                                                             cast=[(later_f32[n], 0) for n in names])
    p.update({n: [w, None] for n, w in zip(names, w0)})

    (y_p, pool_p, k_p, v_p, logf_p), p = _trunk(x_prompt, jnp.zeros((bp, HALO, MIX_W), F32), None, mkh, mvh, p,
                                                later_f32)

    pool_prev = jnp.pad(cache_pool[0], ((0, 0), (HALO - POOL_BUF, 0), (0, 0)))
    head_major = lambda a: jnp.swapaxes(a.astype(BF16), 2, 3)
    (y_s, pool_s, k_s, v_s, logf_s), _ = _trunk(x_sample, pool_prev, (cache_k, cache_v, cache_logf),
                                                head_major(cache_mem_k), head_major(cache_mem_v), p)

    return (y_p, y_s, pool_p, pool_s, k_p, v_p, logf_p, k_s, v_s, logf_s, mem_k_prompt, mem_v_prompt)
```

```python
import functools

import jax
import jax.numpy as jnp
from jax import lax
from jax.experimental import pallas as pl
from jax.experimental.pallas import tpu as pltpu

F32 = jnp.float32
BF16 = jnp.bfloat16

D_MODEL = 1024
MIX_W = D_MODEL // 2
MEM_W = D_MODEL - MIX_W
POOL_WINDOWS = (2, 4, 8, 16)
POOL_GROUP = MIX_W // len(POOL_WINDOWS)
POOL_BUF = max(POOL_WINDOWS) - 1
FOX_HEADS = 8
FOX_HD = MIX_W // FOX_HEADS
MEM_HEADS = 4
MEM_HD = MEM_W // MEM_HEADS
D_FF = 4 * D_MODEL
EPS = 1e-6

V7X_LANES = 128
V7X_SUBLANES_F32 = 8
V7X_VMEM_BYTES = 64 * 1024 * 1024

HALO = 2 * V7X_SUBLANES_F32
FF_CHUNK = 1024
ROW_PARTS = 2
PAIR_W = 2 * FOX_HD
NEG_BIG = -1e30
LOG2E = 1.4426950408889634
FOLD_SRC = FOX_HEADS
FOLD_W = 6
FOX_AHEAD = 2
assert FOLD_SRC + FOLD_W * FOX_HEADS <= V7X_LANES
VMEM_LIMIT = V7X_VMEM_BYTES - 8 * 1024 * 1024

assert HALO >= POOL_BUF and PAIR_W == V7X_LANES and POOL_GROUP == V7X_LANES and MEM_HD == V7X_LANES


def _const_spec(shape):
    zeros = (0,) * len(shape)
    return pl.BlockSpec(shape, lambda *_: zeros, pipeline_mode=pl.Buffered(1))


def _layer_spec(stacked, layer):
    index = (layer,) + (0,) * (stacked.ndim - 1)
    return pl.BlockSpec((None,) + stacked.shape[1:], lambda *_: index, pipeline_mode=pl.Buffered(1))


def _mem_spec(mem, layer, nb):
    return pl.BlockSpec((None, nb) + mem.shape[2:], lambda i, j: (layer, i, 0, 0, 0))


def _unit_rms(x):
    return x * lax.rsqrt(jnp.mean(x * x, axis=-1, keepdims=True) + EPS)


def _rms(x, g):
    return _unit_rms(x) * g


def _dot(a, b):
    return jnp.dot(a, b, preferred_element_type=F32)


def _dot_nt(a, b):
    return lax.dot_general(a, b, (((1,), (1,)), ((), ())), preferred_element_type=F32)


def _log_sigmoid(x):
    return jnp.minimum(x, 0.0) - jnp.log1p(jnp.exp(-jnp.abs(x)))


def _cumsum(x, axis, seg):
    assert seg & (seg - 1) == 0 and x.shape[axis] % seg == 0
    n = x.shape[axis]
    vreg = (V7X_SUBLANES_F32, V7X_LANES)[axis]
    idx = lax.broadcasted_iota(jnp.int32, x.shape, axis) & (seg - 1)
    k = 1
    while k < seg:
        if seg == n and k % vreg == 0:
            zeros = jnp.zeros(x.shape[:axis] + (k,) + x.shape[axis + 1:], x.dtype)
            x = x + jnp.concatenate([zeros, lax.slice_in_dim(x, 0, n - k, axis=axis)], axis=axis)
        else:
            x = x + jnp.where(idx >= k, pltpu.roll(x, k, axis=axis), 0.0)
        k *= 2
    return x


def _mem_logits(q, mk_ref, b):
    qs = (q * (MEM_HD ** -0.5 * LOG2E)).astype(BF16)
    return [_dot_nt(qs[:, h * MEM_HD:(h + 1) * MEM_HD], mk_ref[b, h]) for h in range(MEM_HEADS)]


def _mem_probs(logits):
    probs = []
    for s in logits:
        p = jnp.exp2(s - jnp.max(s, axis=-1, keepdims=True))
        probs.append((p.astype(BF16), jnp.sum(p, axis=-1, keepdims=True)))
    return probs


def _mem_values(probs, mv_ref, b):
    return jnp.concatenate([_dot(p, mv_ref[b, h]) / l for h, (p, l) in enumerate(probs)], axis=-1)


def _pool_mixer(u, halo, pos0, w_pool_ref, scale):
    tt = u.shape[0]
    ext = jnp.concatenate([halo, u], axis=0)
    pos = pos0 + lax.broadcasted_iota(jnp.int32, (tt, 1), 0)
    outs = []
    for g, w in enumerate(POOL_WINDOWS):
        sl = slice(g * POOL_GROUP, (g + 1) * POOL_GROUP)
        s = ext[:, sl]
        k = 1
        while k < w:
            s = s + pltpu.roll(s, k, axis=0)
            k *= 2
        cnt = jnp.minimum(pos + 1, w).astype(F32)
        pooled = s[HALO:] / cnt - u[:, sl]
        outs.append(_dot(pooled.astype(BF16), w_pool_ref[g]))
    return jnp.concatenate(outs, axis=-1) * scale


def _row_parts(m):
    n = ROW_PARTS if m >= 512 else 1
    return [slice(i * m // n, (i + 1) * m // n) for i in range(n)]


def _mlp(x1_parts, g_pre, g_post, w_up_ref, w_down_ref):
    n_chunks = D_FF // FF_CHUNK
    chunk = lambda c: slice(c * FF_CHUNK, (c + 1) * FF_CHUNK)
    act = lambda up: jnp.square(jnp.maximum(up, 0.0)).astype(BF16)
    hm_parts = [_rms(x1, g_pre).astype(BF16) for x1 in x1_parts]
    hm = jnp.concatenate(hm_parts, axis=0)
    parts = _row_parts(hm.shape[0])
    acc = None
    for c in range(n_chunks):
        if c == 0:
            a = jnp.concatenate([act(_dot(h, w_up_ref[:, chunk(c)])) for h in hm_parts], axis=0)
        else:
            a = act(_dot(hm, w_up_ref[:, chunk(c)]))
        if c < n_chunks - 1:
            d = _dot(a, w_down_ref[chunk(c), :])
            acc = d if acc is None else acc + d
    return [x1 + _rms(acc[rows] + _dot(a[rows], w_down_ref[chunk(n_chunks - 1), :]), g_post)
            for x1, rows in zip(x1_parts, parts)]


def _side_specs(jobs, steps, step_of):
    in_specs, out_specs, out_shape = [], [], []
    for kind, a, *rest in jobs:
        if kind == "cast":
            rows = a.shape[1] // steps
            in_specs.append(pl.BlockSpec((1, rows, a.shape[2]), lambda *ids, l=rest[0]: (l, step_of(*ids), 0)))
            out_specs.append(pl.BlockSpec((1, rows, a.shape[2]), lambda *ids: (0, step_of(*ids), 0)))
            out_shape.append(jax.ShapeDtypeStruct((1,) + a.shape[1:], BF16))
        else:
            n_l, n_s, nh, d = a.shape
            h = MEM_HEADS
            n = nh // h
            rows = n_l * n_s * n // steps
            parts = n // rows
            assert kind == "heads" and rows * parts == n

            def where(*ids, parts=parts, n_s=n_s):
                step = step_of(*ids)
                return step // parts // n_s, step // parts % n_s, step % parts

            in_specs.append(pl.BlockSpec((1, 1, rows * h, d), lambda *ids, w=where: w(*ids) + (0,)))
            out_specs.append(pl.BlockSpec((1, 1, h, rows, d),
                                          lambda *ids, w=where: w(*ids)[:2] + (0, w(*ids)[2], 0)))
            out_shape.append(jax.ShapeDtypeStruct((n_l, n_s, h, n, d), BF16))
    return in_specs, out_specs, out_shape


def _side_jobs(kinds, src_refs, dst_refs):
    for kind, src, dst in zip(kinds, src_refs, dst_refs):
        if kind == "cast":
            dst[...] = src[...].astype(BF16)
        else:
            x = src[0, 0].astype(BF16)
            heads, rows = dst.shape[2], dst.shape[3]
            row = lax.broadcasted_iota(jnp.int32, (rows, rows * heads), 0)
            pick = lax.broadcasted_iota(jnp.int32, (rows, rows * heads), 1) - heads * row
            for h in range(heads):
                dst[0, 0, h] = _dot((pick == h).astype(BF16), x).astype(BF16)


def _mem_project_kernel(mem_ref, g_ref, w_ref, *refs, side):
    n = len(side)
    mk_ref, mv_ref, mkh_ref, mvh_ref = refs[n:n + 4]
    _side_jobs(side, refs[:n], refs[n + 4:])
    xn = _unit_rms(mem_ref[0])
    for l in range(w_ref.shape[0]):
        kv = _dot((xn * g_ref[l]).astype(BF16), w_ref[l].astype(BF16))
        for h in range(MEM_HEADS):
            mk = kv[:, h * MEM_HD:(h + 1) * MEM_HD]
            mv = kv[:, MEM_W + h * MEM_HD:MEM_W + (h + 1) * MEM_HD]
            mk_ref[l, 0, :, h, :] = mk
            mv_ref[l, 0, :, h, :] = mv
            mkh_ref[l, 0, h] = mk.astype(BF16)
            mvh_ref[l, 0, h] = mv.astype(BF16)


def _mem_project(mem, g_mem, w_mem_kv, side=()):
    b, n, _ = mem.shape
    depth = w_mem_kv.shape[0]
    out = jax.ShapeDtypeStruct((depth, b, n, MEM_HEADS, MEM_HD), F32)
    out_h = jax.ShapeDtypeStruct((depth, b, MEM_HEADS, n, MEM_HD), BF16)
    g_mem = g_mem.reshape(depth, 1, D_MODEL)
    side_in, side_out, side_shape = _side_specs(side, b, lambda i: i)
    return pl.pallas_call(
        functools.partial(_mem_project_kernel, side=tuple(job[0] for job in side)),
        grid=(b,),
        in_specs=[pl.BlockSpec((1, n, D_MODEL), lambda i: (i, 0, 0)),
                  _const_spec(g_mem.shape), _const_spec(w_mem_kv.shape)] + side_in,
        out_specs=([pl.BlockSpec((depth, 1, n, MEM_HEADS, MEM_HD), lambda i: (0, i, 0, 0, 0))] * 2
                   + [pl.BlockSpec((depth, 1, MEM_HEADS, n, MEM_HD), lambda i: (0, i, 0, 0, 0))] * 2 + side_out),
        out_shape=[out, out, out_h, out_h] + side_shape,
        compiler_params=pltpu.CompilerParams(dimension_semantics=("arbitrary",), vmem_limit_bytes=VMEM_LIMIT),
        name="mem_project",
    )(mem, g_mem, w_mem_kv, *[job[1] for job in side])


def _layer0_kernel(x_ref, prev_ref, mk_ref, mv_ref, g_pre, g_post, g_mpre, g_mpost,
                   w_in, w_out, w_pool, pscale, w_up, w_down, *refs, nb, tt, pos_base, side):
    n = len(side)
    side_in, (y_ref, state_ref), side_out = refs[:n], refs[n:n + 2], refs[n + 2:2 * n + 2]
    halo_ref, = refs[2 * n + 2:]
    t = pl.program_id(1)

    @pl.when(t == 0)
    def _():
        halo_ref[...] = prev_ref[...]

    _side_jobs(side, side_in, side_out)
    x = x_ref[...].reshape(nb * tt, D_MODEL)
    proj = jnp.concatenate([_dot(_rms(x[rows], g_pre[...]).astype(BF16), w_in[...])
                            for rows in _row_parts(nb * tt)], axis=0)
    cats = []
    for b in range(nb):
        rows = slice(b * tt, (b + 1) * tt)
        u = proj[rows, :MIX_W]
        logits = _mem_logits(proj[rows, MIX_W:], mk_ref, b)
        mix = _pool_mixer(u, halo_ref[b], pos_base + t * tt, w_pool, pscale[...])
        halo_ref[b] = u[tt - HALO:, :]
        state_ref[0, b] = u[tt - POOL_BUF:, :]
        mem = _mem_values(_mem_probs(logits), mv_ref, b)
        cats.append(jnp.concatenate([mix, mem], axis=-1))
    cat = (cats[0] if nb == 1 else jnp.concatenate(cats, axis=0)).astype(BF16)
    x1 = [x[rows] + _rms(_dot(cat[rows], w_out[...]), g_post[...]) for rows in _row_parts(nb * tt)]
    y = jnp.concatenate(_mlp(x1, g_mpre[...], g_mpost[...], w_up, w_down), axis=0)
    y_ref[...] = y.reshape(nb, tt, D_MODEL)


def _layer0(x, prev, mem_k, mem_v, p, *, nb, tt, pos_base, side=()):
    b, t, _ = x.shape
    grid = (b // nb, t // tt)
    tile = pl.BlockSpec((nb, tt, D_MODEL), lambda i, j: (i, j, 0))
    side_in, side_out, side_shape = _side_specs(side, grid[0] * grid[1], lambda i, j: i * grid[1] + j)
    return pl.pallas_call(
        functools.partial(_layer0_kernel, nb=nb, tt=tt, pos_base=pos_base, side=tuple(job[0] for job in side)),
        grid=grid,
        in_specs=[tile, pl.BlockSpec((nb, HALO, MIX_W), lambda i, j: (i, 0, 0)),
                  _mem_spec(mem_k, 0, nb), _mem_spec(mem_v, 0, nb),
                  _layer_spec(p["g_mix_pre"], 0), _layer_spec(p["g_mix_post"], 0),
                  _layer_spec(p["g_mlp_pre"], 0), _layer_spec(p["g_mlp_post"], 0),
                  _layer_spec(p["w_in"][0], 0), _layer_spec(p["w_out"][0], 0),
                  _layer_spec(p["w_pool"], 0), _layer_spec(p["pool_scale"], 0),
                  _layer_spec(p["w_up"][0], 0), _layer_spec(p["w_down"][0], 0)] + side_in,
        out_specs=[tile, pl.BlockSpec((1, nb, POOL_BUF, MIX_W), lambda i, j: (0, i, 0, 0))] + side_out,
        out_shape=[jax.ShapeDtypeStruct((b, t, D_MODEL), F32),
                   jax.ShapeDtypeStruct((1, b, POOL_BUF, MIX_W), F32)] + side_shape,
        scratch_shapes=[pltpu.VMEM((nb, HALO, MIX_W), F32)],
        compiler_params=pltpu.CompilerParams(dimension_semantics=("arbitrary", "arbitrary"),
                                             vmem_limit_bytes=VMEM_LIMIT),
        name="layer0",
    )(x, prev, mem_k, mem_v, p["g_mix_pre"], p["g_mix_post"], p["g_mlp_pre"], p["g_mlp_post"],
      p["w_in"][0], p["w_out"][0], p["w_pool"], p["pool_scale"], p["w_up"][0], p["w_down"][0],
      *[job[1] for job in side])


def _bf16_part(x):
    return x.astype(BF16).astype(F32)


def _fold_lanes(h):
    lane = lax.broadcasted_iota(jnp.int32, (1, V7X_LANES), 1)
    own = (h % 2) * FOX_HD
    e0 = FOX_HD - own
    in_head = (lane >= own) & (lane < own + FOX_HD)
    first = ((lane >= e0) & (lane < e0 + 3)).astype(F32)
    second = ((lane >= e0 + 3) & (lane < e0 + 6)).astype(F32)
    return in_head, first, second, lane == e0, e0


def _fold_bias(c):
    cl = c * LOG2E
    hi = _bf16_part(cl)
    mid = _bf16_part(cl - hi)
    lo = _bf16_part(cl - hi - mid)
    lane = lax.broadcasted_iota(jnp.int32, (1, V7X_LANES), 1)
    piece = lax.rem(lane + (3 - FOLD_SRC % 3), 3)
    d = jnp.where(piece == 0, hi, jnp.where(piece == 1, mid, lo))
    return [pltpu.roll(d, (_fold_lanes(h)[4] - (FOLD_SRC + FOLD_W * h)) % V7X_LANES, axis=1)
            for h in range(FOX_HEADS)]


def _fold_kv(k, v, bias):
    ks, vs = [], []
    for h, dh in enumerate(bias):
        in_head, first, second, one, _ = _fold_lanes(h)
        cols = slice((h // 2) * PAIR_W, (h // 2 + 1) * PAIR_W)
        ks.append(jnp.where(in_head, k[:, cols], first - dh * second).astype(BF16))
        vs.append(jnp.where(in_head, v[:, cols], one.astype(F32)).T.astype(BF16))
    return jnp.concatenate(ks, axis=-1), jnp.concatenate(vs, axis=0)


def _fold_q(q, bias):
    qs = []
    for h, dh in enumerate(bias):
        in_head, first, second, _, _ = _fold_lanes(h)
        cols = slice((h // 2) * PAIR_W, (h // 2 + 1) * PAIR_W)
        qs.append(jnp.where(in_head, q[:, cols] * (FOX_HD ** -0.5 * LOG2E), dh * first + second).T.astype(BF16))
    return jnp.concatenate(qs, axis=0)


def _layer1_front_kernel(x_ref, mk_ref, mv_ref, g_kv, g_pre, w_kv, w_f, w_ft, b_f, b_ft, w_in, *refs,
                         nb, tt, fold):
    if fold:
        k_ref, v_ref, logft_ref, mem_ref, qa_ref, ka_ref, va_ref, carry_ref = refs
    else:
        k_ref, v_ref, logft_ref, mem_ref, q_ref, kb_ref, vb_ref, c_ref, ct_ref = refs
    if fold:
        @pl.when(pl.program_id(1) == 0)
        def _():
            carry_ref[...] = jnp.zeros_like(carry_ref)

    xn = _unit_rms(x_ref[...].reshape(nb * tt, D_MODEL))
    hk = (xn * g_kv[...]).astype(BF16)
    logf = _log_sigmoid(_dot(hk, w_f[...]) + b_f[...])
    k = _dot(hk, w_kv[:, :MIX_W])
    v = _dot(hk, w_kv[:, MIX_W:])
    k_ref[...] = k.reshape(nb, tt, MIX_W)
    v_ref[...] = v.reshape(nb, tt, MIX_W)
    c = _cumsum(logf, 0, tt)
    if fold:
        c = c + carry_ref[...]
        carry_ref[...] = c[tt - 1:, :]
        bias = _fold_bias(c)
        ka_ref[0], va_ref[0, 0] = _fold_kv(k, v, bias)
        logft_ref[0] = logf.T[:FOX_HEADS, :]
    proj = _dot((xn * g_pre[...]).astype(BF16), w_in[...])
    q = proj[:, :MIX_W]
    mem_logits = [_mem_logits(proj[b * tt:(b + 1) * tt, MIX_W:], mk_ref, b) for b in range(nb)]
    if fold:
        qa_ref[0, 0] = _fold_q(q, bias)
    else:
        q_ref[...] = (q * (FOX_HD ** -0.5)).astype(BF16).reshape(nb, tt, MIX_W)
        kb_ref[...] = k.astype(BF16).reshape(nb, tt, MIX_W)
        vb_ref[...] = v.astype(BF16).reshape(nb, tt, MIX_W)
        logf_t = _log_sigmoid(_dot_nt(w_ft[...], hk) + b_ft[...])
        logft_ref[0] = logf_t
        ct_ref[0, 0] = _cumsum(logf_t, 1, tt)
    for b in range(nb):
        rows = slice(b * tt, (b + 1) * tt)
        if not fold:
            c_ref[b] = c[rows, :FOX_HEADS]
        mem_ref[b] = _mem_values(_mem_probs(mem_logits[b]), mv_ref, b).astype(BF16)


def _layer1_front(x, mem_k, mem_v, p, *, nb, tt, fold):
    b, t, _ = x.shape
    assert (nb == 1) if fold else (t == tt)
    tile = lambda w: pl.BlockSpec((nb, tt, w), lambda i, j: (i, j, 0))
    act = lambda w, dt: jax.ShapeDtypeStruct((b, t, w), dt)
    out_specs = [tile(MIX_W), tile(MIX_W), pl.BlockSpec((1, FOX_HEADS, nb * tt), lambda i, j: (i, 0, j)), tile(MEM_W)]
    out_shape = [act(MIX_W, F32), act(MIX_W, F32), jax.ShapeDtypeStruct((b // nb, FOX_HEADS, nb * t), F32),
                 act(MEM_W, BF16)]
    if fold:
        wf = FOX_HEADS * PAIR_W
        tile_t = pl.BlockSpec((1, 1, wf, tt), lambda i, j: (i, j, 0, 0))
        act_t = jax.ShapeDtypeStruct((b, t // tt, wf, tt), BF16)
        out_specs += [tile_t, tile(wf), tile_t]
        out_shape += [act_t, act(wf, BF16), act_t]
        scratch = [pltpu.VMEM((1, V7X_LANES), F32)]
    else:
        out_specs += [tile(MIX_W)] * 3 + [tile(FOX_HEADS),
                                          pl.BlockSpec((1, 1, FOX_HEADS, nb * tt), lambda i, j: (i, j, 0, 0))]
        out_shape += [act(MIX_W, BF16)] * 3 + [act(FOX_HEADS, F32),
                                               jax.ShapeDtypeStruct((b // nb, t // tt, FOX_HEADS, nb * tt), F32)]
        scratch = []
    return pl.pallas_call(
        functools.partial(_layer1_front_kernel, nb=nb, tt=tt, fold=fold),
        grid=(b // nb, t // tt),
        in_specs=[tile(D_MODEL), _mem_spec(mem_k, 1, nb), _mem_spec(mem_v, 1, nb),
                  _const_spec(p["g_kv"].shape), _layer_spec(p["g_mix_pre"], 1),
                  _const_spec(p["w_kv"].shape), _const_spec(p["w_f"].shape), _const_spec(p["w_ft"].shape),
                  _const_spec(p["b_f"].shape), _const_spec(p["b_ft"].shape), _layer_spec(p["w_in"][1], 0)],
        out_specs=out_specs,
        out_shape=out_shape,
        scratch_shapes=scratch,
        compiler_params=pltpu.CompilerParams(dimension_semantics=("arbitrary", "arbitrary"),
                                             vmem_limit_bytes=VMEM_LIMIT),
        name="layer1_front",
    )(x, mem_k, mem_v, p["g_kv"], p["g_mix_pre"], p["w_kv"], p["w_f"], p["w_ft"], p["b_f"], p["b_ft"], p["w_in"][1])


def _fox_prompt_kernel(qt_ref, k_ref, vt_ref, o_ref, m_ref, acc_ref, *s_refs, t):
    i = pl.program_id(1)
    half = t // 2
    sub = lax.broadcasted_iota(jnp.int32, (PAIR_W, 1), 0)
    col = [slice(h * PAIR_W, (h + 1) * PAIR_W) for h in range(FOX_HEADS)]
    qt = [qt_ref[0, 0, c, :] for c in col]
    slot = lambda n: s_refs[n % len(s_refs)]

    def chain(n_items, logits, update):
        top = [logits(n) for n in range(FOX_AHEAD)]
        for n in range(n_items):
            if n + FOX_AHEAD < n_items:
                top.append(logits(n + FOX_AHEAD))
            update(n, top[n])

    tri = (lax.broadcasted_iota(jnp.int32, (half, half), 0) <= lax.broadcasted_iota(jnp.int32, (half, half), 1))
    lo = pl.ds(pl.multiple_of(i * t, t), half)
    hi = pl.ds(pl.multiple_of(i * t + half, half), half)

    def diag_logits(h):
        s_lo = _dot(k_ref[0, lo, col[h]], qt[h])
        s_hi = _dot(k_ref[0, hi, col[h]], qt[h][:, half:])
        s_lo = jnp.concatenate([jnp.where(tri, s_lo[:, :half], NEG_BIG), s_lo[:, half:]], axis=1)
        s_hi = jnp.where(tri, s_hi, NEG_BIG)
        slot(h)[:half, :] = s_lo
        slot(h)[half:, half:] = s_hi
        top = jnp.max(s_lo, axis=0, keepdims=True)
        return jnp.concatenate([top[:, :half], jnp.maximum(top[:, half:], jnp.max(s_hi, axis=0, keepdims=True))],
                               axis=1)

    def diag_update(h, m):
        p_lo = jnp.exp2(slot(h)[:half, :] - m).astype(BF16)
        p_hi = jnp.exp2(slot(h)[half:, half:] - m[:, half:]).astype(BF16)
        acc = _dot(vt_ref[0, i, col[h], :half], p_lo)
        acc_ref[h] = jnp.concatenate([acc[:, :half], acc[:, half:] + _dot(vt_ref[0, i, col[h], half:], p_hi)], axis=1)
        m_ref[h] = m

    chain(FOX_HEADS, diag_logits, diag_update)

    def full_tiles(j0, n_tiles):
        def logits(n):
            rows = pl.ds(pl.multiple_of((j0 + n // FOX_HEADS) * t, t), t)
            s = _dot(k_ref[0, rows, col[n % FOX_HEADS]], qt[n % FOX_HEADS])
            slot(n)[...] = s
            return jnp.max(s, axis=0, keepdims=True)

        def update(n, top):
            h = n % FOX_HEADS
            m = m_ref[h]
            m_new = jnp.maximum(m, top)
            p = jnp.exp2(slot(n)[...] - m_new).astype(BF16)
            acc_ref[h] = jnp.exp2(m - m_new) * acc_ref[h] + _dot(vt_ref[0, j0 + n // FOX_HEADS, col[h], :], p)
            m_ref[h] = m_new

        chain(n_tiles * FOX_HEADS, logits, update)

    lax.fori_loop(0, i // 2, lambda jj, c: full_tiles(2 * jj, 2), None)
    lax.fori_loop(2 * (i // 2), i, lambda j, c: full_tiles(j, 1), None)
    for pair in range(FOX_HEADS // 2):
        even, odd = acc_ref[2 * pair], acc_ref[2 * pair + 1]
        own = sub < FOX_HD
        out = jnp.where(own, even, odd) / jnp.where(own, even[FOX_HD:FOX_HD + 1, :], odd[0:1, :])
        o_ref[0, :, pair * PAIR_W:(pair + 1) * PAIR_W] = out.T.astype(BF16)


def _fox_prompt(qt, ka, vt):
    b, n_tiles, w, tile = qt.shape
    per_stream = lambda shape: pl.BlockSpec((1,) + shape, lambda i, j: (i,) + (0,) * len(shape))
    return pl.pallas_call(
        functools.partial(_fox_prompt_kernel, t=tile),
        grid=(b, n_tiles),
        in_specs=[pl.BlockSpec((1, 1, w, tile), lambda i, j: (i, j, 0, 0)),
                  per_stream((n_tiles * tile, w)), per_stream((n_tiles, w, tile))],
        out_specs=pl.BlockSpec((1, tile, MIX_W), lambda i, j: (i, j, 0)),
        out_shape=jax.ShapeDtypeStruct((b, n_tiles * tile, MIX_W), BF16),
        scratch_shapes=([pltpu.VMEM((FOX_HEADS, 1, tile), F32), pltpu.VMEM((FOX_HEADS, PAIR_W, tile), F32)]
                        + [pltpu.VMEM((tile, tile), F32)] * (FOX_AHEAD + 1)),
        compiler_params=pltpu.CompilerParams(dimension_semantics=("arbitrary", "arbitrary"),
                                             vmem_limit_bytes=VMEM_LIMIT),
        name="fox_prompt",
    )(qt, ka, vt)


def _fox_sample_kernel(q_ref, kc_ref, vc_ref, kn_ref, vn_ref, lfc_t_ref, c_ref, ct_ref, o_ref):
    tn = q_ref.shape[1]
    p = lfc_t_ref.shape[2]
    cc = _cumsum(lfc_t_ref[0], 1, p)
    ck_cache = cc - cc[:, p - 1:]
    causal = (lax.broadcasted_iota(jnp.int32, (tn, V7X_LANES), 1)
              <= lax.broadcasted_iota(jnp.int32, (tn, V7X_LANES), 0))
    pad = jnp.zeros((V7X_LANES - tn, FOX_HD), BF16)
    col = [slice(h * FOX_HD, (h + 1) * FOX_HD) for h in range(FOX_HEADS)]
    logits = []
    for h in range(FOX_HEADS):
        qh = q_ref[0, :, col[h]]
        cq = c_ref[0, :, h:h + 1]
        kn = jnp.concatenate([kn_ref[0, :, col[h]], pad], axis=0)
        s_old = _dot(qh, kc_ref[0, h].astype(BF16)) + cq - ck_cache[h:h + 1, :]
        s_new = jnp.where(causal, _dot_nt(qh, kn) + cq - ct_ref[0, h:h + 1, :], NEG_BIG)
        logits.append((s_old, s_new))
    probs = []
    for s_old, s_new in logits:
        m = jnp.maximum(jnp.max(s_old, axis=-1, keepdims=True), jnp.max(s_new, axis=-1, keepdims=True))
        p_old = jnp.exp(s_old - m)
        p_new = jnp.exp(s_new - m)
        l = jnp.sum(p_old, axis=-1, keepdims=True) + jnp.sum(p_new, axis=-1, keepdims=True)
        probs.append((p_old.astype(BF16), p_new.astype(BF16), l))
    outs = []
    for h, (p_old, p_new, l) in enumerate(probs):
        vn = jnp.concatenate([vn_ref[0, :, col[h]], pad], axis=0)
        outs.append((_dot_nt(p_old, vc_ref[0, h].astype(BF16)) + _dot(p_new, vn)) / l)
    o_ref[0] = jnp.concatenate(outs, axis=-1).astype(BF16)


def _fox_sample(q, cache_kt, cache_vt, kb, vb, cache_logf_t, c, ct):
    b, tn, _ = q.shape
    p = cache_kt.shape[3]
    new = lambda w: pl.BlockSpec((1, tn, w), lambda i: (i, 0, 0))
    old = pl.BlockSpec((1, FOX_HEADS, FOX_HD, p), lambda i: (i, 0, 0, 0))
    return pl.pallas_call(
        _fox_sample_kernel,
        grid=(b,),
        in_specs=[new(MIX_W), old, old, new(MIX_W), new(MIX_W),
                  pl.BlockSpec((1, FOX_HEADS, p), lambda i: (i, 0, 0)),
                  new(FOX_HEADS), pl.BlockSpec((1, FOX_HEADS, V7X_LANES), lambda i: (i, 0, 0))],
        out_specs=new(MIX_W),
        out_shape=jax.ShapeDtypeStruct((b, tn, MIX_W), BF16),
        compiler_params=pltpu.CompilerParams(dimension_semantics=("arbitrary",), vmem_limit_bytes=VMEM_LIMIT),
        name="fox_sample",
    )(q, cache_kt, cache_vt, kb, vb, cache_logf_t, c, ct)


def _layer1_back_kernel(x_ref, mix_ref, mem_ref, g_post, g_mpre, g_mpost, w_out, w_up, w_down, *refs, nb, tt, side):
    n = len(side)
    y_ref = refs[n]
    _side_jobs(side, refs[:n], refs[n + 1:])
    m = nb * tt
    x = x_ref[...].reshape(m, D_MODEL)
    mix = mix_ref[...].reshape(m, MIX_W)
    mem = mem_ref[...].reshape(m, MEM_W)
    x1 = [x[rows] + _rms(_dot(mix[rows], w_out[:MIX_W, :]) + _dot(mem[rows], w_out[MIX_W:, :]), g_post[...])
          for rows in _row_parts(m)]
    y = jnp.concatenate(_mlp(x1, g_mpre[...], g_mpost[...], w_up, w_down), axis=0)
    y_ref[...] = y.reshape(nb, tt, D_MODEL)


def _layer1_back(x, mix, mem, p, *, nb, tt, side=()):
    b, t, _ = x.shape
    grid = (b // nb, t // tt)
    tile = lambda w: pl.BlockSpec((nb, tt, w), lambda i, j: (i, j, 0))
    side_in, side_out, side_shape = _side_specs(side, grid[0] * grid[1], lambda i, j: i * grid[1] + j)
    return pl.pallas_call(
        functools.partial(_layer1_back_kernel, nb=nb, tt=tt, side=tuple(job[0] for job in side)),
        grid=grid,
        in_specs=[tile(D_MODEL), tile(MIX_W), tile(MEM_W),
                  _layer_spec(p["g_mix_post"], 1), _layer_spec(p["g_mlp_pre"], 1), _layer_spec(p["g_mlp_post"], 1),
                  _layer_spec(p["w_out"][1], 0), _layer_spec(p["w_up"][1], 0), _layer_spec(p["w_down"][1], 0)]
                 + side_in,
        out_specs=[tile(D_MODEL)] + side_out,
        out_shape=[jax.ShapeDtypeStruct((b, t, D_MODEL), F32)] + side_shape,
        compiler_params=pltpu.CompilerParams(dimension_semantics=("arbitrary", "arbitrary"),
                                             vmem_limit_bytes=VMEM_LIMIT),
        name="layer1_back",
    )(x, mix, mem, p["g_mix_post"], p["g_mlp_pre"], p["g_mlp_post"], p["w_out"][1], p["w_up"][1], p["w_down"][1],
      *[job[1] for job in side])


def _token_tile(b, t):
    if t >= 512:
        return 1, 512
    return b, t


def _trunk(x, pool_prev, cache, mem_k, mem_v, p, later_f32=None, to_head_major=()):
    b, t, _ = x.shape
    nb, tt = _token_tile(b, t)
    pos_base = 0 if cache is None else POOL_BUF
    names = sorted(later_f32) if later_f32 else []
    heads = [("heads", a.reshape(a.shape[:2] + (-1, a.shape[-1]))) for a in to_head_major]
    y0, pool_state, *done = _layer0(x, pool_prev, mem_k, mem_v, p, nb=nb, tt=tt, pos_base=pos_base,
                                    side=[("cast", later_f32[n], 1) for n in names] + heads[:1])
    p = {**p, **{n: [p[n][0], w] for n, w in zip(names, done)}}
    head_major = done[len(names):]
    k, v, logf_t, mem, *att = _layer1_front(y0, mem_k, mem_v, p, nb=nb, tt=tt, fold=cache is None)
    logf = jnp.transpose(logf_t.reshape(b // nb, FOX_HEADS, nb, t), (0, 2, 3, 1)).reshape(b, t, FOX_HEADS)
    if cache is None:
        mix = _fox_prompt(*att)
    else:
        q, kb, vb, c, ct = att
        cache_k, cache_v, cache_logf = cache
        ct_new = jnp.swapaxes(ct.reshape(FOX_HEADS, b, t), 0, 1)
        ct_new = jnp.pad(ct_new, ((0, 0), (0, 0), (0, V7X_LANES - t)))
        to_lanes = lambda a: jnp.transpose(a, (0, 2, 3, 1))
        mix = _fox_sample(q, to_lanes(cache_k), to_lanes(cache_v), kb, vb, jnp.swapaxes(cache_logf, 1, 2), c, ct_new)
    y, *done = _layer1_back(y0, mix, mem, p, nb=nb, tt=tt, side=heads[1:])
    return ((y, pool_state, k.reshape(b, t, FOX_HEADS, FOX_HD), v.reshape(b, t, FOX_HEADS, FOX_HD), logf), p,
            head_major + done)


def kernel(x_prompt, x_sample, cache_pool, cache_k, cache_v, cache_logf, cache_mem_k, cache_mem_v, mem_prompt,
           g_mix_pre, g_mix_post, g_mlp_pre, g_mlp_post, w_in, w_out, w_pool, pool_scale, g_kv, w_kvf, b_f,
           g_mem, w_mem_kv, w_up, w_down):
    depth = w_in.shape[0]
    assert depth == 2 and w_pool.shape[0] == 1, "one pooling layer followed by one forgetting-attention layer"
    bp = x_prompt.shape[0]
    w_f = w_kvf[:, 2 * MIX_W:]
    fold_cols = lambda a: jnp.pad(jnp.concatenate([a, jnp.repeat(a, FOLD_W, axis=-1)], axis=-1),
                                  [(0, 0)] * (a.ndim - 1) + [(0, V7X_LANES - FOLD_SRC - FOLD_W * FOX_HEADS)])
    rows = lambda g: g.reshape(g.shape[0], 1, g.shape[1])
    p = dict(
        g_mix_pre=rows(g_mix_pre), g_mix_post=rows(g_mix_post), g_mlp_pre=rows(g_mlp_pre),
        g_mlp_post=rows(g_mlp_post), pool_scale=rows(pool_scale), g_kv=g_kv.reshape(1, D_MODEL),
        w_pool=w_pool.astype(BF16), w_kv=w_kvf[:, :2 * MIX_W].astype(BF16),
        w_f=fold_cols(w_f).astype(BF16), w_ft=w_f.T.astype(BF16),
        b_f=fold_cols(b_f).reshape(1, V7X_LANES), b_ft=b_f.reshape(FOX_HEADS, 1),
    )
    later_f32 = dict(w_in=w_in, w_out=w_out, w_up=w_up, w_down=w_down)
    names = sorted(later_f32)
    mem_k_prompt, mem_v_prompt, mkh, mvh, *w0 = _mem_project(mem_prompt, g_mem, w_mem_kv,
                                                             side=[("cast", later_f32[n], 0) for n in names])
    p.update({n: [w, None] for n, w in zip(names, w0)})

    (y_p, pool_p, k_p, v_p, logf_p), p, (cmk, cmv) = _trunk(
        x_prompt, jnp.zeros((bp, HALO, MIX_W), F32), None, mkh, mvh, p, later_f32, (cache_mem_k, cache_mem_v))

    pool_prev = jnp.pad(cache_pool[0], ((0, 0), (HALO - POOL_BUF, 0), (0, 0)))
    (y_s, pool_s, k_s, v_s, logf_s), _, _ = _trunk(x_sample, pool_prev, (cache_k, cache_v, cache_logf), cmk, cmv, p)

    return (y_p, y_s, pool_p, pool_s, k_p, v_p, logf_p, k_s, v_s, logf_s, mem_k_prompt, mem_v_prompt)
```

```python
import functools

import jax
import jax.numpy as jnp
from jax import lax
from jax.experimental import pallas as pl
from jax.experimental.pallas import tpu as pltpu

F32 = jnp.float32
BF16 = jnp.bfloat16

D_MODEL = 1024
MIX_W = D_MODEL // 2
MEM_W = D_MODEL - MIX_W
POOL_WINDOWS = (2, 4, 8, 16)
POOL_GROUP = MIX_W // len(POOL_WINDOWS)
POOL_BUF = max(POOL_WINDOWS) - 1
FOX_HEADS = 8
FOX_HD = MIX_W // FOX_HEADS
MEM_HEADS = 4
MEM_HD = MEM_W // MEM_HEADS
D_FF = 4 * D_MODEL
EPS = 1e-6

V7X_LANES = 128
V7X_SUBLANES_F32 = 8
V7X_VMEM_BYTES = 64 * 1024 * 1024

HALO = 2 * V7X_SUBLANES_F32
FF_CHUNK = 1024
TOKEN_TILE = 512
ROW_PARTS = 2
PAIR_W = 2 * FOX_HD
NEG_BIG = -1e30
LOG2E = 1.4426950408889634
FOLD_SRC = FOX_HEADS
FOLD_W = 6
FOX_AHEAD = 2
assert FOLD_SRC + FOLD_W * FOX_HEADS <= V7X_LANES
VMEM_LIMIT = V7X_VMEM_BYTES - 8 * 1024 * 1024

assert HALO >= POOL_BUF and PAIR_W == V7X_LANES and POOL_GROUP == V7X_LANES and MEM_HD == V7X_LANES


def _const_spec(shape):
    zeros = (0,) * len(shape)
    return pl.BlockSpec(shape, lambda *_: zeros, pipeline_mode=pl.Buffered(1))


def _layer_spec(stacked, layer):
    index = (layer,) + (0,) * (stacked.ndim - 1)
    return pl.BlockSpec((None,) + stacked.shape[1:], lambda *_: index, pipeline_mode=pl.Buffered(1))


def _mem_spec(mem, layer, nb):
    return pl.BlockSpec((None, nb) + mem.shape[2:], lambda i, j: (layer, i, 0, 0, 0))


def _unit_rms(x):
    return x * lax.rsqrt(jnp.mean(x * x, axis=-1, keepdims=True) + EPS)


def _rms(x, g):
    return _unit_rms(x) * g


GAIN_KINDS = ("mix_pre", "mix_post", "mlp_pre", "mlp_post")


def _gain(g_ref, kind, layer=0):
    depth = (g_ref.shape[0] - 1) // len(GAIN_KINDS)
    r = len(GAIN_KINDS) * depth if kind == "kv" else GAIN_KINDS.index(kind) * depth + layer
    return g_ref[r:r + 1, :]


def _pack_gains(g_mix_pre, g_mix_post, g_mlp_pre, g_mlp_post, g_kv):
    return jnp.concatenate([g_mix_pre, g_mix_post, g_mlp_pre, g_mlp_post, g_kv[None, :]], axis=0)


def _dot(a, b):
    return jnp.dot(a, b, preferred_element_type=F32)


def _dot_nt(a, b):
    return lax.dot_general(a, b, (((1,), (1,)), ((), ())), preferred_element_type=F32)


def _log_sigmoid(x):
    return jnp.minimum(x, 0.0) - jnp.log1p(jnp.exp(-jnp.abs(x)))


def _cumsum(x, axis, seg):
    assert seg & (seg - 1) == 0 and x.shape[axis] % seg == 0
    n = x.shape[axis]
    vreg = (V7X_SUBLANES_F32, V7X_LANES)[axis]
    idx = lax.broadcasted_iota(jnp.int32, x.shape, axis) & (seg - 1)
    k = 1
    while k < seg:
        if seg == n and k % vreg == 0:
            zeros = jnp.zeros(x.shape[:axis] + (k,) + x.shape[axis + 1:], x.dtype)
            x = x + jnp.concatenate([zeros, lax.slice_in_dim(x, 0, n - k, axis=axis)], axis=axis)
        else:
            x = x + jnp.where(idx >= k, pltpu.roll(x, k, axis=axis), 0.0)
        k *= 2
    return x


def _mem_logits(q, mk_ref, b):
    qs = (q * (MEM_HD ** -0.5 * LOG2E)).astype(BF16)
    return [_dot_nt(qs[:, h * MEM_HD:(h + 1) * MEM_HD], mk_ref[b, h]) for h in range(MEM_HEADS)]


def _mem_probs(logits):
    probs = []
    for s in logits:
        p = jnp.exp2(s - jnp.max(s, axis=-1, keepdims=True))
        probs.append((p.astype(BF16), jnp.sum(p, axis=-1, keepdims=True)))
    return probs


def _mem_values(probs, mv_ref, b):
    return jnp.concatenate([_dot(p, mv_ref[b, h]) / l for h, (p, l) in enumerate(probs)], axis=-1)


def _pool_mixer(u, halo, pos0, w_pool_ref, scale):
    tt = u.shape[0]
    ext = jnp.concatenate([halo, u], axis=0)
    pos = pos0 + lax.broadcasted_iota(jnp.int32, (tt, 1), 0)
    outs = []
    for g, w in enumerate(POOL_WINDOWS):
        sl = slice(g * POOL_GROUP, (g + 1) * POOL_GROUP)
        s = ext[:, sl]
        k = 1
        while k < w:
            s = s + pltpu.roll(s, k, axis=0)
            k *= 2
        cnt = jnp.minimum(pos + 1, w).astype(F32)
        pooled = s[HALO:] / cnt - u[:, sl]
        outs.append(_dot(pooled.astype(BF16), w_pool_ref[g]))
    return jnp.concatenate(outs, axis=-1) * scale


def _row_parts(m):
    n = ROW_PARTS if m >= TOKEN_TILE else 1
    return [slice(i * m // n, (i + 1) * m // n) for i in range(n)]


def _mlp(x1_parts, g_pre, g_post, w_up_ref, w_down_ref):
    n_chunks = D_FF // FF_CHUNK
    chunk = lambda c: slice(c * FF_CHUNK, (c + 1) * FF_CHUNK)
    act = lambda up: jnp.square(jnp.maximum(up, 0.0)).astype(BF16)
    hm_parts = [_rms(x1, g_pre).astype(BF16) for x1 in x1_parts]
    hm = jnp.concatenate(hm_parts, axis=0)
    parts = _row_parts(hm.shape[0])
    acc = None
    for c in range(n_chunks):
        if c == 0:
            a = jnp.concatenate([act(_dot(h, w_up_ref[:, chunk(c)])) for h in hm_parts], axis=0)
        else:
            a = act(_dot(hm, w_up_ref[:, chunk(c)]))
        if c < n_chunks - 1:
            d = _dot(a, w_down_ref[chunk(c), :])
            acc = d if acc is None else acc + d
    return [x1 + _rms(acc[rows] + _dot(a[rows], w_down_ref[chunk(n_chunks - 1), :]), g_post)
            for x1, rows in zip(x1_parts, parts)]


def _side_specs(jobs, steps, step_of):
    in_specs, out_specs, out_shape = [], [], []
    for kind, a, *rest in jobs:
        if kind == "cast":
            rows = a.shape[1] // steps
            in_specs.append(pl.BlockSpec((1, rows, a.shape[2]), lambda *ids, l=rest[0]: (l, step_of(*ids), 0)))
            out_specs.append(pl.BlockSpec((1, rows, a.shape[2]), lambda *ids: (0, step_of(*ids), 0)))
            out_shape.append(jax.ShapeDtypeStruct((1,) + a.shape[1:], BF16))
        else:
            n_l, n_s, nh, d = a.shape
            h = MEM_HEADS
            n = nh // h
            rows = n_l * n_s * n // steps
            parts = n // rows
            assert kind == "heads" and rows * parts == n

            def where(*ids, parts=parts, n_s=n_s):
                step = step_of(*ids)
                return step // parts // n_s, step // parts % n_s, step % parts

            in_specs.append(pl.BlockSpec((1, 1, rows * h, d), lambda *ids, w=where: w(*ids) + (0,)))
            out_specs.append(pl.BlockSpec((1, 1, h, rows, d),
                                          lambda *ids, w=where: w(*ids)[:2] + (0, w(*ids)[2], 0)))
            out_shape.append(jax.ShapeDtypeStruct((n_l, n_s, h, n, d), BF16))
    return in_specs, out_specs, out_shape


def _side_jobs(kinds, src_refs, dst_refs):
    for kind, src, dst in zip(kinds, src_refs, dst_refs):
        if kind == "cast":
            dst[...] = src[...].astype(BF16)
        else:
            x = src[0, 0].astype(BF16)
            heads, rows = dst.shape[2], dst.shape[3]
            row = lax.broadcasted_iota(jnp.int32, (rows, rows * heads), 0)
            pick = lax.broadcasted_iota(jnp.int32, (rows, rows * heads), 1) - heads * row
            for h in range(heads):
                dst[0, 0, h] = _dot((pick == h).astype(BF16), x).astype(BF16)


def _mem_project_kernel(mem_ref, g_ref, w_ref, *refs, side):
    n = len(side)
    mk_ref, mv_ref, mkh_ref, mvh_ref = refs[n:n + 4]
    _side_jobs(side, refs[:n], refs[n + 4:])
    xn = _unit_rms(mem_ref[0])
    for l in range(w_ref.shape[0]):
        kv = _dot((xn * g_ref[l:l + 1, :]).astype(BF16), w_ref[l].astype(BF16))
        for h in range(MEM_HEADS):
            mk = kv[:, h * MEM_HD:(h + 1) * MEM_HD]
            mv = kv[:, MEM_W + h * MEM_HD:MEM_W + (h + 1) * MEM_HD]
            mk_ref[l, 0, :, h, :] = mk
            mv_ref[l, 0, :, h, :] = mv
            mkh_ref[l, 0, h] = mk.astype(BF16)
            mvh_ref[l, 0, h] = mv.astype(BF16)


def _mem_project(mem, g_mem, w_mem_kv, side=()):
    b, n, _ = mem.shape
    depth = w_mem_kv.shape[0]
    out = jax.ShapeDtypeStruct((depth, b, n, MEM_HEADS, MEM_HD), F32)
    out_h = jax.ShapeDtypeStruct((depth, b, MEM_HEADS, n, MEM_HD), BF16)
    side_in, side_out, side_shape = _side_specs(side, b, lambda i: i)
    return pl.pallas_call(
        functools.partial(_mem_project_kernel, side=tuple(job[0] for job in side)),
        grid=(b,),
        in_specs=[pl.BlockSpec((1, n, D_MODEL), lambda i: (i, 0, 0)),
                  _const_spec(g_mem.shape), _const_spec(w_mem_kv.shape)] + side_in,
        out_specs=([pl.BlockSpec((depth, 1, n, MEM_HEADS, MEM_HD), lambda i: (0, i, 0, 0, 0))] * 2
                   + [pl.BlockSpec((depth, 1, MEM_HEADS, n, MEM_HD), lambda i: (0, i, 0, 0, 0))] * 2 + side_out),
        out_shape=[out, out, out_h, out_h] + side_shape,
        compiler_params=pltpu.CompilerParams(dimension_semantics=("arbitrary",), vmem_limit_bytes=VMEM_LIMIT),
        name="mem_project",
    )(mem, g_mem, w_mem_kv, *[job[1] for job in side])


def _layer0_kernel(x_ref, prev_ref, mk_ref, mv_ref, g_ref, w_in, w_out, w_pool, pscale, w_up, w_down, *refs,
                   nb, tt, pos_base, side):
    n = len(side)
    side_in, (y_ref, state_ref), side_out = refs[:n], refs[n:n + 2], refs[n + 2:2 * n + 2]
    halo_ref, = refs[2 * n + 2:]
    t = pl.program_id(1)

    @pl.when(t == 0)
    def _():
        halo_ref[...] = prev_ref[...]

    x = x_ref[...].reshape(nb * tt, D_MODEL)
    proj = jnp.concatenate([_dot(_rms(x[rows], _gain(g_ref, "mix_pre")).astype(BF16), w_in[...])
                            for rows in _row_parts(nb * tt)], axis=0)
    cats = []
    for b in range(nb):
        rows = slice(b * tt, (b + 1) * tt)
        u = proj[rows, :MIX_W]
        logits = _mem_logits(proj[rows, MIX_W:], mk_ref, b)
        mix = _pool_mixer(u, halo_ref[b], pos_base + t * tt, w_pool, pscale[...])
        halo_ref[b] = u[tt - HALO:, :]
        state_ref[0, b] = u[tt - POOL_BUF:, :]
        mem = _mem_values(_mem_probs(logits), mv_ref, b)
        cats.append(jnp.concatenate([mix, mem], axis=-1))
    cat = (cats[0] if nb == 1 else jnp.concatenate(cats, axis=0)).astype(BF16)
    x1 = [x[rows] + _rms(_dot(cat[rows], w_out[...]), _gain(g_ref, "mix_post")) for rows in _row_parts(nb * tt)]
    y = jnp.concatenate(_mlp(x1, _gain(g_ref, "mlp_pre"), _gain(g_ref, "mlp_post"), w_up, w_down), axis=0)
    y_ref[...] = y.reshape(nb, tt, D_MODEL)
    _side_jobs(side, side_in, side_out)


def _layer0(x, prev, mem_k, mem_v, p, *, nb, tt, pos_base, side=()):
    b, t, _ = x.shape
    grid = (b // nb, t // tt)
    tile = pl.BlockSpec((nb, tt, D_MODEL), lambda i, j: (i, j, 0))
    side_in, side_out, side_shape = _side_specs(side, grid[0] * grid[1], lambda i, j: i * grid[1] + j)
    return pl.pallas_call(
        functools.partial(_layer0_kernel, nb=nb, tt=tt, pos_base=pos_base, side=tuple(job[0] for job in side)),
        grid=grid,
        in_specs=[tile, pl.BlockSpec((nb, HALO, MIX_W), lambda i, j: (i, 0, 0)),
                  _mem_spec(mem_k, 0, nb), _mem_spec(mem_v, 0, nb),
                  _const_spec(p["gains"].shape), _layer_spec(p["w_in"][0], 0), _layer_spec(p["w_out"][0], 0),
                  _layer_spec(p["w_pool"], 0), _layer_spec(p["pool_scale"], 0),
                  _layer_spec(p["w_up"][0], 0), _layer_spec(p["w_down"][0], 0)] + side_in,
        out_specs=[tile, pl.BlockSpec((1, nb, POOL_BUF, MIX_W), lambda i, j: (0, i, 0, 0))] + side_out,
        out_shape=[jax.ShapeDtypeStruct((b, t, D_MODEL), F32),
                   jax.ShapeDtypeStruct((1, b, POOL_BUF, MIX_W), F32)] + side_shape,
        scratch_shapes=[pltpu.VMEM((nb, HALO, MIX_W), F32)],
        compiler_params=pltpu.CompilerParams(dimension_semantics=("arbitrary", "arbitrary"),
                                             vmem_limit_bytes=VMEM_LIMIT),
        name="layer0",
    )(x, prev, mem_k, mem_v, p["gains"], p["w_in"][0], p["w_out"][0], p["w_pool"], p["pool_scale"], p["w_up"][0], p["w_down"][0],
      *[job[1] for job in side])


def _bf16_part(x):
    return x.astype(BF16).astype(F32)


def _fold_lanes(h):
    lane = lax.broadcasted_iota(jnp.int32, (1, V7X_LANES), 1)
    own = (h % 2) * FOX_HD
    e0 = FOX_HD - own
    in_head = (lane >= own) & (lane < own + FOX_HD)
    first = ((lane >= e0) & (lane < e0 + 3)).astype(F32)
    second = ((lane >= e0 + 3) & (lane < e0 + 6)).astype(F32)
    return in_head, first, second, lane == e0, e0


def _fold_bias(c):
    cl = c * LOG2E
    hi = _bf16_part(cl)
    mid = _bf16_part(cl - hi)
    lo = _bf16_part(cl - hi - mid)
    lane = lax.broadcasted_iota(jnp.int32, (1, V7X_LANES), 1)
    piece = lax.rem(lane + (3 - FOLD_SRC % 3), 3)
    d = jnp.where(piece == 0, hi, jnp.where(piece == 1, mid, lo))
    return [pltpu.roll(d, (_fold_lanes(h)[4] - (FOLD_SRC + FOLD_W * h)) % V7X_LANES, axis=1)
            for h in range(FOX_HEADS)]


def _fold_kv(k, v, bias):
    ks, vs = [], []
    for h, dh in enumerate(bias):
        in_head, first, second, one, _ = _fold_lanes(h)
        cols = slice((h // 2) * PAIR_W, (h // 2 + 1) * PAIR_W)
        ks.append(jnp.where(in_head, k[:, cols], first - dh * second).astype(BF16))
        vs.append(jnp.where(in_head, v[:, cols], one.astype(F32)).T.astype(BF16))
    return jnp.concatenate(ks, axis=-1), jnp.concatenate(vs, axis=0)


def _fold_q(q, bias):
    qs = []
    for h, dh in enumerate(bias):
        in_head, first, second, _, _ = _fold_lanes(h)
        cols = slice((h // 2) * PAIR_W, (h // 2 + 1) * PAIR_W)
        qs.append(jnp.where(in_head, q[:, cols] * (FOX_HD ** -0.5 * LOG2E), dh * first + second).T.astype(BF16))
    return jnp.concatenate(qs, axis=0)


def _layer1_front_kernel(x_ref, mk_ref, mv_ref, g_ref, w_kv, w_f, w_ft, b_f, b_ft, w_in, *refs,
                         nb, tt, fold):
    if fold:
        k_ref, v_ref, logft_ref, mem_ref, qa_ref, ka_ref, va_ref, carry_ref = refs
    else:
        k_ref, v_ref, logft_ref, mem_ref, q_ref, kb_ref, vb_ref, c_ref, ct_ref = refs
    if fold:
        @pl.when(pl.program_id(1) == 0)
        def _():
            carry_ref[...] = jnp.zeros_like(carry_ref)

    xn = _unit_rms(x_ref[...].reshape(nb * tt, D_MODEL))
    hk = (xn * _gain(g_ref, "kv")).astype(BF16)
    logf = _log_sigmoid(_dot(hk, w_f[...]) + b_f[...])
    k = _dot(hk, w_kv[:, :MIX_W])
    v = _dot(hk, w_kv[:, MIX_W:])
    k_ref[...] = k.reshape(nb, tt, MIX_W)
    v_ref[...] = v.reshape(nb, tt, MIX_W)
    c = _cumsum(logf, 0, tt)
    if fold:
        c = c + carry_ref[...]
        carry_ref[...] = c[tt - 1:, :]
        bias = _fold_bias(c)
        ka_ref[0], va_ref[0, 0] = _fold_kv(k, v, bias)
        logft_ref[0] = logf.T[:FOX_HEADS, :]
    proj = _dot((xn * _gain(g_ref, "mix_pre", 1)).astype(BF16), w_in[...])
    q = proj[:, :MIX_W]
    mem_logits = [_mem_logits(proj[b * tt:(b + 1) * tt, MIX_W:], mk_ref, b) for b in range(nb)]
    if fold:
        qa_ref[0, 0] = _fold_q(q, bias)
    else:
        q_ref[...] = (q * (FOX_HD ** -0.5)).astype(BF16).reshape(nb, tt, MIX_W)
        kb_ref[...] = k.astype(BF16).reshape(nb, tt, MIX_W)
        vb_ref[...] = v.astype(BF16).reshape(nb, tt, MIX_W)
        logf_t = _log_sigmoid(_dot_nt(w_ft[...], hk) + b_ft[...])
        logft_ref[0] = logf_t
        ct_ref[0, 0] = _cumsum(logf_t, 1, tt)
    for b in range(nb):
        rows = slice(b * tt, (b + 1) * tt)
        if not fold:
            c_ref[b] = c[rows, :FOX_HEADS]
        mem_ref[b] = _mem_values(_mem_probs(mem_logits[b]), mv_ref, b).astype(BF16)


def _layer1_front(x, mem_k, mem_v, p, *, nb, tt, fold):
    b, t, _ = x.shape
    assert (nb == 1) if fold else (t == tt)
    tile = lambda w: pl.BlockSpec((nb, tt, w), lambda i, j: (i, j, 0))
    act = lambda w, dt: jax.ShapeDtypeStruct((b, t, w), dt)
    out_specs = [tile(MIX_W), tile(MIX_W), pl.BlockSpec((1, FOX_HEADS, nb * tt), lambda i, j: (i, 0, j)), tile(MEM_W)]
    out_shape = [act(MIX_W, F32), act(MIX_W, F32), jax.ShapeDtypeStruct((b // nb, FOX_HEADS, nb * t), F32),
                 act(MEM_W, BF16)]
    if fold:
        wf = FOX_HEADS * PAIR_W
        tile_t = pl.BlockSpec((1, 1, wf, tt), lambda i, j: (i, j, 0, 0))
        act_t = jax.ShapeDtypeStruct((b, t // tt, wf, tt), BF16)
        out_specs += [tile_t, tile(wf), tile_t]
        out_shape += [act_t, act(wf, BF16), act_t]
        scratch = [pltpu.VMEM((1, V7X_LANES), F32)]
    else:
        out_specs += [tile(MIX_W)] * 3 + [tile(FOX_HEADS),
                                          pl.BlockSpec((1, 1, FOX_HEADS, nb * tt), lambda i, j: (i, j, 0, 0))]
        out_shape += [act(MIX_W, BF16)] * 3 + [act(FOX_HEADS, F32),
                                               jax.ShapeDtypeStruct((b // nb, t // tt, FOX_HEADS, nb * tt), F32)]
        scratch = []
    return pl.pallas_call(
        functools.partial(_layer1_front_kernel, nb=nb, tt=tt, fold=fold),
        grid=(b // nb, t // tt),
        in_specs=[tile(D_MODEL), _mem_spec(mem_k, 1, nb), _mem_spec(mem_v, 1, nb),
                  _const_spec(p["gains"].shape),
                  _const_spec(p["w_kv"].shape), _const_spec(p["w_f"].shape), _const_spec(p["w_ft"].shape),
                  _const_spec(p["b_f"].shape), _const_spec(p["b_ft"].shape), _layer_spec(p["w_in"][1], 0)],
        out_specs=out_specs,
        out_shape=out_shape,
        scratch_shapes=scratch,
        compiler_params=pltpu.CompilerParams(dimension_semantics=("arbitrary", "arbitrary"),
                                             vmem_limit_bytes=VMEM_LIMIT),
        name="layer1_front",
    )(x, mem_k, mem_v, p["gains"], p["w_kv"], p["w_f"], p["w_ft"], p["b_f"], p["b_ft"], p["w_in"][1])


def _fox_prompt_kernel(qt_ref, k_ref, vt_ref, o_ref, m_ref, acc_ref, *s_refs, t):
    i = pl.program_id(1)
    half = t // 2
    sub = lax.broadcasted_iota(jnp.int32, (PAIR_W, 1), 0)
    col = [slice(h * PAIR_W, (h + 1) * PAIR_W) for h in range(FOX_HEADS)]
    qt = [qt_ref[0, 0, c, :] for c in col]
    slot = lambda n: s_refs[n % len(s_refs)]

    def chain(n_items, logits, update):
        top = [logits(n) for n in range(FOX_AHEAD)]
        for n in range(n_items):
            if n + FOX_AHEAD < n_items:
                top.append(logits(n + FOX_AHEAD))
            update(n, top[n])

    tri = (lax.broadcasted_iota(jnp.int32, (half, half), 0) <= lax.broadcasted_iota(jnp.int32, (half, half), 1))
    lo = pl.ds(pl.multiple_of(i * t, t), half)
    hi = pl.ds(pl.multiple_of(i * t + half, half), half)

    def diag_logits(h):
        s_lo = _dot(k_ref[0, lo, col[h]], qt[h])
        s_hi = _dot(k_ref[0, hi, col[h]], qt[h][:, half:])
        s_lo = jnp.concatenate([jnp.where(tri, s_lo[:, :half], NEG_BIG), s_lo[:, half:]], axis=1)
        s_hi = jnp.where(tri, s_hi, NEG_BIG)
        slot(h)[:half, :] = s_lo
        slot(h)[half:, half:] = s_hi
        top = jnp.max(s_lo, axis=0, keepdims=True)
        return jnp.concatenate([top[:, :half], jnp.maximum(top[:, half:], jnp.max(s_hi, axis=0, keepdims=True))],
                               axis=1)

    def diag_update(h, m):
        p_lo = jnp.exp2(slot(h)[:half, :] - m).astype(BF16)
        p_hi = jnp.exp2(slot(h)[half:, half:] - m[:, half:]).astype(BF16)
        acc = _dot(vt_ref[0, i, col[h], :half], p_lo)
        acc_ref[h] = jnp.concatenate([acc[:, :half], acc[:, half:] + _dot(vt_ref[0, i, col[h], half:], p_hi)], axis=1)
        m_ref[h] = m

    chain(FOX_HEADS, diag_logits, diag_update)

    def full_tiles(j0, n_tiles):
        def logits(n):
            rows = pl.ds(pl.multiple_of((j0 + n // FOX_HEADS) * t, t), t)
            s = _dot(k_ref[0, rows, col[n % FOX_HEADS]], qt[n % FOX_HEADS])
            slot(n)[...] = s
            return jnp.max(s, axis=0, keepdims=True)

        def update(n, top):
            h = n % FOX_HEADS
            m = m_ref[h]
            m_new = jnp.maximum(m, top)
            p = jnp.exp2(slot(n)[...] - m_new).astype(BF16)
            acc_ref[h] = jnp.exp2(m - m_new) * acc_ref[h] + _dot(vt_ref[0, j0 + n // FOX_HEADS, col[h], :], p)
            m_ref[h] = m_new

        chain(n_tiles * FOX_HEADS, logits, update)

    lax.fori_loop(0, i // 2, lambda jj, c: full_tiles(2 * jj, 2), None)
    lax.fori_loop(2 * (i // 2), i, lambda j, c: full_tiles(j, 1), None)
    for pair in range(FOX_HEADS // 2):
        even, odd = acc_ref[2 * pair], acc_ref[2 * pair + 1]
        own = sub < FOX_HD
        out = jnp.where(own, even, odd) / jnp.where(own, even[FOX_HD:FOX_HD + 1, :], odd[0:1, :])
        o_ref[0, :, pair * PAIR_W:(pair + 1) * PAIR_W] = out.T.astype(BF16)


def _fox_prompt(qt, ka, vt):
    b, n_tiles, w, tile = qt.shape
    per_stream = lambda shape: pl.BlockSpec((1,) + shape, lambda i, j: (i,) + (0,) * len(shape))
    return pl.pallas_call(
        functools.partial(_fox_prompt_kernel, t=tile),
        grid=(b, n_tiles),
        in_specs=[pl.BlockSpec((1, 1, w, tile), lambda i, j: (i, j, 0, 0)),
                  per_stream((n_tiles * tile, w)), per_stream((n_tiles, w, tile))],
        out_specs=pl.BlockSpec((1, tile, MIX_W), lambda i, j: (i, j, 0)),
        out_shape=jax.ShapeDtypeStruct((b, n_tiles * tile, MIX_W), BF16),
        scratch_shapes=([pltpu.VMEM((FOX_HEADS, 1, tile), F32), pltpu.VMEM((FOX_HEADS, PAIR_W, tile), F32)]
                        + [pltpu.VMEM((tile, tile), F32)] * (FOX_AHEAD + 1)),
        compiler_params=pltpu.CompilerParams(dimension_semantics=("arbitrary", "arbitrary"),
                                             vmem_limit_bytes=VMEM_LIMIT),
        name="fox_prompt",
    )(qt, ka, vt)


def _fox_sample_kernel(q_ref, kc_ref, vc_ref, kn_ref, vn_ref, lfc_t_ref, c_ref, ct_ref, o_ref):
    tn = q_ref.shape[1]
    p = lfc_t_ref.shape[2]
    cc = _cumsum(lfc_t_ref[0], 1, p)
    ck_cache = cc - cc[:, p - 1:]
    causal = (lax.broadcasted_iota(jnp.int32, (tn, V7X_LANES), 1)
              <= lax.broadcasted_iota(jnp.int32, (tn, V7X_LANES), 0))
    pad = jnp.zeros((V7X_LANES - tn, FOX_HD), BF16)
    col = [slice(h * FOX_HD, (h + 1) * FOX_HD) for h in range(FOX_HEADS)]
    logits = []
    for h in range(FOX_HEADS):
        qh = q_ref[0, :, col[h]]
        cq = c_ref[0, :, h:h + 1]
        kn = jnp.concatenate([kn_ref[0, :, col[h]], pad], axis=0)
        s_old = _dot(qh, kc_ref[0, h].astype(BF16)) + cq - ck_cache[h:h + 1, :]
        s_new = jnp.where(causal, _dot_nt(qh, kn) + cq - ct_ref[0, h:h + 1, :], NEG_BIG)
        logits.append((s_old, s_new))
    probs = []
    for s_old, s_new in logits:
        m = jnp.maximum(jnp.max(s_old, axis=-1, keepdims=True), jnp.max(s_new, axis=-1, keepdims=True))
        p_old = jnp.exp(s_old - m)
        p_new = jnp.exp(s_new - m)
        l = jnp.sum(p_old, axis=-1, keepdims=True) + jnp.sum(p_new, axis=-1, keepdims=True)
        probs.append((p_old.astype(BF16), p_new.astype(BF16), l))
    outs = []
    for h, (p_old, p_new, l) in enumerate(probs):
        vn = jnp.concatenate([vn_ref[0, :, col[h]], pad], axis=0)
        outs.append((_dot_nt(p_old, vc_ref[0, h].astype(BF16)) + _dot(p_new, vn)) / l)
    o_ref[0] = jnp.concatenate(outs, axis=-1).astype(BF16)


def _fox_sample(q, cache_kt, cache_vt, kb, vb, cache_logf_t, c, ct):
    b, tn, _ = q.shape
    p = cache_kt.shape[3]
    new = lambda w: pl.BlockSpec((1, tn, w), lambda i: (i, 0, 0))
    old = pl.BlockSpec((1, FOX_HEADS, FOX_HD, p), lambda i: (i, 0, 0, 0))
    return pl.pallas_call(
        _fox_sample_kernel,
        grid=(b,),
        in_specs=[new(MIX_W), old, old, new(MIX_W), new(MIX_W),
                  pl.BlockSpec((1, FOX_HEADS, p), lambda i: (i, 0, 0)),
                  new(FOX_HEADS), pl.BlockSpec((1, FOX_HEADS, V7X_LANES), lambda i: (i, 0, 0))],
        out_specs=new(MIX_W),
        out_shape=jax.ShapeDtypeStruct((b, tn, MIX_W), BF16),
        compiler_params=pltpu.CompilerParams(dimension_semantics=("arbitrary",), vmem_limit_bytes=VMEM_LIMIT),
        name="fox_sample",
    )(q, cache_kt, cache_vt, kb, vb, cache_logf_t, c, ct)


def _layer1_back_kernel(x_ref, mix_ref, mem_ref, g_ref, w_out, w_up, w_down, *refs, nb, tt, side):
    n = len(side)
    y_ref = refs[n]
    m = nb * tt
    x = x_ref[...].reshape(m, D_MODEL)
    mix = mix_ref[...].reshape(m, MIX_W)
    mem = mem_ref[...].reshape(m, MEM_W)
    x1 = [x[rows] + _rms(_dot(mix[rows], w_out[:MIX_W, :]) + _dot(mem[rows], w_out[MIX_W:, :]),
                         _gain(g_ref, "mix_post", 1)) for rows in _row_parts(m)]
    y = jnp.concatenate(_mlp(x1, _gain(g_ref, "mlp_pre", 1), _gain(g_ref, "mlp_post", 1), w_up, w_down), axis=0)
    y_ref[...] = y.reshape(nb, tt, D_MODEL)
    _side_jobs(side, refs[:n], refs[n + 1:])


def _layer1_back(x, mix, mem, p, *, nb, tt, side=()):
    b, t, _ = x.shape
    grid = (b // nb, t // tt)
    tile = lambda w: pl.BlockSpec((nb, tt, w), lambda i, j: (i, j, 0))
    side_in, side_out, side_shape = _side_specs(side, grid[0] * grid[1], lambda i, j: i * grid[1] + j)
    return pl.pallas_call(
        functools.partial(_layer1_back_kernel, nb=nb, tt=tt, side=tuple(job[0] for job in side)),
        grid=grid,
        in_specs=[tile(D_MODEL), tile(MIX_W), tile(MEM_W),
                  _const_spec(p["gains"].shape), _layer_spec(p["w_out"][1], 0), _layer_spec(p["w_up"][1], 0), _layer_spec(p["w_down"][1], 0)]
                 + side_in,
        out_specs=[tile(D_MODEL)] + side_out,
        out_shape=[jax.ShapeDtypeStruct((b, t, D_MODEL), F32)] + side_shape,
        compiler_params=pltpu.CompilerParams(dimension_semantics=("arbitrary", "arbitrary"),
                                             vmem_limit_bytes=VMEM_LIMIT),
        name="layer1_back",
    )(x, mix, mem, p["gains"], p["w_out"][1], p["w_up"][1], p["w_down"][1],
      *[job[1] for job in side])


def _token_tile(b, t):
    if t >= TOKEN_TILE:
        assert t % TOKEN_TILE == 0
        return 1, TOKEN_TILE
    return b, t


def _trunk(x, pool_prev, cache, mem_k, mem_v, p, later_f32=None, to_head_major=()):
    b, t, _ = x.shape
    nb, tt = _token_tile(b, t)
    pos_base = 0 if cache is None else POOL_BUF
    names = sorted(later_f32) if later_f32 else []
    heads = [("heads", a.reshape(a.shape[:2] + (-1, a.shape[-1]))) for a in to_head_major]
    y0, pool_state, *done = _layer0(x, pool_prev, mem_k, mem_v, p, nb=nb, tt=tt, pos_base=pos_base,
                                    side=[("cast", later_f32[n], 1) for n in names] + heads[:1])
    p = {**p, **{n: [p[n][0], w] for n, w in zip(names, done)}}
    head_major = done[len(names):]
    k, v, logf_t, mem, *att = _layer1_front(y0, mem_k, mem_v, p, nb=nb, tt=tt, fold=cache is None)
    logf = jnp.transpose(logf_t.reshape(b // nb, FOX_HEADS, nb, t), (0, 2, 3, 1)).reshape(b, t, FOX_HEADS)
    if cache is None:
        mix = _fox_prompt(*att)
    else:
        q, kb, vb, c, ct = att
        cache_k, cache_v, cache_logf = cache
        ct_new = jnp.swapaxes(ct.reshape(FOX_HEADS, b, t), 0, 1)
        ct_new = jnp.pad(ct_new, ((0, 0), (0, 0), (0, V7X_LANES - t)))
        to_lanes = lambda a: jnp.transpose(a, (0, 2, 3, 1))
        mix = _fox_sample(q, to_lanes(cache_k), to_lanes(cache_v), kb, vb, jnp.swapaxes(cache_logf, 1, 2), c, ct_new)
    y, *done = _layer1_back(y0, mix, mem, p, nb=nb, tt=tt, side=heads[1:])
    return ((y, pool_state, k.reshape(b, t, FOX_HEADS, FOX_HD), v.reshape(b, t, FOX_HEADS, FOX_HD), logf), p,
            head_major + done)


def kernel(x_prompt, x_sample, cache_pool, cache_k, cache_v, cache_logf, cache_mem_k, cache_mem_v, mem_prompt,
           g_mix_pre, g_mix_post, g_mlp_pre, g_mlp_post, w_in, w_out, w_pool, pool_scale, g_kv, w_kvf, b_f,
           g_mem, w_mem_kv, w_up, w_down):
    depth = w_in.shape[0]
    assert depth == 2 and w_pool.shape[0] == 1, "one pooling layer followed by one forgetting-attention layer"
    bp = x_prompt.shape[0]
    w_f = w_kvf[:, 2 * MIX_W:]
    fold_cols = lambda a: jnp.pad(jnp.concatenate([a, jnp.repeat(a, FOLD_W, axis=-1)], axis=-1),
                                  [(0, 0)] * (a.ndim - 1) + [(0, V7X_LANES - FOLD_SRC - FOLD_W * FOX_HEADS)])
    rows = lambda g: g.reshape(g.shape[0], 1, g.shape[1])
    p = dict(
        gains=_pack_gains(g_mix_pre, g_mix_post, g_mlp_pre, g_mlp_post, g_kv), pool_scale=rows(pool_scale),
        w_pool=w_pool.astype(BF16), w_kv=w_kvf[:, :2 * MIX_W].astype(BF16),
        w_f=fold_cols(w_f).astype(BF16), w_ft=w_f.T.astype(BF16),
        b_f=fold_cols(b_f).reshape(1, V7X_LANES), b_ft=b_f.reshape(FOX_HEADS, 1),
    )
    later_f32 = dict(w_in=w_in, w_out=w_out, w_up=w_up, w_down=w_down)
    names = sorted(later_f32)
    mem_k_prompt, mem_v_prompt, mkh, mvh, *w0 = _mem_project(mem_prompt, g_mem, w_mem_kv,
                                                             side=[("cast", later_f32[n], 0) for n in names])
    p.update({n: [w, None] for n, w in zip(names, w0)})

    (y_p, pool_p, k_p, v_p, logf_p), p, (cmk, cmv) = _trunk(
        x_prompt, jnp.zeros((bp, HALO, MIX_W), F32), None, mkh, mvh, p, later_f32, (cache_mem_k, cache_mem_v))

    pool_prev = jnp.pad(cache_pool[0], ((0, 0), (HALO - POOL_BUF, 0), (0, 0)))
    (y_s, pool_s, k_s, v_s, logf_s), _, _ = _trunk(x_sample, pool_prev, (cache_k, cache_v, cache_logf), cmk, cmv, p)

    return (y_p, y_s, pool_p, pool_s, k_p, v_p, logf_p, k_s, v_s, logf_s, mem_k_prompt, mem_v_prompt)
```

```python
import functools

import jax
import jax.numpy as jnp
from jax import lax
from jax.experimental import pallas as pl
from jax.experimental.pallas import tpu as pltpu

F32 = jnp.float32
BF16 = jnp.bfloat16

D_MODEL = 1024
MIX_W = D_MODEL // 2
MEM_W = D_MODEL - MIX_W
POOL_WINDOWS = (2, 4, 8, 16)
POOL_GROUP = MIX_W // len(POOL_WINDOWS)
POOL_BUF = max(POOL_WINDOWS) - 1
FOX_HEADS = 8
FOX_HD = MIX_W // FOX_HEADS
MEM_HEADS = 4
MEM_HD = MEM_W // MEM_HEADS
D_FF = 4 * D_MODEL
EPS = 1e-6

V7X_LANES = 128
V7X_SUBLANES_F32 = 8
V7X_VMEM_BYTES = 64 * 1024 * 1024

HALO = 2 * V7X_SUBLANES_F32
FF_CHUNK = 1024
TOKEN_TILE = 512
ROW_PARTS = 2
PAIR_W = 2 * FOX_HD
NEG_BIG = -1e30
LOG2E = 1.4426950408889634
FOLD_SRC = FOX_HEADS
FOLD_W = 6
FOX_AHEAD = 2
assert FOLD_SRC + FOLD_W * FOX_HEADS <= V7X_LANES
VMEM_LIMIT = V7X_VMEM_BYTES - 8 * 1024 * 1024

assert HALO >= POOL_BUF and PAIR_W == V7X_LANES and POOL_GROUP == V7X_LANES and MEM_HD == V7X_LANES


def _const_spec(shape):
    zeros = (0,) * len(shape)
    return pl.BlockSpec(shape, lambda *_: zeros, pipeline_mode=pl.Buffered(1))


def _layer_spec(stacked, layer):
    index = (layer,) + (0,) * (stacked.ndim - 1)
    return pl.BlockSpec((None,) + stacked.shape[1:], lambda *_: index, pipeline_mode=pl.Buffered(1))


def _mem_spec(mem, layer, nb):
    return pl.BlockSpec((None, nb) + mem.shape[2:], lambda i, j: (layer, i, 0, 0, 0))


def _unit_rms(x):
    return x * lax.rsqrt(jnp.mean(x * x, axis=-1, keepdims=True) + EPS)


def _rms(x, g):
    return _unit_rms(x) * g


GAIN_KINDS = ("mix_pre", "mix_post", "mlp_pre", "mlp_post")


def _gain(g_ref, kind, layer=0):
    depth = (g_ref.shape[0] - 1) // len(GAIN_KINDS)
    r = len(GAIN_KINDS) * depth if kind == "kv" else GAIN_KINDS.index(kind) * depth + layer
    return g_ref[r:r + 1, :]


def _pack_gains(g_mix_pre, g_mix_post, g_mlp_pre, g_mlp_post, g_kv):
    return jnp.concatenate([g_mix_pre, g_mix_post, g_mlp_pre, g_mlp_post, g_kv[None, :]], axis=0)


def _dot(a, b):
    return jnp.dot(a, b, preferred_element_type=F32)


def _dot_nt(a, b):
    return lax.dot_general(a, b, (((1,), (1,)), ((), ())), preferred_element_type=F32)


def _log_sigmoid(x):
    return jnp.minimum(x, 0.0) - jnp.log1p(jnp.exp(-jnp.abs(x)))


def _cumsum(x, axis, seg):
    assert seg & (seg - 1) == 0 and x.shape[axis] % seg == 0
    n = x.shape[axis]
    vreg = (V7X_SUBLANES_F32, V7X_LANES)[axis]
    idx = lax.broadcasted_iota(jnp.int32, x.shape, axis) & (seg - 1)
    k = 1
    while k < seg:
        if seg == n and k % vreg == 0:
            zeros = jnp.zeros(x.shape[:axis] + (k,) + x.shape[axis + 1:], x.dtype)
            x = x + jnp.concatenate([zeros, lax.slice_in_dim(x, 0, n - k, axis=axis)], axis=axis)
        else:
            x = x + jnp.where(idx >= k, pltpu.roll(x, k, axis=axis), 0.0)
        k *= 2
    return x


def _mem_logits(q, mk_ref, b):
    qs = (q * (MEM_HD ** -0.5 * LOG2E)).astype(BF16)
    return [_dot_nt(qs[:, h * MEM_HD:(h + 1) * MEM_HD], mk_ref[b, h]) for h in range(MEM_HEADS)]


def _mem_probs(logits):
    probs = []
    for s in logits:
        p = jnp.exp2(s - jnp.max(s, axis=-1, keepdims=True))
        probs.append((p.astype(BF16), jnp.sum(p, axis=-1, keepdims=True)))
    return probs


def _mem_values(probs, mv_ref, b):
    return jnp.concatenate([_dot(p, mv_ref[b, h]) / l for h, (p, l) in enumerate(probs)], axis=-1)


def _pool_mixer(u, halo, pos0, w_pool_ref, scale):
    tt = u.shape[0]
    ext = jnp.concatenate([halo, u], axis=0)
    pos = pos0 + lax.broadcasted_iota(jnp.int32, (tt, 1), 0)
    outs = []
    for g, w in enumerate(POOL_WINDOWS):
        sl = slice(g * POOL_GROUP, (g + 1) * POOL_GROUP)
        s = ext[:, sl]
        k = 1
        while k < w:
            s = s + pltpu.roll(s, k, axis=0)
            k *= 2
        cnt = jnp.minimum(pos + 1, w).astype(F32)
        pooled = s[HALO:] / cnt - u[:, sl]
        outs.append(_dot(pooled.astype(BF16), w_pool_ref[g]))
    return jnp.concatenate(outs, axis=-1) * scale


def _row_parts(m):
    n = ROW_PARTS if m >= TOKEN_TILE else 1
    return [slice(i * m // n, (i + 1) * m // n) for i in range(n)]


def _mlp(x1_parts, g_pre, g_post, w_up_ref, w_down_ref):
    n_chunks = D_FF // FF_CHUNK
    chunk = lambda c: slice(c * FF_CHUNK, (c + 1) * FF_CHUNK)
    act = lambda up: jnp.square(jnp.maximum(up, 0.0)).astype(BF16)
    hm_parts = [_rms(x1, g_pre).astype(BF16) for x1 in x1_parts]
    hm = jnp.concatenate(hm_parts, axis=0)
    parts = _row_parts(hm.shape[0])
    acc = None
    for c in range(n_chunks):
        if c == 0:
            a = jnp.concatenate([act(_dot(h, w_up_ref[:, chunk(c)])) for h in hm_parts], axis=0)
        else:
            a = act(_dot(hm, w_up_ref[:, chunk(c)]))
        if c < n_chunks - 1:
            d = _dot(a, w_down_ref[chunk(c), :])
            acc = d if acc is None else acc + d
    return [x1 + _rms(acc[rows] + _dot(a[rows], w_down_ref[chunk(n_chunks - 1), :]), g_post)
            for x1, rows in zip(x1_parts, parts)]


def _side_specs(jobs, steps, step_of):
    in_specs, out_specs, out_shape = [], [], []
    for kind, a, *rest in jobs:
        if kind == "cast":
            rows = a.shape[1] // steps
            in_specs.append(pl.BlockSpec((1, rows, a.shape[2]), lambda *ids, l=rest[0]: (l, step_of(*ids), 0)))
            out_specs.append(pl.BlockSpec((1, rows, a.shape[2]), lambda *ids: (0, step_of(*ids), 0)))
            out_shape.append(jax.ShapeDtypeStruct((1,) + a.shape[1:], BF16))
        else:
            n_l, n_s, nh, d = a.shape
            h = MEM_HEADS
            n = nh // h
            rows = n_l * n_s * n // steps
            parts = n // rows
            assert kind == "heads" and rows * parts == n

            def where(*ids, parts=parts, n_s=n_s):
                step = step_of(*ids)
                return step // parts // n_s, step // parts % n_s, step % parts

            in_specs.append(pl.BlockSpec((1, 1, rows * h, d), lambda *ids, w=where: w(*ids) + (0,)))
            out_specs.append(pl.BlockSpec((1, 1, h, rows, d),
                                          lambda *ids, w=where: w(*ids)[:2] + (0, w(*ids)[2], 0)))
            out_shape.append(jax.ShapeDtypeStruct((n_l, n_s, h, n, d), BF16))
    return in_specs, out_specs, out_shape


def _side_jobs(kinds, src_refs, dst_refs):
    for kind, src, dst in zip(kinds, src_refs, dst_refs):
        if kind == "cast":
            dst[...] = src[...].astype(BF16)
        else:
            x = src[0, 0].astype(BF16)
            heads, rows = dst.shape[2], dst.shape[3]
            row = lax.broadcasted_iota(jnp.int32, (rows, rows * heads), 0)
            pick = lax.broadcasted_iota(jnp.int32, (rows, rows * heads), 1) - heads * row
            for h in range(heads):
                dst[0, 0, h] = _dot((pick == h).astype(BF16), x).astype(BF16)


def _mem_project_kernel(mem_ref, g_ref, w_ref, *refs, side):
    n = len(side)
    mk_ref, mv_ref, mkh_ref, mvh_ref = refs[n:n + 4]
    _side_jobs(side, refs[:n], refs[n + 4:])
    xn = _unit_rms(mem_ref[0])
    for l in range(w_ref.shape[0]):
        kv = _dot((xn * g_ref[l:l + 1, :]).astype(BF16), w_ref[l].astype(BF16))
        for h in range(MEM_HEADS):
            mk = kv[:, h * MEM_HD:(h + 1) * MEM_HD]
            mv = kv[:, MEM_W + h * MEM_HD:MEM_W + (h + 1) * MEM_HD]
            mk_ref[l, 0, :, h, :] = mk
            mv_ref[l, 0, :, h, :] = mv
            mkh_ref[l, 0, h] = mk.astype(BF16)
            mvh_ref[l, 0, h] = mv.astype(BF16)


def _mem_project(mem, g_mem, w_mem_kv, side=()):
    b, n, _ = mem.shape
    depth = w_mem_kv.shape[0]
    out = jax.ShapeDtypeStruct((depth, b, n, MEM_HEADS, MEM_HD), F32)
    out_h = jax.ShapeDtypeStruct((depth, b, MEM_HEADS, n, MEM_HD), BF16)
    side_in, side_out, side_shape = _side_specs(side, b, lambda i: i)
    return pl.pallas_call(
        functools.partial(_mem_project_kernel, side=tuple(job[0] for job in side)),
        grid=(b,),
        in_specs=[pl.BlockSpec((1, n, D_MODEL), lambda i: (i, 0, 0)),
                  _const_spec(g_mem.shape), _const_spec(w_mem_kv.shape)] + side_in,
        out_specs=([pl.BlockSpec((depth, 1, n, MEM_HEADS, MEM_HD), lambda i: (0, i, 0, 0, 0))] * 2
                   + [pl.BlockSpec((depth, 1, MEM_HEADS, n, MEM_HD), lambda i: (0, i, 0, 0, 0))] * 2 + side_out),
        out_shape=[out, out, out_h, out_h] + side_shape,
        compiler_params=pltpu.CompilerParams(dimension_semantics=("arbitrary",), vmem_limit_bytes=VMEM_LIMIT),
        name="mem_project",
    )(mem, g_mem, w_mem_kv, *[job[1] for job in side])


def _layer0_kernel(x_ref, prev_ref, mk_ref, mv_ref, g_ref, w_in, w_out, w_pool, pscale, w_up, w_down, *refs,
                   nb, tt, pos_base, side):
    n = len(side)
    side_in, (y_ref, state_ref), side_out = refs[:n], refs[n:n + 2], refs[n + 2:2 * n + 2]
    halo_ref, = refs[2 * n + 2:]
    t = pl.program_id(1)

    @pl.when(t == 0)
    def _():
        halo_ref[...] = prev_ref[...]

    x = x_ref[...].reshape(nb * tt, D_MODEL)
    proj = jnp.concatenate([_dot(_rms(x[rows], _gain(g_ref, "mix_pre")).astype(BF16), w_in[...])
                            for rows in _row_parts(nb * tt)], axis=0)
    cats = []
    for b in range(nb):
        rows = slice(b * tt, (b + 1) * tt)
        u = proj[rows, :MIX_W]
        logits = _mem_logits(proj[rows, MIX_W:], mk_ref, b)
        mix = _pool_mixer(u, halo_ref[b], pos_base + t * tt, w_pool, pscale[...])
        halo_ref[b] = u[tt - HALO:, :]
        state_ref[0, b] = u[tt - POOL_BUF:, :]
        mem = _mem_values(_mem_probs(logits), mv_ref, b)
        cats.append(jnp.concatenate([mix, mem], axis=-1))
    cat = (cats[0] if nb == 1 else jnp.concatenate(cats, axis=0)).astype(BF16)
    x1 = [x[rows] + _rms(_dot(cat[rows], w_out[...]), _gain(g_ref, "mix_post")) for rows in _row_parts(nb * tt)]
    y = jnp.concatenate(_mlp(x1, _gain(g_ref, "mlp_pre"), _gain(g_ref, "mlp_post"), w_up, w_down), axis=0)
    y_ref[...] = y.reshape(nb, tt, D_MODEL)
    _side_jobs(side, side_in, side_out)


def _layer0(x, prev, mem_k, mem_v, p, *, nb, tt, pos_base, side=()):
    b, t, _ = x.shape
    grid = (b // nb, t // tt)
    tile = pl.BlockSpec((nb, tt, D_MODEL), lambda i, j: (i, j, 0))
    side_in, side_out, side_shape = _side_specs(side, grid[0] * grid[1], lambda i, j: i * grid[1] + j)
    return pl.pallas_call(
        functools.partial(_layer0_kernel, nb=nb, tt=tt, pos_base=pos_base, side=tuple(job[0] for job in side)),
        grid=grid,
        in_specs=[tile, pl.BlockSpec((nb, HALO, MIX_W), lambda i, j: (i, 0, 0)),
                  _mem_spec(mem_k, 0, nb), _mem_spec(mem_v, 0, nb),
                  _const_spec(p["gains"].shape), _layer_spec(p["w_in"][0], 0), _layer_spec(p["w_out"][0], 0),
                  _layer_spec(p["w_pool"], 0), _layer_spec(p["pool_scale"], 0),
                  _layer_spec(p["w_up"][0], 0), _layer_spec(p["w_down"][0], 0)] + side_in,
        out_specs=[tile, pl.BlockSpec((1, nb, POOL_BUF, MIX_W), lambda i, j: (0, i, 0, 0))] + side_out,
        out_shape=[jax.ShapeDtypeStruct((b, t, D_MODEL), F32),
                   jax.ShapeDtypeStruct((1, b, POOL_BUF, MIX_W), F32)] + side_shape,
        scratch_shapes=[pltpu.VMEM((nb, HALO, MIX_W), F32)],
        compiler_params=pltpu.CompilerParams(dimension_semantics=("arbitrary", "arbitrary"),
                                             vmem_limit_bytes=VMEM_LIMIT),
        name="layer0",
    )(x, prev, mem_k, mem_v, p["gains"], p["w_in"][0], p["w_out"][0], p["w_pool"], p["pool_scale"], p["w_up"][0], p["w_down"][0],
      *[job[1] for job in side])


def _bf16_part(x):
    return x.astype(BF16).astype(F32)


def _fold_lanes(h):
    lane = lax.broadcasted_iota(jnp.int32, (1, V7X_LANES), 1)
    own = (h % 2) * FOX_HD
    e0 = FOX_HD - own
    in_head = (lane >= own) & (lane < own + FOX_HD)
    first = ((lane >= e0) & (lane < e0 + 3)).astype(F32)
    second = ((lane >= e0 + 3) & (lane < e0 + 6)).astype(F32)
    return in_head, first, second, lane == e0, e0


def _fold_bias(c):
    cl = c * LOG2E
    hi = _bf16_part(cl)
    mid = _bf16_part(cl - hi)
    lo = _bf16_part(cl - hi - mid)
    lane = lax.broadcasted_iota(jnp.int32, (1, V7X_LANES), 1)
    piece = lax.rem(lane + (3 - FOLD_SRC % 3), 3)
    d = jnp.where(piece == 0, hi, jnp.where(piece == 1, mid, lo))
    return [pltpu.roll(d, (_fold_lanes(h)[4] - (FOLD_SRC + FOLD_W * h)) % V7X_LANES, axis=1)
            for h in range(FOX_HEADS)]


def _fold_kv(k, v, bias):
    ks, vs = [], []
    for h, dh in enumerate(bias):
        in_head, first, second, one, _ = _fold_lanes(h)
        cols = slice((h // 2) * PAIR_W, (h // 2 + 1) * PAIR_W)
        ks.append(jnp.where(in_head, k[:, cols], first - dh * second).astype(BF16))
        vs.append(jnp.where(in_head, v[:, cols], one.astype(F32)).T.astype(BF16))
    return jnp.concatenate(ks, axis=-1), jnp.concatenate(vs, axis=0)


def _fold_q(q, bias):
    qs = []
    for h, dh in enumerate(bias):
        in_head, first, second, _, _ = _fold_lanes(h)
        cols = slice((h // 2) * PAIR_W, (h // 2 + 1) * PAIR_W)
        qs.append(jnp.where(in_head, q[:, cols] * (FOX_HD ** -0.5 * LOG2E), dh * first + second).T.astype(BF16))
    return jnp.concatenate(qs, axis=0)


def _layer1_front_kernel(x_ref, mk_ref, mv_ref, g_ref, w_kv, w_f, w_ft, b_f, b_ft, w_in, *refs,
                         nb, tt, fold):
    if fold:
        k_ref, v_ref, logft_ref, mem_ref, qa_ref, ka_ref, va_ref, carry_ref = refs
    else:
        k_ref, v_ref, logft_ref, mem_ref, q_ref, kb_ref, vb_ref, c_ref, ct_ref = refs
    if fold:
        @pl.when(pl.program_id(1) == 0)
        def _():
            carry_ref[...] = jnp.zeros_like(carry_ref)

    xn = _unit_rms(x_ref[...].reshape(nb * tt, D_MODEL))
    hk = (xn * _gain(g_ref, "kv")).astype(BF16)
    logf = _log_sigmoid(_dot(hk, w_f[...]) + b_f[...])
    k = _dot(hk, w_kv[:, :MIX_W])
    v = _dot(hk, w_kv[:, MIX_W:])
    k_ref[...] = k.reshape(nb, tt, MIX_W)
    v_ref[...] = v.reshape(nb, tt, MIX_W)
    c = _cumsum(logf, 0, tt)
    if fold:
        c = c + carry_ref[...]
        carry_ref[...] = c[tt - 1:, :]
        bias = _fold_bias(c)
        ka_ref[0], va_ref[0, 0] = _fold_kv(k, v, bias)
        logft_ref[0] = logf.T[:FOX_HEADS, :]
    proj = _dot((xn * _gain(g_ref, "mix_pre", 1)).astype(BF16), w_in[...])
    q = proj[:, :MIX_W]
    mem_logits = [_mem_logits(proj[b * tt:(b + 1) * tt, MIX_W:], mk_ref, b) for b in range(nb)]
    if fold:
        qa_ref[0, 0] = _fold_q(q, bias)
    else:
        q_ref[...] = (q * (FOX_HD ** -0.5)).astype(BF16).reshape(nb, tt, MIX_W)
        kb_ref[...] = k.astype(BF16).reshape(nb, tt, MIX_W)
        vb_ref[...] = v.astype(BF16).reshape(nb, tt, MIX_W)
        logf_t = _log_sigmoid(_dot_nt(w_ft[...], hk) + b_ft[...])
        logft_ref[0] = logf_t
        ct_ref[0, 0] = _cumsum(logf_t, 1, tt)
    for b in range(nb):
        rows = slice(b * tt, (b + 1) * tt)
        if not fold:
            c_ref[b] = c[rows, :FOX_HEADS]
        mem_ref[b] = _mem_values(_mem_probs(mem_logits[b]), mv_ref, b).astype(BF16)


def _layer1_front(x, mem_k, mem_v, p, *, nb, tt, fold):
    b, t, _ = x.shape
    assert (nb == 1) if fold else (t == tt)
    tile = lambda w: pl.BlockSpec((nb, tt, w), lambda i, j: (i, j, 0))
    act = lambda w, dt: jax.ShapeDtypeStruct((b, t, w), dt)
    out_specs = [tile(MIX_W), tile(MIX_W), pl.BlockSpec((1, FOX_HEADS, nb * tt), lambda i, j: (i, 0, j)), tile(MEM_W)]
    out_shape = [act(MIX_W, F32), act(MIX_W, F32), jax.ShapeDtypeStruct((b // nb, FOX_HEADS, nb * t), F32),
                 act(MEM_W, BF16)]
    if fold:
        wf = FOX_HEADS * PAIR_W
        tile_t = pl.BlockSpec((1, 1, wf, tt), lambda i, j: (i, j, 0, 0))
        act_t = jax.ShapeDtypeStruct((b, t // tt, wf, tt), BF16)
        out_specs += [tile_t, tile(wf), tile_t]
        out_shape += [act_t, act(wf, BF16), act_t]
        scratch = [pltpu.VMEM((1, V7X_LANES), F32)]
    else:
        out_specs += [tile(MIX_W)] * 3 + [tile(FOX_HEADS),
                                          pl.BlockSpec((1, 1, FOX_HEADS, nb * tt), lambda i, j: (i, j, 0, 0))]
        out_shape += [act(MIX_W, BF16)] * 3 + [act(FOX_HEADS, F32),
                                               jax.ShapeDtypeStruct((b // nb, t // tt, FOX_HEADS, nb * tt), F32)]
        scratch = []
    return pl.pallas_call(
        functools.partial(_layer1_front_kernel, nb=nb, tt=tt, fold=fold),
        grid=(b // nb, t // tt),
        in_specs=[tile(D_MODEL), _mem_spec(mem_k, 1, nb), _mem_spec(mem_v, 1, nb),
                  _const_spec(p["gains"].shape),
                  _const_spec(p["w_kv"].shape), _const_spec(p["w_f"].shape), _const_spec(p["w_ft"].shape),
                  _const_spec(p["b_f"].shape), _const_spec(p["b_ft"].shape), _layer_spec(p["w_in"][1], 0)],
        out_specs=out_specs,
        out_shape=out_shape,
        scratch_shapes=scratch,
        compiler_params=pltpu.CompilerParams(dimension_semantics=("arbitrary", "arbitrary"),
                                             vmem_limit_bytes=VMEM_LIMIT),
        name="layer1_front",
    )(x, mem_k, mem_v, p["gains"], p["w_kv"], p["w_f"], p["w_ft"], p["b_f"], p["b_ft"], p["w_in"][1])


def _fox_prompt_kernel(qt_ref, k_ref, vt_ref, o_ref, m_ref, acc_ref, *s_refs, t):
    i = pl.program_id(1)
    half = t // 2
    sub = lax.broadcasted_iota(jnp.int32, (PAIR_W, 1), 0)
    col = [slice(h * PAIR_W, (h + 1) * PAIR_W) for h in range(FOX_HEADS)]
    qt = [qt_ref[0, 0, c, :] for c in col]
    slot = lambda n: s_refs[n % len(s_refs)]

    def chain(n_items, logits, update):
        top = [logits(n) for n in range(FOX_AHEAD)]
        for n in range(n_items):
            if n + FOX_AHEAD < n_items:
                top.append(logits(n + FOX_AHEAD))
            update(n, top[n])

    tri = (lax.broadcasted_iota(jnp.int32, (half, half), 0) <= lax.broadcasted_iota(jnp.int32, (half, half), 1))
    lo = pl.ds(pl.multiple_of(i * t, t), half)
    hi = pl.ds(pl.multiple_of(i * t + half, half), half)

    def diag_logits(h):
        s_lo = _dot(k_ref[0, lo, col[h]], qt[h])
        s_hi = _dot(k_ref[0, hi, col[h]], qt[h][:, half:])
        s_lo = jnp.concatenate([jnp.where(tri, s_lo[:, :half], NEG_BIG), s_lo[:, half:]], axis=1)
        s_hi = jnp.where(tri, s_hi, NEG_BIG)
        slot(h)[:half, :] = s_lo
        slot(h)[half:, half:] = s_hi
        top = jnp.max(s_lo, axis=0, keepdims=True)
        return jnp.concatenate([top[:, :half], jnp.maximum(top[:, half:], jnp.max(s_hi, axis=0, keepdims=True))],
                               axis=1)

    def diag_update(h, m):
        p_lo = jnp.exp2(slot(h)[:half, :] - m).astype(BF16)
        p_hi = jnp.exp2(slot(h)[half:, half:] - m[:, half:]).astype(BF16)
        acc = _dot(vt_ref[0, i, col[h], :half], p_lo)
        acc_ref[h] = jnp.concatenate([acc[:, :half], acc[:, half:] + _dot(vt_ref[0, i, col[h], half:], p_hi)], axis=1)
        m_ref[h] = m

    chain(FOX_HEADS, diag_logits, diag_update)

    def full_tiles(j0, n_tiles):
        def logits(n):
            rows = pl.ds(pl.multiple_of((j0 + n // FOX_HEADS) * t, t), t)
            s = _dot(k_ref[0, rows, col[n % FOX_HEADS]], qt[n % FOX_HEADS])
            slot(n)[...] = s
            return jnp.max(s, axis=0, keepdims=True)

        def update(n, top):
            h = n % FOX_HEADS
            m = m_ref[h]
            m_new = jnp.maximum(m, top)
            p = jnp.exp2(slot(n)[...] - m_new).astype(BF16)
            acc_ref[h] = jnp.exp2(m - m_new) * acc_ref[h] + _dot(vt_ref[0, j0 + n // FOX_HEADS, col[h], :], p)
            m_ref[h] = m_new

        chain(n_tiles * FOX_HEADS, logits, update)

    lax.fori_loop(0, i // 2, lambda jj, c: full_tiles(2 * jj, 2), None)
    lax.fori_loop(2 * (i // 2), i, lambda j, c: full_tiles(j, 1), None)
    for pair in range(FOX_HEADS // 2):
        even, odd = acc_ref[2 * pair], acc_ref[2 * pair + 1]
        own = sub < FOX_HD
        out = jnp.where(own, even, odd) / jnp.where(own, even[FOX_HD:FOX_HD + 1, :], odd[0:1, :])
        o_ref[0, :, pair * PAIR_W:(pair + 1) * PAIR_W] = out.T.astype(BF16)


def _fox_prompt(qt, ka, vt):
    b, n_tiles, w, tile = qt.shape
    per_stream = lambda shape: pl.BlockSpec((1,) + shape, lambda i, j: (i,) + (0,) * len(shape))
    return pl.pallas_call(
        functools.partial(_fox_prompt_kernel, t=tile),
        grid=(b, n_tiles),
        in_specs=[pl.BlockSpec((1, 1, w, tile), lambda i, j: (i, j, 0, 0)),
                  per_stream((n_tiles * tile, w)), per_stream((n_tiles, w, tile))],
        out_specs=pl.BlockSpec((1, tile, MIX_W), lambda i, j: (i, j, 0)),
        out_shape=jax.ShapeDtypeStruct((b, n_tiles * tile, MIX_W), BF16),
        scratch_shapes=([pltpu.VMEM((FOX_HEADS, 1, tile), F32), pltpu.VMEM((FOX_HEADS, PAIR_W, tile), F32)]
                        + [pltpu.VMEM((tile, tile), F32)] * (FOX_AHEAD + 1)),
        compiler_params=pltpu.CompilerParams(dimension_semantics=("arbitrary", "arbitrary"),
                                             vmem_limit_bytes=VMEM_LIMIT),
        name="fox_prompt",
    )(qt, ka, vt)


def _fox_sample_kernel(q_ref, kc_ref, vc_ref, kn_ref, vn_ref, lfc_t_ref, c_ref, ct_ref, o_ref):
    tn = q_ref.shape[1]
    p = lfc_t_ref.shape[2]
    cc = _cumsum(lfc_t_ref[0], 1, p)
    ck_cache = cc - cc[:, p - 1:]
    causal = (lax.broadcasted_iota(jnp.int32, (tn, V7X_LANES), 1)
              <= lax.broadcasted_iota(jnp.int32, (tn, V7X_LANES), 0))
    pad = jnp.zeros((V7X_LANES - tn, FOX_HD), BF16)
    col = [slice(h * FOX_HD, (h + 1) * FOX_HD) for h in range(FOX_HEADS)]
    logits = []
    for h in range(FOX_HEADS):
        qh = q_ref[0, :, col[h]]
        cq = c_ref[0, :, h:h + 1]
        kn = jnp.concatenate([kn_ref[0, :, col[h]], pad], axis=0)
        s_old = _dot(qh, kc_ref[0, h].astype(BF16)) + cq - ck_cache[h:h + 1, :]
        s_new = jnp.where(causal, _dot_nt(qh, kn) + cq - ct_ref[0, h:h + 1, :], NEG_BIG)
        logits.append((s_old, s_new))
    probs = []
    for s_old, s_new in logits:
        m = jnp.maximum(jnp.max(s_old, axis=-1, keepdims=True), jnp.max(s_new, axis=-1, keepdims=True))
        p_old = jnp.exp(s_old - m)
        p_new = jnp.exp(s_new - m)
        l = jnp.sum(p_old, axis=-1, keepdims=True) + jnp.sum(p_new, axis=-1, keepdims=True)
        probs.append((p_old.astype(BF16), p_new.astype(BF16), l))
    outs = []
    for h, (p_old, p_new, l) in enumerate(probs):
        vn = jnp.concatenate([vn_ref[0, :, col[h]], pad], axis=0)
        outs.append((_dot_nt(p_old, vc_ref[0, h].astype(BF16)) + _dot(p_new, vn)) / l)
    o_ref[0] = jnp.concatenate(outs, axis=-1).astype(BF16)


def _fox_sample(q, cache_kt, cache_vt, kb, vb, cache_logf_t, c, ct):
    b, tn, _ = q.shape
    p = cache_kt.shape[3]
    new = lambda w: pl.BlockSpec((1, tn, w), lambda i: (i, 0, 0))
    old = pl.BlockSpec((1, FOX_HEADS, FOX_HD, p), lambda i: (i, 0, 0, 0))
    return pl.pallas_call(
        _fox_sample_kernel,
        grid=(b,),
        in_specs=[new(MIX_W), old, old, new(MIX_W), new(MIX_W),
                  pl.BlockSpec((1, FOX_HEADS, p), lambda i: (i, 0, 0)),
                  new(FOX_HEADS), pl.BlockSpec((1, FOX_HEADS, V7X_LANES), lambda i: (i, 0, 0))],
        out_specs=new(MIX_W),
        out_shape=jax.ShapeDtypeStruct((b, tn, MIX_W), BF16),
        compiler_params=pltpu.CompilerParams(dimension_semantics=("arbitrary",), vmem_limit_bytes=VMEM_LIMIT),
        name="fox_sample",
    )(q, cache_kt, cache_vt, kb, vb, cache_logf_t, c, ct)


def _layer1_back_kernel(x_ref, mix_ref, mem_ref, g_ref, w_out, w_up, w_down, *refs, nb, tt, side):
    n = len(side)
    y_ref = refs[n]
    m = nb * tt
    x = x_ref[...].reshape(m, D_MODEL)
    mix = mix_ref[...].reshape(m, MIX_W)
    mem = mem_ref[...].reshape(m, MEM_W)
    x1 = [x[rows] + _rms(_dot(mix[rows], w_out[:MIX_W, :]) + _dot(mem[rows], w_out[MIX_W:, :]),
                         _gain(g_ref, "mix_post", 1)) for rows in _row_parts(m)]
    y = jnp.concatenate(_mlp(x1, _gain(g_ref, "mlp_pre", 1), _gain(g_ref, "mlp_post", 1), w_up, w_down), axis=0)
    y_ref[...] = y.reshape(nb, tt, D_MODEL)
    _side_jobs(side, refs[:n], refs[n + 1:])


def _layer1_back(x, mix, mem, p, *, nb, tt, side=()):
    b, t, _ = x.shape
    grid = (b // nb, t // tt)
    tile = lambda w: pl.BlockSpec((nb, tt, w), lambda i, j: (i, j, 0))
    side_in, side_out, side_shape = _side_specs(side, grid[0] * grid[1], lambda i, j: i * grid[1] + j)
    return pl.pallas_call(
        functools.partial(_layer1_back_kernel, nb=nb, tt=tt, side=tuple(job[0] for job in side)),
        grid=grid,
        in_specs=[tile(D_MODEL), tile(MIX_W), tile(MEM_W),
                  _const_spec(p["gains"].shape), _layer_spec(p["w_out"][1], 0), _layer_spec(p["w_up"][1], 0), _layer_spec(p["w_down"][1], 0)]
                 + side_in,
        out_specs=[tile(D_MODEL)] + side_out,
        out_shape=[jax.ShapeDtypeStruct((b, t, D_MODEL), F32)] + side_shape,
        compiler_params=pltpu.CompilerParams(dimension_semantics=("arbitrary", "arbitrary"),
                                             vmem_limit_bytes=VMEM_LIMIT),
        name="layer1_back",
    )(x, mix, mem, p["gains"], p["w_out"][1], p["w_up"][1], p["w_down"][1],
      *[job[1] for job in side])


def _token_tile(b, t):
    if t >= TOKEN_TILE:
        assert t % TOKEN_TILE == 0
        return 1, TOKEN_TILE
    return b, t


def _trunk(x, pool_prev, cache, mem_k, mem_v, p, later_f32=None, side=((), ())):
    b, t, _ = x.shape
    nb, tt = _token_tile(b, t)
    pos_base = 0 if cache is None else POOL_BUF
    names = sorted(later_f32) if later_f32 else []
    y0, pool_state, *done = _layer0(x, pool_prev, mem_k, mem_v, p, nb=nb, tt=tt, pos_base=pos_base,
                                    side=[("cast", later_f32[n], 1) for n in names] + list(side[0]))
    p = {**p, **{n: [p[n][0], w] for n, w in zip(names, done)}}
    done_first = done[len(names):]
    k, v, logf_t, mem, *att = _layer1_front(y0, mem_k, mem_v, p, nb=nb, tt=tt, fold=cache is None)
    logf = jnp.transpose(logf_t.reshape(b // nb, FOX_HEADS, nb, t), (0, 2, 3, 1)).reshape(b, t, FOX_HEADS)
    if cache is None:
        mix = _fox_prompt(*att)
    else:
        q, kb, vb, c, ct = att
        ct_new = jnp.swapaxes(ct.reshape(FOX_HEADS, b, t), 0, 1)
        ct_new = jnp.pad(ct_new, ((0, 0), (0, 0), (0, V7X_LANES - t)))
        mix = _fox_sample(q, *cache[:2], kb, vb, cache[2], c, ct_new)
    y, *done_last = _layer1_back(y0, mix, mem, p, nb=nb, tt=tt, side=side[1])
    return ((y, pool_state, k.reshape(b, t, FOX_HEADS, FOX_HD), v.reshape(b, t, FOX_HEADS, FOX_HD), logf), p,
            done_first, done_last)


def kernel(x_prompt, x_sample, cache_pool, cache_k, cache_v, cache_logf, cache_mem_k, cache_mem_v, mem_prompt,
           g_mix_pre, g_mix_post, g_mlp_pre, g_mlp_post, w_in, w_out, w_pool, pool_scale, g_kv, w_kvf, b_f,
           g_mem, w_mem_kv, w_up, w_down):
    depth = w_in.shape[0]
    assert depth == 2 and w_pool.shape[0] == 1, "one pooling layer followed by one forgetting-attention layer"
    bp = x_prompt.shape[0]
    w_f = w_kvf[:, 2 * MIX_W:]
    fold_cols = lambda a: jnp.pad(jnp.concatenate([a, jnp.repeat(a, FOLD_W, axis=-1)], axis=-1),
                                  [(0, 0)] * (a.ndim - 1) + [(0, V7X_LANES - FOLD_SRC - FOLD_W * FOX_HEADS)])
    rows = lambda g: g.reshape(g.shape[0], 1, g.shape[1])
    p = dict(
        gains=_pack_gains(g_mix_pre, g_mix_post, g_mlp_pre, g_mlp_post, g_kv), pool_scale=rows(pool_scale),
        w_pool=w_pool.astype(BF16), w_kv=w_kvf[:, :2 * MIX_W].astype(BF16),
        w_f=fold_cols(w_f).astype(BF16), w_ft=w_f.T.astype(BF16),
        b_f=fold_cols(b_f).reshape(1, V7X_LANES), b_ft=b_f.reshape(FOX_HEADS, 1),
    )
    later_f32 = dict(w_in=w_in, w_out=w_out, w_up=w_up, w_down=w_down)
    names = sorted(later_f32)
    mem_k_prompt, mem_v_prompt, mkh, mvh, *w0 = _mem_project(mem_prompt, g_mem, w_mem_kv,
                                                             side=[("cast", later_f32[n], 0) for n in names])
    p.update({n: [w, None] for n, w in zip(names, w0)})

    heads_view = lambda a: a.reshape(a.shape[:2] + (-1, a.shape[-1]))
    (y_p, pool_p, k_p, v_p, logf_p), p, (cmk,), (cmv,) = _trunk(
        x_prompt, jnp.zeros((bp, HALO, MIX_W), F32), None, mkh, mvh, p, later_f32,
        side=([("heads", heads_view(cache_mem_k))], [("heads", heads_view(cache_mem_v))]))

    pool_prev = jnp.pad(cache_pool[0], ((0, 0), (HALO - POOL_BUF, 0), (0, 0)))
    lanes_view = lambda a: jnp.transpose(a, (0, 2, 3, 1))
    cache = (lanes_view(cache_k), lanes_view(cache_v), jnp.swapaxes(cache_logf, 1, 2))
    (y_s, pool_s, k_s, v_s, logf_s), _, _, _ = _trunk(x_sample, pool_prev, cache, cmk, cmv, p)

    return (y_p, y_s, pool_p, pool_s, k_p, v_p, logf_p, k_s, v_s, logf_s, mem_k_prompt, mem_v_prompt)
```

```python
import functools

import jax
import jax.numpy as jnp
from jax import lax
from jax.experimental import pallas as pl
from jax.experimental.pallas import tpu as pltpu

F32 = jnp.float32
BF16 = jnp.bfloat16

D_MODEL = 1024
MIX_W = D_MODEL // 2
MEM_W = D_MODEL - MIX_W
POOL_WINDOWS = (2, 4, 8, 16)
POOL_GROUP = MIX_W // len(POOL_WINDOWS)
POOL_BUF = max(POOL_WINDOWS) - 1
FOX_HEADS = 8
FOX_HD = MIX_W // FOX_HEADS
MEM_HEADS = 4
MEM_HD = MEM_W // MEM_HEADS
D_FF = 4 * D_MODEL
EPS = 1e-6

V7X_LANES = 128
V7X_SUBLANES_F32 = 8
V7X_VMEM_BYTES = 64 * 1024 * 1024

HALO = 2 * V7X_SUBLANES_F32
FF_CHUNK = 1024
TOKEN_TILE = 512
ROW_PARTS = 2
PAIR_W = 2 * FOX_HD
NEG_BIG = -1e30
LOG2E = 1.4426950408889634
FOLD_SRC = FOX_HEADS
FOLD_W = 6
FOX_AHEAD = 2
assert FOLD_SRC + FOLD_W * FOX_HEADS <= V7X_LANES
VMEM_LIMIT = V7X_VMEM_BYTES - 8 * 1024 * 1024

assert HALO >= POOL_BUF and PAIR_W == V7X_LANES and POOL_GROUP == V7X_LANES and MEM_HD == V7X_LANES


def _const_spec(shape):
    zeros = (0,) * len(shape)
    return pl.BlockSpec(shape, lambda *_: zeros, pipeline_mode=pl.Buffered(1))


def _layer_spec(stacked, layer):
    index = (layer,) + (0,) * (stacked.ndim - 1)
    return pl.BlockSpec((None,) + stacked.shape[1:], lambda *_: index, pipeline_mode=pl.Buffered(1))


def _mem_spec(mem, layer, nb):
    return pl.BlockSpec((None, nb) + mem.shape[2:], lambda i, j: (layer, i, 0, 0, 0))


def _unit_rms(x):
    return x * lax.rsqrt(jnp.mean(x * x, axis=-1, keepdims=True) + EPS)


def _rms(x, g):
    return _unit_rms(x) * g


GAIN_KINDS = ("mix_pre", "mix_post", "mlp_pre", "mlp_post")


def _gain(g_ref, kind, layer=0):
    depth = (g_ref.shape[0] - 1) // len(GAIN_KINDS)
    r = len(GAIN_KINDS) * depth if kind == "kv" else GAIN_KINDS.index(kind) * depth + layer
    return g_ref[r:r + 1, :]


def _pack_gains(g_mix_pre, g_mix_post, g_mlp_pre, g_mlp_post, g_kv):
    return jnp.concatenate([g_mix_pre, g_mix_post, g_mlp_pre, g_mlp_post, g_kv[None, :]], axis=0)


def _dot(a, b):
    return jnp.dot(a, b, preferred_element_type=F32)


def _dot_nt(a, b):
    return lax.dot_general(a, b, (((1,), (1,)), ((), ())), preferred_element_type=F32)


def _log_sigmoid(x):
    return jnp.minimum(x, 0.0) - jnp.log1p(jnp.exp(-jnp.abs(x)))


def _cumsum(x, axis, seg):
    assert seg & (seg - 1) == 0 and x.shape[axis] % seg == 0
    n = x.shape[axis]
    vreg = (V7X_SUBLANES_F32, V7X_LANES)[axis]
    idx = lax.broadcasted_iota(jnp.int32, x.shape, axis) & (seg - 1)
    k = 1
    while k < seg:
        if seg == n and k % vreg == 0:
            zeros = jnp.zeros(x.shape[:axis] + (k,) + x.shape[axis + 1:], x.dtype)
            x = x + jnp.concatenate([zeros, lax.slice_in_dim(x, 0, n - k, axis=axis)], axis=axis)
        else:
            x = x + jnp.where(idx >= k, pltpu.roll(x, k, axis=axis), 0.0)
        k *= 2
    return x


def _mem_logits(q, mk_ref, b):
    qs = (q * (MEM_HD ** -0.5 * LOG2E)).astype(BF16)
    return [_dot_nt(qs[:, h * MEM_HD:(h + 1) * MEM_HD], mk_ref[b, h]) for h in range(MEM_HEADS)]


def _mem_probs(logits):
    probs = []
    for s in logits:
        p = jnp.exp2(s - jnp.max(s, axis=-1, keepdims=True))
        probs.append((p.astype(BF16), jnp.sum(p, axis=-1, keepdims=True)))
    return probs


def _mem_values(probs, mv_ref, b):
    return jnp.concatenate([_dot(p, mv_ref[b, h]) / l for h, (p, l) in enumerate(probs)], axis=-1)


def _pool_mixer(u, halo, pos0, w_pool_ref, scale):
    tt = u.shape[0]
    ext = jnp.concatenate([halo, u], axis=0)
    pos = pos0 + lax.broadcasted_iota(jnp.int32, (tt, 1), 0)
    outs = []
    for g, w in enumerate(POOL_WINDOWS):
        sl = slice(g * POOL_GROUP, (g + 1) * POOL_GROUP)
        s = ext[:, sl]
        k = 1
        while k < w:
            s = s + pltpu.roll(s, k, axis=0)
            k *= 2
        cnt = jnp.minimum(pos + 1, w).astype(F32)
        pooled = s[HALO:] / cnt - u[:, sl]
        outs.append(_dot(pooled.astype(BF16), w_pool_ref[g]))
    return jnp.concatenate(outs, axis=-1) * scale


def _row_parts(m):
    n = ROW_PARTS if m >= TOKEN_TILE else 1
    return [slice(i * m // n, (i + 1) * m // n) for i in range(n)]


def _mlp(x1_parts, g_pre, g_post, w_up_ref, w_down_ref):
    n_chunks = D_FF // FF_CHUNK
    chunk = lambda c: slice(c * FF_CHUNK, (c + 1) * FF_CHUNK)
    act = lambda up: jnp.square(jnp.maximum(up, 0.0)).astype(BF16)
    hm_parts = [_rms(x1, g_pre).astype(BF16) for x1 in x1_parts]
    hm = jnp.concatenate(hm_parts, axis=0)
    parts = _row_parts(hm.shape[0])
    acc = None
    for c in range(n_chunks):
        if c == 0:
            a = jnp.concatenate([act(_dot(h, w_up_ref[:, chunk(c)])) for h in hm_parts], axis=0)
        else:
            a = act(_dot(hm, w_up_ref[:, chunk(c)]))
        if c < n_chunks - 1:
            d = _dot(a, w_down_ref[chunk(c), :])
            acc = d if acc is None else acc + d
    return [x1 + _rms(acc[rows] + _dot(a[rows], w_down_ref[chunk(n_chunks - 1), :]), g_post)
            for x1, rows in zip(x1_parts, parts)]


def _side_specs(jobs, steps, step_of):
    in_specs, out_specs, out_shape = [], [], []
    for kind, a, *rest in jobs:
        if kind == "cast":
            rows = a.shape[1] // steps
            in_specs.append(pl.BlockSpec((1, rows, a.shape[2]), lambda *ids, l=rest[0]: (l, step_of(*ids), 0)))
            out_specs.append(pl.BlockSpec((1, rows, a.shape[2]), lambda *ids: (0, step_of(*ids), 0)))
            out_shape.append(jax.ShapeDtypeStruct((1,) + a.shape[1:], BF16))
        else:
            n_l, n_s, nh, d = a.shape
            h = MEM_HEADS
            n = nh // h
            rows = n_l * n_s * n // steps
            parts = n // rows
            assert kind == "heads" and rows * parts == n

            def where(*ids, parts=parts, n_s=n_s):
                step = step_of(*ids)
                return step // parts // n_s, step // parts % n_s, step % parts

            in_specs.append(pl.BlockSpec((1, 1, rows * h, d), lambda *ids, w=where: w(*ids) + (0,)))
            out_specs.append(pl.BlockSpec((1, 1, h, rows, d),
                                          lambda *ids, w=where: w(*ids)[:2] + (0, w(*ids)[2], 0)))
            out_shape.append(jax.ShapeDtypeStruct((n_l, n_s, h, n, d), BF16))
    return in_specs, out_specs, out_shape


def _side_jobs(kinds, src_refs, dst_refs):
    for kind, src, dst in zip(kinds, src_refs, dst_refs):
        if kind == "cast":
            dst[...] = src[...].astype(BF16)
        else:
            x = src[0, 0].astype(BF16)
            heads, rows = dst.shape[2], dst.shape[3]
            row = lax.broadcasted_iota(jnp.int32, (rows, rows * heads), 0)
            pick = lax.broadcasted_iota(jnp.int32, (rows, rows * heads), 1) - heads * row
            for h in range(heads):
                dst[0, 0, h] = _dot((pick == h).astype(BF16), x).astype(BF16)


def _mem_project_kernel(mem_ref, g_ref, w_ref, *refs, side):
    n = len(side)
    mk_ref, mv_ref, mkh_ref, mvh_ref = refs[n:n + 4]
    _side_jobs(side, refs[:n], refs[n + 4:])
    xn = _unit_rms(mem_ref[0])
    for l in range(w_ref.shape[0]):
        kv = _dot((xn * g_ref[l:l + 1, :]).astype(BF16), w_ref[l].astype(BF16))
        for h in range(MEM_HEADS):
            mk = kv[:, h * MEM_HD:(h + 1) * MEM_HD]
            mv = kv[:, MEM_W + h * MEM_HD:MEM_W + (h + 1) * MEM_HD]
            mk_ref[l, 0, :, h, :] = mk
            mv_ref[l, 0, :, h, :] = mv
            mkh_ref[l, 0, h] = mk.astype(BF16)
            mvh_ref[l, 0, h] = mv.astype(BF16)


def _mem_project(mem, g_mem, w_mem_kv, side=()):
    b, n, _ = mem.shape
    depth = w_mem_kv.shape[0]
    out = jax.ShapeDtypeStruct((depth, b, n, MEM_HEADS, MEM_HD), F32)
    out_h = jax.ShapeDtypeStruct((depth, b, MEM_HEADS, n, MEM_HD), BF16)
    side_in, side_out, side_shape = _side_specs(side, b, lambda i: i)
    return pl.pallas_call(
        functools.partial(_mem_project_kernel, side=tuple(job[0] for job in side)),
        grid=(b,),
        in_specs=[pl.BlockSpec((1, n, D_MODEL), lambda i: (i, 0, 0)),
                  _const_spec(g_mem.shape), _const_spec(w_mem_kv.shape)] + side_in,
        out_specs=([pl.BlockSpec((depth, 1, n, MEM_HEADS, MEM_HD), lambda i: (0, i, 0, 0, 0))] * 2
                   + [pl.BlockSpec((depth, 1, MEM_HEADS, n, MEM_HD), lambda i: (0, i, 0, 0, 0))] * 2 + side_out),
        out_shape=[out, out, out_h, out_h] + side_shape,
        compiler_params=pltpu.CompilerParams(dimension_semantics=("arbitrary",), vmem_limit_bytes=VMEM_LIMIT),
        name="mem_project",
    )(mem, g_mem, w_mem_kv, *[job[1] for job in side])


def _layer0_kernel(x_ref, prev_ref, mk_ref, mv_ref, g_ref, w_in, w_out, w_pool, pscale, w_up, w_down, *refs,
                   nb, tt, pos_base, side):
    n = len(side)
    side_in, (y_ref, state_ref), side_out = refs[:n], refs[n:n + 2], refs[n + 2:2 * n + 2]
    halo_ref, = refs[2 * n + 2:]
    t = pl.program_id(1)

    @pl.when(t == 0)
    def _():
        halo_ref[...] = prev_ref[...]

    x = x_ref[...].reshape(nb * tt, D_MODEL)
    proj = jnp.concatenate([_dot(_rms(x[rows], _gain(g_ref, "mix_pre")).astype(BF16), w_in[...])
                            for rows in _row_parts(nb * tt)], axis=0)
    cats = []
    for b in range(nb):
        rows = slice(b * tt, (b + 1) * tt)
        u = proj[rows, :MIX_W]
        logits = _mem_logits(proj[rows, MIX_W:], mk_ref, b)
        mix = _pool_mixer(u, halo_ref[b], pos_base + t * tt, w_pool, pscale[...])
        halo_ref[b] = u[tt - HALO:, :]
        state_ref[0, b] = u[tt - POOL_BUF:, :]
        mem = _mem_values(_mem_probs(logits), mv_ref, b)
        cats.append(jnp.concatenate([mix, mem], axis=-1))
    cat = (cats[0] if nb == 1 else jnp.concatenate(cats, axis=0)).astype(BF16)
    x1 = [x[rows] + _rms(_dot(cat[rows], w_out[...]), _gain(g_ref, "mix_post")) for rows in _row_parts(nb * tt)]
    y = jnp.concatenate(_mlp(x1, _gain(g_ref, "mlp_pre"), _gain(g_ref, "mlp_post"), w_up, w_down), axis=0)
    y_ref[...] = y.reshape(nb, tt, D_MODEL)
    _side_jobs(side, side_in, side_out)


def _layer0(x, prev, mem_k, mem_v, p, *, nb, tt, pos_base, side=()):
    b, t, _ = x.shape
    grid = (b // nb, t // tt)
    tile = pl.BlockSpec((nb, tt, D_MODEL), lambda i, j: (i, j, 0))
    side_in, side_out, side_shape = _side_specs(side, grid[0] * grid[1], lambda i, j: i * grid[1] + j)
    return pl.pallas_call(
        functools.partial(_layer0_kernel, nb=nb, tt=tt, pos_base=pos_base, side=tuple(job[0] for job in side)),
        grid=grid,
        in_specs=[tile, pl.BlockSpec((nb, HALO, MIX_W), lambda i, j: (i, 0, 0)),
                  _mem_spec(mem_k, 0, nb), _mem_spec(mem_v, 0, nb),
                  _const_spec(p["gains"].shape), _layer_spec(p["w_in"][0], 0), _layer_spec(p["w_out"][0], 0),
                  _layer_spec(p["w_pool"], 0), _layer_spec(p["pool_scale"], 0),
                  _layer_spec(p["w_up"][0], 0), _layer_spec(p["w_down"][0], 0)] + side_in,
        out_specs=[tile, pl.BlockSpec((1, nb, POOL_BUF, MIX_W), lambda i, j: (0, i, 0, 0))] + side_out,
        out_shape=[jax.ShapeDtypeStruct((b, t, D_MODEL), F32),
                   jax.ShapeDtypeStruct((1, b, POOL_BUF, MIX_W), F32)] + side_shape,
        scratch_shapes=[pltpu.VMEM((nb, HALO, MIX_W), F32)],
        compiler_params=pltpu.CompilerParams(dimension_semantics=("arbitrary", "arbitrary"),
                                             vmem_limit_bytes=VMEM_LIMIT),
        name="layer0",
    )(x, prev, mem_k, mem_v, p["gains"], p["w_in"][0], p["w_out"][0], p["w_pool"], p["pool_scale"], p["w_up"][0], p["w_down"][0],
      *[job[1] for job in side])


def _bf16_part(x):
    return x.astype(BF16).astype(F32)


def _fold_lanes(h):
    lane = lax.broadcasted_iota(jnp.int32, (1, V7X_LANES), 1)
    own = (h % 2) * FOX_HD
    e0 = FOX_HD - own
    in_head = (lane >= own) & (lane < own + FOX_HD)
    first = ((lane >= e0) & (lane < e0 + 3)).astype(F32)
    second = ((lane >= e0 + 3) & (lane < e0 + 6)).astype(F32)
    return in_head, first, second, lane == e0, e0


def _fold_bias(c):
    cl = c * LOG2E
    hi = _bf16_part(cl)
    mid = _bf16_part(cl - hi)
    lo = _bf16_part(cl - hi - mid)
    lane = lax.broadcasted_iota(jnp.int32, (1, V7X_LANES), 1)
    piece = lax.rem(lane + (3 - FOLD_SRC % 3), 3)
    d = jnp.where(piece == 0, hi, jnp.where(piece == 1, mid, lo))
    return [pltpu.roll(d, (_fold_lanes(h)[4] - (FOLD_SRC + FOLD_W * h)) % V7X_LANES, axis=1)
            for h in range(FOX_HEADS)]


def _fold_kv(k, v, bias):
    ks, vs = [], []
    for h, dh in enumerate(bias):
        in_head, first, second, one, _ = _fold_lanes(h)
        cols = slice((h // 2) * PAIR_W, (h // 2 + 1) * PAIR_W)
        ks.append(jnp.where(in_head, k[:, cols], first - dh * second).astype(BF16))
        vs.append(jnp.where(in_head, v[:, cols], one.astype(F32)).T.astype(BF16))
    return jnp.concatenate(ks, axis=-1), jnp.concatenate(vs, axis=0)


def _fold_q(q, bias):
    qs = []
    for h, dh in enumerate(bias):
        in_head, first, second, _, _ = _fold_lanes(h)
        cols = slice((h // 2) * PAIR_W, (h // 2 + 1) * PAIR_W)
        qs.append(jnp.where(in_head, q[:, cols] * (FOX_HD ** -0.5 * LOG2E), dh * first + second).T.astype(BF16))
    return jnp.concatenate(qs, axis=0)


def _layer1_front_kernel(x_ref, mk_ref, mv_ref, g_ref, w_kv, w_f, w_ft, b_f, b_ft, w_in, *refs,
                         nb, tt, fold):
    if fold:
        k_ref, v_ref, logft_ref, mem_ref, qa_ref, ka_ref, va_ref, carry_ref = refs
    else:
        k_ref, v_ref, logft_ref, mem_ref, q_ref, kb_ref, vb_ref, c_ref, ct_ref = refs
    if fold:
        @pl.when(pl.program_id(1) == 0)
        def _():
            carry_ref[...] = jnp.zeros_like(carry_ref)

    xn = _unit_rms(x_ref[...].reshape(nb * tt, D_MODEL))
    hk = (xn * _gain(g_ref, "kv")).astype(BF16)
    logf = _log_sigmoid(_dot(hk, w_f[...]) + b_f[...])
    k = _dot(hk, w_kv[:, :MIX_W])
    v = _dot(hk, w_kv[:, MIX_W:])
    k_ref[...] = k.reshape(nb, tt, MIX_W)
    v_ref[...] = v.reshape(nb, tt, MIX_W)
    c = _cumsum(logf, 0, tt)
    if fold:
        c = c + carry_ref[...]
        carry_ref[...] = c[tt - 1:, :]
        bias = _fold_bias(c)
        ka_ref[0], va_ref[0, 0] = _fold_kv(k, v, bias)
        logft_ref[0] = logf.T[:FOX_HEADS, :]
    proj = _dot((xn * _gain(g_ref, "mix_pre", 1)).astype(BF16), w_in[...])
    q = proj[:, :MIX_W]
    mem_logits = [_mem_logits(proj[b * tt:(b + 1) * tt, MIX_W:], mk_ref, b) for b in range(nb)]
    if fold:
        qa_ref[0, 0] = _fold_q(q, bias)
    else:
        q_ref[...] = (q * (FOX_HD ** -0.5)).astype(BF16).reshape(nb, tt, MIX_W)
        kb_ref[...] = k.astype(BF16).reshape(nb, tt, MIX_W)
        vb_ref[...] = v.astype(BF16).reshape(nb, tt, MIX_W)
        logf_t = _log_sigmoid(_dot_nt(w_ft[...], hk) + b_ft[...])
        logft_ref[0] = logf_t
        ct_ref[0, 0] = _cumsum(logf_t, 1, tt)
    for b in range(nb):
        rows = slice(b * tt, (b + 1) * tt)
        if not fold:
            c_ref[b] = c[rows, :FOX_HEADS]
        mem_ref[b] = _mem_values(_mem_probs(mem_logits[b]), mv_ref, b).astype(BF16)


def _layer1_front(x, mem_k, mem_v, p, *, nb, tt, fold):
    b, t, _ = x.shape
    assert (nb == 1) if fold else (t == tt)
    tile = lambda w: pl.BlockSpec((nb, tt, w), lambda i, j: (i, j, 0))
    act = lambda w, dt: jax.ShapeDtypeStruct((b, t, w), dt)
    out_specs = [tile(MIX_W), tile(MIX_W), pl.BlockSpec((1, FOX_HEADS, nb * tt), lambda i, j: (i, 0, j)), tile(MEM_W)]
    out_shape = [act(MIX_W, F32), act(MIX_W, F32), jax.ShapeDtypeStruct((b // nb, FOX_HEADS, nb * t), F32),
                 act(MEM_W, BF16)]
    if fold:
        wf = FOX_HEADS * PAIR_W
        tile_t = pl.BlockSpec((1, 1, wf, tt), lambda i, j: (i, j, 0, 0))
        act_t = jax.ShapeDtypeStruct((b, t // tt, wf, tt), BF16)
        out_specs += [tile_t, tile(wf), tile_t]
        out_shape += [act_t, act(wf, BF16), act_t]
        scratch = [pltpu.VMEM((1, V7X_LANES), F32)]
    else:
        out_specs += [tile(MIX_W)] * 3 + [tile(FOX_HEADS),
                                          pl.BlockSpec((1, 1, FOX_HEADS, nb * tt), lambda i, j: (i, j, 0, 0))]
        out_shape += [act(MIX_W, BF16)] * 3 + [act(FOX_HEADS, F32),
                                               jax.ShapeDtypeStruct((b // nb, t // tt, FOX_HEADS, nb * tt), F32)]
        scratch = []
    return pl.pallas_call(
        functools.partial(_layer1_front_kernel, nb=nb, tt=tt, fold=fold),
        grid=(b // nb, t // tt),
        in_specs=[tile(D_MODEL), _mem_spec(mem_k, 1, nb), _mem_spec(mem_v, 1, nb),
                  _const_spec(p["gains"].shape),
                  _const_spec(p["w_kv"].shape), _const_spec(p["w_f"].shape), _const_spec(p["w_ft"].shape),
                  _const_spec(p["b_f"].shape), _const_spec(p["b_ft"].shape), _layer_spec(p["w_in"][1], 0)],
        out_specs=out_specs,
        out_shape=out_shape,
        scratch_shapes=scratch,
        compiler_params=pltpu.CompilerParams(dimension_semantics=("arbitrary", "arbitrary"),
                                             vmem_limit_bytes=VMEM_LIMIT),
        name="layer1_front",
    )(x, mem_k, mem_v, p["gains"], p["w_kv"], p["w_f"], p["w_ft"], p["b_f"], p["b_ft"], p["w_in"][1])


def _fox_prompt_kernel(qt_ref, k_ref, vt_ref, o_ref, m_ref, acc_ref, *s_refs, t):
    i = pl.program_id(1)
    half = t // 2
    sub = lax.broadcasted_iota(jnp.int32, (PAIR_W, 1), 0)
    col = [slice(h * PAIR_W, (h + 1) * PAIR_W) for h in range(FOX_HEADS)]
    qt = [qt_ref[0, 0, c, :] for c in col]
    slot = lambda n: s_refs[n % len(s_refs)]

    def chain(n_items, logits, update):
        top = [logits(n) for n in range(FOX_AHEAD)]
        for n in range(n_items):
            if n + FOX_AHEAD < n_items:
                top.append(logits(n + FOX_AHEAD))
            update(n, top[n])

    tri = (lax.broadcasted_iota(jnp.int32, (half, half), 0) <= lax.broadcasted_iota(jnp.int32, (half, half), 1))
    lo = pl.ds(pl.multiple_of(i * t, t), half)
    hi = pl.ds(pl.multiple_of(i * t + half, half), half)

    def diag_logits(h):
        s_lo = _dot(k_ref[0, lo, col[h]], qt[h])
        s_hi = _dot(k_ref[0, hi, col[h]], qt[h][:, half:])
        s_lo = jnp.concatenate([jnp.where(tri, s_lo[:, :half], NEG_BIG), s_lo[:, half:]], axis=1)
        s_hi = jnp.where(tri, s_hi, NEG_BIG)
        slot(h)[:half, :] = s_lo
        slot(h)[half:, half:] = s_hi
        top = jnp.max(s_lo, axis=0, keepdims=True)
        return jnp.concatenate([top[:, :half], jnp.maximum(top[:, half:], jnp.max(s_hi, axis=0, keepdims=True))],
                               axis=1)

    def diag_update(h, m):
        p_lo = jnp.exp2(slot(h)[:half, :] - m).astype(BF16)
        p_hi = jnp.exp2(slot(h)[half:, half:] - m[:, half:]).astype(BF16)
        acc = _dot(vt_ref[0, i, col[h], :half], p_lo)
        acc_ref[h] = jnp.concatenate([acc[:, :half], acc[:, half:] + _dot(vt_ref[0, i, col[h], half:], p_hi)], axis=1)
        m_ref[h] = m

    chain(FOX_HEADS, diag_logits, diag_update)

    def full_tiles(j0, n_tiles):
        def logits(n):
            rows = pl.ds(pl.multiple_of((j0 + n // FOX_HEADS) * t, t), t)
            s = _dot(k_ref[0, rows, col[n % FOX_HEADS]], qt[n % FOX_HEADS])
            slot(n)[...] = s
            return jnp.max(s, axis=0, keepdims=True)

        def update(n, top):
            h = n % FOX_HEADS
            m = m_ref[h]
            m_new = jnp.maximum(m, top)
            p = jnp.exp2(slot(n)[...] - m_new).astype(BF16)
            acc_ref[h] = jnp.exp2(m - m_new) * acc_ref[h] + _dot(vt_ref[0, j0 + n // FOX_HEADS, col[h], :], p)
            m_ref[h] = m_new

        chain(n_tiles * FOX_HEADS, logits, update)

    lax.fori_loop(0, i // 2, lambda jj, c: full_tiles(2 * jj, 2), None)
    lax.fori_loop(2 * (i // 2), i, lambda j, c: full_tiles(j, 1), None)
    for pair in range(FOX_HEADS // 2):
        even, odd = acc_ref[2 * pair], acc_ref[2 * pair + 1]
        own = sub < FOX_HD
        out = jnp.where(own, even, odd) / jnp.where(own, even[FOX_HD:FOX_HD + 1, :], odd[0:1, :])
        o_ref[0, :, pair * PAIR_W:(pair + 1) * PAIR_W] = out.T.astype(BF16)


def _fox_prompt(qt, ka, vt):
    b, n_tiles, w, tile = qt.shape
    per_stream = lambda shape: pl.BlockSpec((1,) + shape, lambda i, j: (i,) + (0,) * len(shape))
    return pl.pallas_call(
        functools.partial(_fox_prompt_kernel, t=tile),
        grid=(b, n_tiles),
        in_specs=[pl.BlockSpec((1, 1, w, tile), lambda i, j: (i, j, 0, 0)),
                  per_stream((n_tiles * tile, w)), per_stream((n_tiles, w, tile))],
        out_specs=pl.BlockSpec((1, tile, MIX_W), lambda i, j: (i, j, 0)),
        out_shape=jax.ShapeDtypeStruct((b, n_tiles * tile, MIX_W), BF16),
        scratch_shapes=([pltpu.VMEM((FOX_HEADS, 1, tile), F32), pltpu.VMEM((FOX_HEADS, PAIR_W, tile), F32)]
                        + [pltpu.VMEM((tile, tile), F32)] * (FOX_AHEAD + 1)),
        compiler_params=pltpu.CompilerParams(dimension_semantics=("arbitrary", "arbitrary"),
                                             vmem_limit_bytes=VMEM_LIMIT),
        name="fox_prompt",
    )(qt, ka, vt)


def _fox_sample_kernel(q_ref, kc_ref, vc_ref, kn_ref, vn_ref, lfc_t_ref, c_ref, ct_ref, o_ref):
    tn = q_ref.shape[1]
    p = lfc_t_ref.shape[2]
    cc = _cumsum(lfc_t_ref[0], 1, p)
    ck_cache = cc - cc[:, p - 1:]
    causal = (lax.broadcasted_iota(jnp.int32, (tn, V7X_LANES), 1)
              <= lax.broadcasted_iota(jnp.int32, (tn, V7X_LANES), 0))
    pad = jnp.zeros((V7X_LANES - tn, FOX_HD), BF16)
    col = [slice(h * FOX_HD, (h + 1) * FOX_HD) for h in range(FOX_HEADS)]
    logits = []
    for h in range(FOX_HEADS):
        qh = q_ref[0, :, col[h]]
        cq = c_ref[0, :, h:h + 1]
        kn = jnp.concatenate([kn_ref[0, :, col[h]], pad], axis=0)
        s_old = _dot(qh, kc_ref[0, h].astype(BF16)) + cq - ck_cache[h:h + 1, :]
        s_new = jnp.where(causal, _dot_nt(qh, kn) + cq - ct_ref[0, h:h + 1, :], NEG_BIG)
        logits.append((s_old, s_new))
    probs = []
    for s_old, s_new in logits:
        m = jnp.maximum(jnp.max(s_old, axis=-1, keepdims=True), jnp.max(s_new, axis=-1, keepdims=True))
        p_old = jnp.exp(s_old - m)
        p_new = jnp.exp(s_new - m)
        l = jnp.sum(p_old, axis=-1, keepdims=True) + jnp.sum(p_new, axis=-1, keepdims=True)
        probs.append((p_old.astype(BF16), p_new.astype(BF16), l))
    outs = []
    for h, (p_old, p_new, l) in enumerate(probs):
        vn = jnp.concatenate([vn_ref[0, :, col[h]], pad], axis=0)
        outs.append((_dot_nt(p_old, vc_ref[0, h].astype(BF16)) + _dot(p_new, vn)) / l)
    o_ref[0] = jnp.concatenate(outs, axis=-1).astype(BF16)


def _fox_sample(q, cache_kt, cache_vt, kb, vb, cache_logf_t, c, ct):
    b, tn, _ = q.shape
    p = cache_kt.shape[3]
    new = lambda w: pl.BlockSpec((1, tn, w), lambda i: (i, 0, 0))
    old = pl.BlockSpec((1, FOX_HEADS, FOX_HD, p), lambda i: (i, 0, 0, 0))
    return pl.pallas_call(
        _fox_sample_kernel,
        grid=(b,),
        in_specs=[new(MIX_W), old, old, new(MIX_W), new(MIX_W),
                  pl.BlockSpec((1, FOX_HEADS, p), lambda i: (i, 0, 0)),
                  new(FOX_HEADS), pl.BlockSpec((1, FOX_HEADS, V7X_LANES), lambda i: (i, 0, 0))],
        out_specs=new(MIX_W),
        out_shape=jax.ShapeDtypeStruct((b, tn, MIX_W), BF16),
        compiler_params=pltpu.CompilerParams(dimension_semantics=("arbitrary",), vmem_limit_bytes=VMEM_LIMIT),
        name="fox_sample",
    )(q, cache_kt, cache_vt, kb, vb, cache_logf_t, c, ct)


def _layer1_back_kernel(x_ref, mix_ref, mem_ref, g_ref, w_out, w_up, w_down, *refs, nb, tt, side):
    n = len(side)
    y_ref = refs[n]
    m = nb * tt
    x = x_ref[...].reshape(m, D_MODEL)
    mix = mix_ref[...].reshape(m, MIX_W)
    mem = mem_ref[...].reshape(m, MEM_W)
    x1 = [x[rows] + _rms(_dot(mix[rows], w_out[:MIX_W, :]) + _dot(mem[rows], w_out[MIX_W:, :]),
                         _gain(g_ref, "mix_post", 1)) for rows in _row_parts(m)]
    y = jnp.concatenate(_mlp(x1, _gain(g_ref, "mlp_pre", 1), _gain(g_ref, "mlp_post", 1), w_up, w_down), axis=0)
    y_ref[...] = y.reshape(nb, tt, D_MODEL)
    _side_jobs(side, refs[:n], refs[n + 1:])


def _layer1_back(x, mix, mem, p, *, nb, tt, side=()):
    b, t, _ = x.shape
    grid = (b // nb, t // tt)
    tile = lambda w: pl.BlockSpec((nb, tt, w), lambda i, j: (i, j, 0))
    side_in, side_out, side_shape = _side_specs(side, grid[0] * grid[1], lambda i, j: i * grid[1] + j)
    return pl.pallas_call(
        functools.partial(_layer1_back_kernel, nb=nb, tt=tt, side=tuple(job[0] for job in side)),
        grid=grid,
        in_specs=[tile(D_MODEL), tile(MIX_W), tile(MEM_W),
                  _const_spec(p["gains"].shape), _layer_spec(p["w_out"][1], 0), _layer_spec(p["w_up"][1], 0), _layer_spec(p["w_down"][1], 0)]
                 + side_in,
        out_specs=[tile(D_MODEL)] + side_out,
        out_shape=[jax.ShapeDtypeStruct((b, t, D_MODEL), F32)] + side_shape,
        compiler_params=pltpu.CompilerParams(dimension_semantics=("arbitrary", "arbitrary"),
                                             vmem_limit_bytes=VMEM_LIMIT),
        name="layer1_back",
    )(x, mix, mem, p["gains"], p["w_out"][1], p["w_up"][1], p["w_down"][1],
      *[job[1] for job in side])


def _token_tile(b, t):
    if t >= TOKEN_TILE:
        assert t % TOKEN_TILE == 0
        return 1, TOKEN_TILE
    return b, t


def _trunk_begin(x, pool_prev, mem_k, mem_v, p, *, running, later_f32=None, side=()):
    b, t, _ = x.shape
    nb, tt = _token_tile(b, t)
    names = sorted(later_f32) if later_f32 else []
    y0, pool_state, *done = _layer0(x, pool_prev, mem_k, mem_v, p, nb=nb, tt=tt,
                                    pos_base=POOL_BUF if running else 0,
                                    side=[("cast", later_f32[n], 1) for n in names] + list(side))
    p = {**p, **{n: [p[n][0], w] for n, w in zip(names, done)}}
    k, v, logf_t, mem, *att = _layer1_front(y0, mem_k, mem_v, p, nb=nb, tt=tt, fold=not running)
    logf = jnp.transpose(logf_t.reshape(b // nb, FOX_HEADS, nb, t), (0, 2, 3, 1)).reshape(b, t, FOX_HEADS)
    outs = (pool_state, k.reshape(b, t, FOX_HEADS, FOX_HD), v.reshape(b, t, FOX_HEADS, FOX_HD), logf)
    return (y0, mem, att, outs, (nb, tt)), p, done[len(names):]


def _trunk_end(state, cache, p):
    y0, mem, att, outs, (nb, tt) = state
    b, t, _ = y0.shape
    if cache is None:
        mix = _fox_prompt(*att)
    else:
        q, kb, vb, c, ct = att
        ct_new = jnp.swapaxes(ct.reshape(FOX_HEADS, b, t), 0, 1)
        ct_new = jnp.pad(ct_new, ((0, 0), (0, 0), (0, V7X_LANES - t)))
        mix = _fox_sample(q, *cache[:2], kb, vb, cache[2], c, ct_new)
    y, = _layer1_back(y0, mix, mem, p, nb=nb, tt=tt)
    return (y,) + outs


def kernel(x_prompt, x_sample, cache_pool, cache_k, cache_v, cache_logf, cache_mem_k, cache_mem_v, mem_prompt,
           g_mix_pre, g_mix_post, g_mlp_pre, g_mlp_post, w_in, w_out, w_pool, pool_scale, g_kv, w_kvf, b_f,
           g_mem, w_mem_kv, w_up, w_down):
    depth = w_in.shape[0]
    assert depth == 2 and w_pool.shape[0] == 1, "one pooling layer followed by one forgetting-attention layer"
    bp = x_prompt.shape[0]
    w_f = w_kvf[:, 2 * MIX_W:]
    fold_cols = lambda a: jnp.pad(jnp.concatenate([a, jnp.repeat(a, FOLD_W, axis=-1)], axis=-1),
                                  [(0, 0)] * (a.ndim - 1) + [(0, V7X_LANES - FOLD_SRC - FOLD_W * FOX_HEADS)])
    rows = lambda g: g.reshape(g.shape[0], 1, g.shape[1])
    p = dict(
        gains=_pack_gains(g_mix_pre, g_mix_post, g_mlp_pre, g_mlp_post, g_kv), pool_scale=rows(pool_scale),
        w_pool=w_pool.astype(BF16), w_kv=w_kvf[:, :2 * MIX_W].astype(BF16),
        w_f=fold_cols(w_f).astype(BF16), w_ft=w_f.T.astype(BF16),
        b_f=fold_cols(b_f).reshape(1, V7X_LANES), b_ft=b_f.reshape(FOX_HEADS, 1),
    )
    later_f32 = dict(w_in=w_in, w_out=w_out, w_up=w_up, w_down=w_down)
    names = sorted(later_f32)
    mem_k_prompt, mem_v_prompt, mkh, mvh, *w0 = _mem_project(mem_prompt, g_mem, w_mem_kv,
                                                             side=[("cast", later_f32[n], 0) for n in names])
    p.update({n: [w, None] for n, w in zip(names, w0)})

    heads_view = lambda a: a.reshape(a.shape[:2] + (-1, a.shape[-1]))
    prompt, p, (cmk, cmv) = _trunk_begin(
        x_prompt, jnp.zeros((bp, HALO, MIX_W), F32), mkh, mvh, p, running=False, later_f32=later_f32,
        side=[("heads", heads_view(cache_mem_k)), ("heads", heads_view(cache_mem_v))])

    pool_prev = jnp.pad(cache_pool[0], ((0, 0), (HALO - POOL_BUF, 0), (0, 0)))
    sample, _, _ = _trunk_begin(x_sample, pool_prev, cmk, cmv, p, running=True)
    lanes_view = lambda a: jnp.transpose(a, (0, 2, 3, 1))
    cache = (lanes_view(cache_k), lanes_view(cache_v), jnp.swapaxes(cache_logf, 1, 2))
    y_s, pool_s, k_s, v_s, logf_s = _trunk_end(sample, cache, p)
    y_p, pool_p, k_p, v_p, logf_p = _trunk_end(prompt, None, p)

    return (y_p, y_s, pool_p, pool_s, k_p, v_p, logf_p, k_s, v_s, logf_s, mem_k_prompt, mem_v_prompt)
```

```python
import functools

import jax
import jax.numpy as jnp
from jax import lax
from jax.experimental import pallas as pl
from jax.experimental.pallas import tpu as pltpu

F32 = jnp.float32
BF16 = jnp.bfloat16

D_MODEL = 1024
MIX_W = D_MODEL // 2
MEM_W = D_MODEL - MIX_W
POOL_WINDOWS = (2, 4, 8, 16)
POOL_GROUP = MIX_W // len(POOL_WINDOWS)
POOL_BUF = max(POOL_WINDOWS) - 1
FOX_HEADS = 8
FOX_HD = MIX_W // FOX_HEADS
MEM_HEADS = 4
MEM_HD = MEM_W // MEM_HEADS
D_FF = 4 * D_MODEL
EPS = 1e-6

V7X_LANES = 128
V7X_SUBLANES_F32 = 8
V7X_VMEM_BYTES = 64 * 1024 * 1024

HALO = 2 * V7X_SUBLANES_F32
FF_CHUNK = 1024
TOKEN_TILE = 512
ROW_PARTS = 2
PAIR_W = 2 * FOX_HD
NEG_BIG = -1e30
LOG2E = 1.4426950408889634
FOLD_SRC = FOX_HEADS
FOLD_W = 6
FOX_AHEAD = 2
assert FOLD_SRC + FOLD_W * FOX_HEADS <= V7X_LANES
VMEM_LIMIT = V7X_VMEM_BYTES - 8 * 1024 * 1024

assert HALO >= POOL_BUF and PAIR_W == V7X_LANES and POOL_GROUP == V7X_LANES and MEM_HD == V7X_LANES


def _const_spec(shape):
    zeros = (0,) * len(shape)
    return pl.BlockSpec(shape, lambda *_: zeros, pipeline_mode=pl.Buffered(1))


def _layer_spec(stacked, layer):
    index = (layer,) + (0,) * (stacked.ndim - 1)
    return pl.BlockSpec((None,) + stacked.shape[1:], lambda *_: index, pipeline_mode=pl.Buffered(1))


def _mem_spec(mem, layer, nb):
    return pl.BlockSpec((None, nb) + mem.shape[2:], lambda i, j: (layer, i, 0, 0, 0))


def _unit_rms(x):
    return x * lax.rsqrt(jnp.mean(x * x, axis=-1, keepdims=True) + EPS)


def _rms(x, g):
    return _unit_rms(x) * g


GAIN_KINDS = ("mix_pre", "mix_post", "mlp_pre", "mlp_post")


def _gain(g_ref, kind, layer=0):
    depth = (g_ref.shape[0] - 1) // len(GAIN_KINDS)
    r = len(GAIN_KINDS) * depth if kind == "kv" else GAIN_KINDS.index(kind) * depth + layer
    return g_ref[r:r + 1, :]


def _pack_gains(g_mix_pre, g_mix_post, g_mlp_pre, g_mlp_post, g_kv):
    return jnp.concatenate([g_mix_pre, g_mix_post, g_mlp_pre, g_mlp_post, g_kv[None, :]], axis=0)


def _dot(a, b):
    return jnp.dot(a, b, preferred_element_type=F32)


def _dot_nt(a, b):
    return lax.dot_general(a, b, (((1,), (1,)), ((), ())), preferred_element_type=F32)


def _log_sigmoid(x):
    return jnp.minimum(x, 0.0) - jnp.log1p(jnp.exp(-jnp.abs(x)))


def _cumsum(x, axis, seg):
    assert seg & (seg - 1) == 0 and x.shape[axis] % seg == 0
    n = x.shape[axis]
    vreg = (V7X_SUBLANES_F32, V7X_LANES)[axis]
    idx = lax.broadcasted_iota(jnp.int32, x.shape, axis) & (seg - 1)
    k = 1
    while k < seg:
        if seg == n and k % vreg == 0:
            zeros = jnp.zeros(x.shape[:axis] + (k,) + x.shape[axis + 1:], x.dtype)
            x = x + jnp.concatenate([zeros, lax.slice_in_dim(x, 0, n - k, axis=axis)], axis=axis)
        else:
            x = x + jnp.where(idx >= k, pltpu.roll(x, k, axis=axis), 0.0)
        k *= 2
    return x


def _mem_logits(q, mk_ref, b):
    qs = (q * (MEM_HD ** -0.5 * LOG2E)).astype(BF16)
    return [_dot_nt(qs[:, h * MEM_HD:(h + 1) * MEM_HD], mk_ref[b, h]) for h in range(MEM_HEADS)]


def _mem_probs(logits):
    probs = []
    for s in logits:
        p = jnp.exp2(s - jnp.max(s, axis=-1, keepdims=True))
        probs.append((p.astype(BF16), jnp.sum(p, axis=-1, keepdims=True)))
    return probs


def _mem_values(probs, mv_ref, b):
    return jnp.concatenate([_dot(p, mv_ref[b, h]) / l for h, (p, l) in enumerate(probs)], axis=-1)


def _pool_mixer(u, halo, pos0, w_pool_ref, scale):
    tt = u.shape[0]
    ext = jnp.concatenate([halo, u], axis=0)
    pos = pos0 + lax.broadcasted_iota(jnp.int32, (tt, 1), 0)
    outs = []
    for g, w in enumerate(POOL_WINDOWS):
        sl = slice(g * POOL_GROUP, (g + 1) * POOL_GROUP)
        s = ext[:, sl]
        k = 1
        while k < w:
            s = s + pltpu.roll(s, k, axis=0)
            k *= 2
        cnt = jnp.minimum(pos + 1, w).astype(F32)
        pooled = s[HALO:] / cnt - u[:, sl]
        outs.append(_dot(pooled.astype(BF16), w_pool_ref[g]))
    return jnp.concatenate(outs, axis=-1) * scale


def _row_parts(m):
    n = ROW_PARTS if m >= TOKEN_TILE else 1
    return [slice(i * m // n, (i + 1) * m // n) for i in range(n)]


def _mlp(x1_parts, g_pre, g_post, w_up_ref, w_down_ref):
    n_chunks = D_FF // FF_CHUNK
    chunk = lambda c: slice(c * FF_CHUNK, (c + 1) * FF_CHUNK)
    act = lambda up: jnp.square(jnp.maximum(up, 0.0)).astype(BF16)
    hm_parts = [_rms(x1, g_pre).astype(BF16) for x1 in x1_parts]
    hm = jnp.concatenate(hm_parts, axis=0)
    parts = _row_parts(hm.shape[0])
    acc = None
    for c in range(n_chunks):
        if c == 0:
            a = jnp.concatenate([act(_dot(h, w_up_ref[:, chunk(c)])) for h in hm_parts], axis=0)
        else:
            a = act(_dot(hm, w_up_ref[:, chunk(c)]))
        if c < n_chunks - 1:
            d = _dot(a, w_down_ref[chunk(c), :])
            acc = d if acc is None else acc + d
    return [x1 + _rms(acc[rows] + _dot(a[rows], w_down_ref[chunk(n_chunks - 1), :]), g_post)
            for x1, rows in zip(x1_parts, parts)]


def _side_specs(jobs, steps, step_of):
    in_specs, out_specs, out_shape = [], [], []
    for kind, a, *rest in jobs:
        if kind == "cast":
            rows = a.shape[1] // steps
            in_specs.append(pl.BlockSpec((1, rows, a.shape[2]), lambda *ids, l=rest[0]: (l, step_of(*ids), 0)))
            out_specs.append(pl.BlockSpec((1, rows, a.shape[2]), lambda *ids: (0, step_of(*ids), 0)))
            out_shape.append(jax.ShapeDtypeStruct((1,) + a.shape[1:], BF16))
        else:
            n_l, n_s, nh, d = a.shape
            h = MEM_HEADS
            n = nh // h
            rows = n_l * n_s * n // steps
            parts = n // rows
            assert kind == "heads" and rows * parts == n

            def where(*ids, parts=parts, n_s=n_s):
                step = step_of(*ids)
                return step // parts // n_s, step // parts % n_s, step % parts

            in_specs.append(pl.BlockSpec((1, 1, rows * h, d), lambda *ids, w=where: w(*ids) + (0,)))
            out_specs.append(pl.BlockSpec((1, 1, h, rows, d),
                                          lambda *ids, w=where: w(*ids)[:2] + (0, w(*ids)[2], 0)))
            out_shape.append(jax.ShapeDtypeStruct((n_l, n_s, h, n, d), BF16))
    return in_specs, out_specs, out_shape


def _side_jobs(kinds, src_refs, dst_refs):
    for kind, src, dst in zip(kinds, src_refs, dst_refs):
        if kind == "cast":
            dst[...] = src[...].astype(BF16)
        else:
            x = src[0, 0].astype(BF16)
            heads, rows = dst.shape[2], dst.shape[3]
            row = lax.broadcasted_iota(jnp.int32, (rows, rows * heads), 0)
            pick = lax.broadcasted_iota(jnp.int32, (rows, rows * heads), 1) - heads * row
            for h in range(heads):
                dst[0, 0, h] = _dot((pick == h).astype(BF16), x).astype(BF16)


def _mem_project_kernel(mem_ref, g_ref, w_ref, *refs, side):
    n = len(side)
    mk_ref, mv_ref, mkh_ref, mvh_ref = refs[n:n + 4]
    _side_jobs(side, refs[:n], refs[n + 4:])
    xn = _unit_rms(mem_ref[0])
    for l in range(w_ref.shape[0]):
        kv = _dot((xn * g_ref[l:l + 1, :]).astype(BF16), w_ref[l].astype(BF16))
        for h in range(MEM_HEADS):
            mk = kv[:, h * MEM_HD:(h + 1) * MEM_HD]
            mv = kv[:, MEM_W + h * MEM_HD:MEM_W + (h + 1) * MEM_HD]
            mk_ref[l, 0, :, h, :] = mk
            mv_ref[l, 0, :, h, :] = mv
            mkh_ref[l, 0, h] = mk.astype(BF16)
            mvh_ref[l, 0, h] = mv.astype(BF16)


def _mem_project(mem, g_mem, w_mem_kv, side=()):
    b, n, _ = mem.shape
    depth = w_mem_kv.shape[0]
    out = jax.ShapeDtypeStruct((depth, b, n, MEM_HEADS, MEM_HD), F32)
    out_h = jax.ShapeDtypeStruct((depth, b, MEM_HEADS, n, MEM_HD), BF16)
    side_in, side_out, side_shape = _side_specs(side, b, lambda i: i)
    return pl.pallas_call(
        functools.partial(_mem_project_kernel, side=tuple(job[0] for job in side)),
        grid=(b,),
        in_specs=[pl.BlockSpec((1, n, D_MODEL), lambda i: (i, 0, 0)),
                  _const_spec(g_mem.shape), _const_spec(w_mem_kv.shape)] + side_in,
        out_specs=([pl.BlockSpec((depth, 1, n, MEM_HEADS, MEM_HD), lambda i: (0, i, 0, 0, 0))] * 2
                   + [pl.BlockSpec((depth, 1, MEM_HEADS, n, MEM_HD), lambda i: (0, i, 0, 0, 0))] * 2 + side_out),
        out_shape=[out, out, out_h, out_h] + side_shape,
        compiler_params=pltpu.CompilerParams(dimension_semantics=("arbitrary",), vmem_limit_bytes=VMEM_LIMIT),
        name="mem_project",
    )(mem, g_mem, w_mem_kv, *[job[1] for job in side])


def _layer0_kernel(x_ref, prev_ref, mk_ref, mv_ref, g_ref, w_in, w_out, w_pool, pscale, w_up, w_down, *refs,
                   nb, tt, pos_base, side):
    n = len(side)
    side_in, (y_ref, state_ref), side_out = refs[:n], refs[n:n + 2], refs[n + 2:2 * n + 2]
    halo_ref, = refs[2 * n + 2:]
    t = pl.program_id(1)

    @pl.when(t == 0)
    def _():
        halo_ref[...] = prev_ref[...]

    x = x_ref[...].reshape(nb * tt, D_MODEL)
    proj = jnp.concatenate([_dot(_rms(x[rows], _gain(g_ref, "mix_pre")).astype(BF16), w_in[...])
                            for rows in _row_parts(nb * tt)], axis=0)
    cats = []
    for b in range(nb):
        rows = slice(b * tt, (b + 1) * tt)
        u = proj[rows, :MIX_W]
        logits = _mem_logits(proj[rows, MIX_W:], mk_ref, b)
        mix = _pool_mixer(u, halo_ref[b], pos_base + t * tt, w_pool, pscale[...])
        halo_ref[b] = u[tt - HALO:, :]
        state_ref[0, b] = u[tt - POOL_BUF:, :]
        mem = _mem_values(_mem_probs(logits), mv_ref, b)
        cats.append(jnp.concatenate([mix, mem], axis=-1))
    cat = (cats[0] if nb == 1 else jnp.concatenate(cats, axis=0)).astype(BF16)
    x1 = [x[rows] + _rms(_dot(cat[rows], w_out[...]), _gain(g_ref, "mix_post")) for rows in _row_parts(nb * tt)]
    y = jnp.concatenate(_mlp(x1, _gain(g_ref, "mlp_pre"), _gain(g_ref, "mlp_post"), w_up, w_down), axis=0)
    y_ref[...] = y.reshape(nb, tt, D_MODEL)
    _side_jobs(side, side_in, side_out)


def _layer0(x, prev, mem_k, mem_v, p, *, nb, tt, pos_base, side=()):
    b, t, _ = x.shape
    grid = (b // nb, t // tt)
    tile = pl.BlockSpec((nb, tt, D_MODEL), lambda i, j: (i, j, 0))
    side_in, side_out, side_shape = _side_specs(side, grid[0] * grid[1], lambda i, j: i * grid[1] + j)
    return pl.pallas_call(
        functools.partial(_layer0_kernel, nb=nb, tt=tt, pos_base=pos_base, side=tuple(job[0] for job in side)),
        grid=grid,
        in_specs=[tile, pl.BlockSpec((nb, HALO, MIX_W), lambda i, j: (i, 0, 0)),
                  _mem_spec(mem_k, 0, nb), _mem_spec(mem_v, 0, nb),
                  _const_spec(p["gains"].shape), _layer_spec(p["w_in"][0], 0), _layer_spec(p["w_out"][0], 0),
                  _layer_spec(p["w_pool"], 0), _layer_spec(p["pool_scale"], 0),
                  _layer_spec(p["w_up"][0], 0), _layer_spec(p["w_down"][0], 0)] + side_in,
        out_specs=[tile, pl.BlockSpec((1, nb, POOL_BUF, MIX_W), lambda i, j: (0, i, 0, 0))] + side_out,
        out_shape=[jax.ShapeDtypeStruct((b, t, D_MODEL), F32),
                   jax.ShapeDtypeStruct((1, b, POOL_BUF, MIX_W), F32)] + side_shape,
        scratch_shapes=[pltpu.VMEM((nb, HALO, MIX_W), F32)],
        compiler_params=pltpu.CompilerParams(dimension_semantics=("arbitrary", "arbitrary"),
                                             vmem_limit_bytes=VMEM_LIMIT),
        name="layer0",
    )(x, prev, mem_k, mem_v, p["gains"], p["w_in"][0], p["w_out"][0], p["w_pool"], p["pool_scale"], p["w_up"][0], p["w_down"][0],
      *[job[1] for job in side])


def _bf16_part(x):
    return x.astype(BF16).astype(F32)


def _fold_lanes(h):
    lane = lax.broadcasted_iota(jnp.int32, (1, V7X_LANES), 1)
    own = (h % 2) * FOX_HD
    e0 = FOX_HD - own
    in_head = (lane >= own) & (lane < own + FOX_HD)
    first = ((lane >= e0) & (lane < e0 + 3)).astype(F32)
    second = ((lane >= e0 + 3) & (lane < e0 + 6)).astype(F32)
    return in_head, first, second, lane == e0, e0


def _fold_bias(c):
    cl = c * LOG2E
    hi = _bf16_part(cl)
    mid = _bf16_part(cl - hi)
    lo = _bf16_part(cl - hi - mid)
    lane = lax.broadcasted_iota(jnp.int32, (1, V7X_LANES), 1)
    piece = lax.rem(lane + (3 - FOLD_SRC % 3), 3)
    d = jnp.where(piece == 0, hi, jnp.where(piece == 1, mid, lo))
    return [pltpu.roll(d, (_fold_lanes(h)[4] - (FOLD_SRC + FOLD_W * h)) % V7X_LANES, axis=1)
            for h in range(FOX_HEADS)]


def _fold_kv(k, v, bias):
    ks, vs = [], []
    for h, dh in enumerate(bias):
        in_head, first, second, one, _ = _fold_lanes(h)
        cols = slice((h // 2) * PAIR_W, (h // 2 + 1) * PAIR_W)
        ks.append(jnp.where(in_head, k[:, cols], first - dh * second).astype(BF16))
        vs.append(jnp.where(in_head, v[:, cols], one.astype(F32)).T.astype(BF16))
    return jnp.concatenate(ks, axis=-1), jnp.concatenate(vs, axis=0)


def _fold_q(q, bias):
    qs = []
    for h, dh in enumerate(bias):
        in_head, first, second, _, _ = _fold_lanes(h)
        cols = slice((h // 2) * PAIR_W, (h // 2 + 1) * PAIR_W)
        qs.append(jnp.where(in_head, q[:, cols] * (FOX_HD ** -0.5 * LOG2E), dh * first + second).T.astype(BF16))
    return jnp.concatenate(qs, axis=0)


def _layer1_front_kernel(x_ref, mk_ref, mv_ref, g_ref, w_kv, w_f, w_ft, b_f, b_ft, w_in, *refs,
                         nb, tt, fold):
    if fold:
        k_ref, v_ref, logft_ref, mem_ref, qa_ref, ka_ref, va_ref, carry_ref = refs
    else:
        k_ref, v_ref, logft_ref, mem_ref, q_ref, kb_ref, vb_ref, c_ref, ct_ref = refs
    if fold:
        @pl.when(pl.program_id(1) == 0)
        def _():
            carry_ref[...] = jnp.zeros_like(carry_ref)

    xn = _unit_rms(x_ref[...].reshape(nb * tt, D_MODEL))
    hk = (xn * _gain(g_ref, "kv")).astype(BF16)
    logf = _log_sigmoid(_dot(hk, w_f[...]) + b_f[...])
    k = _dot(hk, w_kv[:, :MIX_W])
    v = _dot(hk, w_kv[:, MIX_W:])
    if fold:
        k_ref[0] = k.T
        v_ref[0] = v.T
    else:
        k_ref[...] = k.reshape(nb, tt, MIX_W)
        v_ref[...] = v.reshape(nb, tt, MIX_W)
    c = _cumsum(logf, 0, tt)
    if fold:
        c = c + carry_ref[...]
        carry_ref[...] = c[tt - 1:, :]
        bias = _fold_bias(c)
        ka_ref[0], va_ref[0, 0] = _fold_kv(k, v, bias)
        logft_ref[0] = logf.T[:FOX_HEADS, :]
    proj = _dot((xn * _gain(g_ref, "mix_pre", 1)).astype(BF16), w_in[...])
    q = proj[:, :MIX_W]
    mem_logits = [_mem_logits(proj[b * tt:(b + 1) * tt, MIX_W:], mk_ref, b) for b in range(nb)]
    if fold:
        qa_ref[0, 0] = _fold_q(q, bias)
    else:
        q_ref[...] = (q * (FOX_HD ** -0.5)).astype(BF16).reshape(nb, tt, MIX_W)
        kb_ref[...] = k.astype(BF16).reshape(nb, tt, MIX_W)
        vb_ref[...] = v.astype(BF16).reshape(nb, tt, MIX_W)
        logf_t = _log_sigmoid(_dot_nt(w_ft[...], hk) + b_ft[...])
        logft_ref[0] = logf_t
        ct_ref[0, 0] = _cumsum(logf_t, 1, tt)
    for b in range(nb):
        rows = slice(b * tt, (b + 1) * tt)
        if not fold:
            c_ref[b] = c[rows, :FOX_HEADS]
        mem_ref[b] = _mem_values(_mem_probs(mem_logits[b]), mv_ref, b).astype(BF16)


def _layer1_front(x, mem_k, mem_v, p, *, nb, tt, fold):
    b, t, _ = x.shape
    assert (nb == 1) if fold else (t == tt)
    tile = lambda w: pl.BlockSpec((nb, tt, w), lambda i, j: (i, j, 0))
    act = lambda w, dt: jax.ShapeDtypeStruct((b, t, w), dt)
    kv_spec = pl.BlockSpec((1, MIX_W, tt), lambda i, j: (i, 0, j)) if fold else tile(MIX_W)
    kv_shape = jax.ShapeDtypeStruct((b, MIX_W, t), F32) if fold else act(MIX_W, F32)
    out_specs = [kv_spec, kv_spec, pl.BlockSpec((1, FOX_HEADS, nb * tt), lambda i, j: (i, 0, j)), tile(MEM_W)]
    out_shape = [kv_shape, kv_shape, jax.ShapeDtypeStruct((b // nb, FOX_HEADS, nb * t), F32), act(MEM_W, BF16)]
    if fold:
        wf = FOX_HEADS * PAIR_W
        tile_t = pl.BlockSpec((1, 1, wf, tt), lambda i, j: (i, j, 0, 0))
        act_t = jax.ShapeDtypeStruct((b, t // tt, wf, tt), BF16)
        out_specs += [tile_t, tile(wf), tile_t]
        out_shape += [act_t, act(wf, BF16), act_t]
        scratch = [pltpu.VMEM((1, V7X_LANES), F32)]
    else:
        out_specs += [tile(MIX_W)] * 3 + [tile(FOX_HEADS),
                                          pl.BlockSpec((1, 1, FOX_HEADS, nb * tt), lambda i, j: (i, j, 0, 0))]
        out_shape += [act(MIX_W, BF16)] * 3 + [act(FOX_HEADS, F32),
                                               jax.ShapeDtypeStruct((b // nb, t // tt, FOX_HEADS, nb * tt), F32)]
        scratch = []
    return pl.pallas_call(
        functools.partial(_layer1_front_kernel, nb=nb, tt=tt, fold=fold),
        grid=(b // nb, t // tt),
        in_specs=[tile(D_MODEL), _mem_spec(mem_k, 1, nb), _mem_spec(mem_v, 1, nb),
                  _const_spec(p["gains"].shape),
                  _const_spec(p["w_kv"].shape), _const_spec(p["w_f"].shape), _const_spec(p["w_ft"].shape),
                  _const_spec(p["b_f"].shape), _const_spec(p["b_ft"].shape), _layer_spec(p["w_in"][1], 0)],
        out_specs=out_specs,
        out_shape=out_shape,
        scratch_shapes=scratch,
        compiler_params=pltpu.CompilerParams(dimension_semantics=("arbitrary", "arbitrary"),
                                             vmem_limit_bytes=VMEM_LIMIT),
        name="layer1_front",
    )(x, mem_k, mem_v, p["gains"], p["w_kv"], p["w_f"], p["w_ft"], p["b_f"], p["b_ft"], p["w_in"][1])


def _fox_prompt_kernel(qt_ref, k_ref, vt_ref, o_ref, m_ref, acc_ref, *s_refs, t):
    i = pl.program_id(1)
    half = t // 2
    sub = lax.broadcasted_iota(jnp.int32, (PAIR_W, 1), 0)
    col = [slice(h * PAIR_W, (h + 1) * PAIR_W) for h in range(FOX_HEADS)]
    qt = [qt_ref[0, 0, c, :] for c in col]
    slot = lambda n: s_refs[n % len(s_refs)]

    def chain(n_items, logits, update):
        top = [logits(n) for n in range(FOX_AHEAD)]
        for n in range(n_items):
            if n + FOX_AHEAD < n_items:
                top.append(logits(n + FOX_AHEAD))
            update(n, top[n])

    tri = (lax.broadcasted_iota(jnp.int32, (half, half), 0) <= lax.broadcasted_iota(jnp.int32, (half, half), 1))
    lo = pl.ds(pl.multiple_of(i * t, t), half)
    hi = pl.ds(pl.multiple_of(i * t + half, half), half)

    def diag_logits(h):
        s_lo = _dot(k_ref[0, lo, col[h]], qt[h])
        s_hi = _dot(k_ref[0, hi, col[h]], qt[h][:, half:])
        s_lo = jnp.concatenate([jnp.where(tri, s_lo[:, :half], NEG_BIG), s_lo[:, half:]], axis=1)
        s_hi = jnp.where(tri, s_hi, NEG_BIG)
        slot(h)[:half, :] = s_lo
        slot(h)[half:, half:] = s_hi
        top = jnp.max(s_lo, axis=0, keepdims=True)
        return jnp.concatenate([top[:, :half], jnp.maximum(top[:, half:], jnp.max(s_hi, axis=0, keepdims=True))],
                               axis=1)

    def diag_update(h, m):
        p_lo = jnp.exp2(slot(h)[:half, :] - m).astype(BF16)
        p_hi = jnp.exp2(slot(h)[half:, half:] - m[:, half:]).astype(BF16)
        acc = _dot(vt_ref[0, i, col[h], :half], p_lo)
        acc_ref[h] = jnp.concatenate([acc[:, :half], acc[:, half:] + _dot(vt_ref[0, i, col[h], half:], p_hi)], axis=1)
        m_ref[h] = m

    chain(FOX_HEADS, diag_logits, diag_update)

    def full_tiles(j0, n_tiles):
        def logits(n):
            rows = pl.ds(pl.multiple_of((j0 + n // FOX_HEADS) * t, t), t)
            s = _dot(k_ref[0, rows, col[n % FOX_HEADS]], qt[n % FOX_HEADS])
            slot(n)[...] = s
            return jnp.max(s, axis=0, keepdims=True)

        def update(n, top):
            h = n % FOX_HEADS
            m = m_ref[h]
            m_new = jnp.maximum(m, top)
            p = jnp.exp2(slot(n)[...] - m_new).astype(BF16)
            acc_ref[h] = jnp.exp2(m - m_new) * acc_ref[h] + _dot(vt_ref[0, j0 + n // FOX_HEADS, col[h], :], p)
            m_ref[h] = m_new

        chain(n_tiles * FOX_HEADS, logits, update)

    lax.fori_loop(0, i // 2, lambda jj, c: full_tiles(2 * jj, 2), None)
    lax.fori_loop(2 * (i // 2), i, lambda j, c: full_tiles(j, 1), None)
    for pair in range(FOX_HEADS // 2):
        even, odd = acc_ref[2 * pair], acc_ref[2 * pair + 1]
        own = sub < FOX_HD
        out = jnp.where(own, even, odd) / jnp.where(own, even[FOX_HD:FOX_HD + 1, :], odd[0:1, :])
        o_ref[0, :, pair * PAIR_W:(pair + 1) * PAIR_W] = out.T.astype(BF16)


def _fox_prompt(qt, ka, vt):
    b, n_tiles, w, tile = qt.shape
    per_stream = lambda shape: pl.BlockSpec((1,) + shape, lambda i, j: (i,) + (0,) * len(shape))
    return pl.pallas_call(
        functools.partial(_fox_prompt_kernel, t=tile),
        grid=(b, n_tiles),
        in_specs=[pl.BlockSpec((1, 1, w, tile), lambda i, j: (i, j, 0, 0)),
                  per_stream((n_tiles * tile, w)), per_stream((n_tiles, w, tile))],
        out_specs=pl.BlockSpec((1, tile, MIX_W), lambda i, j: (i, j, 0)),
        out_shape=jax.ShapeDtypeStruct((b, n_tiles * tile, MIX_W), BF16),
        scratch_shapes=([pltpu.VMEM((FOX_HEADS, 1, tile), F32), pltpu.VMEM((FOX_HEADS, PAIR_W, tile), F32)]
                        + [pltpu.VMEM((tile, tile), F32)] * (FOX_AHEAD + 1)),
        compiler_params=pltpu.CompilerParams(dimension_semantics=("arbitrary", "arbitrary"),
                                             vmem_limit_bytes=VMEM_LIMIT),
        name="fox_prompt",
    )(qt, ka, vt)


def _fox_sample_kernel(q_ref, kc_ref, vc_ref, kn_ref, vn_ref, lfc_t_ref, c_ref, ct_ref, o_ref):
    tn = q_ref.shape[1]
    p = lfc_t_ref.shape[2]
    cc = _cumsum(lfc_t_ref[0], 1, p)
    ck_cache = cc - cc[:, p - 1:]
    causal = (lax.broadcasted_iota(jnp.int32, (tn, V7X_LANES), 1)
              <= lax.broadcasted_iota(jnp.int32, (tn, V7X_LANES), 0))
    pad = jnp.zeros((V7X_LANES - tn, FOX_HD), BF16)
    col = [slice(h * FOX_HD, (h + 1) * FOX_HD) for h in range(FOX_HEADS)]
    logits = []
    for h in range(FOX_HEADS):
        qh = q_ref[0, :, col[h]]
        cq = c_ref[0, :, h:h + 1]
        kn = jnp.concatenate([kn_ref[0, :, col[h]], pad], axis=0)
        s_old = _dot(qh, kc_ref[0, h].astype(BF16)) + cq - ck_cache[h:h + 1, :]
        s_new = jnp.where(causal, _dot_nt(qh, kn) + cq - ct_ref[0, h:h + 1, :], NEG_BIG)
        logits.append((s_old, s_new))
    probs = []
    for s_old, s_new in logits:
        m = jnp.maximum(jnp.max(s_old, axis=-1, keepdims=True), jnp.max(s_new, axis=-1, keepdims=True))
        p_old = jnp.exp(s_old - m)
        p_new = jnp.exp(s_new - m)
        l = jnp.sum(p_old, axis=-1, keepdims=True) + jnp.sum(p_new, axis=-1, keepdims=True)
        probs.append((p_old.astype(BF16), p_new.astype(BF16), l))
    outs = []
    for h, (p_old, p_new, l) in enumerate(probs):
        vn = jnp.concatenate([vn_ref[0, :, col[h]], pad], axis=0)
        outs.append((_dot_nt(p_old, vc_ref[0, h].astype(BF16)) + _dot(p_new, vn)) / l)
    o_ref[0] = jnp.concatenate(outs, axis=-1).astype(BF16)


def _fox_sample(q, cache_kt, cache_vt, kb, vb, cache_logf_t, c, ct):
    b, tn, _ = q.shape
    p = cache_kt.shape[3]
    new = lambda w: pl.BlockSpec((1, tn, w), lambda i: (i, 0, 0))
    old = pl.BlockSpec((1, FOX_HEADS, FOX_HD, p), lambda i: (i, 0, 0, 0))
    return pl.pallas_call(
        _fox_sample_kernel,
        grid=(b,),
        in_specs=[new(MIX_W), old, old, new(MIX_W), new(MIX_W),
                  pl.BlockSpec((1, FOX_HEADS, p), lambda i: (i, 0, 0)),
                  new(FOX_HEADS), pl.BlockSpec((1, FOX_HEADS, V7X_LANES), lambda i: (i, 0, 0))],
        out_specs=new(MIX_W),
        out_shape=jax.ShapeDtypeStruct((b, tn, MIX_W), BF16),
        compiler_params=pltpu.CompilerParams(dimension_semantics=("arbitrary",), vmem_limit_bytes=VMEM_LIMIT),
        name="fox_sample",
    )(q, cache_kt, cache_vt, kb, vb, cache_logf_t, c, ct)


def _layer1_back_kernel(x_ref, mix_ref, mem_ref, g_ref, w_out, w_up, w_down, *refs, nb, tt, side):
    n = len(side)
    y_ref = refs[n]
    m = nb * tt
    x = x_ref[...].reshape(m, D_MODEL)
    mix = mix_ref[...].reshape(m, MIX_W)
    mem = mem_ref[...].reshape(m, MEM_W)
    x1 = [x[rows] + _rms(_dot(mix[rows], w_out[:MIX_W, :]) + _dot(mem[rows], w_out[MIX_W:, :]),
                         _gain(g_ref, "mix_post", 1)) for rows in _row_parts(m)]
    y = jnp.concatenate(_mlp(x1, _gain(g_ref, "mlp_pre", 1), _gain(g_ref, "mlp_post", 1), w_up, w_down), axis=0)
    y_ref[...] = y.reshape(nb, tt, D_MODEL)
    _side_jobs(side, refs[:n], refs[n + 1:])


def _layer1_back(x, mix, mem, p, *, nb, tt, side=()):
    b, t, _ = x.shape
    grid = (b // nb, t // tt)
    tile = lambda w: pl.BlockSpec((nb, tt, w), lambda i, j: (i, j, 0))
    side_in, side_out, side_shape = _side_specs(side, grid[0] * grid[1], lambda i, j: i * grid[1] + j)
    return pl.pallas_call(
        functools.partial(_layer1_back_kernel, nb=nb, tt=tt, side=tuple(job[0] for job in side)),
        grid=grid,
        in_specs=[tile(D_MODEL), tile(MIX_W), tile(MEM_W),
                  _const_spec(p["gains"].shape), _layer_spec(p["w_out"][1], 0), _layer_spec(p["w_up"][1], 0), _layer_spec(p["w_down"][1], 0)]
                 + side_in,
        out_specs=[tile(D_MODEL)] + side_out,
        out_shape=[jax.ShapeDtypeStruct((b, t, D_MODEL), F32)] + side_shape,
        compiler_params=pltpu.CompilerParams(dimension_semantics=("arbitrary", "arbitrary"),
                                             vmem_limit_bytes=VMEM_LIMIT),
        name="layer1_back",
    )(x, mix, mem, p["gains"], p["w_out"][1], p["w_up"][1], p["w_down"][1],
      *[job[1] for job in side])


def _token_tile(b, t):
    if t >= TOKEN_TILE:
        assert t % TOKEN_TILE == 0
        return 1, TOKEN_TILE
    return b, t


def _trunk(x, pool_prev, cache, mem_k, mem_v, p, later_f32=None, side=((), ())):
    b, t, _ = x.shape
    nb, tt = _token_tile(b, t)
    pos_base = 0 if cache is None else POOL_BUF
    names = sorted(later_f32) if later_f32 else []
    y0, pool_state, *done = _layer0(x, pool_prev, mem_k, mem_v, p, nb=nb, tt=tt, pos_base=pos_base,
                                    side=[("cast", later_f32[n], 1) for n in names] + list(side[0]))
    p = {**p, **{n: [p[n][0], w] for n, w in zip(names, done)}}
    done_first = done[len(names):]
    k, v, logf_t, mem, *att = _layer1_front(y0, mem_k, mem_v, p, nb=nb, tt=tt, fold=cache is None)
    logf = jnp.transpose(logf_t.reshape(b // nb, FOX_HEADS, nb, t), (0, 2, 3, 1)).reshape(b, t, FOX_HEADS)
    if cache is None:
        mix = _fox_prompt(*att)
    else:
        q, kb, vb, c, ct = att
        ct_new = jnp.swapaxes(ct.reshape(FOX_HEADS, b, t), 0, 1)
        ct_new = jnp.pad(ct_new, ((0, 0), (0, 0), (0, V7X_LANES - t)))
        mix = _fox_sample(q, *cache[:2], kb, vb, cache[2], c, ct_new)
    y, *done_last = _layer1_back(y0, mix, mem, p, nb=nb, tt=tt, side=side[1])
    if cache is None:
        heads = lambda a: jnp.transpose(a.reshape(b, FOX_HEADS, FOX_HD, t), (0, 3, 1, 2))
    else:
        heads = lambda a: a.reshape(b, t, FOX_HEADS, FOX_HD)
    return (y, pool_state, heads(k), heads(v), logf), p, done_first, done_last


def kernel(x_prompt, x_sample, cache_pool, cache_k, cache_v, cache_logf, cache_mem_k, cache_mem_v, mem_prompt,
           g_mix_pre, g_mix_post, g_mlp_pre, g_mlp_post, w_in, w_out, w_pool, pool_scale, g_kv, w_kvf, b_f,
           g_mem, w_mem_kv, w_up, w_down):
    depth = w_in.shape[0]
    assert depth == 2 and w_pool.shape[0] == 1, "one pooling layer followed by one forgetting-attention layer"
    bp = x_prompt.shape[0]
    w_f = w_kvf[:, 2 * MIX_W:]
    fold_cols = lambda a: jnp.pad(jnp.concatenate([a, jnp.repeat(a, FOLD_W, axis=-1)], axis=-1),
                                  [(0, 0)] * (a.ndim - 1) + [(0, V7X_LANES - FOLD_SRC - FOLD_W * FOX_HEADS)])
    rows = lambda g: g.reshape(g.shape[0], 1, g.shape[1])
    p = dict(
        gains=_pack_gains(g_mix_pre, g_mix_post, g_mlp_pre, g_mlp_post, g_kv), pool_scale=rows(pool_scale),
        w_pool=w_pool.astype(BF16), w_kv=w_kvf[:, :2 * MIX_W].astype(BF16),
        w_f=fold_cols(w_f).astype(BF16), w_ft=w_f.T.astype(BF16),
        b_f=fold_cols(b_f).reshape(1, V7X_LANES), b_ft=b_f.reshape(FOX_HEADS, 1),
    )
    later_f32 = dict(w_in=w_in, w_out=w_out, w_up=w_up, w_down=w_down)
    names = sorted(later_f32)
    mem_k_prompt, mem_v_prompt, mkh, mvh, *w0 = _mem_project(mem_prompt, g_mem, w_mem_kv,
                                                             side=[("cast", later_f32[n], 0) for n in names])
    p.update({n: [w, None] for n, w in zip(names, w0)})

    heads_view = lambda a: a.reshape(a.shape[:2] + (-1, a.shape[-1]))
    (y_p, pool_p, k_p, v_p, logf_p), p, (cmk,), (cmv,) = _trunk(
        x_prompt, jnp.zeros((bp, HALO, MIX_W), F32), None, mkh, mvh, p, later_f32,
        side=([("heads", heads_view(cache_mem_k))], [("heads", heads_view(cache_mem_v))]))

    pool_prev = jnp.pad(cache_pool[0], ((0, 0), (HALO - POOL_BUF, 0), (0, 0)))
    lanes_view = lambda a: jnp.transpose(a, (0, 2, 3, 1))
    cache = (lanes_view(cache_k), lanes_view(cache_v), jnp.swapaxes(cache_logf, 1, 2))
    (y_s, pool_s, k_s, v_s, logf_s), _, _, _ = _trunk(x_sample, pool_prev, cache, cmk, cmv, p)

    return (y_p, y_s, pool_p, pool_s, k_p, v_p, logf_p, k_s, v_s, logf_s, mem_k_prompt, mem_v_prompt)
```

```python
import functools

import jax
import jax.numpy as jnp
from jax import lax
from jax.experimental import pallas as pl
from jax.experimental.pallas import tpu as pltpu

F32 = jnp.float32
BF16 = jnp.bfloat16

D_MODEL = 1024
MIX_W = D_MODEL // 2
MEM_W = D_MODEL - MIX_W
POOL_WINDOWS = (2, 4, 8, 16)
POOL_GROUP = MIX_W // len(POOL_WINDOWS)
POOL_BUF = max(POOL_WINDOWS) - 1
FOX_HEADS = 8
FOX_HD = MIX_W // FOX_HEADS
MEM_HEADS = 4
MEM_HD = MEM_W // MEM_HEADS
D_FF = 4 * D_MODEL
EPS = 1e-6

V7X_LANES = 128
V7X_SUBLANES_F32 = 8
V7X_VMEM_BYTES = 64 * 1024 * 1024

HALO = 2 * V7X_SUBLANES_F32
FF_CHUNK = 1024
TOKEN_TILE = 512
ROW_PARTS = 2
PAIR_W = 2 * FOX_HD
NEG_BIG = -1e30
LOG2E = 1.4426950408889634
FOLD_SRC = FOX_HEADS
FOLD_W = 6
FOX_AHEAD = 2
assert FOLD_SRC + FOLD_W * FOX_HEADS <= V7X_LANES
VMEM_LIMIT = V7X_VMEM_BYTES - 8 * 1024 * 1024

assert HALO >= POOL_BUF and PAIR_W == V7X_LANES and POOL_GROUP == V7X_LANES and MEM_HD == V7X_LANES


def _const_spec(shape):
    zeros = (0,) * len(shape)
    return pl.BlockSpec(shape, lambda *_: zeros, pipeline_mode=pl.Buffered(1))


def _layer_spec(stacked, layer):
    index = (layer,) + (0,) * (stacked.ndim - 1)
    return pl.BlockSpec((None,) + stacked.shape[1:], lambda *_: index, pipeline_mode=pl.Buffered(1))


def _mem_spec(mem, layer, nb):
    return pl.BlockSpec((None, nb) + mem.shape[2:], lambda i, j: (layer, i, 0, 0, 0))


def _unit_rms(x):
    return x * lax.rsqrt(jnp.mean(x * x, axis=-1, keepdims=True) + EPS)


def _rms(x, g):
    return _unit_rms(x) * g


GAIN_KINDS = ("mix_pre", "mix_post", "mlp_pre", "mlp_post")


def _gain(g_ref, kind, layer=0):
    depth = (g_ref.shape[0] - 1) // len(GAIN_KINDS)
    r = len(GAIN_KINDS) * depth if kind == "kv" else GAIN_KINDS.index(kind) * depth + layer
    return g_ref[r:r + 1, :]


def _pack_gains(g_mix_pre, g_mix_post, g_mlp_pre, g_mlp_post, g_kv):
    return jnp.concatenate([g_mix_pre, g_mix_post, g_mlp_pre, g_mlp_post, g_kv[None, :]], axis=0)


def _dot(a, b):
    return jnp.dot(a, b, preferred_element_type=F32)


def _dot_nt(a, b):
    return lax.dot_general(a, b, (((1,), (1,)), ((), ())), preferred_element_type=F32)


def _log_sigmoid(x):
    return jnp.minimum(x, 0.0) - jnp.log1p(jnp.exp(-jnp.abs(x)))


def _cumsum(x, axis, seg):
    assert seg & (seg - 1) == 0 and x.shape[axis] % seg == 0
    n = x.shape[axis]
    vreg = (V7X_SUBLANES_F32, V7X_LANES)[axis]
    idx = lax.broadcasted_iota(jnp.int32, x.shape, axis) & (seg - 1)
    k = 1
    while k < seg:
        if seg == n and k % vreg == 0:
            zeros = jnp.zeros(x.shape[:axis] + (k,) + x.shape[axis + 1:], x.dtype)
            x = x + jnp.concatenate([zeros, lax.slice_in_dim(x, 0, n - k, axis=axis)], axis=axis)
        else:
            x = x + jnp.where(idx >= k, pltpu.roll(x, k, axis=axis), 0.0)
        k *= 2
    return x


def _mem_logits(q, mk_ref, b):
    qs = (q * (MEM_HD ** -0.5 * LOG2E)).astype(BF16)
    return [_dot_nt(qs[:, h * MEM_HD:(h + 1) * MEM_HD], mk_ref[b, h]) for h in range(MEM_HEADS)]


def _mem_probs(logits):
    probs = []
    for s in logits:
        p = jnp.exp2(s - jnp.max(s, axis=-1, keepdims=True))
        probs.append((p.astype(BF16), jnp.sum(p, axis=-1, keepdims=True)))
    return probs


def _mem_values(probs, mv_ref, b):
    return jnp.concatenate([_dot(p, mv_ref[b, h]) / l for h, (p, l) in enumerate(probs)], axis=-1)


def _pool_mixer(u, halo, pos0, w_pool_ref, scale):
    tt = u.shape[0]
    ext = jnp.concatenate([halo, u], axis=0)
    pos = pos0 + lax.broadcasted_iota(jnp.int32, (tt, 1), 0)
    outs = []
    for g, w in enumerate(POOL_WINDOWS):
        sl = slice(g * POOL_GROUP, (g + 1) * POOL_GROUP)
        s = ext[:, sl]
        k = 1
        while k < w:
            s = s + pltpu.roll(s, k, axis=0)
            k *= 2
        cnt = jnp.minimum(pos + 1, w).astype(F32)
        pooled = s[HALO:] / cnt - u[:, sl]
        outs.append(_dot(pooled.astype(BF16), w_pool_ref[g]))
    return jnp.concatenate(outs, axis=-1) * scale


def _row_parts(m):
    n = ROW_PARTS if m >= TOKEN_TILE else 1
    return [slice(i * m // n, (i + 1) * m // n) for i in range(n)]


def _mlp(x1_parts, g_pre, g_post, w_up_ref, w_down_ref):
    n_chunks = D_FF // FF_CHUNK
    chunk = lambda c: slice(c * FF_CHUNK, (c + 1) * FF_CHUNK)
    act = lambda up: jnp.square(jnp.maximum(up, 0.0)).astype(BF16)
    hm_parts = [_rms(x1, g_pre).astype(BF16) for x1 in x1_parts]
    hm = jnp.concatenate(hm_parts, axis=0)
    parts = _row_parts(hm.shape[0])
    acc = None
    for c in range(n_chunks):
        if c == 0:
            a = jnp.concatenate([act(_dot(h, w_up_ref[:, chunk(c)])) for h in hm_parts], axis=0)
        else:
            a = act(_dot(hm, w_up_ref[:, chunk(c)]))
        if c < n_chunks - 1:
            d = _dot(a, w_down_ref[chunk(c), :])
            acc = d if acc is None else acc + d
    return [x1 + _rms(acc[rows] + _dot(a[rows], w_down_ref[chunk(n_chunks - 1), :]), g_post)
            for x1, rows in zip(x1_parts, parts)]


def _side_specs(jobs, steps, step_of):
    in_specs, out_specs, out_shape = [], [], []
    for kind, a, *rest in jobs:
        if kind == "cast":
            rows = a.shape[1] // steps
            in_specs.append(pl.BlockSpec((1, rows, a.shape[2]), lambda *ids, l=rest[0]: (l, step_of(*ids), 0)))
            out_specs.append(pl.BlockSpec((1, rows, a.shape[2]), lambda *ids: (0, step_of(*ids), 0)))
            out_shape.append(jax.ShapeDtypeStruct((1,) + a.shape[1:], BF16))
        else:
            n_l, n_s, nh, d = a.shape
            h = MEM_HEADS
            n = nh // h
            rows = n_l * n_s * n // steps
            parts = n // rows
            assert kind == "heads" and rows * parts == n

            def where(*ids, parts=parts, n_s=n_s):
                step = step_of(*ids)
                return step // parts // n_s, step // parts % n_s, step % parts

            in_specs.append(pl.BlockSpec((1, 1, rows * h, d), lambda *ids, w=where: w(*ids) + (0,)))
            out_specs.append(pl.BlockSpec((1, 1, h, rows, d),
                                          lambda *ids, w=where: w(*ids)[:2] + (0, w(*ids)[2], 0)))
            out_shape.append(jax.ShapeDtypeStruct((n_l, n_s, h, n, d), BF16))
    return in_specs, out_specs, out_shape


def _side_jobs(kinds, src_refs, dst_refs):
    for kind, src, dst in zip(kinds, src_refs, dst_refs):
        if kind == "cast":
            dst[...] = src[...].astype(BF16)
        else:
            x = src[0, 0].astype(BF16)
            heads, rows = dst.shape[2], dst.shape[3]
            row = lax.broadcasted_iota(jnp.int32, (rows, rows * heads), 0)
            pick = lax.broadcasted_iota(jnp.int32, (rows, rows * heads), 1) - heads * row
            for h in range(heads):
                dst[0, 0, h] = _dot((pick == h).astype(BF16), x).astype(BF16)


def _mem_project_kernel(mem_ref, g_ref, w_ref, *refs, side):
    n = len(side)
    mk_ref, mv_ref, mkh_ref, mvh_ref = refs[n:n + 4]
    _side_jobs(side, refs[:n], refs[n + 4:])
    xn = _unit_rms(mem_ref[0])
    for l in range(w_ref.shape[0]):
        kv = _dot((xn * g_ref[l:l + 1, :]).astype(BF16), w_ref[l].astype(BF16))
        for h in range(MEM_HEADS):
            mk = kv[:, h * MEM_HD:(h + 1) * MEM_HD]
            mv = kv[:, MEM_W + h * MEM_HD:MEM_W + (h + 1) * MEM_HD]
            mk_ref[l, 0, :, h, :] = mk
            mv_ref[l, 0, :, h, :] = mv
            mkh_ref[l, 0, h] = mk.astype(BF16)
            mvh_ref[l, 0, h] = mv.astype(BF16)


def _mem_project(mem, g_mem, w_mem_kv, side=()):
    b, n, _ = mem.shape
    depth = w_mem_kv.shape[0]
    out = jax.ShapeDtypeStruct((depth, b, n, MEM_HEADS, MEM_HD), F32)
    out_h = jax.ShapeDtypeStruct((depth, b, MEM_HEADS, n, MEM_HD), BF16)
    side_in, side_out, side_shape = _side_specs(side, b, lambda i: i)
    return pl.pallas_call(
        functools.partial(_mem_project_kernel, side=tuple(job[0] for job in side)),
        grid=(b,),
        in_specs=[pl.BlockSpec((1, n, D_MODEL), lambda i: (i, 0, 0)),
                  _const_spec(g_mem.shape), _const_spec(w_mem_kv.shape)] + side_in,
        out_specs=([pl.BlockSpec((depth, 1, n, MEM_HEADS, MEM_HD), lambda i: (0, i, 0, 0, 0))] * 2
                   + [pl.BlockSpec((depth, 1, MEM_HEADS, n, MEM_HD), lambda i: (0, i, 0, 0, 0))] * 2 + side_out),
        out_shape=[out, out, out_h, out_h] + side_shape,
        compiler_params=pltpu.CompilerParams(dimension_semantics=("arbitrary",), vmem_limit_bytes=VMEM_LIMIT),
        name="mem_project",
    )(mem, g_mem, w_mem_kv, *[job[1] for job in side])


def _layer0_kernel(x_ref, prev_ref, mk_ref, mv_ref, g_ref, w_in, w_out, w_pool, pscale, w_up, w_down, *refs,
                   nb, tt, pos_base, side):
    n = len(side)
    side_in, (y_ref, state_ref), side_out = refs[:n], refs[n:n + 2], refs[n + 2:2 * n + 2]
    halo_ref, = refs[2 * n + 2:]
    t = pl.program_id(1)

    @pl.when(t == 0)
    def _():
        halo_ref[...] = prev_ref[...]

    x = x_ref[...].reshape(nb * tt, D_MODEL)
    proj = jnp.concatenate([_dot(_rms(x[rows], _gain(g_ref, "mix_pre")).astype(BF16), w_in[...])
                            for rows in _row_parts(nb * tt)], axis=0)
    cats = []
    for b in range(nb):
        rows = slice(b * tt, (b + 1) * tt)
        u = proj[rows, :MIX_W]
        logits = _mem_logits(proj[rows, MIX_W:], mk_ref, b)
        mix = _pool_mixer(u, halo_ref[b], pos_base + t * tt, w_pool, pscale[...])
        halo_ref[b] = u[tt - HALO:, :]
        state_ref[0, b] = u[tt - POOL_BUF:, :]
        mem = _mem_values(_mem_probs(logits), mv_ref, b)
        cats.append(jnp.concatenate([mix, mem], axis=-1))
    cat = (cats[0] if nb == 1 else jnp.concatenate(cats, axis=0)).astype(BF16)
    x1 = [x[rows] + _rms(_dot(cat[rows], w_out[...]), _gain(g_ref, "mix_post")) for rows in _row_parts(nb * tt)]
    y = jnp.concatenate(_mlp(x1, _gain(g_ref, "mlp_pre"), _gain(g_ref, "mlp_post"), w_up, w_down), axis=0)
    y_ref[...] = y.reshape(nb, tt, D_MODEL)
    _side_jobs(side, side_in, side_out)


def _layer0(x, prev, mem_k, mem_v, p, *, nb, tt, pos_base, side=()):
    b, t, _ = x.shape
    grid = (b // nb, t // tt)
    tile = pl.BlockSpec((nb, tt, D_MODEL), lambda i, j: (i, j, 0))
    side_in, side_out, side_shape = _side_specs(side, grid[0] * grid[1], lambda i, j: i * grid[1] + j)
    return pl.pallas_call(
        functools.partial(_layer0_kernel, nb=nb, tt=tt, pos_base=pos_base, side=tuple(job[0] for job in side)),
        grid=grid,
        in_specs=[tile, pl.BlockSpec((nb, HALO, MIX_W), lambda i, j: (i, 0, 0)),
                  _mem_spec(mem_k, 0, nb), _mem_spec(mem_v, 0, nb),
                  _const_spec(p["gains"].shape), _layer_spec(p["w_in"][0], 0), _layer_spec(p["w_out"][0], 0),
                  _layer_spec(p["w_pool"], 0), _layer_spec(p["pool_scale"], 0),
                  _layer_spec(p["w_up"][0], 0), _layer_spec(p["w_down"][0], 0)] + side_in,
        out_specs=[tile, pl.BlockSpec((1, nb, POOL_BUF, MIX_W), lambda i, j: (0, i, 0, 0))] + side_out,
        out_shape=[jax.ShapeDtypeStruct((b, t, D_MODEL), F32),
                   jax.ShapeDtypeStruct((1, b, POOL_BUF, MIX_W), F32)] + side_shape,
        scratch_shapes=[pltpu.VMEM((nb, HALO, MIX_W), F32)],
        compiler_params=pltpu.CompilerParams(dimension_semantics=("arbitrary", "arbitrary"),
                                             vmem_limit_bytes=VMEM_LIMIT),
        name="layer0",
    )(x, prev, mem_k, mem_v, p["gains"], p["w_in"][0], p["w_out"][0], p["w_pool"], p["pool_scale"],
      p["w_up"][0], p["w_down"][0],
      *[job[1] for job in side])


def _bf16_part(x):
    return x.astype(BF16).astype(F32)


def _fold_lanes(h):
    lane = lax.broadcasted_iota(jnp.int32, (1, V7X_LANES), 1)
    own = (h % 2) * FOX_HD
    e0 = FOX_HD - own
    in_head = (lane >= own) & (lane < own + FOX_HD)
    first = ((lane >= e0) & (lane < e0 + 3)).astype(F32)
    second = ((lane >= e0 + 3) & (lane < e0 + 6)).astype(F32)
    return in_head, first, second, lane == e0, e0


def _fold_bias(c):
    cl = c * LOG2E
    hi = _bf16_part(cl)
    mid = _bf16_part(cl - hi)
    lo = _bf16_part(cl - hi - mid)
    lane = lax.broadcasted_iota(jnp.int32, (1, V7X_LANES), 1)
    piece = lax.rem(lane + (3 - FOLD_SRC % 3), 3)
    d = jnp.where(piece == 0, hi, jnp.where(piece == 1, mid, lo))
    return [pltpu.roll(d, (_fold_lanes(h)[4] - (FOLD_SRC + FOLD_W * h)) % V7X_LANES, axis=1)
            for h in range(FOX_HEADS)]


def _fold_kv(k, v, bias):
    ks, vs = [], []
    for h, dh in enumerate(bias):
        in_head, first, second, one, _ = _fold_lanes(h)
        cols = slice((h // 2) * PAIR_W, (h // 2 + 1) * PAIR_W)
        ks.append(jnp.where(in_head, k[:, cols], first - dh * second).astype(BF16))
        vs.append(jnp.where(in_head, v[:, cols], one.astype(F32)).T.astype(BF16))
    return jnp.concatenate(ks, axis=-1), jnp.concatenate(vs, axis=0)


def _fold_q(q, bias):
    qs = []
    for h, dh in enumerate(bias):
        in_head, first, second, _, _ = _fold_lanes(h)
        cols = slice((h // 2) * PAIR_W, (h // 2 + 1) * PAIR_W)
        qs.append(jnp.where(in_head, q[:, cols] * (FOX_HD ** -0.5 * LOG2E), dh * first + second).T.astype(BF16))
    return jnp.concatenate(qs, axis=0)


def _layer1_front_kernel(x_ref, mk_ref, mv_ref, g_ref, w_kv, w_f, w_ft, b_f, b_ft, w_in, *refs,
                         nb, tt, fold):
    if fold:
        k_ref, v_ref, logft_ref, mem_ref, qa_ref, ka_ref, va_ref, carry_ref = refs
    else:
        k_ref, v_ref, logft_ref, mem_ref, q_ref, kb_ref, vb_ref, c_ref, ct_ref = refs
    if fold:
        @pl.when(pl.program_id(1) == 0)
        def _():
            carry_ref[...] = jnp.zeros_like(carry_ref)

    xn = _unit_rms(x_ref[...].reshape(nb * tt, D_MODEL))
    hk = (xn * _gain(g_ref, "kv")).astype(BF16)
    logf = _log_sigmoid(_dot(hk, w_f[...]) + b_f[...])
    k = _dot(hk, w_kv[:, :MIX_W])
    v = _dot(hk, w_kv[:, MIX_W:])
    if fold:
        k_ref[0] = k.T
        v_ref[0] = v.T
    else:
        k_ref[...] = k.reshape(nb, tt, MIX_W)
        v_ref[...] = v.reshape(nb, tt, MIX_W)
    c = _cumsum(logf, 0, tt)
    if fold:
        c = c + carry_ref[...]
        carry_ref[...] = c[tt - 1:, :]
        bias = _fold_bias(c)
        ka_ref[0], va_ref[0, 0] = _fold_kv(k, v, bias)
        logft_ref[0] = logf.T[:FOX_HEADS, :]
    proj = _dot((xn * _gain(g_ref, "mix_pre", 1)).astype(BF16), w_in[...])
    q = proj[:, :MIX_W]
    mem_logits = [_mem_logits(proj[b * tt:(b + 1) * tt, MIX_W:], mk_ref, b) for b in range(nb)]
    if fold:
        qa_ref[0, 0] = _fold_q(q, bias)
    else:
        q_ref[...] = (q * (FOX_HD ** -0.5)).astype(BF16).reshape(nb, tt, MIX_W)
        kb_ref[...] = k.astype(BF16).reshape(nb, tt, MIX_W)
        vb_ref[...] = v.astype(BF16).reshape(nb, tt, MIX_W)
        logf_t = _log_sigmoid(_dot_nt(w_ft[...], hk) + b_ft[...])
        logft_ref[0] = logf_t
        ct_ref[0, 0] = _cumsum(logf_t, 1, tt)
    for b in range(nb):
        rows = slice(b * tt, (b + 1) * tt)
        if not fold:
            c_ref[b] = c[rows, :FOX_HEADS]
        mem_ref[b] = _mem_values(_mem_probs(mem_logits[b]), mv_ref, b).astype(BF16)


def _layer1_front(x, mem_k, mem_v, p, *, nb, tt, fold):
    b, t, _ = x.shape
    assert (nb == 1) if fold else (t == tt)
    tile = lambda w: pl.BlockSpec((nb, tt, w), lambda i, j: (i, j, 0))
    act = lambda w, dt: jax.ShapeDtypeStruct((b, t, w), dt)
    kv_spec = pl.BlockSpec((1, MIX_W, tt), lambda i, j: (i, 0, j)) if fold else tile(MIX_W)
    kv_shape = jax.ShapeDtypeStruct((b, MIX_W, t), F32) if fold else act(MIX_W, F32)
    out_specs = [kv_spec, kv_spec, pl.BlockSpec((1, FOX_HEADS, nb * tt), lambda i, j: (i, 0, j)), tile(MEM_W)]
    out_shape = [kv_shape, kv_shape, jax.ShapeDtypeStruct((b // nb, FOX_HEADS, nb * t), F32), act(MEM_W, BF16)]
    if fold:
        wf = FOX_HEADS * PAIR_W
        tile_t = pl.BlockSpec((1, 1, wf, tt), lambda i, j: (i, j, 0, 0))
        act_t = jax.ShapeDtypeStruct((b, t // tt, wf, tt), BF16)
        out_specs += [tile_t, tile(wf), tile_t]
        out_shape += [act_t, act(wf, BF16), act_t]
        scratch = [pltpu.VMEM((1, V7X_LANES), F32)]
    else:
        out_specs += [tile(MIX_W)] * 3 + [tile(FOX_HEADS),
                                          pl.BlockSpec((1, 1, FOX_HEADS, nb * tt), lambda i, j: (i, j, 0, 0))]
        out_shape += [act(MIX_W, BF16)] * 3 + [act(FOX_HEADS, F32),
                                               jax.ShapeDtypeStruct((b // nb, t // tt, FOX_HEADS, nb * tt), F32)]
        scratch = []
    return pl.pallas_call(
        functools.partial(_layer1_front_kernel, nb=nb, tt=tt, fold=fold),
        grid=(b // nb, t // tt),
        in_specs=[tile(D_MODEL), _mem_spec(mem_k, 1, nb), _mem_spec(mem_v, 1, nb),
                  _const_spec(p["gains"].shape),
                  _const_spec(p["w_kv"].shape), _const_spec(p["w_f"].shape), _const_spec(p["w_ft"].shape),
                  _const_spec(p["b_f"].shape), _const_spec(p["b_ft"].shape), _layer_spec(p["w_in"][1], 0)],
        out_specs=out_specs,
        out_shape=out_shape,
        scratch_shapes=scratch,
        compiler_params=pltpu.CompilerParams(dimension_semantics=("arbitrary", "arbitrary"),
                                             vmem_limit_bytes=VMEM_LIMIT),
        name="layer1_front",
    )(x, mem_k, mem_v, p["gains"], p["w_kv"], p["w_f"], p["w_ft"], p["b_f"], p["b_ft"], p["w_in"][1])


def _fox_prompt_kernel(qt_ref, k_ref, vt_ref, o_ref, m_ref, acc_ref, *s_refs, t):
    i = pl.program_id(1)
    half = t // 2
    sub = lax.broadcasted_iota(jnp.int32, (PAIR_W, 1), 0)
    col = [slice(h * PAIR_W, (h + 1) * PAIR_W) for h in range(FOX_HEADS)]
    qt = [qt_ref[0, 0, c, :] for c in col]
    slot = lambda n: s_refs[n % len(s_refs)]

    def chain(n_items, logits, update):
        top = [logits(n) for n in range(FOX_AHEAD)]
        for n in range(n_items):
            if n + FOX_AHEAD < n_items:
                top.append(logits(n + FOX_AHEAD))
            update(n, top[n])

    tri = (lax.broadcasted_iota(jnp.int32, (half, half), 0) <= lax.broadcasted_iota(jnp.int32, (half, half), 1))
    lo = pl.ds(pl.multiple_of(i * t, t), half)
    hi = pl.ds(pl.multiple_of(i * t + half, half), half)

    def diag_logits(h):
        s_lo = _dot(k_ref[0, lo, col[h]], qt[h])
        s_hi = _dot(k_ref[0, hi, col[h]], qt[h][:, half:])
        s_lo = jnp.concatenate([jnp.where(tri, s_lo[:, :half], NEG_BIG), s_lo[:, half:]], axis=1)
        s_hi = jnp.where(tri, s_hi, NEG_BIG)
        slot(h)[:half, :] = s_lo
        slot(h)[half:, half:] = s_hi
        top = jnp.max(s_lo, axis=0, keepdims=True)
        return jnp.concatenate([top[:, :half], jnp.maximum(top[:, half:], jnp.max(s_hi, axis=0, keepdims=True))],
                               axis=1)

    def diag_update(h, m):
        p_lo = jnp.exp2(slot(h)[:half, :] - m).astype(BF16)
        p_hi = jnp.exp2(slot(h)[half:, half:] - m[:, half:]).astype(BF16)
        acc = _dot(vt_ref[0, i, col[h], :half], p_lo)
        acc_ref[h] = jnp.concatenate([acc[:, :half], acc[:, half:] + _dot(vt_ref[0, i, col[h], half:], p_hi)], axis=1)
        m_ref[h] = m

    chain(FOX_HEADS, diag_logits, diag_update)

    def full_tiles(j0, n_tiles):
        def logits(n):
            rows = pl.ds(pl.multiple_of((j0 + n // FOX_HEADS) * t, t), t)
            s = _dot(k_ref[0, rows, col[n % FOX_HEADS]], qt[n % FOX_HEADS])
            slot(n)[...] = s
            return jnp.max(s, axis=0, keepdims=True)

        def update(n, top):
            h = n % FOX_HEADS
            m = m_ref[h]
            m_new = jnp.maximum(m, top)
            p = jnp.exp2(slot(n)[...] - m_new).astype(BF16)
            acc_ref[h] = jnp.exp2(m - m_new) * acc_ref[h] + _dot(vt_ref[0, j0 + n // FOX_HEADS, col[h], :], p)
            m_ref[h] = m_new

        chain(n_tiles * FOX_HEADS, logits, update)

    lax.fori_loop(0, i // 2, lambda jj, c: full_tiles(2 * jj, 2), None)
    lax.fori_loop(2 * (i // 2), i, lambda j, c: full_tiles(j, 1), None)
    for pair in range(FOX_HEADS // 2):
        even, odd = acc_ref[2 * pair], acc_ref[2 * pair + 1]
        own = sub < FOX_HD
        out = jnp.where(own, even, odd) / jnp.where(own, even[FOX_HD:FOX_HD + 1, :], odd[0:1, :])
        o_ref[0, :, pair * PAIR_W:(pair + 1) * PAIR_W] = out.T.astype(BF16)


def _fox_prompt(qt, ka, vt):
    b, n_tiles, w, tile = qt.shape
    per_stream = lambda shape: pl.BlockSpec((1,) + shape, lambda i, j: (i,) + (0,) * len(shape))
    return pl.pallas_call(
        functools.partial(_fox_prompt_kernel, t=tile),
        grid=(b, n_tiles),
        in_specs=[pl.BlockSpec((1, 1, w, tile), lambda i, j: (i, j, 0, 0)),
                  per_stream((n_tiles * tile, w)), per_stream((n_tiles, w, tile))],
        out_specs=pl.BlockSpec((1, tile, MIX_W), lambda i, j: (i, j, 0)),
        out_shape=jax.ShapeDtypeStruct((b, n_tiles * tile, MIX_W), BF16),
        scratch_shapes=([pltpu.VMEM((FOX_HEADS, 1, tile), F32), pltpu.VMEM((FOX_HEADS, PAIR_W, tile), F32)]
                        + [pltpu.VMEM((tile, tile), F32)] * (FOX_AHEAD + 1)),
        compiler_params=pltpu.CompilerParams(dimension_semantics=("arbitrary", "arbitrary"),
                                             vmem_limit_bytes=VMEM_LIMIT),
        name="fox_prompt",
    )(qt, ka, vt)


def _fox_sample_kernel(q_ref, kc_ref, vc_ref, kn_ref, vn_ref, lfc_t_ref, c_ref, ct_ref, o_ref):
    tn = q_ref.shape[1]
    p = lfc_t_ref.shape[2]
    cc = _cumsum(lfc_t_ref[0], 1, p)
    ck_cache = cc - cc[:, p - 1:]
    causal = (lax.broadcasted_iota(jnp.int32, (tn, V7X_LANES), 1)
              <= lax.broadcasted_iota(jnp.int32, (tn, V7X_LANES), 0))
    pad = jnp.zeros((V7X_LANES - tn, FOX_HD), BF16)
    col = [slice(h * FOX_HD, (h + 1) * FOX_HD) for h in range(FOX_HEADS)]
    logits = []
    for h in range(FOX_HEADS):
        qh = q_ref[0, :, col[h]]
        cq = c_ref[0, :, h:h + 1]
        kn = jnp.concatenate([kn_ref[0, :, col[h]], pad], axis=0)
        s_old = _dot(qh, kc_ref[0, h].astype(BF16)) + cq - ck_cache[h:h + 1, :]
        s_new = jnp.where(causal, _dot_nt(qh, kn) + cq - ct_ref[0, h:h + 1, :], NEG_BIG)
        logits.append((s_old, s_new))
    probs = []
    for s_old, s_new in logits:
        m = jnp.maximum(jnp.max(s_old, axis=-1, keepdims=True), jnp.max(s_new, axis=-1, keepdims=True))
        p_old = jnp.exp(s_old - m)
        p_new = jnp.exp(s_new - m)
        l = jnp.sum(p_old, axis=-1, keepdims=True) + jnp.sum(p_new, axis=-1, keepdims=True)
        probs.append((p_old.astype(BF16), p_new.astype(BF16), l))
    outs = []
    for h, (p_old, p_new, l) in enumerate(probs):
        vn = jnp.concatenate([vn_ref[0, :, col[h]], pad], axis=0)
        outs.append((_dot_nt(p_old, vc_ref[0, h].astype(BF16)) + _dot(p_new, vn)) / l)
    o_ref[0] = jnp.concatenate(outs, axis=-1).astype(BF16)


def _fox_sample(q, cache_kt, cache_vt, kb, vb, cache_logf_t, c, ct):
    b, tn, _ = q.shape
    p = cache_kt.shape[3]
    new = lambda w: pl.BlockSpec((1, tn, w), lambda i: (i, 0, 0))
    old = pl.BlockSpec((1, FOX_HEADS, FOX_HD, p), lambda i: (i, 0, 0, 0))
    return pl.pallas_call(
        _fox_sample_kernel,
        grid=(b,),
        in_specs=[new(MIX_W), old, old, new(MIX_W), new(MIX_W),
                  pl.BlockSpec((1, FOX_HEADS, p), lambda i: (i, 0, 0)),
                  new(FOX_HEADS), pl.BlockSpec((1, FOX_HEADS, V7X_LANES), lambda i: (i, 0, 0))],
        out_specs=new(MIX_W),
        out_shape=jax.ShapeDtypeStruct((b, tn, MIX_W), BF16),
        compiler_params=pltpu.CompilerParams(dimension_semantics=("arbitrary",), vmem_limit_bytes=VMEM_LIMIT),
        name="fox_sample",
    )(q, cache_kt, cache_vt, kb, vb, cache_logf_t, c, ct)


def _layer1_back_kernel(x_ref, mix_ref, mem_ref, g_ref, w_out, w_up, w_down, *refs, nb, tt, side):
    n = len(side)
    y_ref = refs[n]
    m = nb * tt
    x = x_ref[...].reshape(m, D_MODEL)
    mix = mix_ref[...].reshape(m, MIX_W)
    mem = mem_ref[...].reshape(m, MEM_W)
    x1 = [x[rows] + _rms(_dot(mix[rows], w_out[:MIX_W, :]) + _dot(mem[rows], w_out[MIX_W:, :]),
                         _gain(g_ref, "mix_post", 1)) for rows in _row_parts(m)]
    y = jnp.concatenate(_mlp(x1, _gain(g_ref, "mlp_pre", 1), _gain(g_ref, "mlp_post", 1), w_up, w_down), axis=0)
    y_ref[...] = y.reshape(nb, tt, D_MODEL)
    _side_jobs(side, refs[:n], refs[n + 1:])


def _layer1_back(x, mix, mem, p, *, nb, tt, side=()):
    b, t, _ = x.shape
    grid = (b // nb, t // tt)
    tile = lambda w: pl.BlockSpec((nb, tt, w), lambda i, j: (i, j, 0))
    side_in, side_out, side_shape = _side_specs(side, grid[0] * grid[1], lambda i, j: i * grid[1] + j)
    return pl.pallas_call(
        functools.partial(_layer1_back_kernel, nb=nb, tt=tt, side=tuple(job[0] for job in side)),
        grid=grid,
        in_specs=[tile(D_MODEL), tile(MIX_W), tile(MEM_W), _const_spec(p["gains"].shape),
                  _layer_spec(p["w_out"][1], 0), _layer_spec(p["w_up"][1], 0), _layer_spec(p["w_down"][1], 0)]
                 + side_in,
        out_specs=[tile(D_MODEL)] + side_out,
        out_shape=[jax.ShapeDtypeStruct((b, t, D_MODEL), F32)] + side_shape,
        compiler_params=pltpu.CompilerParams(dimension_semantics=("arbitrary", "arbitrary"),
                                             vmem_limit_bytes=VMEM_LIMIT),
        name="layer1_back",
    )(x, mix, mem, p["gains"], p["w_out"][1], p["w_up"][1], p["w_down"][1],
      *[job[1] for job in side])


def _token_tile(b, t):
    if t >= TOKEN_TILE:
        assert t % TOKEN_TILE == 0
        return 1, TOKEN_TILE
    return b, t


def _trunk(x, pool_prev, cache, mem_k, mem_v, p, later_f32=None, side=((), ())):
    b, t, _ = x.shape
    nb, tt = _token_tile(b, t)
    pos_base = 0 if cache is None else POOL_BUF
    names = sorted(later_f32) if later_f32 else []
    y0, pool_state, *done = _layer0(x, pool_prev, mem_k, mem_v, p, nb=nb, tt=tt, pos_base=pos_base,
                                    side=[("cast", later_f32[n], 1) for n in names] + list(side[0]))
    p = {**p, **{n: [p[n][0], w] for n, w in zip(names, done)}}
    done_first = done[len(names):]
    k, v, logf_t, mem, *att = _layer1_front(y0, mem_k, mem_v, p, nb=nb, tt=tt, fold=cache is None)
    logf = jnp.transpose(logf_t.reshape(b // nb, FOX_HEADS, nb, t), (0, 2, 3, 1)).reshape(b, t, FOX_HEADS)
    if cache is None:
        mix = _fox_prompt(*att)
    else:
        q, kb, vb, c, ct = att
        ct_new = jnp.swapaxes(ct.reshape(FOX_HEADS, b, t), 0, 1)
        ct_new = jnp.pad(ct_new, ((0, 0), (0, 0), (0, V7X_LANES - t)))
        mix = _fox_sample(q, *cache[:2], kb, vb, cache[2], c, ct_new)
    y, *done_last = _layer1_back(y0, mix, mem, p, nb=nb, tt=tt, side=side[1])
    if cache is None:
        heads = lambda a: jnp.transpose(a.reshape(b, FOX_HEADS, FOX_HD, t), (0, 3, 1, 2))
    else:
        heads = lambda a: a.reshape(b, t, FOX_HEADS, FOX_HD)
    return (y, pool_state, heads(k), heads(v), logf), p, done_first, done_last


def kernel(x_prompt, x_sample, cache_pool, cache_k, cache_v, cache_logf, cache_mem_k, cache_mem_v, mem_prompt,
           g_mix_pre, g_mix_post, g_mlp_pre, g_mlp_post, w_in, w_out, w_pool, pool_scale, g_kv, w_kvf, b_f,
           g_mem, w_mem_kv, w_up, w_down):
    depth = w_in.shape[0]
    assert depth == 2 and w_pool.shape[0] == 1, "one pooling layer followed by one forgetting-attention layer"
    bp = x_prompt.shape[0]
    w_f = w_kvf[:, 2 * MIX_W:]
    fold_cols = lambda a: jnp.pad(jnp.concatenate([a, jnp.repeat(a, FOLD_W, axis=-1)], axis=-1),
                                  [(0, 0)] * (a.ndim - 1) + [(0, V7X_LANES - FOLD_SRC - FOLD_W * FOX_HEADS)])
    rows = lambda g: g.reshape(g.shape[0], 1, g.shape[1])
    p = dict(
        gains=_pack_gains(g_mix_pre, g_mix_post, g_mlp_pre, g_mlp_post, g_kv), pool_scale=rows(pool_scale),
        w_pool=w_pool.astype(BF16), w_kv=w_kvf[:, :2 * MIX_W].astype(BF16),
        w_f=fold_cols(w_f).astype(BF16), w_ft=w_f.T.astype(BF16),
        b_f=fold_cols(b_f).reshape(1, V7X_LANES), b_ft=b_f.reshape(FOX_HEADS, 1),
    )
    later_f32 = dict(w_in=w_in, w_out=w_out, w_up=w_up, w_down=w_down)
    names = sorted(later_f32)
    mem_k_prompt, mem_v_prompt, mkh, mvh, *w0 = _mem_project(mem_prompt, g_mem, w_mem_kv,
                                                             side=[("cast", later_f32[n], 0) for n in names])
    p.update({n: [w, None] for n, w in zip(names, w0)})

    heads_view = lambda a: a.reshape(a.shape[:2] + (-1, a.shape[-1]))
    (y_p, pool_p, k_p, v_p, logf_p), p, (cmk,), (cmv,) = _trunk(
        x_prompt, jnp.zeros((bp, HALO, MIX_W), F32), None, mkh, mvh, p, later_f32,
        side=([("heads", heads_view(cache_mem_k))], [("heads", heads_view(cache_mem_v))]))

    pool_prev = jnp.pad(cache_pool[0], ((0, 0), (HALO - POOL_BUF, 0), (0, 0)))
    lanes_view = lambda a: jnp.transpose(a, (0, 2, 3, 1))
    cache = (lanes_view(cache_k), lanes_view(cache_v), jnp.swapaxes(cache_logf, 1, 2))
    (y_s, pool_s, k_s, v_s, logf_s), _, _, _ = _trunk(x_sample, pool_prev, cache, cmk, cmv, p)

    return (y_p, y_s, pool_p, pool_s, k_p, v_p, logf_p, k_s, v_s, logf_s, mem_k_prompt, mem_v_prompt)
```

```python
import functools
from typing import NamedTuple

import jax
import jax.numpy as jnp
from jax import lax
from jax.experimental import pallas as pl
from jax.experimental.pallas import tpu as pltpu

F32 = jnp.float32
BF16 = jnp.bfloat16

D_MODEL = 1024
MIX_W = D_MODEL // 2
MEM_W = D_MODEL - MIX_W
POOL_WINDOWS = (2, 4, 8, 16)
POOL_GROUP = MIX_W // len(POOL_WINDOWS)
POOL_BUF = max(POOL_WINDOWS) - 1
FOX_HEADS = 8
FOX_HD = MIX_W // FOX_HEADS
MEM_HEADS = 4
MEM_HD = MEM_W // MEM_HEADS
D_FF = 4 * D_MODEL
EPS = 1e-6

V7X_LANES = 128
V7X_SUBLANES_F32 = 8
V7X_VMEM_BYTES = 64 * 1024 * 1024

HALO = 2 * V7X_SUBLANES_F32
FF_CHUNK = 1024
TOKEN_TILE = 512
ROW_PARTS = 2
PAIR_W = 2 * FOX_HD
NEG_BIG = -1e30
LOG2E = 1.4426950408889634
FOLD_SRC = FOX_HEADS
FOLD_W = 6
FOX_AHEAD = 2
assert FOLD_SRC + FOLD_W * FOX_HEADS <= V7X_LANES
VMEM_LIMIT = V7X_VMEM_BYTES - 8 * 1024 * 1024

assert HALO >= POOL_BUF and PAIR_W == V7X_LANES and POOL_GROUP == V7X_LANES and MEM_HD == V7X_LANES


def _const_spec(shape):
    zeros = (0,) * len(shape)
    return pl.BlockSpec(shape, lambda *_: zeros, pipeline_mode=pl.Buffered(1))


def _layer_spec(stacked, layer):
    index = (layer,) + (0,) * (stacked.ndim - 1)
    return pl.BlockSpec((None,) + stacked.shape[1:], lambda *_: index, pipeline_mode=pl.Buffered(1))


def _mem_spec(mem, layer, nb):
    return pl.BlockSpec((None, nb) + mem.shape[2:], lambda i, j: (layer, i, 0, 0, 0))


def _unit_rms(x):
    return x * lax.rsqrt(jnp.mean(x * x, axis=-1, keepdims=True) + EPS)


def _rms(x, g):
    return _unit_rms(x) * g


GAIN_KINDS = ("mix_pre", "mix_post", "mlp_pre", "mlp_post")


def _gain(g_ref, kind, layer=0):
    depth = (g_ref.shape[0] - 1) // len(GAIN_KINDS)
    r = len(GAIN_KINDS) * depth if kind == "kv" else GAIN_KINDS.index(kind) * depth + layer
    return g_ref[r:r + 1, :]


def _pack_gains(g_mix_pre, g_mix_post, g_mlp_pre, g_mlp_post, g_kv):
    return jnp.concatenate([g_mix_pre, g_mix_post, g_mlp_pre, g_mlp_post, g_kv[None, :]], axis=0)


def _dot(a, b):
    return jnp.dot(a, b, preferred_element_type=F32)


def _dot_nt(a, b):
    return lax.dot_general(a, b, (((1,), (1,)), ((), ())), preferred_element_type=F32)


def _log_sigmoid(x):
    return jnp.minimum(x, 0.0) - jnp.log1p(jnp.exp(-jnp.abs(x)))


def _cumsum(x, axis, seg):
    assert seg & (seg - 1) == 0 and x.shape[axis] % seg == 0
    n = x.shape[axis]
    vreg = (V7X_SUBLANES_F32, V7X_LANES)[axis]
    idx = lax.broadcasted_iota(jnp.int32, x.shape, axis) & (seg - 1)
    k = 1
    while k < seg:
        if seg == n and k % vreg == 0:
            zeros = jnp.zeros(x.shape[:axis] + (k,) + x.shape[axis + 1:], x.dtype)
            x = x + jnp.concatenate([zeros, lax.slice_in_dim(x, 0, n - k, axis=axis)], axis=axis)
        else:
            x = x + jnp.where(idx >= k, pltpu.roll(x, k, axis=axis), 0.0)
        k *= 2
    return x


def _mem_logits(q, mk_ref, b):
    qs = (q * (MEM_HD ** -0.5 * LOG2E)).astype(BF16)
    return [_dot_nt(qs[:, h * MEM_HD:(h + 1) * MEM_HD], mk_ref[b, h]) for h in range(MEM_HEADS)]


def _mem_probs(logits):
    probs = []
    for s in logits:
        p = jnp.exp2(s - jnp.max(s, axis=-1, keepdims=True))
        probs.append((p.astype(BF16), jnp.sum(p, axis=-1, keepdims=True)))
    return probs


def _mem_values(probs, mv_ref, b):
    return jnp.concatenate([_dot(p, mv_ref[b, h]) / l for h, (p, l) in enumerate(probs)], axis=-1)


def _pool_mixer(u, halo, pos0, w_pool_ref, scale):
    tt = u.shape[0]
    ext = jnp.concatenate([halo, u], axis=0)
    pos = pos0 + lax.broadcasted_iota(jnp.int32, (tt, 1), 0)
    outs = []
    for g, w in enumerate(POOL_WINDOWS):
        sl = slice(g * POOL_GROUP, (g + 1) * POOL_GROUP)
        s = ext[:, sl]
        k = 1
        while k < w:
            s = s + pltpu.roll(s, k, axis=0)
            k *= 2
        cnt = jnp.minimum(pos + 1, w).astype(F32)
        pooled = s[HALO:] / cnt - u[:, sl]
        outs.append(_dot(pooled.astype(BF16), w_pool_ref[g]))
    return jnp.concatenate(outs, axis=-1) * scale


def _row_parts(m):
    n = ROW_PARTS if m >= TOKEN_TILE else 1
    return [slice(i * m // n, (i + 1) * m // n) for i in range(n)]


def _mlp(x1_parts, g_pre, g_post, w_up_ref, w_down_ref):
    n_chunks = D_FF // FF_CHUNK
    chunk = lambda c: slice(c * FF_CHUNK, (c + 1) * FF_CHUNK)
    act = lambda up: jnp.square(jnp.maximum(up, 0.0)).astype(BF16)
    hm_parts = [_rms(x1, g_pre).astype(BF16) for x1 in x1_parts]
    hm = jnp.concatenate(hm_parts, axis=0)
    parts = _row_parts(hm.shape[0])
    acc = None
    for c in range(n_chunks):
        if c == 0:
            a = jnp.concatenate([act(_dot(h, w_up_ref[:, chunk(c)])) for h in hm_parts], axis=0)
        else:
            a = act(_dot(hm, w_up_ref[:, chunk(c)]))
        if c < n_chunks - 1:
            d = _dot(a, w_down_ref[chunk(c), :])
            acc = d if acc is None else acc + d
    return [x1 + _rms(acc[rows] + _dot(a[rows], w_down_ref[chunk(n_chunks - 1), :]), g_post)
            for x1, rows in zip(x1_parts, parts)]


def _side_specs(jobs, steps, step_of):
    in_specs, out_specs, out_shape = [], [], []
    for kind, a, *rest in jobs:
        if kind == "cast":
            rows = a.shape[1] // steps
            in_specs.append(pl.BlockSpec((1, rows, a.shape[2]), lambda *ids, l=rest[0]: (l, step_of(*ids), 0)))
            out_specs.append(pl.BlockSpec((1, rows, a.shape[2]), lambda *ids: (0, step_of(*ids), 0)))
            out_shape.append(jax.ShapeDtypeStruct((1,) + a.shape[1:], BF16))
        else:
            n_l, n_s, nh, d = a.shape
            h = MEM_HEADS
            n = nh // h
            rows = n_l * n_s * n // steps
            parts = n // rows
            assert kind == "heads" and rows * parts == n

            def where(*ids, parts=parts, n_s=n_s):
                step = step_of(*ids)
                return step // parts // n_s, step // parts % n_s, step % parts

            in_specs.append(pl.BlockSpec((1, 1, rows * h, d), lambda *ids, w=where: w(*ids) + (0,)))
            out_specs.append(pl.BlockSpec((1, 1, h, rows, d),
                                          lambda *ids, w=where: w(*ids)[:2] + (0, w(*ids)[2], 0)))
            out_shape.append(jax.ShapeDtypeStruct((n_l, n_s, h, n, d), BF16))
    return in_specs, out_specs, out_shape


def _side_jobs(kinds, src_refs, dst_refs):
    for kind, src, dst in zip(kinds, src_refs, dst_refs):
        if kind == "cast":
            dst[...] = src[...].astype(BF16)
        else:
            x = src[0, 0].astype(BF16)
            heads, rows = dst.shape[2], dst.shape[3]
            row = lax.broadcasted_iota(jnp.int32, (rows, rows * heads), 0)
            pick = lax.broadcasted_iota(jnp.int32, (rows, rows * heads), 1) - heads * row
            for h in range(heads):
                dst[0, 0, h] = _dot((pick == h).astype(BF16), x).astype(BF16)


def _mem_project_kernel(mem_ref, g_ref, w_ref, *refs, side):
    n = len(side)
    mk_ref, mv_ref, mkh_ref, mvh_ref = refs[n:n + 4]
    _side_jobs(side, refs[:n], refs[n + 4:])
    xn = _unit_rms(mem_ref[0])
    for l in range(w_ref.shape[0]):
        kv = _dot((xn * g_ref[l:l + 1, :]).astype(BF16), w_ref[l].astype(BF16))
        for h in range(MEM_HEADS):
            mk = kv[:, h * MEM_HD:(h + 1) * MEM_HD]
            mv = kv[:, MEM_W + h * MEM_HD:MEM_W + (h + 1) * MEM_HD]
            mk_ref[l, 0, :, h, :] = mk
            mv_ref[l, 0, :, h, :] = mv
            mkh_ref[l, 0, h] = mk.astype(BF16)
            mvh_ref[l, 0, h] = mv.astype(BF16)


def _mem_project(mem, g_mem, w_mem_kv, side=()):
    b, n, _ = mem.shape
    depth = w_mem_kv.shape[0]
    out = jax.ShapeDtypeStruct((depth, b, n, MEM_HEADS, MEM_HD), F32)
    out_h = jax.ShapeDtypeStruct((depth, b, MEM_HEADS, n, MEM_HD), BF16)
    side_in, side_out, side_shape = _side_specs(side, b, lambda i: i)
    return pl.pallas_call(
        functools.partial(_mem_project_kernel, side=tuple(job[0] for job in side)),
        grid=(b,),
        in_specs=[pl.BlockSpec((1, n, D_MODEL), lambda i: (i, 0, 0)),
                  _const_spec(g_mem.shape), _const_spec(w_mem_kv.shape)] + side_in,
        out_specs=([pl.BlockSpec((depth, 1, n, MEM_HEADS, MEM_HD), lambda i: (0, i, 0, 0, 0))] * 2
                   + [pl.BlockSpec((depth, 1, MEM_HEADS, n, MEM_HD), lambda i: (0, i, 0, 0, 0))] * 2 + side_out),
        out_shape=[out, out, out_h, out_h] + side_shape,
        compiler_params=pltpu.CompilerParams(dimension_semantics=("arbitrary",), vmem_limit_bytes=VMEM_LIMIT),
        name="mem_project",
    )(mem, g_mem, w_mem_kv, *[job[1] for job in side])


def _layer0_kernel(x_ref, prev_ref, mk_ref, mv_ref, g_ref, w_in, w_out, w_pool, pscale, w_up, w_down, *refs,
                   nb, tt, pos_base, side):
    n = len(side)
    side_in, (y_ref, state_ref), side_out = refs[:n], refs[n:n + 2], refs[n + 2:2 * n + 2]
    halo_ref, = refs[2 * n + 2:]
    t = pl.program_id(1)

    @pl.when(t == 0)
    def _():
        halo_ref[...] = prev_ref[...]

    x = x_ref[...].reshape(nb * tt, D_MODEL)
    proj = jnp.concatenate([_dot(_rms(x[rows], _gain(g_ref, "mix_pre")).astype(BF16), w_in[...])
                            for rows in _row_parts(nb * tt)], axis=0)
    cats = []
    for b in range(nb):
        rows = slice(b * tt, (b + 1) * tt)
        u = proj[rows, :MIX_W]
        logits = _mem_logits(proj[rows, MIX_W:], mk_ref, b)
        mix = _pool_mixer(u, halo_ref[b], pos_base + t * tt, w_pool, pscale[...])
        halo_ref[b] = u[tt - HALO:, :]
        state_ref[0, b] = u[tt - POOL_BUF:, :]
        mem = _mem_values(_mem_probs(logits), mv_ref, b)
        cats.append(jnp.concatenate([mix, mem], axis=-1))
    cat = (cats[0] if nb == 1 else jnp.concatenate(cats, axis=0)).astype(BF16)
    x1 = [x[rows] + _rms(_dot(cat[rows], w_out[...]), _gain(g_ref, "mix_post")) for rows in _row_parts(nb * tt)]
    y = jnp.concatenate(_mlp(x1, _gain(g_ref, "mlp_pre"), _gain(g_ref, "mlp_post"), w_up, w_down), axis=0)
    y_ref[...] = y.reshape(nb, tt, D_MODEL)
    _side_jobs(side, side_in, side_out)


class _CallIO(NamedTuple):
    in_specs: list
    args: list
    out_specs: list
    out_shape: list
    scratch: list


def _tile_call(kernel, name, x, nb, tt, io):
    return pl.pallas_call(
        kernel,
        grid=(x.shape[0] // nb, x.shape[1] // tt),
        in_specs=io.in_specs,
        out_specs=io.out_specs,
        out_shape=io.out_shape,
        scratch_shapes=io.scratch,
        compiler_params=pltpu.CompilerParams(dimension_semantics=("arbitrary", "arbitrary"),
                                             vmem_limit_bytes=VMEM_LIMIT),
        name=name,
    )(*io.args)


def _layer0_io(x, prev, mem_k, mem_v, p, *, nb, tt, side):
    b, t, _ = x.shape
    grid = (b // nb, t // tt)
    tile = pl.BlockSpec((nb, tt, D_MODEL), lambda i, j: (i, j, 0))
    side_in, side_out, side_shape = _side_specs(side, grid[0] * grid[1], lambda i, j: i * grid[1] + j)
    return _CallIO(
        in_specs=[tile, pl.BlockSpec((nb, HALO, MIX_W), lambda i, j: (i, 0, 0)),
                  _mem_spec(mem_k, 0, nb), _mem_spec(mem_v, 0, nb),
                  _const_spec(p["gains"].shape), _layer_spec(p["w_in"][0], 0), _layer_spec(p["w_out"][0], 0),
                  _layer_spec(p["w_pool"], 0), _layer_spec(p["pool_scale"], 0),
                  _layer_spec(p["w_up"][0], 0), _layer_spec(p["w_down"][0], 0)] + side_in,
        args=[x, prev, mem_k, mem_v, p["gains"], p["w_in"][0], p["w_out"][0], p["w_pool"], p["pool_scale"],
              p["w_up"][0], p["w_down"][0]] + [job[1] for job in side],
        out_specs=[tile, pl.BlockSpec((1, nb, POOL_BUF, MIX_W), lambda i, j: (0, i, 0, 0))] + side_out,
        out_shape=[jax.ShapeDtypeStruct((b, t, D_MODEL), F32),
                   jax.ShapeDtypeStruct((1, b, POOL_BUF, MIX_W), F32)] + side_shape,
        scratch=[pltpu.VMEM((nb, HALO, MIX_W), F32)])


def _layer0(x, prev, mem_k, mem_v, p, *, nb, tt, pos_base, side=()):
    return _tile_call(
        functools.partial(_layer0_kernel, nb=nb, tt=tt, pos_base=pos_base, side=tuple(job[0] for job in side)),
        "layer0", x, nb, tt, _layer0_io(x, prev, mem_k, mem_v, p, nb=nb, tt=tt, side=side))


def _bf16_part(x):
    return x.astype(BF16).astype(F32)


def _fold_lanes(h):
    lane = lax.broadcasted_iota(jnp.int32, (1, V7X_LANES), 1)
    own = (h % 2) * FOX_HD
    e0 = FOX_HD - own
    in_head = (lane >= own) & (lane < own + FOX_HD)
    first = ((lane >= e0) & (lane < e0 + 3)).astype(F32)
    second = ((lane >= e0 + 3) & (lane < e0 + 6)).astype(F32)
    return in_head, first, second, lane == e0, e0


def _fold_bias(c):
    cl = c * LOG2E
    hi = _bf16_part(cl)
    mid = _bf16_part(cl - hi)
    lo = _bf16_part(cl - hi - mid)
    lane = lax.broadcasted_iota(jnp.int32, (1, V7X_LANES), 1)
    piece = lax.rem(lane + (3 - FOLD_SRC % 3), 3)
    d = jnp.where(piece == 0, hi, jnp.where(piece == 1, mid, lo))
    return [pltpu.roll(d, (_fold_lanes(h)[4] - (FOLD_SRC + FOLD_W * h)) % V7X_LANES, axis=1)
            for h in range(FOX_HEADS)]


def _fold_kv(k, v, bias):
    ks, vs = [], []
    for h, dh in enumerate(bias):
        in_head, first, second, one, _ = _fold_lanes(h)
        cols = slice((h // 2) * PAIR_W, (h // 2 + 1) * PAIR_W)
        ks.append(jnp.where(in_head, k[:, cols], first - dh * second).astype(BF16))
        vs.append(jnp.where(in_head, v[:, cols], one.astype(F32)).T.astype(BF16))
    return jnp.concatenate(ks, axis=-1), jnp.concatenate(vs, axis=0)


def _fold_q(q, bias):
    qs = []
    for h, dh in enumerate(bias):
        in_head, first, second, _, _ = _fold_lanes(h)
        cols = slice((h // 2) * PAIR_W, (h // 2 + 1) * PAIR_W)
        qs.append(jnp.where(in_head, q[:, cols] * (FOX_HD ** -0.5 * LOG2E), dh * first + second).T.astype(BF16))
    return jnp.concatenate(qs, axis=0)


def _layer1_front_kernel(x_ref, mk_ref, mv_ref, g_ref, w_kv, w_f, w_ft, b_f, b_ft, w_in, *refs,
                         nb, tt, fold):
    if fold:
        k_ref, v_ref, logft_ref, mem_ref, qa_ref, ka_ref, va_ref, carry_ref = refs
    else:
        k_ref, v_ref, logft_ref, mem_ref, q_ref, kb_ref, vb_ref, c_ref, ct_ref = refs
    if fold:
        @pl.when(pl.program_id(1) == 0)
        def _():
            carry_ref[...] = jnp.zeros_like(carry_ref)

    xn = _unit_rms(x_ref[...].reshape(nb * tt, D_MODEL))
    hk = (xn * _gain(g_ref, "kv")).astype(BF16)
    logf = _log_sigmoid(_dot(hk, w_f[...]) + b_f[...])
    k = _dot(hk, w_kv[:, :MIX_W])
    v = _dot(hk, w_kv[:, MIX_W:])
    if fold:
        k_ref[0] = k.T
        v_ref[0] = v.T
    else:
        k_ref[...] = k.reshape(nb, tt, MIX_W)
        v_ref[...] = v.reshape(nb, tt, MIX_W)
    c = _cumsum(logf, 0, tt)
    if fold:
        c = c + carry_ref[...]
        carry_ref[...] = c[tt - 1:, :]
        bias = _fold_bias(c)
        ka_ref[0], va_ref[0, 0] = _fold_kv(k, v, bias)
        logft_ref[0] = logf.T[:FOX_HEADS, :]
    proj = _dot((xn * _gain(g_ref, "mix_pre", 1)).astype(BF16), w_in[...])
    q = proj[:, :MIX_W]
    mem_logits = [_mem_logits(proj[b * tt:(b + 1) * tt, MIX_W:], mk_ref, b) for b in range(nb)]
    if fold:
        qa_ref[0, 0] = _fold_q(q, bias)
    else:
        q_ref[...] = (q * (FOX_HD ** -0.5)).astype(BF16).reshape(nb, tt, MIX_W)
        kb_ref[...] = k.astype(BF16).reshape(nb, tt, MIX_W)
        vb_ref[...] = v.astype(BF16).reshape(nb, tt, MIX_W)
        logf_t = _log_sigmoid(_dot_nt(w_ft[...], hk) + b_ft[...])
        logft_ref[0] = logf_t
        ct_ref[0, 0] = _cumsum(logf_t, 1, tt)
    for b in range(nb):
        rows = slice(b * tt, (b + 1) * tt)
        if not fold:
            c_ref[b] = c[rows, :FOX_HEADS]
        mem_ref[b] = _mem_values(_mem_probs(mem_logits[b]), mv_ref, b).astype(BF16)


def _layer1_front_io(x, mem_k, mem_v, p, *, nb, tt, fold):
    b, t, _ = x.shape
    assert (nb == 1) if fold else (t == tt)
    tile = lambda w: pl.BlockSpec((nb, tt, w), lambda i, j: (i, j, 0))
    act = lambda w, dt: jax.ShapeDtypeStruct((b, t, w), dt)
    kv_spec = pl.BlockSpec((1, MIX_W, tt), lambda i, j: (i, 0, j)) if fold else tile(MIX_W)
    kv_shape = jax.ShapeDtypeStruct((b, MIX_W, t), F32) if fold else act(MIX_W, F32)
    out_specs = [kv_spec, kv_spec, pl.BlockSpec((1, FOX_HEADS, nb * tt), lambda i, j: (i, 0, j)), tile(MEM_W)]
    out_shape = [kv_shape, kv_shape, jax.ShapeDtypeStruct((b // nb, FOX_HEADS, nb * t), F32), act(MEM_W, BF16)]
    if fold:
        wf = FOX_HEADS * PAIR_W
        tile_t = pl.BlockSpec((1, 1, wf, tt), lambda i, j: (i, j, 0, 0))
        act_t = jax.ShapeDtypeStruct((b, t // tt, wf, tt), BF16)
        out_specs += [tile_t, tile(wf), tile_t]
        out_shape += [act_t, act(wf, BF16), act_t]
        scratch = [pltpu.VMEM((1, V7X_LANES), F32)]
    else:
        out_specs += [tile(MIX_W)] * 3 + [tile(FOX_HEADS),
                                          pl.BlockSpec((1, 1, FOX_HEADS, nb * tt), lambda i, j: (i, j, 0, 0))]
        out_shape += [act(MIX_W, BF16)] * 3 + [act(FOX_HEADS, F32),
                                               jax.ShapeDtypeStruct((b // nb, t // tt, FOX_HEADS, nb * tt), F32)]
        scratch = []
    return _CallIO(
        in_specs=[tile(D_MODEL), _mem_spec(mem_k, 1, nb), _mem_spec(mem_v, 1, nb),
                  _const_spec(p["gains"].shape),
                  _const_spec(p["w_kv"].shape), _const_spec(p["w_f"].shape), _const_spec(p["w_ft"].shape),
                  _const_spec(p["b_f"].shape), _const_spec(p["b_ft"].shape), _layer_spec(p["w_in"][1], 0)],
        args=[x, mem_k, mem_v, p["gains"], p["w_kv"], p["w_f"], p["w_ft"], p["b_f"], p["b_ft"], p["w_in"][1]],
        out_specs=out_specs, out_shape=out_shape, scratch=scratch)


def _layer1_front(x, mem_k, mem_v, p, *, nb, tt, fold):
    return _tile_call(functools.partial(_layer1_front_kernel, nb=nb, tt=tt, fold=fold), "layer1_front", x, nb, tt,
                      _layer1_front_io(x, mem_k, mem_v, p, nb=nb, tt=tt, fold=fold))


def _layer0_front_kernel(*refs, n_in, n_out, nb, tt, pos_base, side, fold):
    n_front_out = 7 if fold else 9
    n_all_in = len(refs) - (n_out + n_front_out) - (2 if fold else 1)
    in0, in1 = refs[:n_in], refs[n_in:n_all_in]
    out0, out1 = refs[n_all_in:n_all_in + n_out], refs[n_all_in + n_out:n_all_in + n_out + n_front_out]
    halo_ref, *scratch1 = refs[n_all_in + n_out + n_front_out:]
    _layer0_kernel(*in0, *out0, halo_ref, nb=nb, tt=tt, pos_base=pos_base, side=side)
    _layer1_front_kernel(out0[0], *in1, *out1, *scratch1, nb=nb, tt=tt, fold=fold)


def _layer0_front(x, prev, mem_k, mem_v, p, *, nb, tt, pos_base, fold, side=()):
    io0 = _layer0_io(x, prev, mem_k, mem_v, p, nb=nb, tt=tt, side=side)
    io1 = _layer1_front_io(x, mem_k, mem_v, p, nb=nb, tt=tt, fold=fold)
    io = _CallIO(io0.in_specs + io1.in_specs[1:], io0.args + io1.args[1:], io0.out_specs + io1.out_specs,
                 io0.out_shape + io1.out_shape, io0.scratch + io1.scratch)
    return _tile_call(
        functools.partial(_layer0_front_kernel, n_in=len(io0.in_specs), n_out=len(io0.out_specs), nb=nb, tt=tt,
                          pos_base=pos_base, side=tuple(job[0] for job in side), fold=fold),
        "layer0_front", x, nb, tt, io)


def _fox_prompt_kernel(qt_ref, k_ref, vt_ref, o_ref, m_ref, acc_ref, *s_refs, t):
    i = pl.program_id(1)
    half = t // 2
    sub = lax.broadcasted_iota(jnp.int32, (PAIR_W, 1), 0)
    col = [slice(h * PAIR_W, (h + 1) * PAIR_W) for h in range(FOX_HEADS)]
    qt = [qt_ref[0, 0, c, :] for c in col]
    slot = lambda n: s_refs[n % len(s_refs)]

    def chain(n_items, logits, update):
        top = [logits(n) for n in range(FOX_AHEAD)]
        for n in range(n_items):
            if n + FOX_AHEAD < n_items:
                top.append(logits(n + FOX_AHEAD))
            update(n, top[n])

    tri = (lax.broadcasted_iota(jnp.int32, (half, half), 0) <= lax.broadcasted_iota(jnp.int32, (half, half), 1))
    lo = pl.ds(pl.multiple_of(i * t, t), half)
    hi = pl.ds(pl.multiple_of(i * t + half, half), half)

    def diag_logits(h):
        s_lo = _dot(k_ref[0, lo, col[h]], qt[h])
        s_hi = _dot(k_ref[0, hi, col[h]], qt[h][:, half:])
        s_lo = jnp.concatenate([jnp.where(tri, s_lo[:, :half], NEG_BIG), s_lo[:, half:]], axis=1)
        s_hi = jnp.where(tri, s_hi, NEG_BIG)
        slot(h)[:half, :] = s_lo
        slot(h)[half:, half:] = s_hi
        top = jnp.max(s_lo, axis=0, keepdims=True)
        return jnp.concatenate([top[:, :half], jnp.maximum(top[:, half:], jnp.max(s_hi, axis=0, keepdims=True))],
                               axis=1)

    def diag_update(h, m):
        p_lo = jnp.exp2(slot(h)[:half, :] - m).astype(BF16)
        p_hi = jnp.exp2(slot(h)[half:, half:] - m[:, half:]).astype(BF16)
        acc = _dot(vt_ref[0, i, col[h], :half], p_lo)
        acc_ref[h] = jnp.concatenate([acc[:, :half], acc[:, half:] + _dot(vt_ref[0, i, col[h], half:], p_hi)], axis=1)
        m_ref[h] = m

    chain(FOX_HEADS, diag_logits, diag_update)

    def full_tiles(j0, n_tiles):
        def logits(n):
            rows = pl.ds(pl.multiple_of((j0 + n // FOX_HEADS) * t, t), t)
            s = _dot(k_ref[0, rows, col[n % FOX_HEADS]], qt[n % FOX_HEADS])
            slot(n)[...] = s
            return jnp.max(s, axis=0, keepdims=True)

        def update(n, top):
            h = n % FOX_HEADS
            m = m_ref[h]
            m_new = jnp.maximum(m, top)
            p = jnp.exp2(slot(n)[...] - m_new).astype(BF16)
            acc_ref[h] = jnp.exp2(m - m_new) * acc_ref[h] + _dot(vt_ref[0, j0 + n // FOX_HEADS, col[h], :], p)
            m_ref[h] = m_new

        chain(n_tiles * FOX_HEADS, logits, update)

    lax.fori_loop(0, i // 2, lambda jj, c: full_tiles(2 * jj, 2), None)
    lax.fori_loop(2 * (i // 2), i, lambda j, c: full_tiles(j, 1), None)
    for pair in range(FOX_HEADS // 2):
        even, odd = acc_ref[2 * pair], acc_ref[2 * pair + 1]
        own = sub < FOX_HD
        out = jnp.where(own, even, odd) / jnp.where(own, even[FOX_HD:FOX_HD + 1, :], odd[0:1, :])
        o_ref[0, :, pair * PAIR_W:(pair + 1) * PAIR_W] = out.T.astype(BF16)


def _fox_prompt(qt, ka, vt):
    b, n_tiles, w, tile = qt.shape
    per_stream = lambda shape: pl.BlockSpec((1,) + shape, lambda i, j: (i,) + (0,) * len(shape))
    return pl.pallas_call(
        functools.partial(_fox_prompt_kernel, t=tile),
        grid=(b, n_tiles),
        in_specs=[pl.BlockSpec((1, 1, w, tile), lambda i, j: (i, j, 0, 0)),
                  per_stream((n_tiles * tile, w)), per_stream((n_tiles, w, tile))],
        out_specs=pl.BlockSpec((1, tile, MIX_W), lambda i, j: (i, j, 0)),
        out_shape=jax.ShapeDtypeStruct((b, n_tiles * tile, MIX_W), BF16),
        scratch_shapes=([pltpu.VMEM((FOX_HEADS, 1, tile), F32), pltpu.VMEM((FOX_HEADS, PAIR_W, tile), F32)]
                        + [pltpu.VMEM((tile, tile), F32)] * (FOX_AHEAD + 1)),
        compiler_params=pltpu.CompilerParams(dimension_semantics=("arbitrary", "arbitrary"),
                                             vmem_limit_bytes=VMEM_LIMIT),
        name="fox_prompt",
    )(qt, ka, vt)


def _fox_sample_kernel(q_ref, kc_ref, vc_ref, kn_ref, vn_ref, lfc_t_ref, c_ref, ct_ref, o_ref):
    tn = q_ref.shape[1]
    p = lfc_t_ref.shape[2]
    cc = _cumsum(lfc_t_ref[0], 1, p)
    ck_cache = cc - cc[:, p - 1:]
    causal = (lax.broadcasted_iota(jnp.int32, (tn, V7X_LANES), 1)
              <= lax.broadcasted_iota(jnp.int32, (tn, V7X_LANES), 0))
    pad = jnp.zeros((V7X_LANES - tn, FOX_HD), BF16)
    col = [slice(h * FOX_HD, (h + 1) * FOX_HD) for h in range(FOX_HEADS)]
    logits = []
    for h in range(FOX_HEADS):
        qh = q_ref[0, :, col[h]]
        cq = c_ref[0, :, h:h + 1]
        kn = jnp.concatenate([kn_ref[0, :, col[h]], pad], axis=0)
        s_old = _dot(qh, kc_ref[0, h].astype(BF16)) + cq - ck_cache[h:h + 1, :]
        s_new = jnp.where(causal, _dot_nt(qh, kn) + cq - ct_ref[0, h:h + 1, :], NEG_BIG)
        logits.append((s_old, s_new))
    probs = []
    for s_old, s_new in logits:
        m = jnp.maximum(jnp.max(s_old, axis=-1, keepdims=True), jnp.max(s_new, axis=-1, keepdims=True))
        p_old = jnp.exp(s_old - m)
        p_new = jnp.exp(s_new - m)
        l = jnp.sum(p_old, axis=-1, keepdims=True) + jnp.sum(p_new, axis=-1, keepdims=True)
        probs.append((p_old.astype(BF16), p_new.astype(BF16), l))
    outs = []
    for h, (p_old, p_new, l) in enumerate(probs):
        vn = jnp.concatenate([vn_ref[0, :, col[h]], pad], axis=0)
        outs.append((_dot_nt(p_old, vc_ref[0, h].astype(BF16)) + _dot(p_new, vn)) / l)
    o_ref[0] = jnp.concatenate(outs, axis=-1).astype(BF16)


def _fox_sample(q, cache_kt, cache_vt, kb, vb, cache_logf_t, c, ct):
    b, tn, _ = q.shape
    p = cache_kt.shape[3]
    new = lambda w: pl.BlockSpec((1, tn, w), lambda i: (i, 0, 0))
    old = pl.BlockSpec((1, FOX_HEADS, FOX_HD, p), lambda i: (i, 0, 0, 0))
    return pl.pallas_call(
        _fox_sample_kernel,
        grid=(b,),
        in_specs=[new(MIX_W), old, old, new(MIX_W), new(MIX_W),
                  pl.BlockSpec((1, FOX_HEADS, p), lambda i: (i, 0, 0)),
                  new(FOX_HEADS), pl.BlockSpec((1, FOX_HEADS, V7X_LANES), lambda i: (i, 0, 0))],
        out_specs=new(MIX_W),
        out_shape=jax.ShapeDtypeStruct((b, tn, MIX_W), BF16),
        compiler_params=pltpu.CompilerParams(dimension_semantics=("arbitrary",), vmem_limit_bytes=VMEM_LIMIT),
        name="fox_sample",
    )(q, cache_kt, cache_vt, kb, vb, cache_logf_t, c, ct)


def _layer1_back_kernel(x_ref, mix_ref, mem_ref, g_ref, w_out, w_up, w_down, *refs, nb, tt, side):
    n = len(side)
    y_ref = refs[n]
    m = nb * tt
    x = x_ref[...].reshape(m, D_MODEL)
    mix = mix_ref[...].reshape(m, MIX_W)
    mem = mem_ref[...].reshape(m, MEM_W)
    x1 = [x[rows] + _rms(_dot(mix[rows], w_out[:MIX_W, :]) + _dot(mem[rows], w_out[MIX_W:, :]),
                         _gain(g_ref, "mix_post", 1)) for rows in _row_parts(m)]
    y = jnp.concatenate(_mlp(x1, _gain(g_ref, "mlp_pre", 1), _gain(g_ref, "mlp_post", 1), w_up, w_down), axis=0)
    y_ref[...] = y.reshape(nb, tt, D_MODEL)
    _side_jobs(side, refs[:n], refs[n + 1:])


def _layer1_back(x, mix, mem, p, *, nb, tt, side=()):
    b, t, _ = x.shape
    grid = (b // nb, t // tt)
    tile = lambda w: pl.BlockSpec((nb, tt, w), lambda i, j: (i, j, 0))
    side_in, side_out, side_shape = _side_specs(side, grid[0] * grid[1], lambda i, j: i * grid[1] + j)
    return pl.pallas_call(
        functools.partial(_layer1_back_kernel, nb=nb, tt=tt, side=tuple(job[0] for job in side)),
        grid=grid,
        in_specs=[tile(D_MODEL), tile(MIX_W), tile(MEM_W), _const_spec(p["gains"].shape),
                  _layer_spec(p["w_out"][1], 0), _layer_spec(p["w_up"][1], 0), _layer_spec(p["w_down"][1], 0)]
                 + side_in,
        out_specs=[tile(D_MODEL)] + side_out,
        out_shape=[jax.ShapeDtypeStruct((b, t, D_MODEL), F32)] + side_shape,
        compiler_params=pltpu.CompilerParams(dimension_semantics=("arbitrary", "arbitrary"),
                                             vmem_limit_bytes=VMEM_LIMIT),
        name="layer1_back",
    )(x, mix, mem, p["gains"], p["w_out"][1], p["w_up"][1], p["w_down"][1],
      *[job[1] for job in side])


def _token_tile(b, t):
    if t >= TOKEN_TILE:
        assert t % TOKEN_TILE == 0
        return 1, TOKEN_TILE
    return b, t


def _trunk(x, pool_prev, cache, mem_k, mem_v, p, later_f32=None, side=((), ())):
    b, t, _ = x.shape
    nb, tt = _token_tile(b, t)
    pos_base = 0 if cache is None else POOL_BUF
    names = sorted(later_f32) if later_f32 else []
    side0 = [("cast", later_f32[n], 1) for n in names] + list(side[0])
    if names:
        y0, pool_state, *done = _layer0(x, pool_prev, mem_k, mem_v, p, nb=nb, tt=tt, pos_base=pos_base, side=side0)
        p = {**p, **{n: [p[n][0], w] for n, w in zip(names, done)}}
        front = _layer1_front(y0, mem_k, mem_v, p, nb=nb, tt=tt, fold=cache is None)
    else:
        y0, pool_state, *rest = _layer0_front(x, pool_prev, mem_k, mem_v, p, nb=nb, tt=tt, pos_base=pos_base,
                                              fold=cache is None, side=side0)
        done, front = rest[:len(side0)], rest[len(side0):]
    done_first = done[len(names):]
    k, v, logf_t, mem, *att = front
    logf = jnp.transpose(logf_t.reshape(b // nb, FOX_HEADS, nb, t), (0, 2, 3, 1)).reshape(b, t, FOX_HEADS)
    if cache is None:
        mix = _fox_prompt(*att)
    else:
        q, kb, vb, c, ct = att
        ct_new = jnp.swapaxes(ct.reshape(FOX_HEADS, b, t), 0, 1)
        ct_new = jnp.pad(ct_new, ((0, 0), (0, 0), (0, V7X_LANES - t)))
        mix = _fox_sample(q, *cache[:2], kb, vb, cache[2], c, ct_new)
    y, *done_last = _layer1_back(y0, mix, mem, p, nb=nb, tt=tt, side=side[1])
    if cache is None:
        heads = lambda a: jnp.transpose(a.reshape(b, FOX_HEADS, FOX_HD, t), (0, 3, 1, 2))
    else:
        heads = lambda a: a.reshape(b, t, FOX_HEADS, FOX_HD)
    return (y, pool_state, heads(k), heads(v), logf), p, done_first, done_last


def kernel(x_prompt, x_sample, cache_pool, cache_k, cache_v, cache_logf, cache_mem_k, cache_mem_v, mem_prompt,
           g_mix_pre, g_mix_post, g_mlp_pre, g_mlp_post, w_in, w_out, w_pool, pool_scale, g_kv, w_kvf, b_f,
           g_mem, w_mem_kv, w_up, w_down):
    depth = w_in.shape[0]
    assert depth == 2 and w_pool.shape[0] == 1, "one pooling layer followed by one forgetting-attention layer"
    bp = x_prompt.shape[0]
    w_f = w_kvf[:, 2 * MIX_W:]
    fold_cols = lambda a: jnp.pad(jnp.concatenate([a, jnp.repeat(a, FOLD_W, axis=-1)], axis=-1),
                                  [(0, 0)] * (a.ndim - 1) + [(0, V7X_LANES - FOLD_SRC - FOLD_W * FOX_HEADS)])
    rows = lambda g: g.reshape(g.shape[0], 1, g.shape[1])
    p = dict(
        gains=_pack_gains(g_mix_pre, g_mix_post, g_mlp_pre, g_mlp_post, g_kv), pool_scale=rows(pool_scale),
        w_pool=w_pool.astype(BF16), w_kv=w_kvf[:, :2 * MIX_W].astype(BF16),
        w_f=fold_cols(w_f).astype(BF16), w_ft=w_f.T.astype(BF16),
        b_f=fold_cols(b_f).reshape(1, V7X_LANES), b_ft=b_f.reshape(FOX_HEADS, 1),
    )
    later_f32 = dict(w_in=w_in, w_out=w_out, w_up=w_up, w_down=w_down)
    names = sorted(later_f32)
    mem_k_prompt, mem_v_prompt, mkh, mvh, *w0 = _mem_project(mem_prompt, g_mem, w_mem_kv,
                                                             side=[("cast", later_f32[n], 0) for n in names])
    p.update({n: [w, None] for n, w in zip(names, w0)})

    heads_view = lambda a: a.reshape(a.shape[:2] + (-1, a.shape[-1]))
    (y_p, pool_p, k_p, v_p, logf_p), p, (cmk,), (cmv,) = _trunk(
        x_prompt, jnp.zeros((bp, HALO, MIX_W), F32), None, mkh, mvh, p, later_f32,
        side=([("heads", heads_view(cache_mem_k))], [("heads", heads_view(cache_mem_v))]))

    pool_prev = jnp.pad(cache_pool[0], ((0, 0), (HALO - POOL_BUF, 0), (0, 0)))
    lanes_view = lambda a: jnp.transpose(a, (0, 2, 3, 1))
    cache = (lanes_view(cache_k), lanes_view(cache_v), jnp.swapaxes(cache_logf, 1, 2))
    (y_s, pool_s, k_s, v_s, logf_s), _, _, _ = _trunk(x_sample, pool_prev, cache, cmk, cmv, p)

    return (y_p, y_s, pool_p, pool_s, k_p, v_p, logf_p, k_s, v_s, logf_s, mem_k_prompt, mem_v_prompt)
```

```python
import functools
from typing import NamedTuple

import jax
import jax.numpy as jnp
from jax import lax
from jax.experimental import pallas as pl
from jax.experimental.pallas import tpu as pltpu

F32 = jnp.float32
BF16 = jnp.bfloat16

D_MODEL = 1024
MIX_W = D_MODEL // 2
MEM_W = D_MODEL - MIX_W
POOL_WINDOWS = (2, 4, 8, 16)
POOL_GROUP = MIX_W // len(POOL_WINDOWS)
POOL_BUF = max(POOL_WINDOWS) - 1
FOX_HEADS = 8
FOX_HD = MIX_W // FOX_HEADS
MEM_HEADS = 4
MEM_HD = MEM_W // MEM_HEADS
D_FF = 4 * D_MODEL
EPS = 1e-6

V7X_LANES = 128
V7X_SUBLANES_F32 = 8
V7X_VMEM_BYTES = 64 * 1024 * 1024

HALO = 2 * V7X_SUBLANES_F32
FF_CHUNK = 1024
TOKEN_TILE = 512
ROW_PARTS = 2
PAIR_W = 2 * FOX_HD
NEG_BIG = -1e30
LOG2E = 1.4426950408889634
FOLD_SRC = FOX_HEADS
FOLD_W = 6
FOX_AHEAD = 2
assert FOLD_SRC + FOLD_W * FOX_HEADS <= V7X_LANES
VMEM_LIMIT = V7X_VMEM_BYTES - 8 * 1024 * 1024

assert HALO >= POOL_BUF and PAIR_W == V7X_LANES and POOL_GROUP == V7X_LANES and MEM_HD == V7X_LANES


def _const_spec(shape):
    zeros = (0,) * len(shape)
    return pl.BlockSpec(shape, lambda *_: zeros, pipeline_mode=pl.Buffered(1))


def _layer_spec(stacked, layer):
    index = (layer,) + (0,) * (stacked.ndim - 1)
    return pl.BlockSpec((None,) + stacked.shape[1:], lambda *_: index, pipeline_mode=pl.Buffered(1))


def _mem_spec(mem, layer, nb):
    return pl.BlockSpec((None, nb) + mem.shape[2:], lambda i, j: (layer, i, 0, 0, 0))


def _unit_rms(x):
    return x * lax.rsqrt(jnp.mean(x * x, axis=-1, keepdims=True) + EPS)


def _rms(x, g):
    return _unit_rms(x) * g


GAIN_KINDS = ("mix_pre", "mix_post", "mlp_pre", "mlp_post")


def _gain(g_ref, kind, layer=0):
    depth = (g_ref.shape[0] - 1) // len(GAIN_KINDS)
    r = len(GAIN_KINDS) * depth if kind == "kv" else GAIN_KINDS.index(kind) * depth + layer
    return g_ref[r:r + 1, :]


def _pack_gains(g_mix_pre, g_mix_post, g_mlp_pre, g_mlp_post, g_kv):
    return jnp.concatenate([g_mix_pre, g_mix_post, g_mlp_pre, g_mlp_post, g_kv[None, :]], axis=0)


def _dot(a, b):
    return jnp.dot(a, b, preferred_element_type=F32)


def _dot_nt(a, b):
    return lax.dot_general(a, b, (((1,), (1,)), ((), ())), preferred_element_type=F32)


def _log_sigmoid(x):
    return jnp.minimum(x, 0.0) - jnp.log1p(jnp.exp(-jnp.abs(x)))


def _cumsum(x, axis, seg):
    assert seg & (seg - 1) == 0 and x.shape[axis] % seg == 0
    n = x.shape[axis]
    vreg = (V7X_SUBLANES_F32, V7X_LANES)[axis]
    idx = lax.broadcasted_iota(jnp.int32, x.shape, axis) & (seg - 1)
    k = 1
    while k < seg:
        if seg == n and k % vreg == 0:
            zeros = jnp.zeros(x.shape[:axis] + (k,) + x.shape[axis + 1:], x.dtype)
            x = x + jnp.concatenate([zeros, lax.slice_in_dim(x, 0, n - k, axis=axis)], axis=axis)
        else:
            x = x + jnp.where(idx >= k, pltpu.roll(x, k, axis=axis), 0.0)
        k *= 2
    return x


def _mem_logits(q, mk_ref, b):
    qs = (q * (MEM_HD ** -0.5 * LOG2E)).astype(BF16)
    return [_dot_nt(qs[:, h * MEM_HD:(h + 1) * MEM_HD], mk_ref[b, h]) for h in range(MEM_HEADS)]


def _mem_probs(logits):
    probs = []
    for s in logits:
        p = jnp.exp2(s - jnp.max(s, axis=-1, keepdims=True))
        probs.append((p.astype(BF16), jnp.sum(p, axis=-1, keepdims=True)))
    return probs


def _mem_values(probs, mv_ref, b):
    return jnp.concatenate([_dot(p, mv_ref[b, h]) / l for h, (p, l) in enumerate(probs)], axis=-1)


def _pool_mixer(u, halo, pos0, w_pool_ref, scale):
    tt = u.shape[0]
    ext = jnp.concatenate([halo, u], axis=0)
    pos = pos0 + lax.broadcasted_iota(jnp.int32, (tt, 1), 0)
    outs = []
    for g, w in enumerate(POOL_WINDOWS):
        sl = slice(g * POOL_GROUP, (g + 1) * POOL_GROUP)
        s = ext[:, sl]
        k = 1
        while k < w:
            s = s + pltpu.roll(s, k, axis=0)
            k *= 2
        cnt = jnp.minimum(pos + 1, w).astype(F32)
        pooled = s[HALO:] / cnt - u[:, sl]
        outs.append(_dot(pooled.astype(BF16), w_pool_ref[g]))
    return jnp.concatenate(outs, axis=-1) * scale


def _row_parts(m):
    n = ROW_PARTS if m >= TOKEN_TILE else 1
    return [slice(i * m // n, (i + 1) * m // n) for i in range(n)]


def _mlp(x1_parts, g_pre, g_post, w_up_ref, w_down_ref):
    n_chunks = D_FF // FF_CHUNK
    chunk = lambda c: slice(c * FF_CHUNK, (c + 1) * FF_CHUNK)
    act = lambda up: jnp.square(jnp.maximum(up, 0.0)).astype(BF16)
    hm_parts = [_rms(x1, g_pre).astype(BF16) for x1 in x1_parts]
    hm = jnp.concatenate(hm_parts, axis=0)
    parts = _row_parts(hm.shape[0])
    acc = None
    for c in range(n_chunks):
        if c == 0:
            a = jnp.concatenate([act(_dot(h, w_up_ref[:, chunk(c)])) for h in hm_parts], axis=0)
        else:
            a = act(_dot(hm, w_up_ref[:, chunk(c)]))
        if c < n_chunks - 1:
            d = _dot(a, w_down_ref[chunk(c), :])
            acc = d if acc is None else acc + d
    return [x1 + _rms(acc[rows] + _dot(a[rows], w_down_ref[chunk(n_chunks - 1), :]), g_post)
            for x1, rows in zip(x1_parts, parts)]


def _side_specs(jobs, steps, step_of):
    in_specs, out_specs, out_shape = [], [], []
    for kind, a, *rest in jobs:
        if kind == "cast":
            rows = a.shape[1] // steps
            in_specs.append(pl.BlockSpec((1, rows, a.shape[2]), lambda *ids, l=rest[0]: (l, step_of(*ids), 0)))
            out_specs.append(pl.BlockSpec((1, rows, a.shape[2]), lambda *ids: (0, step_of(*ids), 0)))
            out_shape.append(jax.ShapeDtypeStruct((1,) + a.shape[1:], BF16))
        else:
            n_l, n_s, nh, d = a.shape
            h = MEM_HEADS
            n = nh // h
            rows = n_l * n_s * n // steps
            parts = n // rows
            assert kind == "heads" and rows * parts == n

            def where(*ids, parts=parts, n_s=n_s):
                step = step_of(*ids)
                return step // parts // n_s, step // parts % n_s, step % parts

            in_specs.append(pl.BlockSpec((1, 1, rows * h, d), lambda *ids, w=where: w(*ids) + (0,)))
            out_specs.append(pl.BlockSpec((1, 1, h, rows, d),
                                          lambda *ids, w=where: w(*ids)[:2] + (0, w(*ids)[2], 0)))
            out_shape.append(jax.ShapeDtypeStruct((n_l, n_s, h, n, d), BF16))
    return in_specs, out_specs, out_shape


def _side_jobs(kinds, src_refs, dst_refs):
    for kind, src, dst in zip(kinds, src_refs, dst_refs):
        if kind == "cast":
            dst[...] = src[...].astype(BF16)
        else:
            x = src[0, 0].astype(BF16)
            heads, rows = dst.shape[2], dst.shape[3]
            row = lax.broadcasted_iota(jnp.int32, (rows, rows * heads), 0)
            pick = lax.broadcasted_iota(jnp.int32, (rows, rows * heads), 1) - heads * row
            for h in range(heads):
                dst[0, 0, h] = _dot((pick == h).astype(BF16), x).astype(BF16)


def _mem_project_kernel(mem_ref, g_ref, w_ref, *refs, side):
    n = len(side)
    mk_ref, mv_ref, mkh_ref, mvh_ref = refs[n:n + 4]
    _side_jobs(side, refs[:n], refs[n + 4:])
    xn = _unit_rms(mem_ref[0])
    for l in range(w_ref.shape[0]):
        kv = _dot((xn * g_ref[l:l + 1, :]).astype(BF16), w_ref[l].astype(BF16))
        for h in range(MEM_HEADS):
            mk = kv[:, h * MEM_HD:(h + 1) * MEM_HD]
            mv = kv[:, MEM_W + h * MEM_HD:MEM_W + (h + 1) * MEM_HD]
            mk_ref[l, 0, :, h, :] = mk
            mv_ref[l, 0, :, h, :] = mv
            mkh_ref[l, 0, h] = mk.astype(BF16)
            mvh_ref[l, 0, h] = mv.astype(BF16)


def _mem_project(mem, g_mem, w_mem_kv, side=()):
    b, n, _ = mem.shape
    depth = w_mem_kv.shape[0]
    out = jax.ShapeDtypeStruct((depth, b, n, MEM_HEADS, MEM_HD), F32)
    out_h = jax.ShapeDtypeStruct((depth, b, MEM_HEADS, n, MEM_HD), BF16)
    side_in, side_out, side_shape = _side_specs(side, b, lambda i: i)
    return pl.pallas_call(
        functools.partial(_mem_project_kernel, side=tuple(job[0] for job in side)),
        grid=(b,),
        in_specs=[pl.BlockSpec((1, n, D_MODEL), lambda i: (i, 0, 0)),
                  _const_spec(g_mem.shape), _const_spec(w_mem_kv.shape)] + side_in,
        out_specs=([pl.BlockSpec((depth, 1, n, MEM_HEADS, MEM_HD), lambda i: (0, i, 0, 0, 0))] * 2
                   + [pl.BlockSpec((depth, 1, MEM_HEADS, n, MEM_HD), lambda i: (0, i, 0, 0, 0))] * 2 + side_out),
        out_shape=[out, out, out_h, out_h] + side_shape,
        compiler_params=pltpu.CompilerParams(dimension_semantics=("arbitrary",), vmem_limit_bytes=VMEM_LIMIT),
        name="mem_project",
    )(mem, g_mem, w_mem_kv, *[job[1] for job in side])


def _layer0_kernel(x_ref, prev_ref, mk_ref, mv_ref, g_ref, w_in, w_out, w_pool, pscale, w_up, w_down, *refs,
                   nb, tt, pos_base, side):
    n = len(side)
    side_in, (y_ref, state_ref), side_out = refs[:n], refs[n:n + 2], refs[n + 2:2 * n + 2]
    halo_ref, = refs[2 * n + 2:]
    t = pl.program_id(1)

    @pl.when(t == 0)
    def _():
        halo_ref[...] = prev_ref[...]

    x = x_ref[...].reshape(nb * tt, D_MODEL)
    proj = jnp.concatenate([_dot(_rms(x[rows], _gain(g_ref, "mix_pre")).astype(BF16), w_in[...])
                            for rows in _row_parts(nb * tt)], axis=0)
    cats = []
    for b in range(nb):
        rows = slice(b * tt, (b + 1) * tt)
        u = proj[rows, :MIX_W]
        logits = _mem_logits(proj[rows, MIX_W:], mk_ref, b)
        mix = _pool_mixer(u, halo_ref[b], pos_base + t * tt, w_pool, pscale[...])
        halo_ref[b] = u[tt - HALO:, :]
        state_ref[0, b] = u[tt - POOL_BUF:, :]
        mem = _mem_values(_mem_probs(logits), mv_ref, b)
        cats.append(jnp.concatenate([mix, mem], axis=-1))
    cat = (cats[0] if nb == 1 else jnp.concatenate(cats, axis=0)).astype(BF16)
    x1 = [x[rows] + _rms(_dot(cat[rows], w_out[...]), _gain(g_ref, "mix_post")) for rows in _row_parts(nb * tt)]
    y = jnp.concatenate(_mlp(x1, _gain(g_ref, "mlp_pre"), _gain(g_ref, "mlp_post"), w_up, w_down), axis=0)
    y_ref[...] = y.reshape(nb, tt, D_MODEL)
    _side_jobs(side, side_in, side_out)


class _CallIO(NamedTuple):
    in_specs: list
    args: list
    out_specs: list
    out_shape: list
    scratch: list


def _tile_call(kernel, name, x, nb, tt, io):
    return pl.pallas_call(
        kernel,
        grid=(x.shape[0] // nb, x.shape[1] // tt),
        in_specs=io.in_specs,
        out_specs=io.out_specs,
        out_shape=io.out_shape,
        scratch_shapes=io.scratch,
        compiler_params=pltpu.CompilerParams(dimension_semantics=("arbitrary", "arbitrary"),
                                             vmem_limit_bytes=VMEM_LIMIT),
        name=name,
    )(*io.args)


def _layer0_io(x, prev, mem_k, mem_v, p, *, nb, tt, side):
    b, t, _ = x.shape
    grid = (b // nb, t // tt)
    tile = pl.BlockSpec((nb, tt, D_MODEL), lambda i, j: (i, j, 0))
    side_in, side_out, side_shape = _side_specs(side, grid[0] * grid[1], lambda i, j: i * grid[1] + j)
    return _CallIO(
        in_specs=[tile, pl.BlockSpec((nb, HALO, MIX_W), lambda i, j: (i, 0, 0)),
                  _mem_spec(mem_k, 0, nb), _mem_spec(mem_v, 0, nb),
                  _const_spec(p["gains"].shape), _layer_spec(p["w_in"][0], 0), _layer_spec(p["w_out"][0], 0),
                  _layer_spec(p["w_pool"], 0), _layer_spec(p["pool_scale"], 0),
                  _layer_spec(p["w_up"][0], 0), _layer_spec(p["w_down"][0], 0)] + side_in,
        args=[x, prev, mem_k, mem_v, p["gains"], p["w_in"][0], p["w_out"][0], p["w_pool"], p["pool_scale"],
              p["w_up"][0], p["w_down"][0]] + [job[1] for job in side],
        out_specs=[tile, pl.BlockSpec((1, nb, POOL_BUF, MIX_W), lambda i, j: (0, i, 0, 0))] + side_out,
        out_shape=[jax.ShapeDtypeStruct((b, t, D_MODEL), F32),
                   jax.ShapeDtypeStruct((1, b, POOL_BUF, MIX_W), F32)] + side_shape,
        scratch=[pltpu.VMEM((nb, HALO, MIX_W), F32)])


def _layer0(x, prev, mem_k, mem_v, p, *, nb, tt, pos_base, side=()):
    return _tile_call(
        functools.partial(_layer0_kernel, nb=nb, tt=tt, pos_base=pos_base, side=tuple(job[0] for job in side)),
        "layer0", x, nb, tt, _layer0_io(x, prev, mem_k, mem_v, p, nb=nb, tt=tt, side=side))


def _bf16_part(x):
    return x.astype(BF16).astype(F32)


def _fold_lanes(h):
    lane = lax.broadcasted_iota(jnp.int32, (1, V7X_LANES), 1)
    own = (h % 2) * FOX_HD
    e0 = FOX_HD - own
    in_head = (lane >= own) & (lane < own + FOX_HD)
    first = ((lane >= e0) & (lane < e0 + 3)).astype(F32)
    second = ((lane >= e0 + 3) & (lane < e0 + 6)).astype(F32)
    return in_head, first, second, lane == e0, e0


def _fold_bias(c):
    cl = c * LOG2E
    hi = _bf16_part(cl)
    mid = _bf16_part(cl - hi)
    lo = _bf16_part(cl - hi - mid)
    lane = lax.broadcasted_iota(jnp.int32, (1, V7X_LANES), 1)
    piece = lax.rem(lane + (3 - FOLD_SRC % 3), 3)
    d = jnp.where(piece == 0, hi, jnp.where(piece == 1, mid, lo))
    return [pltpu.roll(d, (_fold_lanes(h)[4] - (FOLD_SRC + FOLD_W * h)) % V7X_LANES, axis=1)
            for h in range(FOX_HEADS)]


def _fold_kv(k, v, bias):
    ks, vs = [], []
    for h, dh in enumerate(bias):
        in_head, first, second, one, _ = _fold_lanes(h)
        cols = slice((h // 2) * PAIR_W, (h // 2 + 1) * PAIR_W)
        ks.append(jnp.where(in_head, k[:, cols], first - dh * second).astype(BF16))
        vs.append(jnp.where(in_head, v[:, cols], one.astype(F32)).T.astype(BF16))
    return jnp.concatenate(ks, axis=-1), jnp.concatenate(vs, axis=0)


def _fold_q(q, bias):
    qs = []
    for h, dh in enumerate(bias):
        in_head, first, second, _, _ = _fold_lanes(h)
        cols = slice((h // 2) * PAIR_W, (h // 2 + 1) * PAIR_W)
        qs.append(jnp.where(in_head, q[:, cols] * (FOX_HD ** -0.5 * LOG2E), dh * first + second).T.astype(BF16))
    return jnp.concatenate(qs, axis=0)


def _layer1_front_kernel(x_ref, mk_ref, mv_ref, g_ref, w_kv, w_f, w_ft, b_f, b_ft, w_in, *refs,
                         nb, tt, fold):
    if fold:
        k_ref, v_ref, logft_ref, mem_ref, qa_ref, ka_ref, va_ref, carry_ref = refs
    else:
        k_ref, v_ref, logft_ref, mem_ref, q_ref, kb_ref, vb_ref, c_ref, ct_ref = refs
    if fold:
        @pl.when(pl.program_id(1) == 0)
        def _():
            carry_ref[...] = jnp.zeros_like(carry_ref)

    xn = _unit_rms(x_ref[...].reshape(nb * tt, D_MODEL))
    hk = (xn * _gain(g_ref, "kv")).astype(BF16)
    logf = _log_sigmoid(_dot(hk, w_f[...]) + b_f[...])
    k = _dot(hk, w_kv[:, :MIX_W])
    v = _dot(hk, w_kv[:, MIX_W:])
    if fold:
        k_ref[0] = k.T
        v_ref[0] = v.T
    else:
        k_ref[...] = k.reshape(nb, tt, MIX_W)
        v_ref[...] = v.reshape(nb, tt, MIX_W)
    c = _cumsum(logf, 0, tt)
    if fold:
        c = c + carry_ref[...]
        carry_ref[...] = c[tt - 1:, :]
        bias = _fold_bias(c)
        ka_ref[0], va_ref[0, 0] = _fold_kv(k, v, bias)
        logft_ref[0] = logf.T[:FOX_HEADS, :]
    proj = _dot((xn * _gain(g_ref, "mix_pre", 1)).astype(BF16), w_in[...])
    q = proj[:, :MIX_W]
    mem_logits = [_mem_logits(proj[b * tt:(b + 1) * tt, MIX_W:], mk_ref, b) for b in range(nb)]
    if fold:
        qa_ref[0, 0] = _fold_q(q, bias)
    else:
        q_ref[...] = (q * (FOX_HD ** -0.5)).astype(BF16).reshape(nb, tt, MIX_W)
        kb_ref[...] = k.astype(BF16).reshape(nb, tt, MIX_W)
        vb_ref[...] = v.astype(BF16).reshape(nb, tt, MIX_W)
        logf_t = _log_sigmoid(_dot_nt(w_ft[...], hk) + b_ft[...])
        logft_ref[0] = logf_t
        ct_ref[0, 0] = _cumsum(logf_t, 1, tt)
    for b in range(nb):
        rows = slice(b * tt, (b + 1) * tt)
        if not fold:
            c_ref[b] = c[rows, :FOX_HEADS]
        mem_ref[b] = _mem_values(_mem_probs(mem_logits[b]), mv_ref, b).astype(BF16)


def _layer1_front_io(x, mem_k, mem_v, p, *, nb, tt, fold):
    b, t, _ = x.shape
    assert (nb == 1) if fold else (t == tt)
    tile = lambda w: pl.BlockSpec((nb, tt, w), lambda i, j: (i, j, 0))
    act = lambda w, dt: jax.ShapeDtypeStruct((b, t, w), dt)
    kv_spec = pl.BlockSpec((1, MIX_W, tt), lambda i, j: (i, 0, j)) if fold else tile(MIX_W)
    kv_shape = jax.ShapeDtypeStruct((b, MIX_W, t), F32) if fold else act(MIX_W, F32)
    out_specs = [kv_spec, kv_spec, pl.BlockSpec((1, FOX_HEADS, nb * tt), lambda i, j: (i, 0, j)), tile(MEM_W)]
    out_shape = [kv_shape, kv_shape, jax.ShapeDtypeStruct((b // nb, FOX_HEADS, nb * t), F32), act(MEM_W, BF16)]
    if fold:
        wf = FOX_HEADS * PAIR_W
        tile_t = pl.BlockSpec((1, 1, wf, tt), lambda i, j: (i, j, 0, 0))
        act_t = jax.ShapeDtypeStruct((b, t // tt, wf, tt), BF16)
        out_specs += [tile_t, tile(wf), tile_t]
        out_shape += [act_t, act(wf, BF16), act_t]
        scratch = [pltpu.VMEM((1, V7X_LANES), F32)]
    else:
        out_specs += [tile(MIX_W)] * 3 + [tile(FOX_HEADS),
                                          pl.BlockSpec((1, 1, FOX_HEADS, nb * tt), lambda i, j: (i, j, 0, 0))]
        out_shape += [act(MIX_W, BF16)] * 3 + [act(FOX_HEADS, F32),
                                               jax.ShapeDtypeStruct((b // nb, t // tt, FOX_HEADS, nb * tt), F32)]
        scratch = []
    return _CallIO(
        in_specs=[tile(D_MODEL), _mem_spec(mem_k, 1, nb), _mem_spec(mem_v, 1, nb),
                  _const_spec(p["gains"].shape),
                  _const_spec(p["w_kv"].shape), _const_spec(p["w_f"].shape), _const_spec(p["w_ft"].shape),
                  _const_spec(p["b_f"].shape), _const_spec(p["b_ft"].shape), _layer_spec(p["w_in"][1], 0)],
        args=[x, mem_k, mem_v, p["gains"], p["w_kv"], p["w_f"], p["w_ft"], p["b_f"], p["b_ft"], p["w_in"][1]],
        out_specs=out_specs, out_shape=out_shape, scratch=scratch)


def _layer1_front(x, mem_k, mem_v, p, *, nb, tt, fold):
    return _tile_call(functools.partial(_layer1_front_kernel, nb=nb, tt=tt, fold=fold), "layer1_front", x, nb, tt,
                      _layer1_front_io(x, mem_k, mem_v, p, nb=nb, tt=tt, fold=fold))


def _layer0_front_kernel(*refs, n_in, n_out, nb, tt, pos_base, side, fold):
    n_front_out = 7 if fold else 9
    n_all_in = len(refs) - (n_out + n_front_out) - (2 if fold else 1)
    in0, in1 = refs[:n_in], refs[n_in:n_all_in]
    out0, out1 = refs[n_all_in:n_all_in + n_out], refs[n_all_in + n_out:n_all_in + n_out + n_front_out]
    halo_ref, *scratch1 = refs[n_all_in + n_out + n_front_out:]
    _layer0_kernel(*in0, *out0, halo_ref, nb=nb, tt=tt, pos_base=pos_base, side=side)
    _layer1_front_kernel(out0[0], *in1, *out1, *scratch1, nb=nb, tt=tt, fold=fold)


def _layer0_front(x, prev, mem_k, mem_v, p, *, nb, tt, pos_base, fold, side=()):
    io0 = _layer0_io(x, prev, mem_k, mem_v, p, nb=nb, tt=tt, side=side)
    io1 = _layer1_front_io(x, mem_k, mem_v, p, nb=nb, tt=tt, fold=fold)
    io = _CallIO(io0.in_specs + io1.in_specs[1:], io0.args + io1.args[1:], io0.out_specs + io1.out_specs,
                 io0.out_shape + io1.out_shape, io0.scratch + io1.scratch)
    return _tile_call(
        functools.partial(_layer0_front_kernel, n_in=len(io0.in_specs), n_out=len(io0.out_specs), nb=nb, tt=tt,
                          pos_base=pos_base, side=tuple(job[0] for job in side), fold=fold),
        "layer0_front", x, nb, tt, io)


def _fox_prompt_kernel(qt_ref, k_ref, vt_ref, o_ref, m_ref, acc_ref, *s_refs, t):
    i = pl.program_id(1)
    half = t // 2
    sub = lax.broadcasted_iota(jnp.int32, (PAIR_W, 1), 0)
    col = [slice(h * PAIR_W, (h + 1) * PAIR_W) for h in range(FOX_HEADS)]
    qt = [qt_ref[0, 0, c, :] for c in col]
    slot = lambda n: s_refs[n % len(s_refs)]

    def chain(n_items, logits, update):
        top = [logits(n) for n in range(FOX_AHEAD)]
        for n in range(n_items):
            if n + FOX_AHEAD < n_items:
                top.append(logits(n + FOX_AHEAD))
            update(n, top[n])

    tri = (lax.broadcasted_iota(jnp.int32, (half, half), 0) <= lax.broadcasted_iota(jnp.int32, (half, half), 1))
    lo = pl.ds(pl.multiple_of(i * t, t), half)
    hi = pl.ds(pl.multiple_of(i * t + half, half), half)

    def diag_logits(h):
        s_lo = _dot(k_ref[0, lo, col[h]], qt[h])
        s_hi = _dot(k_ref[0, hi, col[h]], qt[h][:, half:])
        s_lo = jnp.concatenate([jnp.where(tri, s_lo[:, :half], NEG_BIG), s_lo[:, half:]], axis=1)
        s_hi = jnp.where(tri, s_hi, NEG_BIG)
        slot(h)[:half, :] = s_lo
        slot(h)[half:, half:] = s_hi
        top = jnp.max(s_lo, axis=0, keepdims=True)
        return jnp.concatenate([top[:, :half], jnp.maximum(top[:, half:], jnp.max(s_hi, axis=0, keepdims=True))],
                               axis=1)

    def diag_update(h, m):
        p_lo = jnp.exp2(slot(h)[:half, :] - m).astype(BF16)
        p_hi = jnp.exp2(slot(h)[half:, half:] - m[:, half:]).astype(BF16)
        acc = _dot(vt_ref[0, i, col[h], :half], p_lo)
        acc_ref[h] = jnp.concatenate([acc[:, :half], acc[:, half:] + _dot(vt_ref[0, i, col[h], half:], p_hi)], axis=1)
        m_ref[h] = m

    chain(FOX_HEADS, diag_logits, diag_update)

    def full_tiles(j0, n_tiles):
        def logits(n):
            rows = pl.ds(pl.multiple_of((j0 + n // FOX_HEADS) * t, t), t)
            s = _dot(k_ref[0, rows, col[n % FOX_HEADS]], qt[n % FOX_HEADS])
            slot(n)[...] = s
            return jnp.max(s, axis=0, keepdims=True)

        def update(n, top):
            h = n % FOX_HEADS
            m = m_ref[h]
            m_new = jnp.maximum(m, top)
            p = jnp.exp2(slot(n)[...] - m_new).astype(BF16)
            acc_ref[h] = jnp.exp2(m - m_new) * acc_ref[h] + _dot(vt_ref[0, j0 + n // FOX_HEADS, col[h], :], p)
            m_ref[h] = m_new

        chain(n_tiles * FOX_HEADS, logits, update)

    lax.fori_loop(0, i // 2, lambda jj, c: full_tiles(2 * jj, 2), None)
    lax.fori_loop(2 * (i // 2), i, lambda j, c: full_tiles(j, 1), None)
    for pair in range(FOX_HEADS // 2):
        even, odd = acc_ref[2 * pair], acc_ref[2 * pair + 1]
        own = sub < FOX_HD
        out = jnp.where(own, even, odd) / jnp.where(own, even[FOX_HD:FOX_HD + 1, :], odd[0:1, :])
        o_ref[0, :, pair * PAIR_W:(pair + 1) * PAIR_W] = out.T.astype(BF16)


def _fox_prompt(qt, ka, vt):
    b, n_tiles, w, tile = qt.shape
    per_stream = lambda shape: pl.BlockSpec((1,) + shape, lambda i, j: (i,) + (0,) * len(shape))
    return pl.pallas_call(
        functools.partial(_fox_prompt_kernel, t=tile),
        grid=(b, n_tiles),
        in_specs=[pl.BlockSpec((1, 1, w, tile), lambda i, j: (i, j, 0, 0)),
                  per_stream((n_tiles * tile, w)), per_stream((n_tiles, w, tile))],
        out_specs=pl.BlockSpec((1, tile, MIX_W), lambda i, j: (i, j, 0)),
        out_shape=jax.ShapeDtypeStruct((b, n_tiles * tile, MIX_W), BF16),
        scratch_shapes=([pltpu.VMEM((FOX_HEADS, 1, tile), F32), pltpu.VMEM((FOX_HEADS, PAIR_W, tile), F32)]
                        + [pltpu.VMEM((tile, tile), F32)] * (FOX_AHEAD + 1)),
        compiler_params=pltpu.CompilerParams(dimension_semantics=("arbitrary", "arbitrary"),
                                             vmem_limit_bytes=VMEM_LIMIT),
        name="fox_prompt",
    )(qt, ka, vt)


def _fox_sample_kernel(q_ref, kc_ref, vc_ref, kn_ref, vn_ref, lfc_t_ref, c_ref, ct_ref, o_ref):
    tn = q_ref.shape[1]
    p = lfc_t_ref.shape[2]
    cc = _cumsum(lfc_t_ref[0], 1, p)
    ck_cache = cc - cc[:, p - 1:]
    causal = (lax.broadcasted_iota(jnp.int32, (tn, V7X_LANES), 1)
              <= lax.broadcasted_iota(jnp.int32, (tn, V7X_LANES), 0))
    pad = jnp.zeros((V7X_LANES - tn, FOX_HD), BF16)
    col = [slice(h * FOX_HD, (h + 1) * FOX_HD) for h in range(FOX_HEADS)]
    logits = []
    for h in range(FOX_HEADS):
        qh = q_ref[0, :, col[h]]
        cq = c_ref[0, :, h:h + 1]
        kn = jnp.concatenate([kn_ref[0, :, col[h]], pad], axis=0)
        s_old = _dot(qh, kc_ref[0, h].astype(BF16)) + cq - ck_cache[h:h + 1, :]
        s_new = jnp.where(causal, _dot_nt(qh, kn) + cq - ct_ref[0, h:h + 1, :], NEG_BIG)
        logits.append((s_old, s_new))
    probs = []
    for s_old, s_new in logits:
        m = jnp.maximum(jnp.max(s_old, axis=-1, keepdims=True), jnp.max(s_new, axis=-1, keepdims=True))
        p_old = jnp.exp(s_old - m)
        p_new = jnp.exp(s_new - m)
        l = jnp.sum(p_old, axis=-1, keepdims=True) + jnp.sum(p_new, axis=-1, keepdims=True)
        probs.append((p_old.astype(BF16), p_new.astype(BF16), l))
    outs = []
    for h, (p_old, p_new, l) in enumerate(probs):
        vn = jnp.concatenate([vn_ref[0, :, col[h]], pad], axis=0)
        outs.append((_dot_nt(p_old, vc_ref[0, h].astype(BF16)) + _dot(p_new, vn)) / l)
    o_ref[0] = jnp.concatenate(outs, axis=-1).astype(BF16)


def _fox_sample(q, cache_kt, cache_vt, kb, vb, cache_logf_t, c, ct):
    b, tn, _ = q.shape
    p = cache_kt.shape[3]
    new = lambda w: pl.BlockSpec((1, tn, w), lambda i: (i, 0, 0))
    old = pl.BlockSpec((1, FOX_HEADS, FOX_HD, p), lambda i: (i, 0, 0, 0))
    return pl.pallas_call(
        _fox_sample_kernel,
        grid=(b,),
        in_specs=[new(MIX_W), old, old, new(MIX_W), new(MIX_W),
                  pl.BlockSpec((1, FOX_HEADS, p), lambda i: (i, 0, 0)),
                  new(FOX_HEADS), pl.BlockSpec((1, FOX_HEADS, V7X_LANES), lambda i: (i, 0, 0))],
        out_specs=new(MIX_W),
        out_shape=jax.ShapeDtypeStruct((b, tn, MIX_W), BF16),
        compiler_params=pltpu.CompilerParams(dimension_semantics=("arbitrary",), vmem_limit_bytes=VMEM_LIMIT),
        name="fox_sample",
    )(q, cache_kt, cache_vt, kb, vb, cache_logf_t, c, ct)


def _layer1_back_kernel(x_ref, mix_ref, mem_ref, g_ref, w_out, w_up, w_down, *refs, nb, tt, side):
    n = len(side)
    y_ref = refs[n]
    m = nb * tt
    x = x_ref[...].reshape(m, D_MODEL)
    mix = mix_ref[...].reshape(m, MIX_W)
    mem = mem_ref[...].reshape(m, MEM_W)
    x1 = [x[rows] + _rms(_dot(mix[rows], w_out[:MIX_W, :]) + _dot(mem[rows], w_out[MIX_W:, :]),
                         _gain(g_ref, "mix_post", 1)) for rows in _row_parts(m)]
    y = jnp.concatenate(_mlp(x1, _gain(g_ref, "mlp_pre", 1), _gain(g_ref, "mlp_post", 1), w_up, w_down), axis=0)
    y_ref[...] = y.reshape(nb, tt, D_MODEL)
    _side_jobs(side, refs[:n], refs[n + 1:])


def _layer1_back(x, mix, mem, p, *, nb, tt, side=()):
    b, t, _ = x.shape
    grid = (b // nb, t // tt)
    tile = lambda w: pl.BlockSpec((nb, tt, w), lambda i, j: (i, j, 0))
    side_in, side_out, side_shape = _side_specs(side, grid[0] * grid[1], lambda i, j: i * grid[1] + j)
    return pl.pallas_call(
        functools.partial(_layer1_back_kernel, nb=nb, tt=tt, side=tuple(job[0] for job in side)),
        grid=grid,
        in_specs=[tile(D_MODEL), tile(MIX_W), tile(MEM_W), _const_spec(p["gains"].shape),
                  _layer_spec(p["w_out"][1], 0), _layer_spec(p["w_up"][1], 0), _layer_spec(p["w_down"][1], 0)]
                 + side_in,
        out_specs=[tile(D_MODEL)] + side_out,
        out_shape=[jax.ShapeDtypeStruct((b, t, D_MODEL), F32)] + side_shape,
        compiler_params=pltpu.CompilerParams(dimension_semantics=("arbitrary", "arbitrary"),
                                             vmem_limit_bytes=VMEM_LIMIT),
        name="layer1_back",
    )(x, mix, mem, p["gains"], p["w_out"][1], p["w_up"][1], p["w_down"][1],
      *[job[1] for job in side])


def _layer1_back_streamed_kernel(x_ref, mix_ref, mem_ref, g_ref, w_out, w_up, w_down, y_ref, x1_ref, hm_ref, acc_ref):
    c = pl.program_id(0)

    @pl.when(c == 0)
    def _():
        x = x_ref[...].reshape(x1_ref.shape)
        mix = mix_ref[...].reshape(x1_ref.shape[0], MIX_W)
        mem = mem_ref[...].reshape(x1_ref.shape[0], MEM_W)
        x1 = x + _rms(_dot(mix, w_out[:MIX_W, :]) + _dot(mem, w_out[MIX_W:, :]), _gain(g_ref, "mix_post", 1))
        x1_ref[...] = x1
        hm_ref[...] = _rms(x1, _gain(g_ref, "mlp_pre", 1)).astype(BF16)
        acc_ref[...] = jnp.zeros_like(acc_ref)

    a = jnp.square(jnp.maximum(_dot(hm_ref[...], w_up[...]), 0.0)).astype(BF16)
    acc_ref[...] += _dot(a, w_down[...])

    @pl.when(c == pl.num_programs(0) - 1)
    def _():
        y_ref[...] = (x1_ref[...] + _rms(acc_ref[...], _gain(g_ref, "mlp_post", 1))).reshape(y_ref.shape)


def _layer1_back_streamed(x, mix, mem, p):
    b, t, _ = x.shape
    whole = lambda w: pl.BlockSpec((b, t, w), lambda c: (0, 0, 0))
    return pl.pallas_call(
        _layer1_back_streamed_kernel,
        grid=(D_FF // FF_CHUNK,),
        in_specs=[whole(D_MODEL), whole(MIX_W), whole(MEM_W), _const_spec(p["gains"].shape),
                  pl.BlockSpec((None, MIX_W + MEM_W, D_MODEL), lambda c: (0, 0, 0)),
                  pl.BlockSpec((None, D_MODEL, FF_CHUNK), lambda c: (0, 0, c)),
                  pl.BlockSpec((None, FF_CHUNK, D_MODEL), lambda c: (0, c, 0))],
        out_specs=whole(D_MODEL),
        out_shape=jax.ShapeDtypeStruct((b, t, D_MODEL), F32),
        scratch_shapes=[pltpu.VMEM((b * t, D_MODEL), F32), pltpu.VMEM((b * t, D_MODEL), BF16),
                        pltpu.VMEM((b * t, D_MODEL), F32)],
        compiler_params=pltpu.CompilerParams(dimension_semantics=("arbitrary",), vmem_limit_bytes=VMEM_LIMIT),
        name="layer1_back_streamed",
    )(x, mix, mem, p["gains"], p["w_out"][1], p["w_up"][1], p["w_down"][1])


def _token_tile(b, t):
    if t >= TOKEN_TILE:
        assert t % TOKEN_TILE == 0
        return 1, TOKEN_TILE
    return b, t


def _trunk(x, pool_prev, cache, mem_k, mem_v, p, later_f32=None, side=((), ())):
    b, t, _ = x.shape
    nb, tt = _token_tile(b, t)
    pos_base = 0 if cache is None else POOL_BUF
    names = sorted(later_f32) if later_f32 else []
    side0 = [("cast", later_f32[n], 1) for n in names] + list(side[0])
    if names:
        y0, pool_state, *done = _layer0(x, pool_prev, mem_k, mem_v, p, nb=nb, tt=tt, pos_base=pos_base, side=side0)
        p = {**p, **{n: [p[n][0], w] for n, w in zip(names, done)}}
        front = _layer1_front(y0, mem_k, mem_v, p, nb=nb, tt=tt, fold=cache is None)
    else:
        y0, pool_state, *rest = _layer0_front(x, pool_prev, mem_k, mem_v, p, nb=nb, tt=tt, pos_base=pos_base,
                                              fold=cache is None, side=side0)
        done, front = rest[:len(side0)], rest[len(side0):]
    done_first = done[len(names):]
    k, v, logf_t, mem, *att = front
    logf = jnp.transpose(logf_t.reshape(b // nb, FOX_HEADS, nb, t), (0, 2, 3, 1)).reshape(b, t, FOX_HEADS)
    if cache is None:
        mix = _fox_prompt(*att)
    else:
        q, kb, vb, c, ct = att
        ct_new = jnp.swapaxes(ct.reshape(FOX_HEADS, b, t), 0, 1)
        ct_new = jnp.pad(ct_new, ((0, 0), (0, 0), (0, V7X_LANES - t)))
        mix = _fox_sample(q, *cache[:2], kb, vb, cache[2], c, ct_new)
    if (nb, tt) == (b, t) and not side[1]:
        y, done_last = _layer1_back_streamed(y0, mix, mem, p), []
    else:
        y, *done_last = _layer1_back(y0, mix, mem, p, nb=nb, tt=tt, side=side[1])
    if cache is None:
        heads = lambda a: jnp.transpose(a.reshape(b, FOX_HEADS, FOX_HD, t), (0, 3, 1, 2))
    else:
        heads = lambda a: a.reshape(b, t, FOX_HEADS, FOX_HD)
    return (y, pool_state, heads(k), heads(v), logf), p, done_first, done_last


def kernel(x_prompt, x_sample, cache_pool, cache_k, cache_v, cache_logf, cache_mem_k, cache_mem_v, mem_prompt,
           g_mix_pre, g_mix_post, g_mlp_pre, g_mlp_post, w_in, w_out, w_pool, pool_scale, g_kv, w_kvf, b_f,
           g_mem, w_mem_kv, w_up, w_down):
    depth = w_in.shape[0]
    assert depth == 2 and w_pool.shape[0] == 1, "one pooling layer followed by one forgetting-attention layer"
    bp = x_prompt.shape[0]
    w_f = w_kvf[:, 2 * MIX_W:]
    fold_cols = lambda a: jnp.pad(jnp.concatenate([a, jnp.repeat(a, FOLD_W, axis=-1)], axis=-1),
                                  [(0, 0)] * (a.ndim - 1) + [(0, V7X_LANES - FOLD_SRC - FOLD_W * FOX_HEADS)])
    rows = lambda g: g.reshape(g.shape[0], 1, g.shape[1])
    p = dict(
        gains=_pack_gains(g_mix_pre, g_mix_post, g_mlp_pre, g_mlp_post, g_kv), pool_scale=rows(pool_scale),
        w_pool=w_pool.astype(BF16), w_kv=w_kvf[:, :2 * MIX_W].astype(BF16),
        w_f=fold_cols(w_f).astype(BF16), w_ft=w_f.T.astype(BF16),
        b_f=fold_cols(b_f).reshape(1, V7X_LANES), b_ft=b_f.reshape(FOX_HEADS, 1),
    )
    later_f32 = dict(w_in=w_in, w_out=w_out, w_up=w_up, w_down=w_down)
    names = sorted(later_f32)
    mem_k_prompt, mem_v_prompt, mkh, mvh, *w0 = _mem_project(mem_prompt, g_mem, w_mem_kv,
                                                             side=[("cast", later_f32[n], 0) for n in names])
    p.update({n: [w, None] for n, w in zip(names, w0)})

    heads_view = lambda a: a.reshape(a.shape[:2] + (-1, a.shape[-1]))
    (y_p, pool_p, k_p, v_p, logf_p), p, (cmk,), (cmv,) = _trunk(
        x_prompt, jnp.zeros((bp, HALO, MIX_W), F32), None, mkh, mvh, p, later_f32,
        side=([("heads", heads_view(cache_mem_k))], [("heads", heads_view(cache_mem_v))]))

    pool_prev = jnp.pad(cache_pool[0], ((0, 0), (HALO - POOL_BUF, 0), (0, 0)))
    lanes_view = lambda a: jnp.transpose(a, (0, 2, 3, 1))
    cache = (lanes_view(cache_k), lanes_view(cache_v), jnp.swapaxes(cache_logf, 1, 2))
    (y_s, pool_s, k_s, v_s, logf_s), _, _, _ = _trunk(x_sample, pool_prev, cache, cmk, cmv, p)

    return (y_p, y_s, pool_p, pool_s, k_p, v_p, logf_p, k_s, v_s, logf_s, mem_k_prompt, mem_v_prompt)
```

```python
import functools
from typing import NamedTuple

import jax
import jax.numpy as jnp
from jax import lax
from jax.experimental import pallas as pl
from jax.experimental.pallas import tpu as pltpu

F32 = jnp.float32
BF16 = jnp.bfloat16

D_MODEL = 1024
MIX_W = D_MODEL // 2
MEM_W = D_MODEL - MIX_W
POOL_WINDOWS = (2, 4, 8, 16)
POOL_GROUP = MIX_W // len(POOL_WINDOWS)
POOL_BUF = max(POOL_WINDOWS) - 1
FOX_HEADS = 8
FOX_HD = MIX_W // FOX_HEADS
MEM_HEADS = 4
MEM_HD = MEM_W // MEM_HEADS
D_FF = 4 * D_MODEL
EPS = 1e-6

V7X_LANES = 128
V7X_SUBLANES_F32 = 8
V7X_VMEM_BYTES = 64 * 1024 * 1024

HALO = 2 * V7X_SUBLANES_F32
FF_CHUNK = 1024
TOKEN_TILE = 512
ROW_PARTS = 2
PAIR_W = 2 * FOX_HD
NEG_BIG = -1e30
LOG2E = 1.4426950408889634
FOLD_SRC = FOX_HEADS
FOLD_W = 6
FOX_AHEAD = 2
assert FOLD_SRC + FOLD_W * FOX_HEADS <= V7X_LANES
VMEM_LIMIT = V7X_VMEM_BYTES - 8 * 1024 * 1024

assert HALO >= POOL_BUF and PAIR_W == V7X_LANES and POOL_GROUP == V7X_LANES and MEM_HD == V7X_LANES


def _const_spec(shape):
    zeros = (0,) * len(shape)
    return pl.BlockSpec(shape, lambda *_: zeros, pipeline_mode=pl.Buffered(1))


def _layer_spec(stacked, layer):
    index = (layer,) + (0,) * (stacked.ndim - 1)
    return pl.BlockSpec((None,) + stacked.shape[1:], lambda *_: index, pipeline_mode=pl.Buffered(1))


def _mem_spec(mem, layer, nb):
    return pl.BlockSpec((None, nb) + mem.shape[2:], lambda i, j: (layer, i, 0, 0, 0))


def _unit_rms(x):
    return x * lax.rsqrt(jnp.mean(x * x, axis=-1, keepdims=True) + EPS)


def _rms(x, g):
    return _unit_rms(x) * g


GAIN_KINDS = ("mix_pre", "mix_post", "mlp_pre", "mlp_post")


def _gain(g_ref, kind, layer=0):
    depth = (g_ref.shape[0] - 1) // len(GAIN_KINDS)
    r = len(GAIN_KINDS) * depth if kind == "kv" else GAIN_KINDS.index(kind) * depth + layer
    return g_ref[r:r + 1, :]


def _pack_gains(g_mix_pre, g_mix_post, g_mlp_pre, g_mlp_post, g_kv):
    return jnp.concatenate([g_mix_pre, g_mix_post, g_mlp_pre, g_mlp_post, g_kv[None, :]], axis=0)


def _dot(a, b):
    return jnp.dot(a, b, preferred_element_type=F32)


def _dot_nt(a, b):
    return lax.dot_general(a, b, (((1,), (1,)), ((), ())), preferred_element_type=F32)


def _log_sigmoid(x):
    return jnp.minimum(x, 0.0) - jnp.log1p(jnp.exp(-jnp.abs(x)))


def _cumsum(x, axis, seg):
    assert seg & (seg - 1) == 0 and x.shape[axis] % seg == 0
    n = x.shape[axis]
    vreg = (V7X_SUBLANES_F32, V7X_LANES)[axis]
    idx = lax.broadcasted_iota(jnp.int32, x.shape, axis) & (seg - 1)
    k = 1
    while k < seg:
        if seg == n and k % vreg == 0:
            zeros = jnp.zeros(x.shape[:axis] + (k,) + x.shape[axis + 1:], x.dtype)
            x = x + jnp.concatenate([zeros, lax.slice_in_dim(x, 0, n - k, axis=axis)], axis=axis)
        else:
            x = x + jnp.where(idx >= k, pltpu.roll(x, k, axis=axis), 0.0)
        k *= 2
    return x


def _mem_logits(q, mk_ref, b):
    qs = (q * (MEM_HD ** -0.5 * LOG2E)).astype(BF16)
    return [_dot_nt(qs[:, h * MEM_HD:(h + 1) * MEM_HD], mk_ref[b, h]) for h in range(MEM_HEADS)]


def _mem_probs(logits):
    probs = []
    for s in logits:
        p = jnp.exp2(s - jnp.max(s, axis=-1, keepdims=True))
        probs.append((p.astype(BF16), jnp.sum(p, axis=-1, keepdims=True)))
    return probs


def _mem_values(probs, mv_ref, b):
    return jnp.concatenate([_dot(p, mv_ref[b, h]) / l for h, (p, l) in enumerate(probs)], axis=-1)


def _pool_mixer(u, halo, pos0, w_pool_ref, scale):
    tt = u.shape[0]
    ext = jnp.concatenate([halo, u], axis=0)
    pos = pos0 + lax.broadcasted_iota(jnp.int32, (tt, 1), 0)
    outs = []
    for g, w in enumerate(POOL_WINDOWS):
        sl = slice(g * POOL_GROUP, (g + 1) * POOL_GROUP)
        s = ext[:, sl]
        k = 1
        while k < w:
            s = s + pltpu.roll(s, k, axis=0)
            k *= 2
        cnt = jnp.minimum(pos + 1, w).astype(F32)
        pooled = s[HALO:] / cnt - u[:, sl]
        outs.append(_dot(pooled.astype(BF16), w_pool_ref[g]))
    return jnp.concatenate(outs, axis=-1) * scale


def _row_parts(m):
    n = ROW_PARTS if m >= TOKEN_TILE else 1
    return [slice(i * m // n, (i + 1) * m // n) for i in range(n)]


def _mlp(x1_parts, g_pre, g_post, w_up_ref, w_down_ref):
    n_chunks = D_FF // FF_CHUNK
    chunk = lambda c: slice(c * FF_CHUNK, (c + 1) * FF_CHUNK)
    act = lambda up: jnp.square(jnp.maximum(up, 0.0)).astype(BF16)
    hm_parts = [_rms(x1, g_pre).astype(BF16) for x1 in x1_parts]
    hm = jnp.concatenate(hm_parts, axis=0)
    parts = _row_parts(hm.shape[0])
    acc = None
    for c in range(n_chunks):
        if c == 0:
            a = jnp.concatenate([act(_dot(h, w_up_ref[:, chunk(c)])) for h in hm_parts], axis=0)
        else:
            a = act(_dot(hm, w_up_ref[:, chunk(c)]))
        if c < n_chunks - 1:
            d = _dot(a, w_down_ref[chunk(c), :])
            acc = d if acc is None else acc + d
    return [x1 + _rms(acc[rows] + _dot(a[rows], w_down_ref[chunk(n_chunks - 1), :]), g_post)
            for x1, rows in zip(x1_parts, parts)]


def _side_specs(jobs, steps, step_of):
    in_specs, out_specs, out_shape = [], [], []
    for kind, a, *rest in jobs:
        if kind == "cast":
            rows = a.shape[1] // steps
            in_specs.append(pl.BlockSpec((1, rows, a.shape[2]), lambda *ids, l=rest[0]: (l, step_of(*ids), 0)))
            out_specs.append(pl.BlockSpec((1, rows, a.shape[2]), lambda *ids: (0, step_of(*ids), 0)))
            out_shape.append(jax.ShapeDtypeStruct((1,) + a.shape[1:], BF16))
        else:
            n_l, n_s, nh, d = a.shape
            h = MEM_HEADS
            n = nh // h
            rows = n_l * n_s * n // steps
            parts = n // rows
            assert kind == "heads" and rows * parts == n

            def where(*ids, parts=parts, n_s=n_s):
                step = step_of(*ids)
                return step // parts // n_s, step // parts % n_s, step % parts

            in_specs.append(pl.BlockSpec((1, 1, rows * h, d), lambda *ids, w=where: w(*ids) + (0,)))
            out_specs.append(pl.BlockSpec((1, 1, h, rows, d),
                                          lambda *ids, w=where: w(*ids)[:2] + (0, w(*ids)[2], 0)))
            out_shape.append(jax.ShapeDtypeStruct((n_l, n_s, h, n, d), BF16))
    return in_specs, out_specs, out_shape


def _side_jobs(kinds, src_refs, dst_refs):
    for kind, src, dst in zip(kinds, src_refs, dst_refs):
        if kind == "cast":
            dst[...] = src[...].astype(BF16)
        else:
            x = src[0, 0].astype(BF16)
            heads, rows = dst.shape[2], dst.shape[3]
            row = lax.broadcasted_iota(jnp.int32, (rows, rows * heads), 0)
            pick = lax.broadcasted_iota(jnp.int32, (rows, rows * heads), 1) - heads * row
            for h in range(heads):
                dst[0, 0, h] = _dot((pick == h).astype(BF16), x).astype(BF16)


def _mem_project_kernel(mem_ref, g_ref, w_ref, *refs, side):
    n = len(side)
    mk_ref, mv_ref, mkh_ref, mvh_ref = refs[n:n + 4]
    _side_jobs(side, refs[:n], refs[n + 4:])
    xn = _unit_rms(mem_ref[0])
    for l in range(w_ref.shape[0]):
        kv = _dot((xn * g_ref[l:l + 1, :]).astype(BF16), w_ref[l].astype(BF16))
        for h in range(MEM_HEADS):
            mk = kv[:, h * MEM_HD:(h + 1) * MEM_HD]
            mv = kv[:, MEM_W + h * MEM_HD:MEM_W + (h + 1) * MEM_HD]
            mk_ref[l, 0, :, h, :] = mk
            mv_ref[l, 0, :, h, :] = mv
            mkh_ref[l, 0, h] = mk.astype(BF16)
            mvh_ref[l, 0, h] = mv.astype(BF16)


def _mem_project(mem, g_mem, w_mem_kv, side=()):
    b, n, _ = mem.shape
    depth = w_mem_kv.shape[0]
    out = jax.ShapeDtypeStruct((depth, b, n, MEM_HEADS, MEM_HD), F32)
    out_h = jax.ShapeDtypeStruct((depth, b, MEM_HEADS, n, MEM_HD), BF16)
    side_in, side_out, side_shape = _side_specs(side, b, lambda i: i)
    return pl.pallas_call(
        functools.partial(_mem_project_kernel, side=tuple(job[0] for job in side)),
        grid=(b,),
        in_specs=[pl.BlockSpec((1, n, D_MODEL), lambda i: (i, 0, 0)),
                  _const_spec(g_mem.shape), _const_spec(w_mem_kv.shape)] + side_in,
        out_specs=([pl.BlockSpec((depth, 1, n, MEM_HEADS, MEM_HD), lambda i: (0, i, 0, 0, 0))] * 2
                   + [pl.BlockSpec((depth, 1, MEM_HEADS, n, MEM_HD), lambda i: (0, i, 0, 0, 0))] * 2 + side_out),
        out_shape=[out, out, out_h, out_h] + side_shape,
        compiler_params=pltpu.CompilerParams(dimension_semantics=("arbitrary",), vmem_limit_bytes=VMEM_LIMIT),
        name="mem_project",
    )(mem, g_mem, w_mem_kv, *[job[1] for job in side])


def _layer0_kernel(x_ref, prev_ref, mk_ref, mv_ref, g_ref, w_in, w_out, w_pool, pscale, w_up, w_down, *refs,
                   nb, tt, pos_base, side):
    n = len(side)
    side_in, (y_ref, state_ref), side_out = refs[:n], refs[n:n + 2], refs[n + 2:2 * n + 2]
    halo_ref, = refs[2 * n + 2:]
    t = pl.program_id(1)

    @pl.when(t == 0)
    def _():
        halo_ref[...] = prev_ref[...]

    x = x_ref[...].reshape(nb * tt, D_MODEL)
    proj = jnp.concatenate([_dot(_rms(x[rows], _gain(g_ref, "mix_pre")).astype(BF16), w_in[...])
                            for rows in _row_parts(nb * tt)], axis=0)
    cats = []
    for b in range(nb):
        rows = slice(b * tt, (b + 1) * tt)
        u = proj[rows, :MIX_W]
        logits = _mem_logits(proj[rows, MIX_W:], mk_ref, b)
        mix = _pool_mixer(u, halo_ref[b], pos_base + t * tt, w_pool, pscale[...])
        halo_ref[b] = u[tt - HALO:, :]
        state_ref[0, b] = u[tt - POOL_BUF:, :]
        mem = _mem_values(_mem_probs(logits), mv_ref, b)
        cats.append(jnp.concatenate([mix, mem], axis=-1))
    cat = (cats[0] if nb == 1 else jnp.concatenate(cats, axis=0)).astype(BF16)
    x1 = [x[rows] + _rms(_dot(cat[rows], w_out[...]), _gain(g_ref, "mix_post")) for rows in _row_parts(nb * tt)]
    y = jnp.concatenate(_mlp(x1, _gain(g_ref, "mlp_pre"), _gain(g_ref, "mlp_post"), w_up, w_down), axis=0)
    y_ref[...] = y.reshape(nb, tt, D_MODEL)
    _side_jobs(side, side_in, side_out)


class _CallIO(NamedTuple):
    in_specs: list
    args: list
    out_specs: list
    out_shape: list
    scratch: list


def _tile_call(kernel, name, x, nb, tt, io):
    return pl.pallas_call(
        kernel,
        grid=(x.shape[0] // nb, x.shape[1] // tt),
        in_specs=io.in_specs,
        out_specs=io.out_specs,
        out_shape=io.out_shape,
        scratch_shapes=io.scratch,
        compiler_params=pltpu.CompilerParams(dimension_semantics=("arbitrary", "arbitrary"),
                                             vmem_limit_bytes=VMEM_LIMIT),
        name=name,
    )(*io.args)


def _layer0_io(x, prev, mem_k, mem_v, p, *, nb, tt, side):
    b, t, _ = x.shape
    grid = (b // nb, t // tt)
    tile = pl.BlockSpec((nb, tt, D_MODEL), lambda i, j: (i, j, 0))
    side_in, side_out, side_shape = _side_specs(side, grid[0] * grid[1], lambda i, j: i * grid[1] + j)
    return _CallIO(
        in_specs=[tile, pl.BlockSpec((nb, HALO, MIX_W), lambda i, j: (i, 0, 0)),
                  _mem_spec(mem_k, 0, nb), _mem_spec(mem_v, 0, nb),
                  _const_spec(p["gains"].shape), _layer_spec(p["w_in"][0], 0), _layer_spec(p["w_out"][0], 0),
                  _layer_spec(p["w_pool"], 0), _layer_spec(p["pool_scale"], 0),
                  _layer_spec(p["w_up"][0], 0), _layer_spec(p["w_down"][0], 0)] + side_in,
        args=[x, prev, mem_k, mem_v, p["gains"], p["w_in"][0], p["w_out"][0], p["w_pool"], p["pool_scale"],
              p["w_up"][0], p["w_down"][0]] + [job[1] for job in side],
        out_specs=[tile, pl.BlockSpec((1, nb, POOL_BUF, MIX_W), lambda i, j: (0, i, 0, 0))] + side_out,
        out_shape=[jax.ShapeDtypeStruct((b, t, D_MODEL), F32),
                   jax.ShapeDtypeStruct((1, b, POOL_BUF, MIX_W), F32)] + side_shape,
        scratch=[pltpu.VMEM((nb, HALO, MIX_W), F32)])


def _layer0(x, prev, mem_k, mem_v, p, *, nb, tt, pos_base, side=()):
    return _tile_call(
        functools.partial(_layer0_kernel, nb=nb, tt=tt, pos_base=pos_base, side=tuple(job[0] for job in side)),
        "layer0", x, nb, tt, _layer0_io(x, prev, mem_k, mem_v, p, nb=nb, tt=tt, side=side))


def _bf16_part(x):
    return x.astype(BF16).astype(F32)


def _fold_lanes(h):
    lane = lax.broadcasted_iota(jnp.int32, (1, V7X_LANES), 1)
    own = (h % 2) * FOX_HD
    e0 = FOX_HD - own
    in_head = (lane >= own) & (lane < own + FOX_HD)
    first = ((lane >= e0) & (lane < e0 + 3)).astype(F32)
    second = ((lane >= e0 + 3) & (lane < e0 + 6)).astype(F32)
    return in_head, first, second, lane == e0, e0


def _fold_bias(c):
    cl = c * LOG2E
    hi = _bf16_part(cl)
    mid = _bf16_part(cl - hi)
    lo = _bf16_part(cl - hi - mid)
    lane = lax.broadcasted_iota(jnp.int32, (1, V7X_LANES), 1)
    piece = lax.rem(lane + (3 - FOLD_SRC % 3), 3)
    d = jnp.where(piece == 0, hi, jnp.where(piece == 1, mid, lo))
    return [pltpu.roll(d, (_fold_lanes(h)[4] - (FOLD_SRC + FOLD_W * h)) % V7X_LANES, axis=1)
            for h in range(FOX_HEADS)]


def _fold_kv(k, v, bias):
    ks, vs = [], []
    for h, dh in enumerate(bias):
        in_head, first, second, one, _ = _fold_lanes(h)
        cols = slice((h // 2) * PAIR_W, (h // 2 + 1) * PAIR_W)
        ks.append(jnp.where(in_head, k[:, cols], first - dh * second).astype(BF16))
        vs.append(jnp.where(in_head, v[:, cols], one.astype(F32)).T.astype(BF16))
    return jnp.concatenate(ks, axis=-1), jnp.concatenate(vs, axis=0)


def _fold_q(q, bias):
    qs = []
    for h, dh in enumerate(bias):
        in_head, first, second, _, _ = _fold_lanes(h)
        cols = slice((h // 2) * PAIR_W, (h // 2 + 1) * PAIR_W)
        qs.append(jnp.where(in_head, q[:, cols] * (FOX_HD ** -0.5 * LOG2E), dh * first + second).T.astype(BF16))
    return jnp.concatenate(qs, axis=0)


def _layer1_front_kernel(x_ref, mk_ref, mv_ref, g_ref, w_kv, w_f, w_ft, b_f, b_ft, w_in, *refs,
                         nb, tt, fold):
    if fold:
        k_ref, v_ref, logft_ref, mem_ref, qa_ref, ka_ref, va_ref, carry_ref = refs
    else:
        k_ref, v_ref, logft_ref, mem_ref, q_ref, kb_ref, vb_ref, c_ref, ct_ref = refs
    if fold:
        @pl.when(pl.program_id(1) == 0)
        def _():
            carry_ref[...] = jnp.zeros_like(carry_ref)

    xn = _unit_rms(x_ref[...].reshape(nb * tt, D_MODEL))
    hk = (xn * _gain(g_ref, "kv")).astype(BF16)
    logf = _log_sigmoid(_dot(hk, w_f[...]) + b_f[...])
    k = _dot(hk, w_kv[:, :MIX_W])
    v = _dot(hk, w_kv[:, MIX_W:])
    if fold:
        k_ref[0] = k.T
        v_ref[0] = v.T
    else:
        k_ref[...] = k.reshape(nb, tt, MIX_W)
        v_ref[...] = v.reshape(nb, tt, MIX_W)
    c = _cumsum(logf, 0, tt)
    if fold:
        c = c + carry_ref[...]
        carry_ref[...] = c[tt - 1:, :]
        bias = _fold_bias(c)
        ka_ref[0], va_ref[0, 0] = _fold_kv(k, v, bias)
        logft_ref[0] = logf.T[:FOX_HEADS, :]
    proj = _dot((xn * _gain(g_ref, "mix_pre", 1)).astype(BF16), w_in[...])
    q = proj[:, :MIX_W]
    mem_logits = [_mem_logits(proj[b * tt:(b + 1) * tt, MIX_W:], mk_ref, b) for b in range(nb)]
    if fold:
        qa_ref[0, 0] = _fold_q(q, bias)
    else:
        q_ref[...] = (q * (FOX_HD ** -0.5)).astype(BF16).reshape(nb, tt, MIX_W)
        kb_ref[...] = k.astype(BF16).reshape(nb, tt, MIX_W)
        vb_ref[...] = v.astype(BF16).reshape(nb, tt, MIX_W)
        logf_t = _log_sigmoid(_dot_nt(w_ft[...], hk) + b_ft[...])
        logft_ref[0] = logf_t
        ct = _cumsum(logf_t, 1, tt)
        for b in range(nb):
            ct_ref[b] = jnp.concatenate([ct[:, b * tt:(b + 1) * tt],
                                         jnp.zeros((FOX_HEADS, V7X_LANES - tt), F32)], axis=1)
    for b in range(nb):
        rows = slice(b * tt, (b + 1) * tt)
        if not fold:
            c_ref[b] = c[rows, :FOX_HEADS]
        mem_ref[b] = _mem_values(_mem_probs(mem_logits[b]), mv_ref, b).astype(BF16)


def _layer1_front_io(x, mem_k, mem_v, p, *, nb, tt, fold):
    b, t, _ = x.shape
    assert (nb == 1) if fold else (t == tt)
    tile = lambda w: pl.BlockSpec((nb, tt, w), lambda i, j: (i, j, 0))
    act = lambda w, dt: jax.ShapeDtypeStruct((b, t, w), dt)
    kv_spec = pl.BlockSpec((1, MIX_W, tt), lambda i, j: (i, 0, j)) if fold else tile(MIX_W)
    kv_shape = jax.ShapeDtypeStruct((b, MIX_W, t), F32) if fold else act(MIX_W, F32)
    out_specs = [kv_spec, kv_spec, pl.BlockSpec((1, FOX_HEADS, nb * tt), lambda i, j: (i, 0, j)), tile(MEM_W)]
    out_shape = [kv_shape, kv_shape, jax.ShapeDtypeStruct((b // nb, FOX_HEADS, nb * t), F32), act(MEM_W, BF16)]
    if fold:
        wf = FOX_HEADS * PAIR_W
        tile_t = pl.BlockSpec((1, 1, wf, tt), lambda i, j: (i, j, 0, 0))
        act_t = jax.ShapeDtypeStruct((b, t // tt, wf, tt), BF16)
        out_specs += [tile_t, tile(wf), tile_t]
        out_shape += [act_t, act(wf, BF16), act_t]
        scratch = [pltpu.VMEM((1, V7X_LANES), F32)]
    else:
        out_specs += [tile(MIX_W)] * 3 + [tile(FOX_HEADS),
                                          pl.BlockSpec((nb, FOX_HEADS, V7X_LANES), lambda i, j: (i, 0, 0))]
        out_shape += [act(MIX_W, BF16)] * 3 + [act(FOX_HEADS, F32),
                                               jax.ShapeDtypeStruct((b, FOX_HEADS, V7X_LANES), F32)]
        scratch = []
    return _CallIO(
        in_specs=[tile(D_MODEL), _mem_spec(mem_k, 1, nb), _mem_spec(mem_v, 1, nb),
                  _const_spec(p["gains"].shape),
                  _const_spec(p["w_kv"].shape), _const_spec(p["w_f"].shape), _const_spec(p["w_ft"].shape),
                  _const_spec(p["b_f"].shape), _const_spec(p["b_ft"].shape), _layer_spec(p["w_in"][1], 0)],
        args=[x, mem_k, mem_v, p["gains"], p["w_kv"], p["w_f"], p["w_ft"], p["b_f"], p["b_ft"], p["w_in"][1]],
        out_specs=out_specs, out_shape=out_shape, scratch=scratch)


def _layer1_front(x, mem_k, mem_v, p, *, nb, tt, fold):
    return _tile_call(functools.partial(_layer1_front_kernel, nb=nb, tt=tt, fold=fold), "layer1_front", x, nb, tt,
                      _layer1_front_io(x, mem_k, mem_v, p, nb=nb, tt=tt, fold=fold))


def _layer0_front_kernel(*refs, n_in, n_out, nb, tt, pos_base, side, fold):
    n_front_out = 7 if fold else 9
    n_all_in = len(refs) - (n_out + n_front_out) - (2 if fold else 1)
    in0, in1 = refs[:n_in], refs[n_in:n_all_in]
    out0, out1 = refs[n_all_in:n_all_in + n_out], refs[n_all_in + n_out:n_all_in + n_out + n_front_out]
    halo_ref, *scratch1 = refs[n_all_in + n_out + n_front_out:]
    _layer0_kernel(*in0, *out0, halo_ref, nb=nb, tt=tt, pos_base=pos_base, side=side)
    _layer1_front_kernel(out0[0], *in1, *out1, *scratch1, nb=nb, tt=tt, fold=fold)


def _layer0_front(x, prev, mem_k, mem_v, p, *, nb, tt, pos_base, fold, side=()):
    io0 = _layer0_io(x, prev, mem_k, mem_v, p, nb=nb, tt=tt, side=side)
    io1 = _layer1_front_io(x, mem_k, mem_v, p, nb=nb, tt=tt, fold=fold)
    io = _CallIO(io0.in_specs + io1.in_specs[1:], io0.args + io1.args[1:], io0.out_specs + io1.out_specs,
                 io0.out_shape + io1.out_shape, io0.scratch + io1.scratch)
    return _tile_call(
        functools.partial(_layer0_front_kernel, n_in=len(io0.in_specs), n_out=len(io0.out_specs), nb=nb, tt=tt,
                          pos_base=pos_base, side=tuple(job[0] for job in side), fold=fold),
        "layer0_front", x, nb, tt, io)


def _fox_prompt_kernel(qt_ref, k_ref, vt_ref, o_ref, m_ref, acc_ref, *s_refs, t):
    i = pl.program_id(1)
    half = t // 2
    sub = lax.broadcasted_iota(jnp.int32, (PAIR_W, 1), 0)
    col = [slice(h * PAIR_W, (h + 1) * PAIR_W) for h in range(FOX_HEADS)]
    qt = [qt_ref[0, 0, c, :] for c in col]
    slot = lambda n: s_refs[n % len(s_refs)]

    def chain(n_items, logits, update):
        top = [logits(n) for n in range(FOX_AHEAD)]
        for n in range(n_items):
            if n + FOX_AHEAD < n_items:
                top.append(logits(n + FOX_AHEAD))
            update(n, top[n])

    tri = (lax.broadcasted_iota(jnp.int32, (half, half), 0) <= lax.broadcasted_iota(jnp.int32, (half, half), 1))
    lo = pl.ds(pl.multiple_of(i * t, t), half)
    hi = pl.ds(pl.multiple_of(i * t + half, half), half)

    def diag_logits(h):
        s_lo = _dot(k_ref[0, lo, col[h]], qt[h])
        s_hi = _dot(k_ref[0, hi, col[h]], qt[h][:, half:])
        s_lo = jnp.concatenate([jnp.where(tri, s_lo[:, :half], NEG_BIG), s_lo[:, half:]], axis=1)
        s_hi = jnp.where(tri, s_hi, NEG_BIG)
        slot(h)[:half, :] = s_lo
        slot(h)[half:, half:] = s_hi
        top = jnp.max(s_lo, axis=0, keepdims=True)
        return jnp.concatenate([top[:, :half], jnp.maximum(top[:, half:], jnp.max(s_hi, axis=0, keepdims=True))],
                               axis=1)

    def diag_update(h, m):
        p_lo = jnp.exp2(slot(h)[:half, :] - m).astype(BF16)
        p_hi = jnp.exp2(slot(h)[half:, half:] - m[:, half:]).astype(BF16)
        acc = _dot(vt_ref[0, i, col[h], :half], p_lo)
        acc_ref[h] = jnp.concatenate([acc[:, :half], acc[:, half:] + _dot(vt_ref[0, i, col[h], half:], p_hi)], axis=1)
        m_ref[h] = m

    chain(FOX_HEADS, diag_logits, diag_update)

    def full_tiles(j0, n_tiles):
        def logits(n):
            rows = pl.ds(pl.multiple_of((j0 + n // FOX_HEADS) * t, t), t)
            s = _dot(k_ref[0, rows, col[n % FOX_HEADS]], qt[n % FOX_HEADS])
            slot(n)[...] = s
            return jnp.max(s, axis=0, keepdims=True)

        def update(n, top):
            h = n % FOX_HEADS
            m = m_ref[h]
            m_new = jnp.maximum(m, top)
            p = jnp.exp2(slot(n)[...] - m_new).astype(BF16)
            acc_ref[h] = jnp.exp2(m - m_new) * acc_ref[h] + _dot(vt_ref[0, j0 + n // FOX_HEADS, col[h], :], p)
            m_ref[h] = m_new

        chain(n_tiles * FOX_HEADS, logits, update)

    lax.fori_loop(0, i // 2, lambda jj, c: full_tiles(2 * jj, 2), None)
    lax.fori_loop(2 * (i // 2), i, lambda j, c: full_tiles(j, 1), None)
    for pair in range(FOX_HEADS // 2):
        even, odd = acc_ref[2 * pair], acc_ref[2 * pair + 1]
        own = sub < FOX_HD
        out = jnp.where(own, even, odd) / jnp.where(own, even[FOX_HD:FOX_HD + 1, :], odd[0:1, :])
        o_ref[0, :, pair * PAIR_W:(pair + 1) * PAIR_W] = out.T.astype(BF16)


def _fox_prompt(qt, ka, vt):
    b, n_tiles, w, tile = qt.shape
    per_stream = lambda shape: pl.BlockSpec((1,) + shape, lambda i, j: (i,) + (0,) * len(shape))
    return pl.pallas_call(
        functools.partial(_fox_prompt_kernel, t=tile),
        grid=(b, n_tiles),
        in_specs=[pl.BlockSpec((1, 1, w, tile), lambda i, j: (i, j, 0, 0)),
                  per_stream((n_tiles * tile, w)), per_stream((n_tiles, w, tile))],
        out_specs=pl.BlockSpec((1, tile, MIX_W), lambda i, j: (i, j, 0)),
        out_shape=jax.ShapeDtypeStruct((b, n_tiles * tile, MIX_W), BF16),
        scratch_shapes=([pltpu.VMEM((FOX_HEADS, 1, tile), F32), pltpu.VMEM((FOX_HEADS, PAIR_W, tile), F32)]
                        + [pltpu.VMEM((tile, tile), F32)] * (FOX_AHEAD + 1)),
        compiler_params=pltpu.CompilerParams(dimension_semantics=("arbitrary", "arbitrary"),
                                             vmem_limit_bytes=VMEM_LIMIT),
        name="fox_prompt",
    )(qt, ka, vt)


def _fox_sample_kernel(q_ref, kc_ref, vc_ref, kn_ref, vn_ref, lfc_t_ref, c_ref, ct_ref, o_ref):
    tn = q_ref.shape[1]
    p = lfc_t_ref.shape[2]
    cc = _cumsum(lfc_t_ref[0], 1, p)
    ck_cache = cc - cc[:, p - 1:]
    causal = (lax.broadcasted_iota(jnp.int32, (tn, V7X_LANES), 1)
              <= lax.broadcasted_iota(jnp.int32, (tn, V7X_LANES), 0))
    pad = jnp.zeros((V7X_LANES - tn, FOX_HD), BF16)
    col = [slice(h * FOX_HD, (h + 1) * FOX_HD) for h in range(FOX_HEADS)]
    logits = []
    for h in range(FOX_HEADS):
        qh = q_ref[0, :, col[h]]
        cq = c_ref[0, :, h:h + 1]
        kn = jnp.concatenate([kn_ref[0, :, col[h]], pad], axis=0)
        s_old = _dot(qh, kc_ref[0, h].astype(BF16)) + cq - ck_cache[h:h + 1, :]
        s_new = jnp.where(causal, _dot_nt(qh, kn) + cq - ct_ref[0, h:h + 1, :], NEG_BIG)
        logits.append((s_old, s_new))
    probs = []
    for s_old, s_new in logits:
        m = jnp.maximum(jnp.max(s_old, axis=-1, keepdims=True), jnp.max(s_new, axis=-1, keepdims=True))
        p_old = jnp.exp(s_old - m)
        p_new = jnp.exp(s_new - m)
        l = jnp.sum(p_old, axis=-1, keepdims=True) + jnp.sum(p_new, axis=-1, keepdims=True)
        probs.append((p_old.astype(BF16), p_new.astype(BF16), l))
    outs = []
    for h, (p_old, p_new, l) in enumerate(probs):
        vn = jnp.concatenate([vn_ref[0, :, col[h]], pad], axis=0)
        outs.append((_dot_nt(p_old, vc_ref[0, h].astype(BF16)) + _dot(p_new, vn)) / l)
    o_ref[0] = jnp.concatenate(outs, axis=-1).astype(BF16)


def _fox_sample(q, cache_kt, cache_vt, kb, vb, cache_logf_t, c, ct):
    b, tn, _ = q.shape
    p = cache_kt.shape[3]
    new = lambda w: pl.BlockSpec((1, tn, w), lambda i: (i, 0, 0))
    old = pl.BlockSpec((1, FOX_HEADS, FOX_HD, p), lambda i: (i, 0, 0, 0))
    return pl.pallas_call(
        _fox_sample_kernel,
        grid=(b,),
        in_specs=[new(MIX_W), old, old, new(MIX_W), new(MIX_W),
                  pl.BlockSpec((1, FOX_HEADS, p), lambda i: (i, 0, 0)),
                  new(FOX_HEADS), pl.BlockSpec((1, FOX_HEADS, V7X_LANES), lambda i: (i, 0, 0))],
        out_specs=new(MIX_W),
        out_shape=jax.ShapeDtypeStruct((b, tn, MIX_W), BF16),
        compiler_params=pltpu.CompilerParams(dimension_semantics=("arbitrary",), vmem_limit_bytes=VMEM_LIMIT),
        name="fox_sample",
    )(q, cache_kt, cache_vt, kb, vb, cache_logf_t, c, ct)


def _layer1_back_kernel(x_ref, mix_ref, mem_ref, g_ref, w_out, w_up, w_down, *refs, nb, tt, side):
    n = len(side)
    y_ref = refs[n]
    m = nb * tt
    x = x_ref[...].reshape(m, D_MODEL)
    mix = mix_ref[...].reshape(m, MIX_W)
    mem = mem_ref[...].reshape(m, MEM_W)
    x1 = [x[rows] + _rms(_dot(mix[rows], w_out[:MIX_W, :]) + _dot(mem[rows], w_out[MIX_W:, :]),
                         _gain(g_ref, "mix_post", 1)) for rows in _row_parts(m)]
    y = jnp.concatenate(_mlp(x1, _gain(g_ref, "mlp_pre", 1), _gain(g_ref, "mlp_post", 1), w_up, w_down), axis=0)
    y_ref[...] = y.reshape(nb, tt, D_MODEL)
    _side_jobs(side, refs[:n], refs[n + 1:])


def _layer1_back(x, mix, mem, p, *, nb, tt, side=()):
    b, t, _ = x.shape
    grid = (b // nb, t // tt)
    tile = lambda w: pl.BlockSpec((nb, tt, w), lambda i, j: (i, j, 0))
    side_in, side_out, side_shape = _side_specs(side, grid[0] * grid[1], lambda i, j: i * grid[1] + j)
    return pl.pallas_call(
        functools.partial(_layer1_back_kernel, nb=nb, tt=tt, side=tuple(job[0] for job in side)),
        grid=grid,
        in_specs=[tile(D_MODEL), tile(MIX_W), tile(MEM_W), _const_spec(p["gains"].shape),
                  _layer_spec(p["w_out"][1], 0), _layer_spec(p["w_up"][1], 0), _layer_spec(p["w_down"][1], 0)]
                 + side_in,
        out_specs=[tile(D_MODEL)] + side_out,
        out_shape=[jax.ShapeDtypeStruct((b, t, D_MODEL), F32)] + side_shape,
        compiler_params=pltpu.CompilerParams(dimension_semantics=("arbitrary", "arbitrary"),
                                             vmem_limit_bytes=VMEM_LIMIT),
        name="layer1_back",
    )(x, mix, mem, p["gains"], p["w_out"][1], p["w_up"][1], p["w_down"][1],
      *[job[1] for job in side])


def _token_tile(b, t):
    if t >= TOKEN_TILE:
        assert t % TOKEN_TILE == 0
        return 1, TOKEN_TILE
    return b, t


def _trunk(x, pool_prev, cache, mem_k, mem_v, p, later_f32=None, side=((), ())):
    b, t, _ = x.shape
    nb, tt = _token_tile(b, t)
    pos_base = 0 if cache is None else POOL_BUF
    names = sorted(later_f32) if later_f32 else []
    side0 = [("cast", later_f32[n], 1) for n in names] + list(side[0])
    if names:
        y0, pool_state, *done = _layer0(x, pool_prev, mem_k, mem_v, p, nb=nb, tt=tt, pos_base=pos_base, side=side0)
        p = {**p, **{n: [p[n][0], w] for n, w in zip(names, done)}}
        front = _layer1_front(y0, mem_k, mem_v, p, nb=nb, tt=tt, fold=cache is None)
    else:
        y0, pool_state, *rest = _layer0_front(x, pool_prev, mem_k, mem_v, p, nb=nb, tt=tt, pos_base=pos_base,
                                              fold=cache is None, side=side0)
        done, front = rest[:len(side0)], rest[len(side0):]
    done_first = done[len(names):]
    k, v, logf_t, mem, *att = front
    logf = jnp.transpose(logf_t.reshape(b // nb, FOX_HEADS, nb, t), (0, 2, 3, 1)).reshape(b, t, FOX_HEADS)
    if cache is None:
        mix = _fox_prompt(*att)
    else:
        q, kb, vb, c, ct = att
        mix = _fox_sample(q, *cache[:2], kb, vb, cache[2], c, ct)
    y, *done_last = _layer1_back(y0, mix, mem, p, nb=nb, tt=tt, side=side[1])
    if cache is None:
        heads = lambda a: jnp.transpose(a.reshape(b, FOX_HEADS, FOX_HD, t), (0, 3, 1, 2))
    else:
        heads = lambda a: a.reshape(b, t, FOX_HEADS, FOX_HD)
    return (y, pool_state, heads(k), heads(v), logf), p, done_first, done_last


def kernel(x_prompt, x_sample, cache_pool, cache_k, cache_v, cache_logf, cache_mem_k, cache_mem_v, mem_prompt,
           g_mix_pre, g_mix_post, g_mlp_pre, g_mlp_post, w_in, w_out, w_pool, pool_scale, g_kv, w_kvf, b_f,
           g_mem, w_mem_kv, w_up, w_down):
    depth = w_in.shape[0]
    assert depth == 2 and w_pool.shape[0] == 1, "one pooling layer followed by one forgetting-attention layer"
    bp = x_prompt.shape[0]
    w_f = w_kvf[:, 2 * MIX_W:]
    fold_cols = lambda a: jnp.pad(jnp.concatenate([a, jnp.repeat(a, FOLD_W, axis=-1)], axis=-1),
                                  [(0, 0)] * (a.ndim - 1) + [(0, V7X_LANES - FOLD_SRC - FOLD_W * FOX_HEADS)])
    rows = lambda g: g.reshape(g.shape[0], 1, g.shape[1])
    p = dict(
        gains=_pack_gains(g_mix_pre, g_mix_post, g_mlp_pre, g_mlp_post, g_kv), pool_scale=rows(pool_scale),
        w_pool=w_pool.astype(BF16), w_kv=w_kvf[:, :2 * MIX_W].astype(BF16),
        w_f=fold_cols(w_f).astype(BF16), w_ft=w_f.T.astype(BF16),
        b_f=fold_cols(b_f).reshape(1, V7X_LANES), b_ft=b_f.reshape(FOX_HEADS, 1),
    )
    later_f32 = dict(w_in=w_in, w_out=w_out, w_up=w_up, w_down=w_down)
    names = sorted(later_f32)
    mem_k_prompt, mem_v_prompt, mkh, mvh, *w0 = _mem_project(mem_prompt, g_mem, w_mem_kv,
                                                             side=[("cast", later_f32[n], 0) for n in names])
    p.update({n: [w, None] for n, w in zip(names, w0)})

    heads_view = lambda a: a.reshape(a.shape[:2] + (-1, a.shape[-1]))
    (y_p, pool_p, k_p, v_p, logf_p), p, (cmk,), (cmv,) = _trunk(
        x_prompt, jnp.zeros((bp, HALO, MIX_W), F32), None, mkh, mvh, p, later_f32,
        side=([("heads", heads_view(cache_mem_k))], [("heads", heads_view(cache_mem_v))]))

    pool_prev = jnp.pad(cache_pool[0], ((0, 0), (HALO - POOL_BUF, 0), (0, 0)))
    lanes_view = lambda a: jnp.transpose(a, (0, 2, 3, 1))
    cache = (lanes_view(cache_k), lanes_view(cache_v), jnp.swapaxes(cache_logf, 1, 2))
    (y_s, pool_s, k_s, v_s, logf_s), _, _, _ = _trunk(x_sample, pool_prev, cache, cmk, cmv, p)

    return (y_p, y_s, pool_p, pool_s, k_p, v_p, logf_p, k_s, v_s, logf_s, mem_k_prompt, mem_v_prompt)
```
